```python
import math
import jax, jax.numpy as jnp
from jax import lax
import numpy as np

D_MODEL = 2048
BATCH = 4
SEQ = 2048
DEPTH = 2
DEC_BATCH = 128
DEC_SEQ = 1
PAST_LEN = 8192
PAGE_SIZE = 128

D_MIX = D_MODEL
EPS = 1e-6
ATT_HEAD_DIM = 64
ATT_HEADS = D_MIX // (4 * ATT_HEAD_DIM)
ATT_KV_HEADS = max(1, ATT_HEADS // 4)
ATT_GROUP = ATT_HEADS // ATT_KV_HEADS
D_ATT = ATT_HEADS * ATT_HEAD_DIM
WINDOW = 128
ROPE_THETA = 500000.0
ROPE_DIM = ATT_HEAD_DIM // 4
SSM_HEAD_DIM = 64
SSM_HEADS = D_MIX // (2 * SSM_HEAD_DIM)
D_SSM = SSM_HEADS * SSM_HEAD_DIM
SSM_GROUPS = 2
D_STATE = 128
CONV_WIDTH = 4
CONV_DIM = D_SSM + 2 * SSM_GROUPS * D_STATE
SSM_CHUNK = 128
MLSTM_HEADS = 4
MLSTM_HEAD_DIM = D_MIX // (4 * MLSTM_HEADS)
D_MLSTM = MLSTM_HEADS * MLSTM_HEAD_DIM
MLSTM_CHUNK = 128
D_FF = 4 * D_MODEL
IN_WIDTH = D_ATT + 2 * ATT_KV_HEADS * ATT_HEAD_DIM + D_SSM + CONV_DIM + SSM_HEADS + 4 * D_MLSTM + 2 * MLSTM_HEADS

kernel_name = 'hybrid_swa_ssd_mlstm_step'


def rms_norm(x, w):
    xf = x.astype(jnp.float32)
    y = xf * lax.rsqrt(jnp.mean(xf * xf, axis=-1, keepdims=True) + EPS)
    return (y * w.astype(jnp.float32)).astype(x.dtype)


def _split_points():
    widths = (D_ATT, ATT_KV_HEADS * ATT_HEAD_DIM, ATT_KV_HEADS * ATT_HEAD_DIM, D_SSM, CONV_DIM, SSM_HEADS,
              D_MLSTM, D_MLSTM, D_MLSTM, D_MLSTM, MLSTM_HEADS, MLSTM_HEADS)
    pts, acc = [], 0
    for w in widths[:-1]:
        acc += w
        pts.append(acc)
    return pts


def partial_rope(x, pos):
    half = ROPE_DIM // 2
    inv = jnp.power(jnp.float32(ROPE_THETA), -jnp.arange(half, dtype=jnp.float32) / half)
    ang = pos.astype(jnp.float32)[:, None] * inv[None, :]
    cos = jnp.cos(ang)[None, :, None, :]
    sin = jnp.sin(ang)[None, :, None, :]
    xr = x[..., :ROPE_DIM].astype(jnp.float32)
    x1, x2 = xr[..., :half], xr[..., half:]
    rot = jnp.concatenate([x1 * cos - x2 * sin, x2 * cos + x1 * sin], axis=-1).astype(x.dtype)
    return jnp.concatenate([rot, x[..., ROPE_DIM:]], axis=-1)


def sink_attention(q, k, v, mask, sinks):
    s = jnp.einsum('...qhgd,...khd->...hgqk', q, k).astype(jnp.float32) * (ATT_HEAD_DIM ** -0.5)
    s = jnp.where(mask, s, -jnp.inf)
    sink = jnp.broadcast_to(sinks.astype(jnp.float32)[:, :, None, None], s.shape[:-1] + (1,))
    p = jax.nn.softmax(jnp.concatenate([s, sink], axis=-1), axis=-1)[..., :-1]
    return jnp.einsum('...hgqk,...khd->...qhgd', p.astype(v.dtype), v)


def swa_prompt(q, k, v, sinks):
    B, S = q.shape[0], q.shape[1]
    nb = S // WINDOW
    qb = q.reshape(B, nb, WINDOW, ATT_KV_HEADS, ATT_GROUP, ATT_HEAD_DIM)
    kb = k.reshape(B, nb, WINDOW, ATT_KV_HEADS, ATT_HEAD_DIM)
    vb = v.reshape(B, nb, WINDOW, ATT_KV_HEADS, ATT_HEAD_DIM)
    pad = ((0, 0), (1, 0), (0, 0), (0, 0), (0, 0))
    kk = jnp.concatenate([jnp.pad(kb, pad)[:, :-1], kb], axis=2)
    vv = jnp.concatenate([jnp.pad(vb, pad)[:, :-1], vb], axis=2)
    blk = jnp.arange(nb)[:, None]
    qpos = blk * WINDOW + jnp.arange(WINDOW)[None, :]
    kpos = (blk - 1) * WINDOW + jnp.arange(2 * WINDOW)[None, :]
    diff = qpos[:, :, None] - kpos[:, None, :]
    mask = (diff >= 0) & (diff <= WINDOW) & (kpos[:, None, :] >= 0)
    o = sink_attention(qb, kk, vv, mask[None, :, None, None], sinks)
    return o.reshape(B, S, D_ATT)


def swa_sample(q, k, v, k_buf, v_buf, sinks):
    Bd, T = q.shape[0], q.shape[1]
    wb = k_buf.shape[1]
    kk = jnp.concatenate([k_buf, k], axis=1)
    vv = jnp.concatenate([v_buf, v], axis=1)
    qpos = PAST_LEN + jnp.arange(T)
    kpos = PAST_LEN - wb + jnp.arange(wb + T)
    diff = qpos[:, None] - kpos[None, :]
    mask = (diff >= 0) & (diff <= WINDOW)
    qg = q.reshape(Bd, T, ATT_KV_HEADS, ATT_GROUP, ATT_HEAD_DIM)
    o = sink_attention(qg, kk, vv, mask[None, None, None], sinks)
    return o.reshape(Bd, T, D_ATT), kk[:, -wb:], vv[:, -wb:]


def causal_conv(xbc, buf, w, b):
    L = xbc.shape[1]
    xp = jnp.concatenate([buf, xbc], axis=1)
    y = sum(xp[:, j:j + L] * w[j] for j in range(CONV_WIDTH)) + b
    return jax.nn.silu(y), xp[:, -(CONV_WIDTH - 1):]


def ssd_scan(x, dt, A, Bm, Cm, S0):
    f32 = jnp.float32
    Bsz, L, H, P = x.shape
    G, N = Bm.shape[2], Bm.shape[3]
    R = H // G
    Q = SSM_CHUNK if L % SSM_CHUNK == 0 else L
    nc = L // Q

    def chunks(t):
        t = t.astype(f32).reshape((Bsz, nc, Q) + t.shape[2:])
        return jnp.moveaxis(t, 1, 0)

    xs = (chunks(x.reshape(Bsz, L, G, R, P)), chunks(dt.reshape(Bsz, L, G, R)), chunks(Bm), chunks(Cm))
    A_gr = A.astype(f32).reshape(G, R)
    tril = jnp.tril(jnp.ones((Q, Q), dtype=bool))

    def step(S, inp):
        xq, dq, bq, cq = inp
        a = jnp.cumsum(dq * A_gr, axis=1)
        at = jnp.moveaxis(a, 1, -1)
        dk = jnp.moveaxis(dq, 1, -1)
        seg = jnp.where(tril, at[..., :, None] - at[..., None, :], -jnp.inf)
        cb = jnp.einsum('bqgn,bkgn->bgqk', cq, bq)
        w = jnp.exp(seg) * cb[:, :, None] * dk[..., None, :]
        y = jnp.einsum('bgrqk,bkgrp->bqgrp', w, xq)
        y = y + jnp.einsum('bqgn,bgrpn->bqgrp', cq, S) * jnp.exp(a)[..., None]
        wk = jnp.exp(at[..., -1:] - at) * dk
        S = S * jnp.exp(at[..., -1])[..., None, None] + jnp.einsum('bgrk,bkgrp,bkgn->bgrpn', wk, xq, bq)
        return S, y

    S, ys = lax.scan(step, S0.astype(f32).reshape(Bsz, G, R, P, N), xs)
    y = jnp.moveaxis(ys, 0, 1).reshape(Bsz, L, H, P)
    return y, S.reshape(Bsz, H, P, N)


def mlstm_scan(q, k, v, ig, fg, C0, n0, m0):
    f32 = jnp.float32
    Bsz, L, H, _ = q.shape
    Q = MLSTM_CHUNK if L % MLSTM_CHUNK == 0 else L
    nc = L // Q

    def chunks(t):
        t = t.astype(f32).reshape((Bsz, nc, Q) + t.shape[2:])
        return jnp.moveaxis(t, 1, 0)

    xs = (chunks(q), chunks(k), chunks(v), chunks(ig), chunks(fg))
    tril = jnp.tril(jnp.ones((Q, Q), dtype=bool))

    def step(carry, inp):
        C, n, m = carry
        qc, kc, vc, ic, fc = inp
        b = jnp.moveaxis(jnp.cumsum(jax.nn.log_sigmoid(fc), axis=1), 1, -1)
        it = jnp.moveaxis(ic, 1, -1)
        logw = jnp.where(tril, b[..., :, None] - b[..., None, :] + it[..., None, :], -jnp.inf)
        log_inter = b + m[..., None]
        mt = jnp.maximum(log_inter, jnp.max(logw, axis=-1))
        sw = jnp.exp(logw - mt[..., None]) * jnp.einsum('bqhd,bkhd->bhqk', qc, kc)
        g = jnp.exp(log_inter - mt)
        gq = jnp.moveaxis(g, -1, 1)[..., None]
        num = jnp.einsum('bhqk,bkhe->bqhe', sw, vc) + jnp.einsum('bqhd,bhde->bqhe', qc, C) * gq
        den = jnp.sum(sw, axis=-1) + jnp.einsum('bqhd,bhd->bhq', qc, n) * g
        h = num / jnp.moveaxis(jnp.maximum(jnp.abs(den), jnp.exp(-mt)), -1, 1)[..., None]
        m_new = mt[..., -1]
        wk = jnp.exp(b[..., -1:] - b + it - m_new[..., None])
        g_end = jnp.exp(b[..., -1] + m - m_new)
        C = C * g_end[..., None, None] + jnp.einsum('bhk,bkhd,bkhe->bhde', wk, kc, vc)
        n = n * g_end[..., None] + jnp.einsum('bhk,bkhd->bhd', wk, kc)
        return (C, n, m_new), h

    (C, n, m), hs = lax.scan(step, (C0.astype(f32), n0.astype(f32), m0.astype(f32)), xs)
    return jnp.moveaxis(hs, 0, 1).reshape(Bsz, L, H, v.shape[-1]), C, n, m


def token_mixers(u, lp, kv_buf, conv_buf, ssm0, C0, n0, m0):
    f32 = jnp.float32
    Bsz, L, _ = u.shape
    prompt = kv_buf is None
    proj = u @ lp['w_in']
    (q, k, v, z, xbc, dt, mq, mk, mv, mo, mi, mf) = jnp.split(proj, _split_points(), axis=-1)

    pos = jnp.arange(L, dtype=jnp.int32) + (0 if prompt else PAST_LEN)
    q = partial_rope(q.reshape(Bsz, L, ATT_HEADS, ATT_HEAD_DIM), pos)
    k = partial_rope(k.reshape(Bsz, L, ATT_KV_HEADS, ATT_HEAD_DIM), pos)
    v = v.reshape(Bsz, L, ATT_KV_HEADS, ATT_HEAD_DIM)
    if prompt:
        att = swa_prompt(q, k, v, lp['sinks'])
        keep = min(WINDOW, L)
        k_new, v_new = k[:, L - keep:], v[:, L - keep:]
    else:
        att, k_new, v_new = swa_sample(q, k, v, kv_buf[0], kv_buf[1], lp['sinks'])

    if prompt:
        conv_buf = jnp.zeros((Bsz, CONV_WIDTH - 1, CONV_DIM), xbc.dtype)
        ssm0 = jnp.zeros((Bsz, SSM_HEADS, SSM_HEAD_DIM, D_STATE), f32)
    xbc, conv_new = causal_conv(xbc, conv_buf, lp['conv_w'], lp['conv_b'])
    xs, bm, cm = jnp.split(xbc, [D_SSM, D_SSM + SSM_GROUPS * D_STATE], axis=-1)
    xs = xs.reshape(Bsz, L, SSM_HEADS, SSM_HEAD_DIM)
    dt = jax.nn.softplus(dt.astype(f32) + lp['dt_bias'].astype(f32))
    A = -jnp.exp(lp['a_log'].astype(f32))
    y, ssm_new = ssd_scan(xs, dt, A, bm.reshape(Bsz, L, SSM_GROUPS, D_STATE),
                          cm.reshape(Bsz, L, SSM_GROUPS, D_STATE), ssm0)
    y = y + lp['d_skip'].astype(f32)[:, None] * xs.astype(f32)
    y = y.reshape(Bsz, L, D_SSM) * jax.nn.silu(z.astype(f32))
    y = rms_norm(y.reshape(Bsz, L, SSM_GROUPS, D_SSM // SSM_GROUPS),
                 lp['ssm_norm'].reshape(SSM_GROUPS, D_SSM // SSM_GROUPS)).reshape(Bsz, L, D_SSM)

    if prompt:
        C0 = jnp.zeros((Bsz, MLSTM_HEADS, MLSTM_HEAD_DIM, MLSTM_HEAD_DIM), f32)
        n0 = jnp.zeros((Bsz, MLSTM_HEADS, MLSTM_HEAD_DIM), f32)
        m0 = jnp.zeros((Bsz, MLSTM_HEADS), f32)
    hd = (Bsz, L, MLSTM_HEADS, MLSTM_HEAD_DIM)
    h, C_new, n_new, m_new = mlstm_scan(
        mq.reshape(hd), mk.reshape(hd) * (MLSTM_HEAD_DIM ** -0.5), mv.reshape(hd),
        mi.astype(f32) + lp['igate_b'].astype(f32), mf.astype(f32) + lp['fgate_b'].astype(f32), C0, n0, m0)
    h = rms_norm(h, lp['mlstm_norm'].reshape(MLSTM_HEADS, MLSTM_HEAD_DIM)).reshape(Bsz, L, D_MLSTM)
    h = h * jax.nn.sigmoid(mo.astype(f32))

    mix = jnp.concatenate([att, y.astype(u.dtype), h.astype(u.dtype)], axis=-1) @ lp['w_out']
    return mix, (k_new, v_new, conv_new, ssm_new, C_new, n_new, m_new)


def decoder_layer(x, lp, kv_buf, conv_buf, ssm0, C0, n0, m0):
    mix, new_state = token_mixers(rms_norm(x, lp['norm_mix']), lp, kv_buf, conv_buf, ssm0, C0, n0, m0)
    x = x + mix
    u = rms_norm(x, lp['norm_mlp'])
    x = x + jnp.square(jax.nn.relu(u @ lp['w_up'])) @ lp['w_down']
    return x, new_state


def setup_inputs(seed: int = 0) -> dict:
    key = jax.random.key(seed)
    ks = jax.random.split(key, 32)
    f32 = jnp.float32
    win = min(WINDOW, PAST_LEN)

    def nrm(k, shape, s):
        return jax.random.normal(k, shape, f32) * s

    dt0 = jnp.exp(jax.random.uniform(ks[14], (DEPTH, SSM_HEADS), f32, math.log(1e-3), math.log(1e-1)))
    return {
        'x_prompt': nrm(ks[0], (BATCH, SEQ, D_MODEL), 1.0),
        'x_sample': nrm(ks[1], (DEC_BATCH, DEC_SEQ, D_MODEL), 1.0),
        'cache_swa_k': nrm(ks[2], (DEPTH, DEC_BATCH, win, ATT_KV_HEADS, ATT_HEAD_DIM), 1.0),
        'cache_swa_v': nrm(ks[3], (DEPTH, DEC_BATCH, win, ATT_KV_HEADS, ATT_HEAD_DIM), 1.0),
        'state_conv': nrm(ks[4], (DEPTH, DEC_BATCH, CONV_WIDTH - 1, CONV_DIM), 1.0),
        'state_ssm': nrm(ks[5], (DEPTH, DEC_BATCH, SSM_HEADS, SSM_HEAD_DIM, D_STATE), 0.3),
        'state_mlstm_C': nrm(ks[6], (DEPTH, DEC_BATCH, MLSTM_HEADS, MLSTM_HEAD_DIM, MLSTM_HEAD_DIM), 0.3),
        'state_mlstm_n': nrm(ks[7], (DEPTH, DEC_BATCH, MLSTM_HEADS, MLSTM_HEAD_DIM), 0.3),
        'state_mlstm_m': nrm(ks[8], (DEPTH, DEC_BATCH, MLSTM_HEADS), 1.0),
        'w_norm_mix': 1.0 + nrm(ks[9], (DEPTH, D_MODEL), 0.02),
        'w_in': nrm(ks[10], (DEPTH, D_MODEL, IN_WIDTH), D_MODEL ** -0.5),
        'attn_sinks': nrm(ks[11], (DEPTH, ATT_KV_HEADS, ATT_GROUP), 0.5),
        'conv_w': nrm(ks[12], (DEPTH, CONV_WIDTH, CONV_DIM), CONV_WIDTH ** -0.5),
        'conv_b': nrm(ks[13], (DEPTH, CONV_DIM), 0.02),
        'dt_bias': dt0 + jnp.log(-jnp.expm1(-dt0)),
        'a_log': jnp.log(jax.random.uniform(ks[15], (DEPTH, SSM_HEADS), f32, 1.0, 16.0)),
        'd_skip': 1.0 + nrm(ks[16], (DEPTH, SSM_HEADS), 0.1),
        'w_norm_ssm': 1.0 + nrm(ks[17], (DEPTH, D_SSM), 0.02),
        'igate_b': nrm(ks[18], (DEPTH, MLSTM_HEADS), 0.1),
        'fgate_b': jnp.linspace(3.0, 6.0, MLSTM_HEADS, dtype=f32)[None, :] + nrm(ks[19], (DEPTH, MLSTM_HEADS), 0.1),
        'w_norm_mlstm': 1.0 + nrm(ks[20], (DEPTH, D_MLSTM), 0.02),
        'w_out': nrm(ks[21], (DEPTH, D_MIX, D_MODEL), D_MIX ** -0.5),
        'w_norm_mlp': 1.0 + nrm(ks[22], (DEPTH, D_MODEL), 0.02),
        'w_up': nrm(ks[23], (DEPTH, D_MODEL, D_FF), D_MODEL ** -0.5),
        'w_down': nrm(ks[24], (DEPTH, D_FF, D_MODEL), D_FF ** -0.5),
        'w_norm_final': 1.0 + nrm(ks[25], (D_MODEL,), 0.02),
    }


def reference(x_prompt, x_sample, cache_swa_k, cache_swa_v, state_conv, state_ssm, state_mlstm_C,
              state_mlstm_n, state_mlstm_m, w_norm_mix, w_in, attn_sinks, conv_w, conv_b, dt_bias, a_log,
              d_skip, w_norm_ssm, igate_b, fgate_b, w_norm_mlstm, w_out, w_norm_mlp, w_up, w_down,
              w_norm_final):
    hp, hs = x_prompt, x_sample
    st_p, st_s = [], []
    for l in range(DEPTH):
        lp = {'norm_mix': w_norm_mix[l], 'w_in': w_in[l], 'sinks': attn_sinks[l], 'conv_w': conv_w[l],
              'conv_b': conv_b[l], 'dt_bias': dt_bias[l], 'a_log': a_log[l], 'd_skip': d_skip[l],
              'ssm_norm': w_norm_ssm[l], 'igate_b': igate_b[l], 'fgate_b': fgate_b[l],
              'mlstm_norm': w_norm_mlstm[l], 'w_out': w_out[l], 'norm_mlp': w_norm_mlp[l],
              'w_up': w_up[l], 'w_down': w_down[l]}
        hp, sp = decoder_layer(hp, lp, None, None, None, None, None, None)
        hs, ss = decoder_layer(hs, lp, (cache_swa_k[l], cache_swa_v[l]), state_conv[l], state_ssm[l],
                               state_mlstm_C[l], state_mlstm_n[l], state_mlstm_m[l])
        st_p.append(sp)
        st_s.append(ss)
    y_prompt = rms_norm(hp, w_norm_final)
    y_sample = rms_norm(hs, w_norm_final)
    p_k, p_v, p_conv, p_ssm, p_C, p_n, p_m = [jnp.stack([s[i] for s in st_p]) for i in range(7)]
    s_k, s_v, s_conv, s_ssm, s_C, s_n, s_m = [jnp.stack([s[i] for s in st_s]) for i in range(7)]
    return (y_prompt, y_sample, p_k, p_v, p_conv, p_ssm, p_C, p_n, p_m,
            s_k, s_v, s_conv, s_ssm, s_C, s_n, s_m)
```

```python
import functools
import math

import jax
import jax.numpy as jnp
from jax import lax
from jax.experimental import pallas as pl
from jax.experimental.pallas import tpu as pltpu

F32 = jnp.float32
BF16 = jnp.bfloat16
HIGHEST = lax.Precision.HIGHEST

D_MODEL = 2048
EPS = 1e-6
PAST_LEN = 8192
ATT_HEAD_DIM = 64
ATT_HEADS = 8
ATT_KV_HEADS = 2
ATT_GROUP = ATT_HEADS // ATT_KV_HEADS
D_ATT = ATT_HEADS * ATT_HEAD_DIM
D_KV = ATT_KV_HEADS * ATT_HEAD_DIM
WINDOW = 128
ROPE_THETA = 500000.0
ROPE_DIM = ATT_HEAD_DIM // 4
SSM_HEAD_DIM = 64
SSM_HEADS = 16
D_SSM = SSM_HEADS * SSM_HEAD_DIM
SSM_GROUPS = 2
D_STATE = 128
CONV_WIDTH = 4
CONV_DIM = D_SSM + 2 * SSM_GROUPS * D_STATE
CHUNK = 128
MLSTM_HEADS = 4
MLSTM_HEAD_DIM = 128
D_MLSTM = MLSTM_HEADS * MLSTM_HEAD_DIM
D_FF = 4 * D_MODEL

C_XBC = 0
C_Q = 1536
C_Z = 2048
C_MQ = 3072
C_MK = 3584
C_MV = 4096
C_MO = 4608
C_K = 5120
C_V = 5248
C_G = 5376
NP = 5632
GATE_LANE = 16

LANE = 128
VMEM_LIMIT = 56 * 1024 * 1024


def _cparams(sem):
    return pltpu.CompilerParams(dimension_semantics=sem, vmem_limit_bytes=VMEM_LIMIT)


def _silu(x):
    return x * jax.nn.sigmoid(x)


def _dot(a, b):
    return jnp.dot(a.astype(BF16), b.astype(BF16), preferred_element_type=F32)


def _dot_nt(a, b):
    return lax.dot_general(a.astype(BF16), b.astype(BF16), (((1,), (1,)), ((), ())),
                           preferred_element_type=F32)


def _dot_tn(a, b):
    return lax.dot_general(a.astype(BF16), b.astype(BF16), (((0,), (0,)), ((), ())),
                           preferred_element_type=F32)


def _tril(n):
    row = lax.broadcasted_iota(jnp.int32, (n, n), 0)
    col = lax.broadcasted_iota(jnp.int32, (n, n), 1)
    return row >= col


def _cumsum_rows(x):
    t = _tril(x.shape[0]).astype(F32)
    return jnp.dot(t, x, precision=HIGHEST, preferred_element_type=F32)


def _rms_matmul_kernel(x_ref, g_ref, w_ref, o_ref, u_ref):
    @pl.when(pl.program_id(1) == 0)
    def _():
        x = x_ref[...]
        ms = jnp.mean(x * x, axis=-1, keepdims=True)
        u_ref[...] = (x * lax.rsqrt(ms + EPS) * g_ref[...]).astype(BF16)

    o_ref[...] = jnp.dot(u_ref[...], w_ref[...], preferred_element_type=F32)


def _rms_matmul(x, g, w, tm, tn):
    m, k = x.shape
    n = w.shape[1]
    return pl.pallas_call(
        _rms_matmul_kernel,
        grid=(m // tm, n // tn),
        in_specs=[pl.BlockSpec((tm, k), lambda i, j: (i, 0)),
                  pl.BlockSpec((1, k), lambda i, j: (0, 0)),
                  pl.BlockSpec((k, tn), lambda i, j: (0, j))],
        out_specs=pl.BlockSpec((tm, tn), lambda i, j: (i, j)),
        out_shape=jax.ShapeDtypeStruct((m, n), F32),
        scratch_shapes=[pltpu.VMEM((tm, k), BF16)],
        compiler_params=_cparams(("parallel", "arbitrary")),
        name="rms_inproj",
    )(x, g, w)


def _outproj_kernel(a_ref, y_ref, h_ref, x_ref, w_ref, o_ref):
    acc = jnp.dot(a_ref[...], w_ref[0:D_ATT, :], preferred_element_type=F32)
    acc = acc + jnp.dot(y_ref[...], w_ref[D_ATT:D_ATT + D_SSM, :], preferred_element_type=F32)
    acc = acc + jnp.dot(h_ref[...], w_ref[D_ATT + D_SSM:, :], preferred_element_type=F32)
    o_ref[...] = x_ref[...] + acc


def _outproj(att, y, h, x, w, tm, tn):
    m = x.shape[0]
    n = w.shape[1]
    return pl.pallas_call(
        _outproj_kernel,
        grid=(m // tm, n // tn),
        in_specs=[pl.BlockSpec((tm, D_ATT), lambda i, j: (i, 0)),
                  pl.BlockSpec((tm, D_SSM), lambda i, j: (i, 0)),
                  pl.BlockSpec((tm, D_MLSTM), lambda i, j: (i, 0)),
                  pl.BlockSpec((tm, tn), lambda i, j: (i, j)),
                  pl.BlockSpec((w.shape[0], tn), lambda i, j: (0, j))],
        out_specs=pl.BlockSpec((tm, tn), lambda i, j: (i, j)),
        out_shape=jax.ShapeDtypeStruct((m, n), F32),
        compiler_params=_cparams(("parallel", "arbitrary")),
        name="outproj",
    )(att, y, h, x, w)


def _mlp_kernel(x_ref, g_ref, wu_ref, wd_ref, gf_ref, o_ref, u_ref, acc_ref, *, final_norm):
    j = pl.program_id(1)

    @pl.when(j == 0)
    def _():
        x = x_ref[...]
        ms = jnp.mean(x * x, axis=-1, keepdims=True)
        u_ref[...] = (x * lax.rsqrt(ms + EPS) * g_ref[...]).astype(BF16)
        acc_ref[...] = jnp.zeros_like(acc_ref)

    h = jnp.dot(u_ref[...], wu_ref[...], preferred_element_type=F32)
    h = jnp.square(jnp.maximum(h, 0.0))
    acc_ref[...] += jnp.dot(h.astype(BF16), wd_ref[...], preferred_element_type=F32)

    @pl.when(j == pl.num_programs(1) - 1)
    def _():
        y = x_ref[...] + acc_ref[...]
        if final_norm:
            ms = jnp.mean(y * y, axis=-1, keepdims=True)
            y = y * lax.rsqrt(ms + EPS) * gf_ref[...]
        o_ref[...] = y


def _mlp(x, g, wu, wd, gf, final_norm, tm, tf):
    m, d = x.shape
    ff = wu.shape[1]
    return pl.pallas_call(
        functools.partial(_mlp_kernel, final_norm=final_norm),
        grid=(m // tm, ff // tf),
        in_specs=[pl.BlockSpec((tm, d), lambda i, j: (i, 0)),
                  pl.BlockSpec((1, d), lambda i, j: (0, 0)),
                  pl.BlockSpec((d, tf), lambda i, j: (0, j)),
                  pl.BlockSpec((tf, d), lambda i, j: (j, 0)),
                  pl.BlockSpec((1, d), lambda i, j: (0, 0))],
        out_specs=pl.BlockSpec((tm, d), lambda i, j: (i, 0)),
        out_shape=jax.ShapeDtypeStruct((m, d), F32),
        scratch_shapes=[pltpu.VMEM((tm, d), BF16), pltpu.VMEM((tm, d), F32)],
        compiler_params=_cparams(("parallel", "arbitrary")),
        name="mlp",
    )(x, g, wu, wd, gf)


def _rope_tables(pos):
    half = ROPE_DIM // 2
    inv = jnp.power(jnp.float32(ROPE_THETA), -jnp.arange(half, dtype=jnp.float32) / half)
    ang = pos.astype(jnp.float32)[:, None] * inv[None, :]
    cos = jnp.cos(ang)
    sin = jnp.sin(ang)
    n = pos.shape[0]
    rest = ATT_HEAD_DIM - ROPE_DIM
    c = jnp.concatenate([cos, cos, jnp.ones((n, rest), F32)], axis=1)
    s1 = jnp.concatenate([-sin, jnp.zeros((n, half + rest), F32)], axis=1)
    s2 = jnp.concatenate([jnp.zeros((n, half), F32), sin, jnp.zeros((n, rest), F32)], axis=1)
    rep = LANE // ATT_HEAD_DIM
    return jnp.tile(c, (1, rep)), jnp.tile(s1, (1, rep)), jnp.tile(s2, (1, rep))


def _rope(x, c, s1, s2):
    width = x.shape[-1]
    rep = width // LANE
    half = ROPE_DIM // 2
    if rep > 1:
        c = jnp.concatenate([c] * rep, axis=-1)
        s1 = jnp.concatenate([s1] * rep, axis=-1)
        s2 = jnp.concatenate([s2] * rep, axis=-1)
    axis = x.ndim - 1
    return x * c + pltpu.roll(x, width - half, axis) * s1 + pltpu.roll(x, half, axis) * s2


def _attn_prompt_kernel(sink_ref, q_ref, k_ref, v_ref, c_ref, s1_ref, s2_ref,
                        o_ref, ko_ref, vo_ref, kk_ref, vv_ref):
    j = pl.program_id(1)
    w = WINDOW

    @pl.when(j == 0)
    def _():
        kk_ref[0:w, :] = jnp.zeros((w, D_KV), F32)
        vv_ref[0:w, :] = jnp.zeros((w, D_KV), F32)

    c = c_ref[...]
    s1 = s1_ref[...]
    s2 = s2_ref[...]
    krot = _rope(k_ref[...], c, s1, s2)
    v = v_ref[...]
    kk_ref[w:2 * w, :] = krot
    vv_ref[w:2 * w, :] = v
    qrot = _rope(q_ref[...], c, s1, s2)
    kk = kk_ref[...].astype(BF16)
    vv = vv_ref[...].astype(BF16)

    row = lax.broadcasted_iota(jnp.int32, (w, 2 * w), 0)
    col = lax.broadcasted_iota(jnp.int32, (w, 2 * w), 1)
    first_col = jnp.where(j > 0, 0, w)
    mask = (col >= row) & (col <= row + w) & (col >= first_col)
    scale = ATT_HEAD_DIM ** -0.5

    for h in range(ATT_KV_HEADS):
        kh = kk[:, h * ATT_HEAD_DIM:(h + 1) * ATT_HEAD_DIM]
        vh = vv[:, h * ATT_HEAD_DIM:(h + 1) * ATT_HEAD_DIM]
        for g in range(ATT_GROUP):
            hg = h * ATT_GROUP + g
            qh = qrot[:, hg * ATT_HEAD_DIM:(hg + 1) * ATT_HEAD_DIM]
            s = _dot_nt(qh, kh) * scale
            s = jnp.where(mask, s, -jnp.inf)
            sink = sink_ref[hg]
            m = jnp.maximum(jnp.max(s, axis=-1, keepdims=True), sink)
            p = jnp.exp(s - m)
            denom = jnp.sum(p, axis=-1, keepdims=True) + jnp.exp(sink - m)
            p = p / denom
            o = _dot(p, vh)
            o_ref[:, hg * ATT_HEAD_DIM:(hg + 1) * ATT_HEAD_DIM] = o.astype(o_ref.dtype)

    kk_ref[0:w, :] = krot
    vv_ref[0:w, :] = v

    @pl.when(j == pl.num_programs(1) - 1)
    def _():
        ko_ref[...] = krot
        vo_ref[...] = v


def _attn_prompt(proj, sinks, tables, bsz, seq):
    nb = seq // WINDOW
    w = WINDOW
    c, s1, s2 = tables
    row = lambda b, j: b * nb + j
    tab = pl.BlockSpec((w, LANE), lambda b, j: (j, 0))
    return pl.pallas_call(
        _attn_prompt_kernel,
        grid=(bsz, nb),
        in_specs=[pl.BlockSpec(memory_space=pltpu.SMEM),
                  pl.BlockSpec((w, D_ATT), lambda b, j: (row(b, j), C_Q // D_ATT)),
                  pl.BlockSpec((w, D_KV), lambda b, j: (row(b, j), C_K // D_KV)),
                  pl.BlockSpec((w, D_KV), lambda b, j: (row(b, j), C_V // D_KV)),
                  tab, tab, tab],
        out_specs=[pl.BlockSpec((w, D_ATT), lambda b, j: (row(b, j), 0)),
                   pl.BlockSpec((None, w, D_KV), lambda b, j: (b, 0, 0)),
                   pl.BlockSpec((None, w, D_KV), lambda b, j: (b, 0, 0))],
        out_shape=[jax.ShapeDtypeStruct((bsz * seq, D_ATT), BF16),
                   jax.ShapeDtypeStruct((bsz, w, D_KV), F32),
                   jax.ShapeDtypeStruct((bsz, w, D_KV), F32)],
        scratch_shapes=[pltpu.VMEM((2 * w, D_KV), F32), pltpu.VMEM((2 * w, D_KV), F32)],
        compiler_params=_cparams(("parallel", "arbitrary")),
        name="attn_prompt",
    )(sinks, proj, proj, proj, c, s1, s2)


def _group_rmsnorm(y, w, groups):
    width = y.shape[-1] // groups
    outs = []
    for g in range(groups):
        yg = y[:, g * width:(g + 1) * width]
        ms = jnp.mean(yg * yg, axis=-1, keepdims=True)
        outs.append(yg * lax.rsqrt(ms + EPS) * w[:, g * width:(g + 1) * width])
    return jnp.concatenate(outs, axis=-1)


def _ssd_prompt_kernel(xbc_ref, z_ref, g_ref, cw_ref, cb_ref, dtb_ref, alog_ref, dskip_ref, wn_ref,
                       y_ref, sfin_ref, xext_ref, s_ref, ybuf_ref, xw_ref):
    c = pl.program_id(1)
    q = CHUNK
    pad = 8
    hp = SSM_HEAD_DIM
    heads_per_group = SSM_HEADS // SSM_GROUPS
    gw = heads_per_group * hp

    @pl.when(c == 0)
    def _():
        xext_ref[0:pad, :] = jnp.zeros((pad, CONV_DIM), F32)
        s_ref[...] = jnp.zeros_like(s_ref)

    xext_ref[pad:pad + q, :] = xbc_ref[...]
    cw = cw_ref[...]
    acc = xext_ref[pad - 3:pad - 3 + q, :] * cw[0:1, :]
    acc = acc + xext_ref[pad - 2:pad - 2 + q, :] * cw[1:2, :]
    acc = acc + xext_ref[pad - 1:pad - 1 + q, :] * cw[2:3, :]
    acc = acc + xext_ref[pad:pad + q, :] * cw[3:4, :]
    xc = _silu(acc + cb_ref[...])
    xext_ref[0:pad, :] = xext_ref[q:q + pad, :]

    xs = xc[:, 0:D_SSM]
    bm = xc[:, D_SSM:D_SSM + SSM_GROUPS * D_STATE]
    cm = xc[:, D_SSM + SSM_GROUPS * D_STATE:]

    dt = jax.nn.softplus(g_ref[:, 0:LANE] + dtb_ref[...])
    a_neg = -jnp.exp(alog_ref[...])
    a_col = _cumsum_rows(dt * a_neg)
    a_row = a_col.T
    dt_row = dt.T
    ea_col = jnp.exp(a_col)
    wk_col = jnp.exp(a_col[q - 1:q, :] - a_col) * dt
    tril = _tril(q)

    for g in range(SSM_GROUPS):
        bg = bm[:, g * D_STATE:(g + 1) * D_STATE]
        cg = cm[:, g * D_STATE:(g + 1) * D_STATE]
        cb = _dot_nt(cg, bg)
        cs = _dot_nt(cg, s_ref[g * gw:(g + 1) * gw, :])
        for r in range(heads_per_group):
            h = g * heads_per_group + r
            seg = a_col[:, h:h + 1] - a_row[h:h + 1, :]
            wmat = jnp.exp(jnp.where(tril, seg, -jnp.inf)) * cb * dt_row[h:h + 1, :]
            xh = xs[:, h * hp:(h + 1) * hp]
            yh = _dot(wmat, xh)
            yh = yh + cs[:, r * hp:(r + 1) * hp] * ea_col[:, h:h + 1]
            ybuf_ref[:, h * hp:(h + 1) * hp] = yh
            xw_ref[:, r * hp:(r + 1) * hp] = xh * wk_col[:, h:h + 1]
        upd = _dot_tn(xw_ref[...], bg)
        for r in range(heads_per_group):
            h = g * heads_per_group + r
            decay = jnp.exp(a_row[h:h + 1, q - 1:q])
            s_ref[h * hp:(h + 1) * hp, :] = s_ref[h * hp:(h + 1) * hp, :] * decay + upd[r * hp:(r + 1) * hp, :]

    y = ybuf_ref[...] + dskip_ref[...] * xs
    y = y * _silu(z_ref[...])
    y_ref[...] = _group_rmsnorm(y, wn_ref[...], SSM_GROUPS).astype(y_ref.dtype)

    @pl.when(c == pl.num_programs(1) - 1)
    def _():
        sfin_ref[...] = s_ref[...]


def _ssd_prompt(proj, cw, cb, dtb, alog, dskip_rep, wn, bsz, seq):
    nc = seq // CHUNK
    q = CHUNK
    row = lambda b, c: b * nc + c
    vec = lambda n: pl.BlockSpec((1, n), lambda b, c: (0, 0))
    return pl.pallas_call(
        _ssd_prompt_kernel,
        grid=(bsz, nc),
        in_specs=[pl.BlockSpec((q, CONV_DIM), lambda b, c: (row(b, c), C_XBC // CONV_DIM)),
                  pl.BlockSpec((q, D_SSM), lambda b, c: (row(b, c), C_Z // D_SSM)),
                  pl.BlockSpec((q, 2 * LANE), lambda b, c: (row(b, c), C_G // (2 * LANE))),
                  pl.BlockSpec((CONV_WIDTH, CONV_DIM), lambda b, c: (0, 0)),
                  vec(CONV_DIM), vec(LANE), vec(LANE), vec(D_SSM), vec(D_SSM)],
        out_specs=[pl.BlockSpec((q, D_SSM), lambda b, c: (row(b, c), 0)),
                   pl.BlockSpec((None, D_SSM, D_STATE), lambda b, c: (b, 0, 0))],
        out_shape=[jax.ShapeDtypeStruct((bsz * seq, D_SSM), BF16),
                   jax.ShapeDtypeStruct((bsz, D_SSM, D_STATE), F32)],
        scratch_shapes=[pltpu.VMEM((q + 8, CONV_DIM), F32),
                        pltpu.VMEM((D_SSM, D_STATE), F32),
                        pltpu.VMEM((q, D_SSM), F32),
                        pltpu.VMEM((q, D_SSM // SSM_GROUPS), F32)],
        compiler_params=_cparams(("parallel", "arbitrary")),
        name="ssd_prompt",
    )(proj, proj, proj, cw, cb, dtb, alog, dskip_rep, wn)


def _mlstm_prompt_kernel(q_ref, k_ref, v_ref, o_ref, g_ref, bi_ref, bf_ref, wn_ref,
                         h_ref, cfin_ref, nfin_ref, mfin_ref, c_ref, n_ref, m_ref):
    c = pl.program_id(1)
    t = CHUNK
    d = MLSTM_HEAD_DIM

    @pl.when(c == 0)
    def _():
        c_ref[...] = jnp.zeros_like(c_ref)
        n_ref[...] = jnp.zeros_like(n_ref)
        m_ref[...] = jnp.zeros_like(m_ref)

    gi = g_ref[:, 0:LANE] + bi_ref[...]
    gf = g_ref[:, LANE:2 * LANE] + bf_ref[...]
    b_col = _cumsum_rows(jax.nn.log_sigmoid(gf))
    b_row = b_col.T
    i_row = gi.T
    tril = _tril(t)
    k_all = k_ref[...] * (d ** -0.5)

    for h in range(MLSTM_HEADS):
        ln = GATE_LANE + h
        bq = b_col[:, ln:ln + 1]
        logw = jnp.where(tril, bq - b_row[ln:ln + 1, :] + i_row[ln:ln + 1, :], -jnp.inf)
        m_prev = m_ref[h:h + 1, 0:1]
        log_inter = bq + m_prev
        mt = jnp.maximum(log_inter, jnp.max(logw, axis=-1, keepdims=True))
        qh = q_ref[:, h * d:(h + 1) * d]
        kh = k_all[:, h * d:(h + 1) * d]
        vh = v_ref[:, h * d:(h + 1) * d]
        ch = c_ref[h * d:(h + 1) * d, :]
        nh = n_ref[h:h + 1, :]
        sw = jnp.exp(logw - mt) * _dot_nt(qh, kh)
        gq = jnp.exp(log_inter - mt)
        num = _dot(sw, vh) + _dot(qh, ch) * gq
        den = jnp.sum(sw, axis=-1, keepdims=True) + jnp.sum(qh * nh, axis=-1, keepdims=True) * gq
        hh = num / jnp.maximum(jnp.abs(den), jnp.exp(-mt))

        m_new = mt[t - 1:t, :]
        b_last = b_col[t - 1:t, ln:ln + 1]
        wk = jnp.exp(b_last - bq + gi[:, ln:ln + 1] - m_new)
        g_end = jnp.exp(b_last + m_prev - m_new)
        kw = kh * wk
        c_ref[h * d:(h + 1) * d, :] = ch * g_end + _dot_tn(kw, vh)
        n_ref[h:h + 1, :] = nh * g_end + jnp.sum(kw, axis=0, keepdims=True)
        m_ref[h:h + 1, :] = jnp.broadcast_to(m_new, (1, LANE))

        ms = jnp.mean(hh * hh, axis=-1, keepdims=True)
        hn = hh * lax.rsqrt(ms + EPS) * wn_ref[:, h * d:(h + 1) * d]
        h_ref[:, h * d:(h + 1) * d] = (hn * jax.nn.sigmoid(o_ref[:, h * d:(h + 1) * d])).astype(h_ref.dtype)

    @pl.when(c == pl.num_programs(1) - 1)
    def _():
        cfin_ref[...] = c_ref[...]
        nfin_ref[...] = n_ref[...]
        mfin_ref[...] = m_ref[...]


def _mlstm_prompt(proj, bi, bf, wn, bsz, seq):
    nc = seq // CHUNK
    t = CHUNK
    row = lambda b, c: b * nc + c
    seg = lambda col: pl.BlockSpec((t, D_MLSTM), lambda b, c: (row(b, c), col // D_MLSTM))
    vec = lambda n: pl.BlockSpec((1, n), lambda b, c: (0, 0))
    return pl.pallas_call(
        _mlstm_prompt_kernel,
        grid=(bsz, nc),
        in_specs=[seg(C_MQ), seg(C_MK), seg(C_MV), seg(C_MO),
                  pl.BlockSpec((t, 2 * LANE), lambda b, c: (row(b, c), C_G // (2 * LANE))),
                  vec(LANE), vec(LANE), vec(D_MLSTM)],
        out_specs=[pl.BlockSpec((t, D_MLSTM), lambda b, c: (row(b, c), 0)),
                   pl.BlockSpec((None, D_MLSTM, MLSTM_HEAD_DIM), lambda b, c: (b, 0, 0)),
                   pl.BlockSpec((None, 8, LANE), lambda b, c: (b, 0, 0)),
                   pl.BlockSpec((None, 8, LANE), lambda b, c: (b, 0, 0))],
        out_shape=[jax.ShapeDtypeStruct((bsz * seq, D_MLSTM), BF16),
                   jax.ShapeDtypeStruct((bsz, D_MLSTM, MLSTM_HEAD_DIM), F32),
                   jax.ShapeDtypeStruct((bsz, 8, LANE), F32),
                   jax.ShapeDtypeStruct((bsz, 8, LANE), F32)],
        scratch_shapes=[pltpu.VMEM((D_MLSTM, MLSTM_HEAD_DIM), F32),
                        pltpu.VMEM((8, LANE), F32),
                        pltpu.VMEM((8, LANE), F32)],
        compiler_params=_cparams(("parallel", "arbitrary")),
        name="mlstm_prompt",
    )(proj, proj, proj, proj, proj, bi, bf, wn)


def _attn_sample_kernel(q_ref, k_ref, v_ref, kc_ref, vc_ref, sink_ref, c_ref, s1_ref, s2_ref,
                        o_ref, ko_ref, vo_ref):
    w = kc_ref.shape[1]
    c = c_ref[...]
    s1 = s1_ref[...]
    s2 = s2_ref[...]
    qb = _rope(q_ref[...], c, s1, s2)
    kn = _rope(k_ref[...], c, s1, s2)
    vn = v_ref[...]
    kc = kc_ref[...]
    vc = vc_ref[...]
    ko_ref[:, 0:w - 1, :] = kc[:, 1:w, :]
    ko_ref[:, w - 1:w, :] = kn
    vo_ref[:, 0:w - 1, :] = vc[:, 1:w, :]
    vo_ref[:, w - 1:w, :] = vn

    scale = ATT_HEAD_DIM ** -0.5
    s = jnp.einsum("bhd,bkd->bhk", qb.astype(BF16), kc.astype(BF16),
                   preferred_element_type=F32) * scale
    sn = jnp.sum(qb * kn, axis=-1, keepdims=True) * scale
    sink = sink_ref[...][None, :, 0:1]
    m = jnp.maximum(jnp.maximum(jnp.max(s, axis=-1, keepdims=True), sn), sink)
    p = jnp.exp(s - m)
    pn = jnp.exp(sn - m)
    denom = jnp.sum(p, axis=-1, keepdims=True) + pn + jnp.exp(sink - m)
    p = p / denom
    pn = pn / denom
    o = jnp.einsum("bhk,bkd->bhd", p.astype(BF16), vc.astype(BF16), preferred_element_type=F32)
    o = o + pn * vn
    head = lax.broadcasted_iota(jnp.int32, o.shape[:2] + (ATT_HEAD_DIM,), 1)
    o_ref[...] = jnp.where(head < ATT_GROUP, o[:, :, 0:ATT_HEAD_DIM], o[:, :, ATT_HEAD_DIM:]).astype(o_ref.dtype)


def _attn_sample(qb, kn, vn, kc, vc, sink_rows, tables, tb):
    bd, w, _ = kc.shape
    c, s1, s2 = tables
    blk3 = lambda r, n: pl.BlockSpec((tb, r, n), lambda i: (i, 0, 0))
    vec = pl.BlockSpec((1, LANE), lambda i: (0, 0))
    return pl.pallas_call(
        _attn_sample_kernel,
        grid=(bd // tb,),
        in_specs=[blk3(ATT_HEADS, LANE), blk3(1, LANE), blk3(1, LANE), blk3(w, LANE), blk3(w, LANE),
                  pl.BlockSpec((ATT_HEADS, LANE), lambda i: (0, 0)), vec, vec, vec],
        out_specs=[blk3(ATT_HEADS, ATT_HEAD_DIM), blk3(w, LANE), blk3(w, LANE)],
        out_shape=[jax.ShapeDtypeStruct((bd, ATT_HEADS, ATT_HEAD_DIM), BF16),
                   jax.ShapeDtypeStruct((bd, w, LANE), F32),
                   jax.ShapeDtypeStruct((bd, w, LANE), F32)],
        compiler_params=_cparams(("parallel",)),
        name="attn_sample",
    )(qb, kn, vn, kc, vc, sink_rows, c, s1, s2)


def _lane_place(cols, lane0):
    m = cols[0].shape[0]
    lane = lax.broadcasted_iota(jnp.int32, (m, LANE), 1)
    out = jnp.zeros((m, LANE), F32)
    for i, col in enumerate(cols):
        out = jnp.where(lane == lane0 + i, col, out)
    return out


def _sample_pre_kernel(proj_ref, cs_ref, cw_ref, cb_ref, dtb_ref, alog_ref, bi_ref, bf_ref, n_ref, m_ref,
                       conv_ref, xs_ref, xdt_ref, bm_ref, cm_ref, ea_ref,
                       ks_ref, nn_ref, g_ref, esw_ref, sw_ref, den_ref, mt_ref):
    xbc = proj_ref[:, C_XBC:C_XBC + CONV_DIM]
    s0 = cs_ref[:, 0:CONV_DIM]
    s1 = cs_ref[:, CONV_DIM:2 * CONV_DIM]
    s2 = cs_ref[:, 2 * CONV_DIM:3 * CONV_DIM]
    cw = cw_ref[...]
    acc = s0 * cw[0:1, :]
    acc = acc + s1 * cw[1:2, :]
    acc = acc + s2 * cw[2:3, :]
    acc = acc + xbc * cw[3:4, :]
    xc = _silu(acc + cb_ref[...])
    conv_ref[:, 0:CONV_DIM] = s1
    conv_ref[:, CONV_DIM:2 * CONV_DIM] = s2
    conv_ref[:, 2 * CONV_DIM:3 * CONV_DIM] = xbc

    xs = xc[:, 0:D_SSM]
    xs_ref[...] = xs
    bm_ref[...] = xc[:, D_SSM:D_SSM + SSM_GROUPS * D_STATE]
    cm_ref[...] = xc[:, D_SSM + SSM_GROUPS * D_STATE:]
    dt = jax.nn.softplus(proj_ref[:, C_G:C_G + LANE] + dtb_ref[...])
    ea_ref[...] = jnp.exp(dt * (-jnp.exp(alog_ref[...])))
    hp = SSM_HEAD_DIM
    for h in range(SSM_HEADS):
        xdt_ref[:, h * hp:(h + 1) * hp] = xs[:, h * hp:(h + 1) * hp] * dt[:, h:h + 1]

    d = MLSTM_HEAD_DIM
    gi = proj_ref[:, C_G:C_G + LANE] + bi_ref[...]
    gf = proj_ref[:, C_G + LANE:C_G + 2 * LANE] + bf_ref[...]
    log_inter = jax.nn.log_sigmoid(gf) + m_ref[...]
    mt = jnp.maximum(log_inter, gi)
    gq = jnp.exp(log_inter - mt)
    esw = jnp.exp(gi - mt)
    ks = proj_ref[:, C_MK:C_MK + D_MLSTM] * (d ** -0.5)
    ks_ref[...] = ks
    qk_cols = []
    qn_cols = []
    for h in range(MLSTM_HEADS):
        qh = proj_ref[:, C_MQ + h * d:C_MQ + (h + 1) * d]
        kh = ks[:, h * d:(h + 1) * d]
        nh = n_ref[:, h * d:(h + 1) * d]
        qk_cols.append(jnp.sum(qh * kh, axis=-1, keepdims=True))
        qn_cols.append(jnp.sum(qh * nh, axis=-1, keepdims=True))
        ln = GATE_LANE + h
        nn_ref[:, h * d:(h + 1) * d] = nh * gq[:, ln:ln + 1] + kh * esw[:, ln:ln + 1]
    sw = esw * _lane_place(qk_cols, GATE_LANE)
    den = sw + _lane_place(qn_cols, GATE_LANE) * gq
    g_ref[...] = gq
    esw_ref[...] = esw
    sw_ref[...] = sw
    den_ref[...] = jnp.maximum(jnp.abs(den), jnp.exp(-mt))
    mt_ref[...] = mt


def _sample_pre(proj, conv_state, cw, cb, dtb, alog, bi, bf, n_state, m_tile):
    bd = proj.shape[0]
    tile = jax.ShapeDtypeStruct((bd, LANE), F32)
    outs = [jax.ShapeDtypeStruct((bd, (CONV_WIDTH - 1) * CONV_DIM), F32),
            jax.ShapeDtypeStruct((bd, D_SSM), F32),
            jax.ShapeDtypeStruct((bd, D_SSM), F32),
            jax.ShapeDtypeStruct((bd, SSM_GROUPS * D_STATE), F32),
            jax.ShapeDtypeStruct((bd, SSM_GROUPS * D_STATE), F32),
            tile,
            jax.ShapeDtypeStruct((bd, D_MLSTM), F32),
            jax.ShapeDtypeStruct((bd, D_MLSTM), F32),
            tile, tile, tile, tile, tile]
    return pl.pallas_call(
        _sample_pre_kernel,
        out_shape=outs,
        compiler_params=pltpu.CompilerParams(vmem_limit_bytes=VMEM_LIMIT),
        name="sample_pre",
    )(proj, conv_state, cw, cb, dtb, alog, bi, bf, n_state, m_tile)


def _column_tile(row):
    return jnp.broadcast_to(row, (LANE, LANE)).T


def _ssm_sample_kernel(ea_ref, s_ref, xdt_ref, b_ref, c_ref, so_ref, y_ref):
    tb = s_ref.shape[0]
    base = pl.program_id(0) * tb
    hp = SSM_HEAD_DIM
    heads_per_tile = LANE // hp
    n_tiles = D_SSM // LANE
    tiles_per_group = n_tiles // SSM_GROUPS
    top = lax.broadcasted_iota(jnp.int32, (LANE, 1), 0) < hp

    def body(b, carry):
        xrow = xdt_ref[b]
        brow = b_ref[b]
        crow = c_ref[b]
        for t in range(n_tiles):
            g = t // tiles_per_group
            xcol = _column_tile(xrow[:, t * LANE:(t + 1) * LANE])
            bg = brow[:, g * D_STATE:(g + 1) * D_STATE]
            cg = jnp.broadcast_to(crow[:, g * D_STATE:(g + 1) * D_STATE], (8, D_STATE))
            e0 = ea_ref[base + b, heads_per_tile * t]
            e1 = ea_ref[base + b, heads_per_tile * t + 1]
            decay = jnp.where(top, e0, e1)
            sn = s_ref[b, t * LANE:(t + 1) * LANE, :] * decay + xcol * bg
            so_ref[b, t * LANE:(t + 1) * LANE, :] = sn
            y_ref[b, :, t * LANE:(t + 1) * LANE] = _dot_nt(cg, sn)[0:1, :]
        return carry

    lax.fori_loop(0, tb, body, 0)


def _ssm_sample(ea, s, xdt3, b3, c3, tb):
    bd = s.shape[0]
    blk3 = lambda r, n: pl.BlockSpec((tb, r, n), lambda i: (i, 0, 0))
    return pl.pallas_call(
        _ssm_sample_kernel,
        grid=(bd // tb,),
        in_specs=[pl.BlockSpec(memory_space=pltpu.SMEM),
                  blk3(D_SSM, D_STATE), blk3(1, D_SSM), blk3(1, SSM_GROUPS * D_STATE),
                  blk3(1, SSM_GROUPS * D_STATE)],
        out_specs=[blk3(D_SSM, D_STATE), blk3(1, D_SSM)],
        out_shape=[jax.ShapeDtypeStruct((bd, D_SSM, D_STATE), F32),
                   jax.ShapeDtypeStruct((bd, 1, D_SSM), F32)],
        compiler_params=_cparams(("parallel",)),
        name="ssm_sample",
    )(ea, s, xdt3, b3, c3)


def _mlstm_sample_kernel(g_ref, esw_ref, c_ref, q_ref, k_ref, v_ref, co_ref, qc_ref):
    tb = c_ref.shape[0]
    base = pl.program_id(0) * tb
    d = MLSTM_HEAD_DIM

    def body(b, carry):
        qrow = q_ref[b]
        krow = k_ref[b]
        vrow = v_ref[b]
        for h in range(MLSTM_HEADS):
            qcol = _column_tile(qrow[:, h * d:(h + 1) * d])
            kcol = _column_tile(krow[:, h * d:(h + 1) * d])
            ch = c_ref[b, h * d:(h + 1) * d, :]
            g_end = g_ref[base + b, GATE_LANE + h]
            wk = esw_ref[base + b, GATE_LANE + h]
            qc_ref[b, :, h * d:(h + 1) * d] = jnp.sum(qcol * ch, axis=0, keepdims=True)
            co_ref[b, h * d:(h + 1) * d, :] = ch * g_end + (kcol * wk) * vrow[:, h * d:(h + 1) * d]
        return carry

    lax.fori_loop(0, tb, body, 0)


def _mlstm_sample(gq, esw, c, q3, k3, v3, tb):
    bd = c.shape[0]
    blk3 = lambda r, n: pl.BlockSpec((tb, r, n), lambda i: (i, 0, 0))
    smem = pl.BlockSpec(memory_space=pltpu.SMEM)
    return pl.pallas_call(
        _mlstm_sample_kernel,
        grid=(bd // tb,),
        in_specs=[smem, smem, blk3(D_MLSTM, MLSTM_HEAD_DIM), blk3(1, D_MLSTM), blk3(1, D_MLSTM),
                  blk3(1, D_MLSTM)],
        out_specs=[blk3(D_MLSTM, MLSTM_HEAD_DIM), blk3(1, D_MLSTM)],
        out_shape=[jax.ShapeDtypeStruct((bd, D_MLSTM, MLSTM_HEAD_DIM), F32),
                   jax.ShapeDtypeStruct((bd, 1, D_MLSTM), F32)],
        compiler_params=_cparams(("parallel",)),
        name="mlstm_sample",
    )(gq, esw, c, q3, k3, v3)


def _sample_post_kernel(proj_ref, y_ref, xs_ref, dskip_ref, wns_ref, qc_ref, g_ref, sw_ref, den_ref, wnm_ref,
                        ys_ref, hs_ref):
    y = y_ref[...] + dskip_ref[...] * xs_ref[...]
    y = y * _silu(proj_ref[:, C_Z:C_Z + D_SSM])
    ys_ref[...] = _group_rmsnorm(y, wns_ref[...], SSM_GROUPS).astype(ys_ref.dtype)

    d = MLSTM_HEAD_DIM
    for h in range(MLSTM_HEADS):
        ln = GATE_LANE + h
        vh = proj_ref[:, C_MV + h * d:C_MV + (h + 1) * d]
        num = sw_ref[:, ln:ln + 1] * vh + qc_ref[:, h * d:(h + 1) * d] * g_ref[:, ln:ln + 1]
        hh = num / den_ref[:, ln:ln + 1]
        ms = jnp.mean(hh * hh, axis=-1, keepdims=True)
        hn = hh * lax.rsqrt(ms + EPS) * wnm_ref[:, h * d:(h + 1) * d]
        gate = jax.nn.sigmoid(proj_ref[:, C_MO + h * d:C_MO + (h + 1) * d])
        hs_ref[:, h * d:(h + 1) * d] = (hn * gate).astype(hs_ref.dtype)


def _sample_post(proj, y, xs, dskip_rep, wns, qc, gq, sw, den, wnm):
    bd = proj.shape[0]
    return pl.pallas_call(
        _sample_post_kernel,
        out_shape=[jax.ShapeDtypeStruct((bd, D_SSM), BF16), jax.ShapeDtypeStruct((bd, D_MLSTM), BF16)],
        compiler_params=pltpu.CompilerParams(vmem_limit_bytes=VMEM_LIMIT),
        name="sample_post",
    )(proj, y, xs, dskip_rep, wns, qc, gq, sw, den, wnm)


def _split_points():
    widths = (D_ATT, D_KV, D_KV, D_SSM, CONV_DIM, SSM_HEADS, D_MLSTM, D_MLSTM, D_MLSTM, D_MLSTM,
              MLSTM_HEADS, MLSTM_HEADS)
    pts, acc = [], 0
    for wd in widths[:-1]:
        acc += wd
        pts.append(acc)
    return pts


def _prep_w_in(w):
    q, k, v, z, xbc, dt, mq, mk, mv, mo, mi, mf = jnp.split(w, _split_points(), axis=1)
    zeros = lambda n: jnp.zeros((w.shape[0], n), w.dtype)
    tail = LANE - GATE_LANE - MLSTM_HEADS
    gates = jnp.concatenate([dt, mi, zeros(tail), zeros(GATE_LANE), mf, zeros(tail)], axis=1)
    return jnp.concatenate([xbc, q, z, mq, mk, mv, mo, k, v, gates], axis=1).astype(BF16)


def _lane_vec(v, lane0):
    return jnp.pad(v.astype(F32), (lane0, LANE - lane0 - v.shape[0]))[None, :]


def _pick(n, candidates):
    for c in candidates:
        if n % c == 0:
            return c
    return n


def kernel(x_prompt, x_sample, cache_swa_k, cache_swa_v, state_conv, state_ssm, state_mlstm_C, state_mlstm_n,
           state_mlstm_m, w_norm_mix, w_in, attn_sinks, conv_w, conv_b, dt_bias, a_log, d_skip, w_norm_ssm,
           igate_b, fgate_b, w_norm_mlstm, w_out, w_norm_mlp, w_up, w_down, w_norm_final):
    bsz, seq, d_model = x_prompt.shape
    bd = x_sample.shape[0]
    depth = w_in.shape[0]
    win = cache_swa_k.shape[2]
    assert x_sample.shape[1] == 1 and seq % CHUNK == 0 and d_model == D_MODEL

    mp = bsz * seq
    tm_p = _pick(mp, (1024, 512, 256, 128))
    tm_mlp = _pick(mp, (512, 256, 128))
    tb = _pick(bd, (8,))

    hp = x_prompt.reshape(mp, d_model)
    hs = x_sample.reshape(bd, d_model)
    tab_p = _rope_tables(jnp.arange(seq, dtype=jnp.int32))
    tab_s = _rope_tables(jnp.arange(1, dtype=jnp.int32) + PAST_LEN)
    gf = w_norm_final[None, :]

    st_p, st_s = [], []
    for l in range(depth):
        w_in_l = _prep_w_in(w_in[l])
        w_out_l = w_out[l].astype(BF16)
        w_up_l = w_up[l].astype(BF16)
        w_down_l = w_down[l].astype(BF16)
        g_mix = w_norm_mix[l][None, :]
        g_mlp = w_norm_mlp[l][None, :]
        cw = conv_w[l]
        cb = conv_b[l][None, :]
        dtb = _lane_vec(dt_bias[l], 0)
        alog = _lane_vec(a_log[l], 0)
        dskip_rep = jnp.repeat(d_skip[l].astype(F32), SSM_HEAD_DIM)[None, :]
        wns = w_norm_ssm[l][None, :]
        bi = _lane_vec(igate_b[l], GATE_LANE)
        bf = _lane_vec(fgate_b[l], GATE_LANE)
        wnm = w_norm_mlstm[l][None, :]
        sinks = attn_sinks[l].reshape(ATT_HEADS).astype(F32)
        last = l == depth - 1

        proj = _rms_matmul(hp, g_mix, w_in_l, tm_p, 512)
        att, pk, pv = _attn_prompt(proj, sinks, tab_p, bsz, seq)
        y, p_ssm = _ssd_prompt(proj, cw, cb, dtb, alog, dskip_rep, wns, bsz, seq)
        hm, p_c, p_n, p_m = _mlstm_prompt(proj, bi, bf, wnm, bsz, seq)
        x1 = _outproj(att, y, hm, hp, w_out_l, tm_p, 512)
        hp = _mlp(x1, g_mlp, w_up_l, w_down_l, gf, last, tm_mlp, 512)
        p_conv = proj.reshape(bsz, seq, NP)[:, seq - (CONV_WIDTH - 1):, C_XBC:C_XBC + CONV_DIM]
        st_p.append((pk.reshape(bsz, WINDOW, ATT_KV_HEADS, ATT_HEAD_DIM),
                     pv.reshape(bsz, WINDOW, ATT_KV_HEADS, ATT_HEAD_DIM),
                     p_conv,
                     p_ssm.reshape(bsz, SSM_HEADS, SSM_HEAD_DIM, D_STATE),
                     p_c.reshape(bsz, MLSTM_HEADS, MLSTM_HEAD_DIM, MLSTM_HEAD_DIM),
                     p_n[:, 0:MLSTM_HEADS, :],
                     p_m[:, 0:MLSTM_HEADS, 0]))

        proj_s = _rms_matmul(hs, g_mix, w_in_l, bd, 512)
        q8 = proj_s[:, C_Q:C_Q + D_ATT].reshape(bd, ATT_HEADS, ATT_HEAD_DIM)
        zero = jnp.zeros_like(q8)
        first_kv = (jnp.arange(ATT_HEADS) < ATT_GROUP)[None, :, None]
        qb = jnp.where(first_kv, jnp.concatenate([q8, zero], -1), jnp.concatenate([zero, q8], -1))
        kn = proj_s[:, C_K:C_K + D_KV].reshape(bd, 1, D_KV)
        vn = proj_s[:, C_V:C_V + D_KV].reshape(bd, 1, D_KV)
        sink_rows = jnp.broadcast_to(sinks[:, None], (ATT_HEADS, LANE))
        att_s, sk, sv = _attn_sample(qb, kn, vn, cache_swa_k[l].reshape(bd, win, D_KV),
                                     cache_swa_v[l].reshape(bd, win, D_KV), sink_rows, tab_s, tb)

        m_tile = jnp.pad(state_mlstm_m[l], ((0, 0), (GATE_LANE, LANE - GATE_LANE - MLSTM_HEADS)))
        (conv_new, xs, xdt, bm, cm, ea, ks, n_new, gq, esw, sw, den, mt) = _sample_pre(
            proj_s, state_conv[l].reshape(bd, (CONV_WIDTH - 1) * CONV_DIM), cw, cb, dtb, alog, bi, bf,
            state_mlstm_n[l].reshape(bd, D_MLSTM), m_tile)
        s_new, y3 = _ssm_sample(ea, state_ssm[l].reshape(bd, D_SSM, D_STATE), xdt.reshape(bd, 1, D_SSM),
                                bm.reshape(bd, 1, -1), cm.reshape(bd, 1, -1), tb)
        c_new, qc3 = _mlstm_sample(gq, esw, state_mlstm_C[l].reshape(bd, D_MLSTM, MLSTM_HEAD_DIM),
                                   proj_s[:, C_MQ:C_MQ + D_MLSTM].reshape(bd, 1, D_MLSTM),
                                   ks.reshape(bd, 1, D_MLSTM),
                                   proj_s[:, C_MV:C_MV + D_MLSTM].reshape(bd, 1, D_MLSTM), tb)
        ys, hms = _sample_post(proj_s, y3.reshape(bd, D_SSM), xs, dskip_rep, wns, qc3.reshape(bd, D_MLSTM),
                               gq, sw, den, wnm)
        x1s = _outproj(att_s.reshape(bd, D_ATT), ys, hms, hs, w_out_l, bd, 512)
        hs = _mlp(x1s, g_mlp, w_up_l, w_down_l, gf, last, bd, 512)
        st_s.append((sk.reshape(bd, win, ATT_KV_HEADS, ATT_HEAD_DIM),
                     sv.reshape(bd, win, ATT_KV_HEADS, ATT_HEAD_DIM),
                     conv_new.reshape(bd, CONV_WIDTH - 1, CONV_DIM),
                     s_new.reshape(bd, SSM_HEADS, SSM_HEAD_DIM, D_STATE),
                     c_new.reshape(bd, MLSTM_HEADS, MLSTM_HEAD_DIM, MLSTM_HEAD_DIM),
                     n_new.reshape(bd, MLSTM_HEADS, MLSTM_HEAD_DIM),
                     mt[:, GATE_LANE:GATE_LANE + MLSTM_HEADS]))

    y_prompt = hp.reshape(bsz, seq, d_model)
    y_sample = hs.reshape(bd, 1, d_model)
    p_out = [jnp.stack([s[i] for s in st_p]) for i in range(7)]
    s_out = [jnp.stack([s[i] for s in st_s]) for i in range(7)]
    return (y_prompt, y_sample, *p_out, *s_out)
```

```python
import functools
import math

import jax
import jax.numpy as jnp
from jax import lax
from jax.experimental import pallas as pl
from jax.experimental.pallas import tpu as pltpu

F32 = jnp.float32
BF16 = jnp.bfloat16
HIGHEST = lax.Precision.HIGHEST

D_MODEL = 2048
EPS = 1e-6
PAST_LEN = 8192
ATT_HEAD_DIM = 64
ATT_HEADS = 8
ATT_KV_HEADS = 2
ATT_GROUP = ATT_HEADS // ATT_KV_HEADS
D_ATT = ATT_HEADS * ATT_HEAD_DIM
D_KV = ATT_KV_HEADS * ATT_HEAD_DIM
WINDOW = 128
ROPE_THETA = 500000.0
ROPE_DIM = ATT_HEAD_DIM // 4
SSM_HEAD_DIM = 64
SSM_HEADS = 16
D_SSM = SSM_HEADS * SSM_HEAD_DIM
SSM_GROUPS = 2
D_STATE = 128
CONV_WIDTH = 4
CONV_DIM = D_SSM + 2 * SSM_GROUPS * D_STATE
CHUNK = 128
MLSTM_HEADS = 4
MLSTM_HEAD_DIM = 128
D_MLSTM = MLSTM_HEADS * MLSTM_HEAD_DIM
D_FF = 4 * D_MODEL

C_XBC = 0
C_Q = 1536
C_Z = 2048
C_MQ = 3072
C_MK = 3584
C_MV = 4096
C_MO = 4608
C_K = 5120
C_V = 5248
C_G = 5376
NP = 5632
GATE_LANE = 16

LANE = 128
VMEM_LIMIT = 56 * 1024 * 1024


def _cparams(sem):
    return pltpu.CompilerParams(dimension_semantics=sem, vmem_limit_bytes=VMEM_LIMIT)


def _silu(x):
    return x * jax.nn.sigmoid(x)


def _dot(a, b):
    return jnp.dot(a.astype(BF16), b.astype(BF16), preferred_element_type=F32)


def _dot_nt(a, b):
    return lax.dot_general(a.astype(BF16), b.astype(BF16), (((1,), (1,)), ((), ())),
                           preferred_element_type=F32)


def _dot_tn(a, b):
    return lax.dot_general(a.astype(BF16), b.astype(BF16), (((0,), (0,)), ((), ())),
                           preferred_element_type=F32)


def _tril(n):
    row = lax.broadcasted_iota(jnp.int32, (n, n), 0)
    col = lax.broadcasted_iota(jnp.int32, (n, n), 1)
    return row >= col


def _cumsum_rows(x):
    t = _tril(x.shape[0]).astype(F32)
    return jnp.dot(t, x, precision=HIGHEST, preferred_element_type=F32)


def _rms_matmul_kernel(x_ref, g_ref, w_ref, o_ref, u_ref):
    @pl.when(pl.program_id(1) == 0)
    def _():
        x = x_ref[...]
        ms = jnp.mean(x * x, axis=-1, keepdims=True)
        u_ref[...] = (x * lax.rsqrt(ms + EPS) * g_ref[...]).astype(BF16)

    o_ref[...] = jnp.dot(u_ref[...], w_ref[...], preferred_element_type=F32)


def _rms_matmul(x, g, w, tm, tn):
    m, k = x.shape
    n = w.shape[1]
    return pl.pallas_call(
        _rms_matmul_kernel,
        grid=(m // tm, n // tn),
        in_specs=[pl.BlockSpec((tm, k), lambda i, j: (i, 0)),
                  pl.BlockSpec((1, k), lambda i, j: (0, 0)),
                  pl.BlockSpec((k, tn), lambda i, j: (0, j))],
        out_specs=pl.BlockSpec((tm, tn), lambda i, j: (i, j)),
        out_shape=jax.ShapeDtypeStruct((m, n), F32),
        scratch_shapes=[pltpu.VMEM((tm, k), BF16)],
        compiler_params=_cparams(("parallel", "arbitrary")),
        name="rms_inproj",
    )(x, g, w)


_SRC_Q, _SRC_K, _SRC_V, _SRC_Z, _SRC_XBC, _SRC_DT, _SRC_MQ = 0, 512, 640, 768, 1792, 3328, 3344
_SRC_MI = _SRC_MQ + 4 * D_MLSTM
IN_WIDTH = _SRC_MI + 2 * MLSTM_HEADS


def _prep_w_in_kernel(w_ref, o_ref):
    def put(dst, src, n):
        o_ref[:, dst:dst + n] = w_ref[:, src:src + n].astype(BF16)

    put(C_XBC, _SRC_XBC, CONV_DIM)
    put(C_Q, _SRC_Q, D_ATT)
    put(C_Z, _SRC_Z, D_SSM)
    put(C_MQ, _SRC_MQ, 4 * D_MLSTM)
    put(C_K, _SRC_K, D_KV)
    put(C_V, _SRC_V, D_KV)
    rows = w_ref.shape[0]
    lane = lax.broadcasted_iota(jnp.int32, (rows, LANE), 1)
    gate = (lane >= GATE_LANE) & (lane < GATE_LANE + MLSTM_HEADS)
    head = w_ref[:, _SRC_DT:_SRC_DT + LANE]
    tail = w_ref[:, IN_WIDTH - LANE:IN_WIDTH]
    mi = pltpu.roll(tail, (GATE_LANE - (LANE - 2 * MLSTM_HEADS)) % LANE, 1)
    mf = pltpu.roll(tail, (GATE_LANE - (LANE - MLSTM_HEADS)) % LANE, 1)
    lo = jnp.where(lane < SSM_HEADS, head, jnp.where(gate, mi, 0.0))
    hi = jnp.where(gate, mf, 0.0)
    o_ref[:, C_G:C_G + LANE] = lo.astype(BF16)
    o_ref[:, C_G + LANE:C_G + 2 * LANE] = hi.astype(BF16)


def _prep_w_in(w_in, layer, tr):
    k = w_in.shape[1]
    return pl.pallas_call(
        _prep_w_in_kernel,
        grid=(k // tr,),
        in_specs=[pl.BlockSpec((None, tr, IN_WIDTH), lambda i: (layer, i, 0))],
        out_specs=pl.BlockSpec((tr, NP), lambda i: (i, 0)),
        out_shape=jax.ShapeDtypeStruct((k, NP), BF16),
        compiler_params=_cparams(("parallel",)),
        name="prep_w_in",
    )(w_in)


def _outproj_kernel(a_ref, y_ref, h_ref, x_ref, w_ref, o_ref):
    acc = _dot(a_ref[...], w_ref[0:D_ATT, :])
    acc = acc + _dot(y_ref[...], w_ref[D_ATT:D_ATT + D_SSM, :])
    acc = acc + _dot(h_ref[...], w_ref[D_ATT + D_SSM:, :])
    o_ref[...] = x_ref[...] + acc


def _outproj(att, y, h, x, w, layer, tm, tn):
    m = x.shape[0]
    n = w.shape[2]
    return pl.pallas_call(
        _outproj_kernel,
        grid=(m // tm, n // tn),
        in_specs=[pl.BlockSpec((tm, D_ATT), lambda i, j: (i, 0)),
                  pl.BlockSpec((tm, D_SSM), lambda i, j: (i, 0)),
                  pl.BlockSpec((tm, D_MLSTM), lambda i, j: (i, 0)),
                  pl.BlockSpec((tm, tn), lambda i, j: (i, j)),
                  pl.BlockSpec((None, w.shape[1], tn), lambda i, j: (layer, 0, j))],
        out_specs=pl.BlockSpec((tm, tn), lambda i, j: (i, j)),
        out_shape=jax.ShapeDtypeStruct((m, n), F32),
        compiler_params=_cparams(("parallel", "arbitrary")),
        name="outproj",
    )(att, y, h, x, w)


def _mlp_kernel(x_ref, g_ref, wu_ref, wd_ref, gf_ref, o_ref, u_ref, *, final_norm):
    j = pl.program_id(1)

    @pl.when(j == 0)
    def _():
        x = x_ref[...]
        ms = jnp.mean(x * x, axis=-1, keepdims=True)
        u_ref[...] = (x * lax.rsqrt(ms + EPS) * g_ref[...]).astype(BF16)
        o_ref[...] = x

    h = _dot(u_ref[...], wu_ref[...])
    h = jnp.square(jnp.maximum(h, 0.0))
    o_ref[...] += _dot(h, wd_ref[...])

    if final_norm:
        @pl.when(j == pl.num_programs(1) - 1)
        def _():
            y = o_ref[...]
            ms = jnp.mean(y * y, axis=-1, keepdims=True)
            o_ref[...] = y * lax.rsqrt(ms + EPS) * gf_ref[...]


def _mlp(x, g, wu, wd, gf, layer, final_norm, tm, tf):
    m, d = x.shape
    ff = wu.shape[2]
    return pl.pallas_call(
        functools.partial(_mlp_kernel, final_norm=final_norm),
        grid=(m // tm, ff // tf),
        in_specs=[pl.BlockSpec((tm, d), lambda i, j: (i, 0), pipeline_mode=pl.Buffered(1)),
                  pl.BlockSpec((1, d), lambda i, j: (0, 0)),
                  pl.BlockSpec((None, d, tf), lambda i, j: (layer, 0, j)),
                  pl.BlockSpec((None, tf, d), lambda i, j: (layer, j, 0)),
                  pl.BlockSpec((1, d), lambda i, j: (0, 0))],
        out_specs=pl.BlockSpec((tm, d), lambda i, j: (i, 0)),
        out_shape=jax.ShapeDtypeStruct((m, d), F32),
        scratch_shapes=[pltpu.VMEM((tm, d), BF16)],
        compiler_params=_cparams(("parallel", "arbitrary")),
        name="mlp",
    )(x, g, wu, wd, gf)


def _rope_tables(pos):
    half = ROPE_DIM // 2
    inv = jnp.power(jnp.float32(ROPE_THETA), -jnp.arange(half, dtype=jnp.float32) / half)
    ang = pos.astype(jnp.float32)[:, None] * inv[None, :]
    cos = jnp.cos(ang)
    sin = jnp.sin(ang)
    n = pos.shape[0]
    rest = ATT_HEAD_DIM - ROPE_DIM
    c = jnp.concatenate([cos, cos, jnp.ones((n, rest), F32)], axis=1)
    s1 = jnp.concatenate([-sin, jnp.zeros((n, half + rest), F32)], axis=1)
    s2 = jnp.concatenate([jnp.zeros((n, half), F32), sin, jnp.zeros((n, rest), F32)], axis=1)
    rep = LANE // ATT_HEAD_DIM
    return jnp.tile(c, (1, rep)), jnp.tile(s1, (1, rep)), jnp.tile(s2, (1, rep))


def _rope(x, c, s1, s2):
    width = x.shape[-1]
    rep = width // LANE
    half = ROPE_DIM // 2
    if rep > 1:
        c = jnp.concatenate([c] * rep, axis=-1)
        s1 = jnp.concatenate([s1] * rep, axis=-1)
        s2 = jnp.concatenate([s2] * rep, axis=-1)
    axis = x.ndim - 1
    return x * c + pltpu.roll(x, width - half, axis) * s1 + pltpu.roll(x, half, axis) * s2


def _attn_prompt_kernel(sink_ref, q_ref, k_ref, v_ref, c_ref, s1_ref, s2_ref,
                        o_ref, ko_ref, vo_ref, kk_ref, vv_ref):
    j = pl.program_id(1)
    w = WINDOW

    @pl.when(j == 0)
    def _():
        kk_ref[0:w, :] = jnp.zeros((w, D_KV), F32)
        vv_ref[0:w, :] = jnp.zeros((w, D_KV), F32)

    c = c_ref[...]
    s1 = s1_ref[...]
    s2 = s2_ref[...]
    krot = _rope(k_ref[...], c, s1, s2)
    v = v_ref[...]
    kk_ref[w:2 * w, :] = krot
    vv_ref[w:2 * w, :] = v
    qrot = _rope(q_ref[...], c, s1, s2)
    kk = kk_ref[...].astype(BF16)
    vv = vv_ref[...].astype(BF16)

    row = lax.broadcasted_iota(jnp.int32, (w, 2 * w), 0)
    col = lax.broadcasted_iota(jnp.int32, (w, 2 * w), 1)
    first_col = jnp.where(j > 0, 0, w)
    mask = (col >= row) & (col <= row + w) & (col >= first_col)
    scale = ATT_HEAD_DIM ** -0.5

    for h in range(ATT_KV_HEADS):
        kh = kk[:, h * ATT_HEAD_DIM:(h + 1) * ATT_HEAD_DIM]
        vh = vv[:, h * ATT_HEAD_DIM:(h + 1) * ATT_HEAD_DIM]
        for g in range(ATT_GROUP):
            hg = h * ATT_GROUP + g
            qh = qrot[:, hg * ATT_HEAD_DIM:(hg + 1) * ATT_HEAD_DIM]
            s = _dot_nt(qh, kh) * scale
            s = jnp.where(mask, s, -jnp.inf)
            sink = sink_ref[hg]
            m = jnp.maximum(jnp.max(s, axis=-1, keepdims=True), sink)
            p = jnp.exp(s - m)
            denom = jnp.sum(p, axis=-1, keepdims=True) + jnp.exp(sink - m)
            p = p / denom
            o = _dot(p, vh)
            o_ref[:, hg * ATT_HEAD_DIM:(hg + 1) * ATT_HEAD_DIM] = o.astype(o_ref.dtype)

    kk_ref[0:w, :] = krot
    vv_ref[0:w, :] = v

    @pl.when(j == pl.num_programs(1) - 1)
    def _():
        ko_ref[...] = krot
        vo_ref[...] = v


def _attn_prompt(proj, sinks, tables, bsz, seq):
    nb = seq // WINDOW
    w = WINDOW
    c, s1, s2 = tables
    row = lambda b, j: b * nb + j
    tab = pl.BlockSpec((w, LANE), lambda b, j: (j, 0))
    return pl.pallas_call(
        _attn_prompt_kernel,
        grid=(bsz, nb),
        in_specs=[pl.BlockSpec(memory_space=pltpu.SMEM),
                  pl.BlockSpec((w, D_ATT), lambda b, j: (row(b, j), C_Q // D_ATT)),
                  pl.BlockSpec((w, D_KV), lambda b, j: (row(b, j), C_K // D_KV)),
                  pl.BlockSpec((w, D_KV), lambda b, j: (row(b, j), C_V // D_KV)),
                  tab, tab, tab],
        out_specs=[pl.BlockSpec((w, D_ATT), lambda b, j: (row(b, j), 0)),
                   pl.BlockSpec((None, w, D_KV), lambda b, j: (b, 0, 0)),
                   pl.BlockSpec((None, w, D_KV), lambda b, j: (b, 0, 0))],
        out_shape=[jax.ShapeDtypeStruct((bsz * seq, D_ATT), BF16),
                   jax.ShapeDtypeStruct((bsz, w, D_KV), F32),
                   jax.ShapeDtypeStruct((bsz, w, D_KV), F32)],
        scratch_shapes=[pltpu.VMEM((2 * w, D_KV), F32), pltpu.VMEM((2 * w, D_KV), F32)],
        compiler_params=_cparams(("parallel", "arbitrary")),
        name="attn_prompt",
    )(sinks, proj, proj, proj, c, s1, s2)


def _group_rmsnorm(y, w, groups):
    width = y.shape[-1] // groups
    outs = []
    for g in range(groups):
        yg = y[:, g * width:(g + 1) * width]
        ms = jnp.mean(yg * yg, axis=-1, keepdims=True)
        outs.append(yg * lax.rsqrt(ms + EPS) * w[:, g * width:(g + 1) * width])
    return jnp.concatenate(outs, axis=-1)


def _ssd_prompt_kernel(xbc_ref, z_ref, g_ref, cw_ref, cb_ref, dtb_ref, alog_ref, dskip_ref, wn_ref,
                       y_ref, sfin_ref, xext_ref, s_ref, ybuf_ref, xw_ref):
    c = pl.program_id(1)
    q = CHUNK
    pad = 8
    hp = SSM_HEAD_DIM
    heads_per_group = SSM_HEADS // SSM_GROUPS
    gw = heads_per_group * hp

    @pl.when(c == 0)
    def _():
        xext_ref[0:pad, :] = jnp.zeros((pad, CONV_DIM), F32)
        s_ref[...] = jnp.zeros_like(s_ref)

    xext_ref[pad:pad + q, :] = xbc_ref[...]
    cw = cw_ref[...]
    acc = xext_ref[pad - 3:pad - 3 + q, :] * cw[0:1, :]
    acc = acc + xext_ref[pad - 2:pad - 2 + q, :] * cw[1:2, :]
    acc = acc + xext_ref[pad - 1:pad - 1 + q, :] * cw[2:3, :]
    acc = acc + xext_ref[pad:pad + q, :] * cw[3:4, :]
    xc = _silu(acc + cb_ref[...])
    xext_ref[0:pad, :] = xext_ref[q:q + pad, :]

    xs = xc[:, 0:D_SSM]
    bm = xc[:, D_SSM:D_SSM + SSM_GROUPS * D_STATE]
    cm = xc[:, D_SSM + SSM_GROUPS * D_STATE:]

    dt = jax.nn.softplus(g_ref[:, 0:LANE] + dtb_ref[...])
    a_neg = -jnp.exp(alog_ref[...])
    a_col = _cumsum_rows(dt * a_neg)
    a_row = a_col.T
    dt_row = dt.T
    ea_col = jnp.exp(a_col)
    wk_col = jnp.exp(a_col[q - 1:q, :] - a_col) * dt
    tril = _tril(q)

    for g in range(SSM_GROUPS):
        bg = bm[:, g * D_STATE:(g + 1) * D_STATE]
        cg = cm[:, g * D_STATE:(g + 1) * D_STATE]
        cb = _dot_nt(cg, bg)
        cs = _dot_nt(cg, s_ref[g * gw:(g + 1) * gw, :])
        for r in range(heads_per_group):
            h = g * heads_per_group + r
            seg = a_col[:, h:h + 1] - a_row[h:h + 1, :]
            wmat = jnp.exp(jnp.where(tril, seg, -jnp.inf)) * cb * dt_row[h:h + 1, :]
            xh = xs[:, h * hp:(h + 1) * hp]
            yh = _dot(wmat, xh)
            yh = yh + cs[:, r * hp:(r + 1) * hp] * ea_col[:, h:h + 1]
            ybuf_ref[:, h * hp:(h + 1) * hp] = yh
            xw_ref[:, r * hp:(r + 1) * hp] = xh * wk_col[:, h:h + 1]
        upd = _dot_tn(xw_ref[...], bg)
        for r in range(heads_per_group):
            h = g * heads_per_group + r
            decay = jnp.exp(a_row[h:h + 1, q - 1:q])
            s_ref[h * hp:(h + 1) * hp, :] = s_ref[h * hp:(h + 1) * hp, :] * decay + upd[r * hp:(r + 1) * hp, :]

    y = ybuf_ref[...] + dskip_ref[...] * xs
    y = y * _silu(z_ref[...])
    y_ref[...] = _group_rmsnorm(y, wn_ref[...], SSM_GROUPS).astype(y_ref.dtype)

    @pl.when(c == pl.num_programs(1) - 1)
    def _():
        sfin_ref[...] = s_ref[...]


def _ssd_prompt(proj, cw, cb, dtb, alog, dskip_rep, wn, bsz, seq):
    nc = seq // CHUNK
    q = CHUNK
    row = lambda b, c: b * nc + c
    vec = lambda n: pl.BlockSpec((1, n), lambda b, c: (0, 0))
    return pl.pallas_call(
        _ssd_prompt_kernel,
        grid=(bsz, nc),
        in_specs=[pl.BlockSpec((q, CONV_DIM), lambda b, c: (row(b, c), C_XBC // CONV_DIM)),
                  pl.BlockSpec((q, D_SSM), lambda b, c: (row(b, c), C_Z // D_SSM)),
                  pl.BlockSpec((q, 2 * LANE), lambda b, c: (row(b, c), C_G // (2 * LANE))),
                  pl.BlockSpec((CONV_WIDTH, CONV_DIM), lambda b, c: (0, 0)),
                  vec(CONV_DIM), vec(LANE), vec(LANE), vec(D_SSM), vec(D_SSM)],
        out_specs=[pl.BlockSpec((q, D_SSM), lambda b, c: (row(b, c), 0)),
                   pl.BlockSpec((None, D_SSM, D_STATE), lambda b, c: (b, 0, 0))],
        out_shape=[jax.ShapeDtypeStruct((bsz * seq, D_SSM), BF16),
                   jax.ShapeDtypeStruct((bsz, D_SSM, D_STATE), F32)],
        scratch_shapes=[pltpu.VMEM((q + 8, CONV_DIM), F32),
                        pltpu.VMEM((D_SSM, D_STATE), F32),
                        pltpu.VMEM((q, D_SSM), F32),
                        pltpu.VMEM((q, D_SSM // SSM_GROUPS), F32)],
        compiler_params=_cparams(("parallel", "arbitrary")),
        name="ssd_prompt",
    )(proj, proj, proj, cw, cb, dtb, alog, dskip_rep, wn)


def _mlstm_prompt_kernel(q_ref, k_ref, v_ref, o_ref, g_ref, bi_ref, bf_ref, wn_ref,
                         h_ref, cfin_ref, nfin_ref, mfin_ref, c_ref, n_ref, m_ref):
    c = pl.program_id(1)
    t = CHUNK
    d = MLSTM_HEAD_DIM

    @pl.when(c == 0)
    def _():
        c_ref[...] = jnp.zeros_like(c_ref)
        n_ref[...] = jnp.zeros_like(n_ref)
        m_ref[...] = jnp.zeros_like(m_ref)

    gi = g_ref[:, 0:LANE] + bi_ref[...]
    gf = g_ref[:, LANE:2 * LANE] + bf_ref[...]
    b_col = _cumsum_rows(jax.nn.log_sigmoid(gf))
    b_row = b_col.T
    i_row = gi.T
    tril = _tril(t)
    k_all = k_ref[...] * (d ** -0.5)

    for h in range(MLSTM_HEADS):
        ln = GATE_LANE + h
        bq = b_col[:, ln:ln + 1]
        logw = jnp.where(tril, bq - b_row[ln:ln + 1, :] + i_row[ln:ln + 1, :], -jnp.inf)
        m_prev = m_ref[h:h + 1, 0:1]
        log_inter = bq + m_prev
        mt = jnp.maximum(log_inter, jnp.max(logw, axis=-1, keepdims=True))
        qh = q_ref[:, h * d:(h + 1) * d]
        kh = k_all[:, h * d:(h + 1) * d]
        vh = v_ref[:, h * d:(h + 1) * d]
        ch = c_ref[h * d:(h + 1) * d, :]
        nh = n_ref[h:h + 1, :]
        sw = jnp.exp(logw - mt) * _dot_nt(qh, kh)
        gq = jnp.exp(log_inter - mt)
        num = _dot(sw, vh) + _dot(qh, ch) * gq
        den = jnp.sum(sw, axis=-1, keepdims=True) + jnp.sum(qh * nh, axis=-1, keepdims=True) * gq
        hh = num / jnp.maximum(jnp.abs(den), jnp.exp(-mt))

        m_new = mt[t - 1:t, :]
        b_last = b_col[t - 1:t, ln:ln + 1]
        wk = jnp.exp(b_last - bq + gi[:, ln:ln + 1] - m_new)
        g_end = jnp.exp(b_last + m_prev - m_new)
        kw = kh * wk
        c_ref[h * d:(h + 1) * d, :] = ch * g_end + _dot_tn(kw, vh)
        n_ref[h:h + 1, :] = nh * g_end + jnp.sum(kw, axis=0, keepdims=True)
        m_ref[h:h + 1, :] = jnp.broadcast_to(m_new, (1, LANE))

        ms = jnp.mean(hh * hh, axis=-1, keepdims=True)
        hn = hh * lax.rsqrt(ms + EPS) * wn_ref[:, h * d:(h + 1) * d]
        h_ref[:, h * d:(h + 1) * d] = (hn * jax.nn.sigmoid(o_ref[:, h * d:(h + 1) * d])).astype(h_ref.dtype)

    @pl.when(c == pl.num_programs(1) - 1)
    def _():
        cfin_ref[...] = c_ref[...]
        nfin_ref[...] = n_ref[...]
        mfin_ref[...] = m_ref[...]


def _mlstm_prompt(proj, bi, bf, wn, bsz, seq):
    nc = seq // CHUNK
    t = CHUNK
    row = lambda b, c: b * nc + c
    seg = lambda col: pl.BlockSpec((t, D_MLSTM), lambda b, c: (row(b, c), col // D_MLSTM))
    vec = lambda n: pl.BlockSpec((1, n), lambda b, c: (0, 0))
    return pl.pallas_call(
        _mlstm_prompt_kernel,
        grid=(bsz, nc),
        in_specs=[seg(C_MQ), seg(C_MK), seg(C_MV), seg(C_MO),
                  pl.BlockSpec((t, 2 * LANE), lambda b, c: (row(b, c), C_G // (2 * LANE))),
                  vec(LANE), vec(LANE), vec(D_MLSTM)],
        out_specs=[pl.BlockSpec((t, D_MLSTM), lambda b, c: (row(b, c), 0)),
                   pl.BlockSpec((None, D_MLSTM, MLSTM_HEAD_DIM), lambda b, c: (b, 0, 0)),
                   pl.BlockSpec((None, 8, LANE), lambda b, c: (b, 0, 0)),
                   pl.BlockSpec((None, 8, LANE), lambda b, c: (b, 0, 0))],
        out_shape=[jax.ShapeDtypeStruct((bsz * seq, D_MLSTM), BF16),
                   jax.ShapeDtypeStruct((bsz, D_MLSTM, MLSTM_HEAD_DIM), F32),
                   jax.ShapeDtypeStruct((bsz, 8, LANE), F32),
                   jax.ShapeDtypeStruct((bsz, 8, LANE), F32)],
        scratch_shapes=[pltpu.VMEM((D_MLSTM, MLSTM_HEAD_DIM), F32),
                        pltpu.VMEM((8, LANE), F32),
                        pltpu.VMEM((8, LANE), F32)],
        compiler_params=_cparams(("parallel", "arbitrary")),
        name="mlstm_prompt",
    )(proj, proj, proj, proj, proj, bi, bf, wn)


def _stacked_out(prev, depth, shape, block, layer):
    spec = pl.BlockSpec((None,) + block, lambda i: (layer,) + (i,) + (0,) * (len(block) - 1))
    out_shape = jax.ShapeDtypeStruct((depth,) + shape, F32)
    extra_in = [] if prev is None else [prev]
    extra_spec = [] if prev is None else [pl.BlockSpec(memory_space=pl.ANY)]
    return spec, out_shape, extra_in, extra_spec


def _drop_aliased(kernel_fn, n_in, n_prev):
    def wrapped(*refs):
        return kernel_fn(*refs[:n_in], *refs[n_in + n_prev:])
    return wrapped


def _attn_sample_kernel(q_ref, k_ref, v_ref, kc_ref, vc_ref, sink_ref, c_ref, s1_ref, s2_ref,
                        o_ref, ko_ref, vo_ref):
    w = kc_ref.shape[1]
    c = c_ref[...]
    s1 = s1_ref[...]
    s2 = s2_ref[...]
    qb = _rope(q_ref[...], c, s1, s2)
    kn = _rope(k_ref[...], c, s1, s2)
    vn = v_ref[...]
    kc = kc_ref[...]
    vc = vc_ref[...]
    ko_ref[:, 0:w - 1, :] = kc[:, 1:w, :]
    ko_ref[:, w - 1:w, :] = kn
    vo_ref[:, 0:w - 1, :] = vc[:, 1:w, :]
    vo_ref[:, w - 1:w, :] = vn

    scale = ATT_HEAD_DIM ** -0.5
    s = jnp.einsum("bhd,bkd->bhk", qb.astype(BF16), kc.astype(BF16),
                   preferred_element_type=F32) * scale
    sn = jnp.sum(qb * kn, axis=-1, keepdims=True) * scale
    sink = sink_ref[...][None, :, 0:1]
    m = jnp.maximum(jnp.maximum(jnp.max(s, axis=-1, keepdims=True), sn), sink)
    p = jnp.exp(s - m)
    pn = jnp.exp(sn - m)
    denom = jnp.sum(p, axis=-1, keepdims=True) + pn + jnp.exp(sink - m)
    p = p / denom
    pn = pn / denom
    o = jnp.einsum("bhk,bkd->bhd", p.astype(BF16), vc.astype(BF16), preferred_element_type=F32)
    o = o + pn * vn
    head = lax.broadcasted_iota(jnp.int32, o.shape[:2] + (ATT_HEAD_DIM,), 1)
    o_ref[...] = jnp.where(head < ATT_GROUP, o[:, :, 0:ATT_HEAD_DIM], o[:, :, ATT_HEAD_DIM:]).astype(o_ref.dtype)


def _attn_sample(qb, kn, vn, kc, vc, sink_rows, tables, layer, prev, tb):
    depth, bd, w, _ = kc.shape
    c, s1, s2 = tables
    blk3 = lambda r, n: pl.BlockSpec((tb, r, n), lambda i: (i, 0, 0))
    cache = pl.BlockSpec((None, tb, w, LANE), lambda i: (layer, i, 0, 0))
    vec = pl.BlockSpec((1, LANE), lambda i: (0, 0))
    prev_k, prev_v = (None, None) if prev is None else prev
    k_spec, k_shape, k_in, k_in_spec = _stacked_out(prev_k, depth, (bd, w, LANE), (tb, w, LANE), layer)
    v_spec, v_shape, v_in, v_in_spec = _stacked_out(prev_v, depth, (bd, w, LANE), (tb, w, LANE), layer)
    n_in = 9
    n_prev = len(k_in) + len(v_in)
    return pl.pallas_call(
        _drop_aliased(_attn_sample_kernel, n_in, n_prev),
        grid=(bd // tb,),
        in_specs=[blk3(ATT_HEADS, LANE), blk3(1, LANE), blk3(1, LANE), cache, cache,
                  pl.BlockSpec((ATT_HEADS, LANE), lambda i: (0, 0)), vec, vec, vec] + k_in_spec + v_in_spec,
        out_specs=[blk3(ATT_HEADS, ATT_HEAD_DIM), k_spec, v_spec],
        out_shape=[jax.ShapeDtypeStruct((bd, ATT_HEADS, ATT_HEAD_DIM), BF16), k_shape, v_shape],
        input_output_aliases={n_in + t: 1 + t for t in range(n_prev)},
        compiler_params=_cparams(("parallel",)),
        name="attn_sample",
    )(qb, kn, vn, kc, vc, sink_rows, c, s1, s2, *k_in, *v_in)


def _lane_place(cols, lane0):
    m = cols[0].shape[0]
    lane = lax.broadcasted_iota(jnp.int32, (m, LANE), 1)
    out = jnp.zeros((m, LANE), F32)
    for i, col in enumerate(cols):
        out = jnp.where(lane == lane0 + i, col, out)
    return out


def _sample_pre_kernel(proj_ref, cs_ref, cw_ref, cb_ref, dtb_ref, alog_ref, bi_ref, bf_ref, n_ref, m_ref,
                       conv_ref, xs_ref, xdt_ref, bm_ref, cm_ref, ea_ref,
                       ks_ref, nn_ref, g_ref, esw_ref, sw_ref, den_ref, mt_ref):
    xbc = proj_ref[:, C_XBC:C_XBC + CONV_DIM]
    s0 = cs_ref[:, 0:CONV_DIM]
    s1 = cs_ref[:, CONV_DIM:2 * CONV_DIM]
    s2 = cs_ref[:, 2 * CONV_DIM:3 * CONV_DIM]
    cw = cw_ref[...]
    acc = s0 * cw[0:1, :]
    acc = acc + s1 * cw[1:2, :]
    acc = acc + s2 * cw[2:3, :]
    acc = acc + xbc * cw[3:4, :]
    xc = _silu(acc + cb_ref[...])
    conv_ref[:, 0:CONV_DIM] = s1
    conv_ref[:, CONV_DIM:2 * CONV_DIM] = s2
    conv_ref[:, 2 * CONV_DIM:3 * CONV_DIM] = xbc

    xs = xc[:, 0:D_SSM]
    xs_ref[...] = xs
    bm_ref[...] = xc[:, D_SSM:D_SSM + SSM_GROUPS * D_STATE]
    cm_ref[...] = xc[:, D_SSM + SSM_GROUPS * D_STATE:]
    dt = jax.nn.softplus(proj_ref[:, C_G:C_G + LANE] + dtb_ref[...])
    ea_ref[...] = jnp.exp(dt * (-jnp.exp(alog_ref[...])))
    hp = SSM_HEAD_DIM
    for h in range(SSM_HEADS):
        xdt_ref[:, h * hp:(h + 1) * hp] = xs[:, h * hp:(h + 1) * hp] * dt[:, h:h + 1]

    d = MLSTM_HEAD_DIM
    gi = proj_ref[:, C_G:C_G + LANE] + bi_ref[...]
    gf = proj_ref[:, C_G + LANE:C_G + 2 * LANE] + bf_ref[...]
    log_inter = jax.nn.log_sigmoid(gf) + m_ref[...]
    mt = jnp.maximum(log_inter, gi)
    gq = jnp.exp(log_inter - mt)
    esw = jnp.exp(gi - mt)
    ks = proj_ref[:, C_MK:C_MK + D_MLSTM] * (d ** -0.5)
    ks_ref[...] = ks
    qk_cols = []
    qn_cols = []
    for h in range(MLSTM_HEADS):
        qh = proj_ref[:, C_MQ + h * d:C_MQ + (h + 1) * d]
        kh = ks[:, h * d:(h + 1) * d]
        nh = n_ref[:, h * d:(h + 1) * d]
        qk_cols.append(jnp.sum(qh * kh, axis=-1, keepdims=True))
        qn_cols.append(jnp.sum(qh * nh, axis=-1, keepdims=True))
        ln = GATE_LANE + h
        nn_ref[:, h * d:(h + 1) * d] = nh * gq[:, ln:ln + 1] + kh * esw[:, ln:ln + 1]
    sw = esw * _lane_place(qk_cols, GATE_LANE)
    den = sw + _lane_place(qn_cols, GATE_LANE) * gq
    g_ref[...] = gq
    esw_ref[...] = esw
    sw_ref[...] = sw
    den_ref[...] = jnp.maximum(jnp.abs(den), jnp.exp(-mt))
    mt_ref[...] = mt


def _sample_pre(proj, conv_state, cw, cb, dtb, alog, bi, bf, n_state, m_tile):
    bd = proj.shape[0]
    tile = jax.ShapeDtypeStruct((bd, LANE), F32)
    outs = [jax.ShapeDtypeStruct((bd, (CONV_WIDTH - 1) * CONV_DIM), F32),
            jax.ShapeDtypeStruct((bd, D_SSM), F32),
            jax.ShapeDtypeStruct((bd, D_SSM), F32),
            jax.ShapeDtypeStruct((bd, SSM_GROUPS * D_STATE), F32),
            jax.ShapeDtypeStruct((bd, SSM_GROUPS * D_STATE), F32),
            tile,
            jax.ShapeDtypeStruct((bd, D_MLSTM), F32),
            jax.ShapeDtypeStruct((bd, D_MLSTM), F32),
            tile, tile, tile, tile, tile]
    return pl.pallas_call(
        _sample_pre_kernel,
        out_shape=outs,
        compiler_params=pltpu.CompilerParams(vmem_limit_bytes=VMEM_LIMIT),
        name="sample_pre",
    )(proj, conv_state, cw, cb, dtb, alog, bi, bf, n_state, m_tile)


def _column_tile(row):
    return jnp.broadcast_to(row, (LANE, LANE)).T


def _ssm_sample_kernel(ea_ref, s_ref, xdt_ref, b_ref, c_ref, so_ref, y_ref):
    tb = s_ref.shape[0]
    base = pl.program_id(0) * tb
    hp = SSM_HEAD_DIM
    heads_per_tile = LANE // hp
    n_tiles = D_SSM // LANE
    tiles_per_group = n_tiles // SSM_GROUPS
    top = lax.broadcasted_iota(jnp.int32, (LANE, 1), 0) < hp

    def body(b, carry):
        xrow = xdt_ref[b]
        brow = b_ref[b]
        crow = c_ref[b]
        for t in range(n_tiles):
            g = t // tiles_per_group
            xcol = _column_tile(xrow[:, t * LANE:(t + 1) * LANE])
            bg = brow[:, g * D_STATE:(g + 1) * D_STATE]
            cg = jnp.broadcast_to(crow[:, g * D_STATE:(g + 1) * D_STATE], (8, D_STATE))
            e0 = ea_ref[base + b, heads_per_tile * t]
            e1 = ea_ref[base + b, heads_per_tile * t + 1]
            decay = jnp.where(top, e0, e1)
            sn = s_ref[b, t * LANE:(t + 1) * LANE, :] * decay + xcol * bg
            so_ref[b, t * LANE:(t + 1) * LANE, :] = sn
            y_ref[b, :, t * LANE:(t + 1) * LANE] = _dot_nt(cg, sn)[0:1, :]
        return carry

    lax.fori_loop(0, tb, body, 0)


def _ssm_sample(ea, s, xdt3, b3, c3, layer, prev, tb):
    depth, bd = s.shape[0], s.shape[1]
    blk3 = lambda r, n: pl.BlockSpec((tb, r, n), lambda i: (i, 0, 0))
    s_spec, s_shape, s_in, s_in_spec = _stacked_out(prev, depth, (bd, D_SSM, D_STATE), (tb, D_SSM, D_STATE), layer)
    n_in = 5
    return pl.pallas_call(
        _drop_aliased(_ssm_sample_kernel, n_in, len(s_in)),
        grid=(bd // tb,),
        in_specs=[pl.BlockSpec(memory_space=pltpu.SMEM),
                  pl.BlockSpec((None, tb, D_SSM, D_STATE), lambda i: (layer, i, 0, 0)),
                  blk3(1, D_SSM), blk3(1, SSM_GROUPS * D_STATE),
                  blk3(1, SSM_GROUPS * D_STATE)] + s_in_spec,
        out_specs=[s_spec, blk3(1, D_SSM)],
        out_shape=[s_shape, jax.ShapeDtypeStruct((bd, 1, D_SSM), F32)],
        input_output_aliases={n_in + t: t for t in range(len(s_in))},
        compiler_params=_cparams(("parallel",)),
        name="ssm_sample",
    )(ea, s, xdt3, b3, c3, *s_in)


def _mlstm_sample_kernel(g_ref, esw_ref, c_ref, q_ref, k_ref, v_ref, co_ref, qc_ref):
    tb = c_ref.shape[0]
    base = pl.program_id(0) * tb
    d = MLSTM_HEAD_DIM

    def body(b, carry):
        qrow = q_ref[b]
        krow = k_ref[b]
        vrow = v_ref[b]
        for h in range(MLSTM_HEADS):
            qcol = _column_tile(qrow[:, h * d:(h + 1) * d])
            kcol = _column_tile(krow[:, h * d:(h + 1) * d])
            ch = c_ref[b, h * d:(h + 1) * d, :]
            g_end = g_ref[base + b, GATE_LANE + h]
            wk = esw_ref[base + b, GATE_LANE + h]
            qc_ref[b, :, h * d:(h + 1) * d] = jnp.sum(qcol * ch, axis=0, keepdims=True)
            co_ref[b, h * d:(h + 1) * d, :] = ch * g_end + (kcol * wk) * vrow[:, h * d:(h + 1) * d]
        return carry

    lax.fori_loop(0, tb, body, 0)


def _mlstm_sample(gq, esw, c, q3, k3, v3, layer, prev, tb):
    depth, bd = c.shape[0], c.shape[1]
    blk3 = lambda r, n: pl.BlockSpec((tb, r, n), lambda i: (i, 0, 0))
    smem = pl.BlockSpec(memory_space=pltpu.SMEM)
    c_spec, c_shape, c_in, c_in_spec = _stacked_out(prev, depth, (bd, D_MLSTM, MLSTM_HEAD_DIM),
                                                    (tb, D_MLSTM, MLSTM_HEAD_DIM), layer)
    n_in = 6
    return pl.pallas_call(
        _drop_aliased(_mlstm_sample_kernel, n_in, len(c_in)),
        grid=(bd // tb,),
        in_specs=[smem, smem, pl.BlockSpec((None, tb, D_MLSTM, MLSTM_HEAD_DIM), lambda i: (layer, i, 0, 0)),
                  blk3(1, D_MLSTM), blk3(1, D_MLSTM), blk3(1, D_MLSTM)] + c_in_spec,
        out_specs=[c_spec, blk3(1, D_MLSTM)],
        out_shape=[c_shape, jax.ShapeDtypeStruct((bd, 1, D_MLSTM), F32)],
        input_output_aliases={n_in + t: t for t in range(len(c_in))},
        compiler_params=_cparams(("parallel",)),
        name="mlstm_sample",
    )(gq, esw, c, q3, k3, v3, *c_in)


def _sample_post_kernel(proj_ref, y_ref, xs_ref, dskip_ref, wns_ref, qc_ref, g_ref, sw_ref, den_ref, wnm_ref,
                        ys_ref, hs_ref):
    y = y_ref[...] + dskip_ref[...] * xs_ref[...]
    y = y * _silu(proj_ref[:, C_Z:C_Z + D_SSM])
    ys_ref[...] = _group_rmsnorm(y, wns_ref[...], SSM_GROUPS).astype(ys_ref.dtype)

    d = MLSTM_HEAD_DIM
    for h in range(MLSTM_HEADS):
        ln = GATE_LANE + h
        vh = proj_ref[:, C_MV + h * d:C_MV + (h + 1) * d]
        num = sw_ref[:, ln:ln + 1] * vh + qc_ref[:, h * d:(h + 1) * d] * g_ref[:, ln:ln + 1]
        hh = num / den_ref[:, ln:ln + 1]
        ms = jnp.mean(hh * hh, axis=-1, keepdims=True)
        hn = hh * lax.rsqrt(ms + EPS) * wnm_ref[:, h * d:(h + 1) * d]
        gate = jax.nn.sigmoid(proj_ref[:, C_MO + h * d:C_MO + (h + 1) * d])
        hs_ref[:, h * d:(h + 1) * d] = (hn * gate).astype(hs_ref.dtype)


def _sample_post(proj, y, xs, dskip_rep, wns, qc, gq, sw, den, wnm):
    bd = proj.shape[0]
    return pl.pallas_call(
        _sample_post_kernel,
        out_shape=[jax.ShapeDtypeStruct((bd, D_SSM), BF16), jax.ShapeDtypeStruct((bd, D_MLSTM), BF16)],
        compiler_params=pltpu.CompilerParams(vmem_limit_bytes=VMEM_LIMIT),
        name="sample_post",
    )(proj, y, xs, dskip_rep, wns, qc, gq, sw, den, wnm)


def _lane_vec(v, lane0):
    return jnp.pad(v.astype(F32), (lane0, LANE - lane0 - v.shape[0]))[None, :]


def _pick(n, candidates):
    for c in candidates:
        if n % c == 0:
            return c
    return n


def kernel(x_prompt, x_sample, cache_swa_k, cache_swa_v, state_conv, state_ssm, state_mlstm_C, state_mlstm_n,
           state_mlstm_m, w_norm_mix, w_in, attn_sinks, conv_w, conv_b, dt_bias, a_log, d_skip, w_norm_ssm,
           igate_b, fgate_b, w_norm_mlstm, w_out, w_norm_mlp, w_up, w_down, w_norm_final):
    bsz, seq, d_model = x_prompt.shape
    bd = x_sample.shape[0]
    depth = w_in.shape[0]
    win = cache_swa_k.shape[2]
    assert x_sample.shape[1] == 1 and seq % CHUNK == 0 and d_model == D_MODEL

    mp = bsz * seq
    tm_p = _pick(mp, (1024, 512, 256, 128))
    tm_mlp = _pick(mp, (1024, 512, 256, 128))
    tb = _pick(bd, (8,))

    hp = x_prompt.reshape(mp, d_model)
    hs = x_sample.reshape(bd, d_model)
    tab_p = _rope_tables(jnp.arange(seq, dtype=jnp.int32))
    tab_s = _rope_tables(jnp.arange(1, dtype=jnp.int32) + PAST_LEN)
    gf = w_norm_final[None, :]
    kc_all = cache_swa_k.reshape(depth, bd, win, D_KV)
    vc_all = cache_swa_v.reshape(depth, bd, win, D_KV)
    ssm_all = state_ssm.reshape(depth, bd, D_SSM, D_STATE)
    mem_all = state_mlstm_C.reshape(depth, bd, D_MLSTM, MLSTM_HEAD_DIM)

    st_p, st_s = [], []
    kv_new = s_new = c_new = None
    for l in range(depth):
        w_in_l = _prep_w_in(w_in, l, 256)
        g_mix = w_norm_mix[l][None, :]
        g_mlp = w_norm_mlp[l][None, :]
        cw = conv_w[l]
        cb = conv_b[l][None, :]
        dtb = _lane_vec(dt_bias[l], 0)
        alog = _lane_vec(a_log[l], 0)
        dskip_rep = jnp.repeat(d_skip[l].astype(F32), SSM_HEAD_DIM)[None, :]
        wns = w_norm_ssm[l][None, :]
        bi = _lane_vec(igate_b[l], GATE_LANE)
        bf = _lane_vec(fgate_b[l], GATE_LANE)
        wnm = w_norm_mlstm[l][None, :]
        sinks = attn_sinks[l].reshape(ATT_HEADS).astype(F32)
        last = l == depth - 1

        proj = _rms_matmul(hp, g_mix, w_in_l, tm_p, 512)
        att, pk, pv = _attn_prompt(proj, sinks, tab_p, bsz, seq)
        y, p_ssm = _ssd_prompt(proj, cw, cb, dtb, alog, dskip_rep, wns, bsz, seq)
        hm, p_c, p_n, p_m = _mlstm_prompt(proj, bi, bf, wnm, bsz, seq)
        x1 = _outproj(att, y, hm, hp, w_out, l, tm_p, 512)
        hp = _mlp(x1, g_mlp, w_up, w_down, gf, l, last, tm_mlp, 512)
        p_conv = proj.reshape(bsz, seq, NP)[:, seq - (CONV_WIDTH - 1):, C_XBC:C_XBC + CONV_DIM]
        st_p.append((pk.reshape(bsz, WINDOW, ATT_KV_HEADS, ATT_HEAD_DIM),
                     pv.reshape(bsz, WINDOW, ATT_KV_HEADS, ATT_HEAD_DIM),
                     p_conv,
                     p_ssm.reshape(bsz, SSM_HEADS, SSM_HEAD_DIM, D_STATE),
                     p_c.reshape(bsz, MLSTM_HEADS, MLSTM_HEAD_DIM, MLSTM_HEAD_DIM),
                     p_n[:, 0:MLSTM_HEADS, :],
                     p_m[:, 0:MLSTM_HEADS, 0]))

        proj_s = _rms_matmul(hs, g_mix, w_in_l, bd, 512)
        q8 = proj_s[:, C_Q:C_Q + D_ATT].reshape(bd, ATT_HEADS, ATT_HEAD_DIM)
        zero = jnp.zeros_like(q8)
        first_kv = (jnp.arange(ATT_HEADS) < ATT_GROUP)[None, :, None]
        qb = jnp.where(first_kv, jnp.concatenate([q8, zero], -1), jnp.concatenate([zero, q8], -1))
        kn = proj_s[:, C_K:C_K + D_KV].reshape(bd, 1, D_KV)
        vn = proj_s[:, C_V:C_V + D_KV].reshape(bd, 1, D_KV)
        sink_rows = jnp.broadcast_to(sinks[:, None], (ATT_HEADS, LANE))
        att_s, sk, sv = _attn_sample(qb, kn, vn, kc_all, vc_all, sink_rows, tab_s, l, kv_new, tb)
        kv_new = (sk, sv)

        m_tile = jnp.pad(state_mlstm_m[l], ((0, 0), (GATE_LANE, LANE - GATE_LANE - MLSTM_HEADS)))
        (conv_new, xs, xdt, bm, cm, ea, ks, n_new, gq, esw, sw, den, mt) = _sample_pre(
            proj_s, state_conv[l].reshape(bd, (CONV_WIDTH - 1) * CONV_DIM), cw, cb, dtb, alog, bi, bf,
            state_mlstm_n[l].reshape(bd, D_MLSTM), m_tile)
        s_new, y3 = _ssm_sample(ea, ssm_all, xdt.reshape(bd, 1, D_SSM), bm.reshape(bd, 1, -1),
                                cm.reshape(bd, 1, -1), l, s_new, tb)
        c_new, qc3 = _mlstm_sample(gq, esw, mem_all,
                                   proj_s[:, C_MQ:C_MQ + D_MLSTM].reshape(bd, 1, D_MLSTM),
                                   ks.reshape(bd, 1, D_MLSTM),
                                   proj_s[:, C_MV:C_MV + D_MLSTM].reshape(bd, 1, D_MLSTM), l, c_new, tb)
        ys, hms = _sample_post(proj_s, y3.reshape(bd, D_SSM), xs, dskip_rep, wns, qc3.reshape(bd, D_MLSTM),
                               gq, sw, den, wnm)
        x1s = _outproj(att_s.reshape(bd, D_ATT), ys, hms, hs, w_out, l, bd, 512)
        hs = _mlp(x1s, g_mlp, w_up, w_down, gf, l, last, bd, 512)
        st_s.append((conv_new.reshape(bd, CONV_WIDTH - 1, CONV_DIM),
                     n_new.reshape(bd, MLSTM_HEADS, MLSTM_HEAD_DIM),
                     mt[:, GATE_LANE:GATE_LANE + MLSTM_HEADS]))

    y_prompt = hp.reshape(bsz, seq, d_model)
    y_sample = hs.reshape(bd, 1, d_model)
    p_out = [jnp.stack([s[i] for s in st_p]) for i in range(7)]
    s_conv, s_n, s_m = [jnp.stack([s[i] for s in st_s]) for i in range(3)]
    s_k = kv_new[0].reshape(depth, bd, win, ATT_KV_HEADS, ATT_HEAD_DIM)
    s_v = kv_new[1].reshape(depth, bd, win, ATT_KV_HEADS, ATT_HEAD_DIM)
    s_ssm = s_new.reshape(depth, bd, SSM_HEADS, SSM_HEAD_DIM, D_STATE)
    s_c = c_new.reshape(depth, bd, MLSTM_HEADS, MLSTM_HEAD_DIM, MLSTM_HEAD_DIM)
    return (y_prompt, y_sample, *p_out, s_k, s_v, s_conv, s_ssm, s_c, s_n, s_m)
```

```python
import functools
import math

import jax
import jax.numpy as jnp
from jax import lax
from jax.experimental import pallas as pl
from jax.experimental.pallas import tpu as pltpu

F32 = jnp.float32
BF16 = jnp.bfloat16
HIGHEST = lax.Precision.HIGHEST

D_MODEL = 2048
EPS = 1e-6
PAST_LEN = 8192
ATT_HEAD_DIM = 64
ATT_HEADS = 8
ATT_KV_HEADS = 2
ATT_GROUP = ATT_HEADS // ATT_KV_HEADS
D_ATT = ATT_HEADS * ATT_HEAD_DIM
D_KV = ATT_KV_HEADS * ATT_HEAD_DIM
WINDOW = 128
ROPE_THETA = 500000.0
ROPE_DIM = ATT_HEAD_DIM // 4
SSM_HEAD_DIM = 64
SSM_HEADS = 16
D_SSM = SSM_HEADS * SSM_HEAD_DIM
SSM_GROUPS = 2
D_STATE = 128
CONV_WIDTH = 4
CONV_DIM = D_SSM + 2 * SSM_GROUPS * D_STATE
CHUNK = 128
MLSTM_HEADS = 4
MLSTM_HEAD_DIM = 128
D_MLSTM = MLSTM_HEADS * MLSTM_HEAD_DIM
D_FF = 4 * D_MODEL

C_XBC = 0
C_Q = 1536
C_Z = 2048
C_MQ = 3072
C_MK = 3584
C_MV = 4096
C_MO = 4608
C_K = 5120
C_V = 5248
C_G = 5376
NP = 5632
GATE_LANE = 16

LANE = 128
VMEM_LIMIT = 56 * 1024 * 1024


def _cparams(sem):
    return pltpu.CompilerParams(dimension_semantics=sem, vmem_limit_bytes=VMEM_LIMIT)


def _silu(x):
    return x * jax.nn.sigmoid(x)


def _dot(a, b):
    return jnp.dot(a.astype(BF16), b.astype(BF16), preferred_element_type=F32)


def _dot_nt(a, b):
    return lax.dot_general(a.astype(BF16), b.astype(BF16), (((1,), (1,)), ((), ())),
                           preferred_element_type=F32)


def _dot_tn(a, b):
    return lax.dot_general(a.astype(BF16), b.astype(BF16), (((0,), (0,)), ((), ())),
                           preferred_element_type=F32)


def _tril(n):
    row = lax.broadcasted_iota(jnp.int32, (n, n), 0)
    col = lax.broadcasted_iota(jnp.int32, (n, n), 1)
    return row >= col


def _cumsum_rows(x):
    t = _tril(x.shape[0]).astype(F32)
    return jnp.dot(t, x, precision=HIGHEST, preferred_element_type=F32)


def _rms_matmul_kernel(x_ref, g_ref, w_ref, o_ref, u_ref):
    @pl.when(pl.program_id(1) == 0)
    def _():
        x = x_ref[...]
        ms = jnp.mean(x * x, axis=-1, keepdims=True)
        u_ref[...] = (x * lax.rsqrt(ms + EPS) * g_ref[...]).astype(BF16)

    o_ref[...] = jnp.dot(u_ref[...], w_ref[...], preferred_element_type=F32)


def _rms_matmul(x, g, w, tm, tn):
    m, k = x.shape
    n = w.shape[1]
    return pl.pallas_call(
        _rms_matmul_kernel,
        grid=(m // tm, n // tn),
        in_specs=[pl.BlockSpec((tm, k), lambda i, j: (i, 0)),
                  pl.BlockSpec((1, k), lambda i, j: (0, 0)),
                  pl.BlockSpec((k, tn), lambda i, j: (0, j))],
        out_specs=pl.BlockSpec((tm, tn), lambda i, j: (i, j)),
        out_shape=jax.ShapeDtypeStruct((m, n), F32),
        scratch_shapes=[pltpu.VMEM((tm, k), BF16)],
        compiler_params=_cparams(("parallel", "arbitrary")),
        name="rms_inproj",
    )(x, g, w)


_SRC_Q, _SRC_K, _SRC_V, _SRC_Z, _SRC_XBC, _SRC_DT, _SRC_MQ = 0, 512, 640, 768, 1792, 3328, 3344
_SRC_MI = _SRC_MQ + 4 * D_MLSTM
IN_WIDTH = _SRC_MI + 2 * MLSTM_HEADS


def _prep_w_in_kernel(w_ref, o_ref):
    def put(dst, src, n):
        o_ref[:, dst:dst + n] = w_ref[:, src:src + n].astype(BF16)

    put(C_XBC, _SRC_XBC, CONV_DIM)
    put(C_Q, _SRC_Q, D_ATT)
    put(C_Z, _SRC_Z, D_SSM)
    put(C_MQ, _SRC_MQ, 4 * D_MLSTM)
    put(C_K, _SRC_K, D_KV)
    put(C_V, _SRC_V, D_KV)
    rows = w_ref.shape[0]
    lane = lax.broadcasted_iota(jnp.int32, (rows, LANE), 1)
    gate = (lane >= GATE_LANE) & (lane < GATE_LANE + MLSTM_HEADS)
    head = w_ref[:, _SRC_DT:_SRC_DT + LANE]
    tail = w_ref[:, IN_WIDTH - LANE:IN_WIDTH]
    mi = pltpu.roll(tail, (GATE_LANE - (LANE - 2 * MLSTM_HEADS)) % LANE, 1)
    mf = pltpu.roll(tail, (GATE_LANE - (LANE - MLSTM_HEADS)) % LANE, 1)
    lo = jnp.where(lane < SSM_HEADS, head, jnp.where(gate, mi, 0.0))
    hi = jnp.where(gate, mf, 0.0)
    o_ref[:, C_G:C_G + LANE] = lo.astype(BF16)
    o_ref[:, C_G + LANE:C_G + 2 * LANE] = hi.astype(BF16)


def _prep_w_in(w_in, layer, tr):
    k = w_in.shape[1]
    return pl.pallas_call(
        _prep_w_in_kernel,
        grid=(k // tr,),
        in_specs=[pl.BlockSpec((None, tr, IN_WIDTH), lambda i: (layer, i, 0))],
        out_specs=pl.BlockSpec((tr, NP), lambda i: (i, 0)),
        out_shape=jax.ShapeDtypeStruct((k, NP), BF16),
        compiler_params=_cparams(("parallel",)),
        name="prep_w_in",
    )(w_in)


def _outproj_kernel(a_ref, y_ref, h_ref, x_ref, w_ref, o_ref, wb_ref):
    @pl.when(pl.program_id(0) == 0)
    def _():
        wb_ref[...] = w_ref[...].astype(BF16)

    acc = jnp.dot(a_ref[...], wb_ref[0:D_ATT, :], preferred_element_type=F32)
    acc = acc + jnp.dot(y_ref[...], wb_ref[D_ATT:D_ATT + D_SSM, :], preferred_element_type=F32)
    acc = acc + jnp.dot(h_ref[...], wb_ref[D_ATT + D_SSM:, :], preferred_element_type=F32)
    o_ref[...] = x_ref[...] + acc


def _outproj(att, y, h, x, w, layer, tm):
    m, n = x.shape
    k = w.shape[1]
    rows = lambda width: pl.BlockSpec((tm, width), lambda i: (i, 0))
    return pl.pallas_call(
        _outproj_kernel,
        grid=(m // tm,),
        in_specs=[rows(D_ATT), rows(D_SSM), rows(D_MLSTM), rows(n),
                  pl.BlockSpec((None, k, n), lambda i: (layer, 0, 0), pipeline_mode=pl.Buffered(1))],
        out_specs=rows(n),
        out_shape=jax.ShapeDtypeStruct((m, n), F32),
        scratch_shapes=[pltpu.VMEM((k, n), BF16)],
        compiler_params=_cparams(("arbitrary",)),
        name="outproj",
    )(att, y, h, x, w)


def _mlp_kernel(x_ref, g_ref, wu_ref, wd_ref, gf_ref, o_ref, u_ref, *, final_norm):
    j = pl.program_id(1)

    @pl.when(j == 0)
    def _():
        x = x_ref[...]
        ms = jnp.mean(x * x, axis=-1, keepdims=True)
        u_ref[...] = (x * lax.rsqrt(ms + EPS) * g_ref[...]).astype(BF16)
        o_ref[...] = x

    h = _dot(u_ref[...], wu_ref[...])
    h = jnp.square(jnp.maximum(h, 0.0))
    o_ref[...] += _dot(h, wd_ref[...])

    if final_norm:
        @pl.when(j == pl.num_programs(1) - 1)
        def _():
            y = o_ref[...]
            ms = jnp.mean(y * y, axis=-1, keepdims=True)
            o_ref[...] = y * lax.rsqrt(ms + EPS) * gf_ref[...]


def _mlp(x, g, wu, wd, gf, layer, final_norm, tm, tf):
    m, d = x.shape
    ff = wu.shape[2]
    return pl.pallas_call(
        functools.partial(_mlp_kernel, final_norm=final_norm),
        grid=(m // tm, ff // tf),
        in_specs=[pl.BlockSpec((tm, d), lambda i, j: (i, 0), pipeline_mode=pl.Buffered(1)),
                  pl.BlockSpec((1, d), lambda i, j: (0, 0)),
                  pl.BlockSpec((None, d, tf), lambda i, j: (layer, 0, j)),
                  pl.BlockSpec((None, tf, d), lambda i, j: (layer, j, 0)),
                  pl.BlockSpec((1, d), lambda i, j: (0, 0))],
        out_specs=pl.BlockSpec((tm, d), lambda i, j: (i, 0)),
        out_shape=jax.ShapeDtypeStruct((m, d), F32),
        scratch_shapes=[pltpu.VMEM((tm, d), BF16)],
        compiler_params=_cparams(("parallel", "arbitrary")),
        name="mlp",
    )(x, g, wu, wd, gf)


def _rope_tables(pos):
    half = ROPE_DIM // 2
    inv = jnp.power(jnp.float32(ROPE_THETA), -jnp.arange(half, dtype=jnp.float32) / half)
    ang = pos.astype(jnp.float32)[:, None] * inv[None, :]
    cos = jnp.cos(ang)
    sin = jnp.sin(ang)
    n = pos.shape[0]
    rest = ATT_HEAD_DIM - ROPE_DIM
    c = jnp.concatenate([cos, cos, jnp.ones((n, rest), F32)], axis=1)
    s1 = jnp.concatenate([-sin, jnp.zeros((n, half + rest), F32)], axis=1)
    s2 = jnp.concatenate([jnp.zeros((n, half), F32), sin, jnp.zeros((n, rest), F32)], axis=1)
    rep = LANE // ATT_HEAD_DIM
    return jnp.tile(c, (1, rep)), jnp.tile(s1, (1, rep)), jnp.tile(s2, (1, rep))


def _rope(x, c, s1, s2):
    width = x.shape[-1]
    rep = width // LANE
    half = ROPE_DIM // 2
    if rep > 1:
        c = jnp.concatenate([c] * rep, axis=-1)
        s1 = jnp.concatenate([s1] * rep, axis=-1)
        s2 = jnp.concatenate([s2] * rep, axis=-1)
    axis = x.ndim - 1
    return x * c + pltpu.roll(x, width - half, axis) * s1 + pltpu.roll(x, half, axis) * s2


def _attn_prompt_init(ko_ref, vo_ref):
    ko_ref[...] = jnp.zeros_like(ko_ref)
    vo_ref[...] = jnp.zeros_like(vo_ref)


def _attn_prompt_body(j, sink_ref, q_ref, k_ref, v_ref, c_ref, s1_ref, s2_ref, o_ref, ko_ref, vo_ref):
    w = WINDOW
    c = c_ref[...]
    s1 = s1_ref[...]
    s2 = s2_ref[...]
    krot = _rope(k_ref[...], c, s1, s2)
    v = v_ref[...]
    qrot = _rope(q_ref[...], c, s1, s2)
    kk = jnp.concatenate([ko_ref[...], krot], axis=0).astype(BF16)
    vv = jnp.concatenate([vo_ref[...], v], axis=0).astype(BF16)

    row = lax.broadcasted_iota(jnp.int32, (w, 2 * w), 0)
    col = lax.broadcasted_iota(jnp.int32, (w, 2 * w), 1)
    first_col = jnp.where(j > 0, 0, w)
    mask = (col >= row) & (col <= row + w) & (col >= first_col)
    scale = ATT_HEAD_DIM ** -0.5

    for h in range(ATT_KV_HEADS):
        kh = kk[:, h * ATT_HEAD_DIM:(h + 1) * ATT_HEAD_DIM]
        vh = vv[:, h * ATT_HEAD_DIM:(h + 1) * ATT_HEAD_DIM]
        for g in range(ATT_GROUP):
            hg = h * ATT_GROUP + g
            qh = qrot[:, hg * ATT_HEAD_DIM:(hg + 1) * ATT_HEAD_DIM]
            s = _dot_nt(qh, kh) * scale
            s = jnp.where(mask, s, -jnp.inf)
            sink = sink_ref[hg]
            m = jnp.maximum(jnp.max(s, axis=-1, keepdims=True), sink)
            p = jnp.exp(s - m)
            denom = jnp.sum(p, axis=-1, keepdims=True) + jnp.exp(sink - m)
            p = p / denom
            o = _dot(p, vh)
            o_ref[:, hg * ATT_HEAD_DIM:(hg + 1) * ATT_HEAD_DIM] = o.astype(o_ref.dtype)
            yield

    ko_ref[...] = krot
    vo_ref[...] = v


def _group_rmsnorm(y, w, groups):
    width = y.shape[-1] // groups
    outs = []
    for g in range(groups):
        yg = y[:, g * width:(g + 1) * width]
        ms = jnp.mean(yg * yg, axis=-1, keepdims=True)
        outs.append(yg * lax.rsqrt(ms + EPS) * w[:, g * width:(g + 1) * width])
    return jnp.concatenate(outs, axis=-1)


CONV_PAD = 8


def _ssd_prompt_init(s_ref, tail_ref):
    tail_ref[...] = jnp.zeros_like(tail_ref)
    s_ref[...] = jnp.zeros_like(s_ref)


def _ssd_prompt_body(xbc_ref, z_ref, g_ref, cw_ref, cb_ref, dtb_ref, alog_ref, dskip_ref, wn_ref,
                     y_ref, s_ref, tail_ref, ybuf_ref, xw_ref):
    q = CHUNK
    pad = CONV_PAD
    hp = SSM_HEAD_DIM
    heads_per_group = SSM_HEADS // SSM_GROUPS
    gw = heads_per_group * hp

    x = xbc_ref[...]
    cw = cw_ref[...]
    row8 = lax.broadcasted_iota(jnp.int32, (pad, CONV_DIM), 0)
    acc = None
    for j in range(CONV_WIDTH - 1):
        shift = CONV_WIDTH - 1 - j
        rolled = pltpu.roll(x, shift, 0)
        head = jnp.where(row8 < shift, tail_ref[j * pad:(j + 1) * pad, :], rolled[0:pad, :])
        tail_ref[j * pad:(j + 1) * pad, :] = rolled[0:pad, :]
        term = jnp.concatenate([head, rolled[pad:, :]], axis=0) * cw[j:j + 1, :]
        acc = term if acc is None else acc + term
    acc = acc + x * cw[CONV_WIDTH - 1:CONV_WIDTH, :]
    xc = _silu(acc + cb_ref[...])

    xs = xc[:, 0:D_SSM]
    bm = xc[:, D_SSM:D_SSM + SSM_GROUPS * D_STATE]
    cm = xc[:, D_SSM + SSM_GROUPS * D_STATE:]

    dt = jax.nn.softplus(g_ref[:, 0:LANE] + dtb_ref[...])
    a_neg = -jnp.exp(alog_ref[...])
    a_col = _cumsum_rows(dt * a_neg)
    a_row = a_col.T
    dt_row = dt.T
    ea_col = jnp.exp(a_col)
    wk_col = jnp.exp(a_col[q - 1:q, :] - a_col) * dt
    tril = _tril(q)
    yield

    for g in range(SSM_GROUPS):
        bg = bm[:, g * D_STATE:(g + 1) * D_STATE]
        cg = cm[:, g * D_STATE:(g + 1) * D_STATE]
        cb = _dot_nt(cg, bg)
        cs = _dot_nt(cg, s_ref[g * gw:(g + 1) * gw, :])
        for r in range(heads_per_group):
            h = g * heads_per_group + r
            seg = a_col[:, h:h + 1] - a_row[h:h + 1, :]
            wmat = jnp.exp(jnp.where(tril, seg, -jnp.inf)) * cb * dt_row[h:h + 1, :]
            xh = xs[:, h * hp:(h + 1) * hp]
            yh = _dot(wmat, xh)
            yh = yh + cs[:, r * hp:(r + 1) * hp] * ea_col[:, h:h + 1]
            ybuf_ref[:, h * hp:(h + 1) * hp] = yh
            xw_ref[:, r * hp:(r + 1) * hp] = xh * wk_col[:, h:h + 1]
            yield
        upd = _dot_tn(xw_ref[...], bg)
        for r in range(heads_per_group):
            h = g * heads_per_group + r
            decay = jnp.exp(a_row[h:h + 1, q - 1:q])
            s_ref[h * hp:(h + 1) * hp, :] = s_ref[h * hp:(h + 1) * hp, :] * decay + upd[r * hp:(r + 1) * hp, :]
        yield

    y = ybuf_ref[...] + dskip_ref[...] * xs
    y = y * _silu(z_ref[...])
    y_ref[...] = _group_rmsnorm(y, wn_ref[...], SSM_GROUPS).astype(y_ref.dtype)


def _mlstm_prompt_init(c_ref, n_ref, m_ref):
    c_ref[...] = jnp.zeros_like(c_ref)
    n_ref[...] = jnp.zeros_like(n_ref)
    m_ref[...] = jnp.zeros_like(m_ref)


def _mlstm_prompt_body(q_ref, k_ref, v_ref, o_ref, g_ref, bi_ref, bf_ref, wn_ref, h_ref, c_ref, n_ref, m_ref):
    t = CHUNK
    d = MLSTM_HEAD_DIM

    gi = g_ref[:, 0:LANE] + bi_ref[...]
    gf = g_ref[:, LANE:2 * LANE] + bf_ref[...]
    b_col = _cumsum_rows(jax.nn.log_sigmoid(gf))
    b_row = b_col.T
    i_row = gi.T
    tril = _tril(t)
    k_all = k_ref[...] * (d ** -0.5)

    for h in range(MLSTM_HEADS):
        ln = GATE_LANE + h
        bq = b_col[:, ln:ln + 1]
        logw = jnp.where(tril, bq - b_row[ln:ln + 1, :] + i_row[ln:ln + 1, :], -jnp.inf)
        m_prev = m_ref[h:h + 1, 0:1]
        log_inter = bq + m_prev
        mt = jnp.maximum(log_inter, jnp.max(logw, axis=-1, keepdims=True))
        qh = q_ref[:, h * d:(h + 1) * d]
        kh = k_all[:, h * d:(h + 1) * d]
        vh = v_ref[:, h * d:(h + 1) * d]
        ch = c_ref[h * d:(h + 1) * d, :]
        nh = n_ref[h:h + 1, :]
        sw = jnp.exp(logw - mt) * _dot_nt(qh, kh)
        gq = jnp.exp(log_inter - mt)
        num = _dot(sw, vh) + _dot(qh, ch) * gq
        den = jnp.sum(sw, axis=-1, keepdims=True) + jnp.sum(qh * nh, axis=-1, keepdims=True) * gq
        hh = num / jnp.maximum(jnp.abs(den), jnp.exp(-mt))

        m_new = mt[t - 1:t, :]
        b_last = b_col[t - 1:t, ln:ln + 1]
        wk = jnp.exp(b_last - bq + gi[:, ln:ln + 1] - m_new)
        g_end = jnp.exp(b_last + m_prev - m_new)
        kw = kh * wk
        c_ref[h * d:(h + 1) * d, :] = ch * g_end + _dot_tn(kw, vh)
        n_ref[h:h + 1, :] = nh * g_end + jnp.sum(kw, axis=0, keepdims=True)
        m_ref[h:h + 1, :] = jnp.broadcast_to(m_new, (1, LANE))

        ms = jnp.mean(hh * hh, axis=-1, keepdims=True)
        hn = hh * lax.rsqrt(ms + EPS) * wn_ref[:, h * d:(h + 1) * d]
        h_ref[:, h * d:(h + 1) * d] = (hn * jax.nn.sigmoid(o_ref[:, h * d:(h + 1) * d])).astype(h_ref.dtype)
        yield

N_ATT_IN, N_SSD_IN, N_MLSTM_IN = 7, 9, 7


def _interleave(stages):
    live = [[gen, 0, n] for gen, n in stages]
    while live:
        item = min(live, key=lambda s: s[1] / s[2])
        try:
            next(item[0])
            item[1] += 1
        except StopIteration:
            live.remove(item)


def _mixers_prompt_kernel(*refs):
    att_in = refs[0:N_ATT_IN]
    ssd_in = refs[N_ATT_IN:N_ATT_IN + N_SSD_IN]
    ml_in = refs[N_ATT_IN + N_SSD_IN:N_ATT_IN + N_SSD_IN + N_MLSTM_IN]
    outs = refs[N_ATT_IN + N_SSD_IN + N_MLSTM_IN:]
    att_ref, ko_ref, vo_ref, y_ref, s_ref, h_ref, c_ref, n_ref, m_ref, tail_ref, ybuf_ref, xw_ref = outs
    j = pl.program_id(1)

    @pl.when(j == 0)
    def _():
        _attn_prompt_init(ko_ref, vo_ref)
        _ssd_prompt_init(s_ref, tail_ref)
        _mlstm_prompt_init(c_ref, n_ref, m_ref)

    g_ref = ssd_in[2]
    _interleave([
        (_attn_prompt_body(j, *att_in, att_ref, ko_ref, vo_ref), ATT_HEADS + 1),
        (_ssd_prompt_body(*ssd_in, y_ref, s_ref, tail_ref, ybuf_ref, xw_ref), SSM_HEADS + SSM_GROUPS + 2),
        (_mlstm_prompt_body(*ml_in[0:4], g_ref, *ml_in[4:], h_ref, c_ref, n_ref, m_ref), MLSTM_HEADS + 1),
    ])


def _mixers_prompt(proj, sinks, tables, cw, cb, dtb, alog, dskip_rep, wns, bi, bf, wnm, bsz, seq):
    nb = seq // CHUNK
    t = CHUNK
    c, s1, s2 = tables
    row = lambda b, j: b * nb + j
    seg = lambda col, width: pl.BlockSpec((t, width), lambda b, j: (row(b, j), col // width))
    vec = lambda n: pl.BlockSpec((1, n), lambda b, j: (0, 0))
    tab = pl.BlockSpec((t, LANE), lambda b, j: (j, 0))
    rows_out = lambda width: pl.BlockSpec((t, width), lambda b, j: (row(b, j), 0))
    per_seq = lambda r, n: pl.BlockSpec((None, r, n), lambda b, j: (b, 0, 0))
    in_specs = (
        [pl.BlockSpec(memory_space=pltpu.SMEM), seg(C_Q, D_ATT), seg(C_K, D_KV), seg(C_V, D_KV), tab, tab, tab]
        + [seg(C_XBC, CONV_DIM), seg(C_Z, D_SSM), seg(C_G, 2 * LANE),
           pl.BlockSpec((CONV_WIDTH, CONV_DIM), lambda b, j: (0, 0)),
           vec(CONV_DIM), vec(LANE), vec(LANE), vec(D_SSM), vec(D_SSM)]
        + [seg(C_MQ, D_MLSTM), seg(C_MK, D_MLSTM), seg(C_MV, D_MLSTM), seg(C_MO, D_MLSTM),
           vec(LANE), vec(LANE), vec(D_MLSTM)])
    assert len(in_specs) == N_ATT_IN + N_SSD_IN + N_MLSTM_IN
    return pl.pallas_call(
        _mixers_prompt_kernel,
        grid=(bsz, nb),
        in_specs=in_specs,
        out_specs=[rows_out(D_ATT), per_seq(t, D_KV), per_seq(t, D_KV),
                   rows_out(D_SSM), per_seq(D_SSM, D_STATE),
                   rows_out(D_MLSTM), per_seq(D_MLSTM, MLSTM_HEAD_DIM), per_seq(8, LANE), per_seq(8, LANE)],
        out_shape=[jax.ShapeDtypeStruct((bsz * seq, D_ATT), BF16),
                   jax.ShapeDtypeStruct((bsz, t, D_KV), F32),
                   jax.ShapeDtypeStruct((bsz, t, D_KV), F32),
                   jax.ShapeDtypeStruct((bsz * seq, D_SSM), BF16),
                   jax.ShapeDtypeStruct((bsz, D_SSM, D_STATE), F32),
                   jax.ShapeDtypeStruct((bsz * seq, D_MLSTM), BF16),
                   jax.ShapeDtypeStruct((bsz, D_MLSTM, MLSTM_HEAD_DIM), F32),
                   jax.ShapeDtypeStruct((bsz, 8, LANE), F32),
                   jax.ShapeDtypeStruct((bsz, 8, LANE), F32)],
        scratch_shapes=[pltpu.VMEM(((CONV_WIDTH - 1) * CONV_PAD, CONV_DIM), F32),
                        pltpu.VMEM((t, D_SSM), F32),
                        pltpu.VMEM((t, D_SSM // SSM_GROUPS), F32)],
        compiler_params=_cparams(("parallel", "arbitrary")),
        name="mixers_prompt",
    )(sinks, proj, proj, proj, c, s1, s2,
      proj, proj, proj, cw, cb, dtb, alog, dskip_rep, wns,
      proj, proj, proj, proj, bi, bf, wnm)


def _stacked_out(prev, depth, shape, block, layer):
    spec = pl.BlockSpec((None,) + block, lambda i: (layer,) + (i,) + (0,) * (len(block) - 1))
    out_shape = jax.ShapeDtypeStruct((depth,) + shape, F32)
    extra_in = [] if prev is None else [prev]
    extra_spec = [] if prev is None else [pl.BlockSpec(memory_space=pl.ANY)]
    return spec, out_shape, extra_in, extra_spec


def _drop_aliased(kernel_fn, n_in, n_prev):
    def wrapped(*refs):
        return kernel_fn(*refs[:n_in], *refs[n_in + n_prev:])
    return wrapped


def _attn_sample_kernel(q_ref, k_ref, v_ref, kc_ref, vc_ref, sink_ref, c_ref, s1_ref, s2_ref,
                        o_ref, ko_ref, vo_ref):
    w = kc_ref.shape[1]
    c = c_ref[...]
    s1 = s1_ref[...]
    s2 = s2_ref[...]
    qb = _rope(q_ref[...], c, s1, s2)
    kn = _rope(k_ref[...], c, s1, s2)
    vn = v_ref[...]
    kc = kc_ref[...]
    vc = vc_ref[...]
    ko_ref[:, 0:w - 1, :] = kc[:, 1:w, :]
    ko_ref[:, w - 1:w, :] = kn
    vo_ref[:, 0:w - 1, :] = vc[:, 1:w, :]
    vo_ref[:, w - 1:w, :] = vn

    scale = ATT_HEAD_DIM ** -0.5
    s = jnp.einsum("bhd,bkd->bhk", qb.astype(BF16), kc.astype(BF16),
                   preferred_element_type=F32) * scale
    sn = jnp.sum(qb * kn, axis=-1, keepdims=True) * scale
    sink = sink_ref[...][None, :, 0:1]
    m = jnp.maximum(jnp.maximum(jnp.max(s, axis=-1, keepdims=True), sn), sink)
    p = jnp.exp(s - m)
    pn = jnp.exp(sn - m)
    denom = jnp.sum(p, axis=-1, keepdims=True) + pn + jnp.exp(sink - m)
    p = p / denom
    pn = pn / denom
    o = jnp.einsum("bhk,bkd->bhd", p.astype(BF16), vc.astype(BF16), preferred_element_type=F32)
    o = o + pn * vn
    head = lax.broadcasted_iota(jnp.int32, o.shape[:2] + (ATT_HEAD_DIM,), 1)
    o_ref[...] = jnp.where(head < ATT_GROUP, o[:, :, 0:ATT_HEAD_DIM], o[:, :, ATT_HEAD_DIM:]).astype(o_ref.dtype)


def _attn_sample(qb, kn, vn, kc, vc, sink_rows, tables, layer, prev, tb):
    depth, bd, w, _ = kc.shape
    c, s1, s2 = tables
    blk3 = lambda r, n: pl.BlockSpec((tb, r, n), lambda i: (i, 0, 0))
    cache = pl.BlockSpec((None, tb, w, LANE), lambda i: (layer, i, 0, 0))
    vec = pl.BlockSpec((1, LANE), lambda i: (0, 0))
    prev_k, prev_v = (None, None) if prev is None else prev
    k_spec, k_shape, k_in, k_in_spec = _stacked_out(prev_k, depth, (bd, w, LANE), (tb, w, LANE), layer)
    v_spec, v_shape, v_in, v_in_spec = _stacked_out(prev_v, depth, (bd, w, LANE), (tb, w, LANE), layer)
    n_in = 9
    n_prev = len(k_in) + len(v_in)
    return pl.pallas_call(
        _drop_aliased(_attn_sample_kernel, n_in, n_prev),
        grid=(bd // tb,),
        in_specs=[blk3(ATT_HEADS, LANE), blk3(1, LANE), blk3(1, LANE), cache, cache,
                  pl.BlockSpec((ATT_HEADS, LANE), lambda i: (0, 0)), vec, vec, vec] + k_in_spec + v_in_spec,
        out_specs=[blk3(ATT_HEADS, ATT_HEAD_DIM), k_spec, v_spec],
        out_shape=[jax.ShapeDtypeStruct((bd, ATT_HEADS, ATT_HEAD_DIM), BF16), k_shape, v_shape],
        input_output_aliases={n_in + t: 1 + t for t in range(n_prev)},
        compiler_params=_cparams(("parallel",)),
        name="attn_sample",
    )(qb, kn, vn, kc, vc, sink_rows, c, s1, s2, *k_in, *v_in)


def _lane_place(cols, lane0):
    m = cols[0].shape[0]
    lane = lax.broadcasted_iota(jnp.int32, (m, LANE), 1)
    out = jnp.zeros((m, LANE), F32)
    for i, col in enumerate(cols):
        out = jnp.where(lane == lane0 + i, col, out)
    return out


def _sample_pre_kernel(proj_ref, cs_ref, cw_ref, cb_ref, dtb_ref, alog_ref, bi_ref, bf_ref, n_ref, m_ref,
                       conv_ref, xs_ref, xdt_ref, bm_ref, cm_ref, ea_ref,
                       ks_ref, nn_ref, g_ref, esw_ref, sw_ref, den_ref, mt_ref):
    xbc = proj_ref[:, C_XBC:C_XBC + CONV_DIM]
    s0 = cs_ref[:, 0:CONV_DIM]
    s1 = cs_ref[:, CONV_DIM:2 * CONV_DIM]
    s2 = cs_ref[:, 2 * CONV_DIM:3 * CONV_DIM]
    cw = cw_ref[...]
    acc = s0 * cw[0:1, :]
    acc = acc + s1 * cw[1:2, :]
    acc = acc + s2 * cw[2:3, :]
    acc = acc + xbc * cw[3:4, :]
    xc = _silu(acc + cb_ref[...])
    conv_ref[:, 0:CONV_DIM] = s1
    conv_ref[:, CONV_DIM:2 * CONV_DIM] = s2
    conv_ref[:, 2 * CONV_DIM:3 * CONV_DIM] = xbc

    xs = xc[:, 0:D_SSM]
    xs_ref[...] = xs
    bm_ref[...] = xc[:, D_SSM:D_SSM + SSM_GROUPS * D_STATE]
    cm_ref[...] = xc[:, D_SSM + SSM_GROUPS * D_STATE:]
    dt = jax.nn.softplus(proj_ref[:, C_G:C_G + LANE] + dtb_ref[...])
    ea_ref[...] = jnp.exp(dt * (-jnp.exp(alog_ref[...])))
    hp = SSM_HEAD_DIM
    for h in range(SSM_HEADS):
        xdt_ref[:, h * hp:(h + 1) * hp] = xs[:, h * hp:(h + 1) * hp] * dt[:, h:h + 1]

    d = MLSTM_HEAD_DIM
    gi = proj_ref[:, C_G:C_G + LANE] + bi_ref[...]
    gf = proj_ref[:, C_G + LANE:C_G + 2 * LANE] + bf_ref[...]
    log_inter = jax.nn.log_sigmoid(gf) + m_ref[...]
    mt = jnp.maximum(log_inter, gi)
    gq = jnp.exp(log_inter - mt)
    esw = jnp.exp(gi - mt)
    ks = proj_ref[:, C_MK:C_MK + D_MLSTM] * (d ** -0.5)
    ks_ref[...] = ks
    qk_cols = []
    qn_cols = []
    for h in range(MLSTM_HEADS):
        qh = proj_ref[:, C_MQ + h * d:C_MQ + (h + 1) * d]
        kh = ks[:, h * d:(h + 1) * d]
        nh = n_ref[:, h * d:(h + 1) * d]
        qk_cols.append(jnp.sum(qh * kh, axis=-1, keepdims=True))
        qn_cols.append(jnp.sum(qh * nh, axis=-1, keepdims=True))
        ln = GATE_LANE + h
        nn_ref[:, h * d:(h + 1) * d] = nh * gq[:, ln:ln + 1] + kh * esw[:, ln:ln + 1]
    sw = esw * _lane_place(qk_cols, GATE_LANE)
    den = sw + _lane_place(qn_cols, GATE_LANE) * gq
    g_ref[...] = gq
    esw_ref[...] = esw
    sw_ref[...] = sw
    den_ref[...] = jnp.maximum(jnp.abs(den), jnp.exp(-mt))
    mt_ref[...] = mt


def _sample_pre(proj, conv_state, cw, cb, dtb, alog, bi, bf, n_state, m_tile):
    bd = proj.shape[0]
    tile = jax.ShapeDtypeStruct((bd, LANE), F32)
    outs = [jax.ShapeDtypeStruct((bd, (CONV_WIDTH - 1) * CONV_DIM), F32),
            jax.ShapeDtypeStruct((bd, D_SSM), F32),
            jax.ShapeDtypeStruct((bd, D_SSM), F32),
            jax.ShapeDtypeStruct((bd, SSM_GROUPS * D_STATE), F32),
            jax.ShapeDtypeStruct((bd, SSM_GROUPS * D_STATE), F32),
            tile,
            jax.ShapeDtypeStruct((bd, D_MLSTM), F32),
            jax.ShapeDtypeStruct((bd, D_MLSTM), F32),
            tile, tile, tile, tile, tile]
    return pl.pallas_call(
        _sample_pre_kernel,
        out_shape=outs,
        compiler_params=pltpu.CompilerParams(vmem_limit_bytes=VMEM_LIMIT),
        name="sample_pre",
    )(proj, conv_state, cw, cb, dtb, alog, bi, bf, n_state, m_tile)


def _column_tile(row):
    return jnp.broadcast_to(row, (LANE, LANE)).T


def _ssm_sample_kernel(ea_ref, s_ref, xdt_ref, b_ref, c_ref, so_ref, y_ref):
    tb = s_ref.shape[0]
    base = pl.program_id(0) * tb
    hp = SSM_HEAD_DIM
    heads_per_tile = LANE // hp
    n_tiles = D_SSM // LANE
    tiles_per_group = n_tiles // SSM_GROUPS
    top = lax.broadcasted_iota(jnp.int32, (LANE, 1), 0) < hp

    def body(b, carry):
        xrow = xdt_ref[b]
        brow = b_ref[b]
        crow = c_ref[b]
        for t in range(n_tiles):
            g = t // tiles_per_group
            xcol = _column_tile(xrow[:, t * LANE:(t + 1) * LANE])
            bg = brow[:, g * D_STATE:(g + 1) * D_STATE]
            cg = jnp.broadcast_to(crow[:, g * D_STATE:(g + 1) * D_STATE], (8, D_STATE))
            e0 = ea_ref[base + b, heads_per_tile * t]
            e1 = ea_ref[base + b, heads_per_tile * t + 1]
            decay = jnp.where(top, e0, e1)
            sn = s_ref[b, t * LANE:(t + 1) * LANE, :] * decay + xcol * bg
            so_ref[b, t * LANE:(t + 1) * LANE, :] = sn
            y_ref[b, :, t * LANE:(t + 1) * LANE] = _dot_nt(cg, sn)[0:1, :]
        return carry

    lax.fori_loop(0, tb, body, 0)


def _ssm_sample(ea, s, xdt3, b3, c3, layer, prev, tb):
    depth, bd = s.shape[0], s.shape[1]
    blk3 = lambda r, n: pl.BlockSpec((tb, r, n), lambda i: (i, 0, 0))
    s_spec, s_shape, s_in, s_in_spec = _stacked_out(prev, depth, (bd, D_SSM, D_STATE), (tb, D_SSM, D_STATE), layer)
    n_in = 5
    return pl.pallas_call(
        _drop_aliased(_ssm_sample_kernel, n_in, len(s_in)),
        grid=(bd // tb,),
        in_specs=[pl.BlockSpec(memory_space=pltpu.SMEM),
                  pl.BlockSpec((None, tb, D_SSM, D_STATE), lambda i: (layer, i, 0, 0)),
                  blk3(1, D_SSM), blk3(1, SSM_GROUPS * D_STATE),
                  blk3(1, SSM_GROUPS * D_STATE)] + s_in_spec,
        out_specs=[s_spec, blk3(1, D_SSM)],
        out_shape=[s_shape, jax.ShapeDtypeStruct((bd, 1, D_SSM), F32)],
        input_output_aliases={n_in + t: t for t in range(len(s_in))},
        compiler_params=_cparams(("parallel",)),
        name="ssm_sample",
    )(ea, s, xdt3, b3, c3, *s_in)


def _mlstm_sample_kernel(g_ref, esw_ref, c_ref, q_ref, k_ref, v_ref, co_ref, qc_ref):
    tb = c_ref.shape[0]
    base = pl.program_id(0) * tb
    d = MLSTM_HEAD_DIM

    def body(b, carry):
        qrow = q_ref[b]
        krow = k_ref[b]
        vrow = v_ref[b]
        for h in range(MLSTM_HEADS):
            qcol = _column_tile(qrow[:, h * d:(h + 1) * d])
            kcol = _column_tile(krow[:, h * d:(h + 1) * d])
            ch = c_ref[b, h * d:(h + 1) * d, :]
            g_end = g_ref[base + b, GATE_LANE + h]
            wk = esw_ref[base + b, GATE_LANE + h]
            qc_ref[b, :, h * d:(h + 1) * d] = jnp.sum(qcol * ch, axis=0, keepdims=True)
            co_ref[b, h * d:(h + 1) * d, :] = ch * g_end + (kcol * wk) * vrow[:, h * d:(h + 1) * d]
        return carry

    lax.fori_loop(0, tb, body, 0)


def _mlstm_sample(gq, esw, c, q3, k3, v3, layer, prev, tb):
    depth, bd = c.shape[0], c.shape[1]
    blk3 = lambda r, n: pl.BlockSpec((tb, r, n), lambda i: (i, 0, 0))
    smem = pl.BlockSpec(memory_space=pltpu.SMEM)
    c_spec, c_shape, c_in, c_in_spec = _stacked_out(prev, depth, (bd, D_MLSTM, MLSTM_HEAD_DIM),
                                                    (tb, D_MLSTM, MLSTM_HEAD_DIM), layer)
    n_in = 6
    return pl.pallas_call(
        _drop_aliased(_mlstm_sample_kernel, n_in, len(c_in)),
        grid=(bd // tb,),
        in_specs=[smem, smem, pl.BlockSpec((None, tb, D_MLSTM, MLSTM_HEAD_DIM), lambda i: (layer, i, 0, 0)),
                  blk3(1, D_MLSTM), blk3(1, D_MLSTM), blk3(1, D_MLSTM)] + c_in_spec,
        out_specs=[c_spec, blk3(1, D_MLSTM)],
        out_shape=[c_shape, jax.ShapeDtypeStruct((bd, 1, D_MLSTM), F32)],
        input_output_aliases={n_in + t: t for t in range(len(c_in))},
        compiler_params=_cparams(("parallel",)),
        name="mlstm_sample",
    )(gq, esw, c, q3, k3, v3, *c_in)


def _sample_post_kernel(proj_ref, y_ref, xs_ref, dskip_ref, wns_ref, qc_ref, g_ref, sw_ref, den_ref, wnm_ref,
                        ys_ref, hs_ref):
    y = y_ref[...] + dskip_ref[...] * xs_ref[...]
    y = y * _silu(proj_ref[:, C_Z:C_Z + D_SSM])
    ys_ref[...] = _group_rmsnorm(y, wns_ref[...], SSM_GROUPS).astype(ys_ref.dtype)

    d = MLSTM_HEAD_DIM
    for h in range(MLSTM_HEADS):
        ln = GATE_LANE + h
        vh = proj_ref[:, C_MV + h * d:C_MV + (h + 1) * d]
        num = sw_ref[:, ln:ln + 1] * vh + qc_ref[:, h * d:(h + 1) * d] * g_ref[:, ln:ln + 1]
        hh = num / den_ref[:, ln:ln + 1]
        ms = jnp.mean(hh * hh, axis=-1, keepdims=True)
        hn = hh * lax.rsqrt(ms + EPS) * wnm_ref[:, h * d:(h + 1) * d]
        gate = jax.nn.sigmoid(proj_ref[:, C_MO + h * d:C_MO + (h + 1) * d])
        hs_ref[:, h * d:(h + 1) * d] = (hn * gate).astype(hs_ref.dtype)


def _sample_post(proj, y, xs, dskip_rep, wns, qc, gq, sw, den, wnm):
    bd = proj.shape[0]
    return pl.pallas_call(
        _sample_post_kernel,
        out_shape=[jax.ShapeDtypeStruct((bd, D_SSM), BF16), jax.ShapeDtypeStruct((bd, D_MLSTM), BF16)],
        compiler_params=pltpu.CompilerParams(vmem_limit_bytes=VMEM_LIMIT),
        name="sample_post",
    )(proj, y, xs, dskip_rep, wns, qc, gq, sw, den, wnm)


def _lane_vec(v, lane0):
    return jnp.pad(v.astype(F32), (lane0, LANE - lane0 - v.shape[0]))[None, :]


def _pick(n, candidates):
    for c in candidates:
        if n % c == 0:
            return c
    return n


def kernel(x_prompt, x_sample, cache_swa_k, cache_swa_v, state_conv, state_ssm, state_mlstm_C, state_mlstm_n,
           state_mlstm_m, w_norm_mix, w_in, attn_sinks, conv_w, conv_b, dt_bias, a_log, d_skip, w_norm_ssm,
           igate_b, fgate_b, w_norm_mlstm, w_out, w_norm_mlp, w_up, w_down, w_norm_final):
    bsz, seq, d_model = x_prompt.shape
    bd = x_sample.shape[0]
    depth = w_in.shape[0]
    win = cache_swa_k.shape[2]
    assert x_sample.shape[1] == 1 and seq % CHUNK == 0 and d_model == D_MODEL

    mp = bsz * seq
    tm_p = _pick(mp, (1024, 512, 256, 128))
    tm_mlp = _pick(mp, (1024, 512, 256, 128))
    tb = _pick(bd, (8,))

    hp = x_prompt.reshape(mp, d_model)
    hs = x_sample.reshape(bd, d_model)
    tab_p = _rope_tables(jnp.arange(seq, dtype=jnp.int32))
    tab_s = _rope_tables(jnp.arange(1, dtype=jnp.int32) + PAST_LEN)
    gf = w_norm_final[None, :]
    kc_all = cache_swa_k.reshape(depth, bd, win, D_KV)
    vc_all = cache_swa_v.reshape(depth, bd, win, D_KV)
    ssm_all = state_ssm.reshape(depth, bd, D_SSM, D_STATE)
    mem_all = state_mlstm_C.reshape(depth, bd, D_MLSTM, MLSTM_HEAD_DIM)

    st_p, st_s = [], []
    kv_new = s_new = c_new = None
    for l in range(depth):
        w_in_l = _prep_w_in(w_in, l, 256)
        g_mix = w_norm_mix[l][None, :]
        g_mlp = w_norm_mlp[l][None, :]
        cw = conv_w[l]
        cb = conv_b[l][None, :]
        dtb = _lane_vec(dt_bias[l], 0)
        alog = _lane_vec(a_log[l], 0)
        dskip_rep = jnp.repeat(d_skip[l].astype(F32), SSM_HEAD_DIM)[None, :]
        wns = w_norm_ssm[l][None, :]
        bi = _lane_vec(igate_b[l], GATE_LANE)
        bf = _lane_vec(fgate_b[l], GATE_LANE)
        wnm = w_norm_mlstm[l][None, :]
        sinks = attn_sinks[l].reshape(ATT_HEADS).astype(F32)
        last = l == depth - 1

        proj = _rms_matmul(hp, g_mix, w_in_l, tm_p, NP // 4)
        att, pk, pv, y, p_ssm, hm, p_c, p_n, p_m = _mixers_prompt(
            proj, sinks, tab_p, cw, cb, dtb, alog, dskip_rep, wns, bi, bf, wnm, bsz, seq)
        x1 = _outproj(att, y, hm, hp, w_out, l, _pick(mp, (512, 256, 128)))
        hp = _mlp(x1, g_mlp, w_up, w_down, gf, l, last, tm_mlp, 512)
        p_conv = proj.reshape(bsz, seq, NP)[:, seq - (CONV_WIDTH - 1):, C_XBC:C_XBC + CONV_DIM]
        st_p.append((pk.reshape(bsz, WINDOW, ATT_KV_HEADS, ATT_HEAD_DIM),
                     pv.reshape(bsz, WINDOW, ATT_KV_HEADS, ATT_HEAD_DIM),
                     p_conv,
                     p_ssm.reshape(bsz, SSM_HEADS, SSM_HEAD_DIM, D_STATE),
                     p_c.reshape(bsz, MLSTM_HEADS, MLSTM_HEAD_DIM, MLSTM_HEAD_DIM),
                     p_n[:, 0:MLSTM_HEADS, :],
                     p_m[:, 0:MLSTM_HEADS, 0]))

        proj_s = _rms_matmul(hs, g_mix, w_in_l, bd, NP // 4)
        q8 = proj_s[:, C_Q:C_Q + D_ATT].reshape(bd, ATT_HEADS, ATT_HEAD_DIM)
        zero = jnp.zeros_like(q8)
        first_kv = (jnp.arange(ATT_HEADS) < ATT_GROUP)[None, :, None]
        qb = jnp.where(first_kv, jnp.concatenate([q8, zero], -1), jnp.concatenate([zero, q8], -1))
        kn = proj_s[:, C_K:C_K + D_KV].reshape(bd, 1, D_KV)
        vn = proj_s[:, C_V:C_V + D_KV].reshape(bd, 1, D_KV)
        sink_rows = jnp.broadcast_to(sinks[:, None], (ATT_HEADS, LANE))
        att_s, sk, sv = _attn_sample(qb, kn, vn, kc_all, vc_all, sink_rows, tab_s, l, kv_new, tb)
        kv_new = (sk, sv)

        m_tile = jnp.pad(state_mlstm_m[l], ((0, 0), (GATE_LANE, LANE - GATE_LANE - MLSTM_HEADS)))
        (conv_new, xs, xdt, bm, cm, ea, ks, n_new, gq, esw, sw, den, mt) = _sample_pre(
            proj_s, state_conv[l].reshape(bd, (CONV_WIDTH - 1) * CONV_DIM), cw, cb, dtb, alog, bi, bf,
            state_mlstm_n[l].reshape(bd, D_MLSTM), m_tile)
        s_new, y3 = _ssm_sample(ea, ssm_all, xdt.reshape(bd, 1, D_SSM), bm.reshape(bd, 1, -1),
                                cm.reshape(bd, 1, -1), l, s_new, tb)
        c_new, qc3 = _mlstm_sample(gq, esw, mem_all,
                                   proj_s[:, C_MQ:C_MQ + D_MLSTM].reshape(bd, 1, D_MLSTM),
                                   ks.reshape(bd, 1, D_MLSTM),
                                   proj_s[:, C_MV:C_MV + D_MLSTM].reshape(bd, 1, D_MLSTM), l, c_new, tb)
        ys, hms = _sample_post(proj_s, y3.reshape(bd, D_SSM), xs, dskip_rep, wns, qc3.reshape(bd, D_MLSTM),
                               gq, sw, den, wnm)
        x1s = _outproj(att_s.reshape(bd, D_ATT), ys, hms, hs, w_out, l, bd)
        hs = _mlp(x1s, g_mlp, w_up, w_down, gf, l, last, bd, 512)
        st_s.append((conv_new.reshape(bd, CONV_WIDTH - 1, CONV_DIM),
                     n_new.reshape(bd, MLSTM_HEADS, MLSTM_HEAD_DIM),
                     mt[:, GATE_LANE:GATE_LANE + MLSTM_HEADS]))

    y_prompt = hp.reshape(bsz, seq, d_model)
    y_sample = hs.reshape(bd, 1, d_model)
    p_out = [jnp.stack([s[i] for s in st_p]) for i in range(7)]
    s_conv, s_n, s_m = [jnp.stack([s[i] for s in st_s]) for i in range(3)]
    s_k = kv_new[0].reshape(depth, bd, win, ATT_KV_HEADS, ATT_HEAD_DIM)
    s_v = kv_new[1].reshape(depth, bd, win, ATT_KV_HEADS, ATT_HEAD_DIM)
    s_ssm = s_new.reshape(depth, bd, SSM_HEADS, SSM_HEAD_DIM, D_STATE)
    s_c = c_new.reshape(depth, bd, MLSTM_HEADS, MLSTM_HEAD_DIM, MLSTM_HEAD_DIM)
    return (y_prompt, y_sample, *p_out, s_k, s_v, s_conv, s_ssm, s_c, s_n, s_m)
```

```python
import functools
import math

import jax
import jax.numpy as jnp
from jax import lax
from jax.experimental import pallas as pl
from jax.experimental.pallas import tpu as pltpu

F32 = jnp.float32
BF16 = jnp.bfloat16
HIGHEST = lax.Precision.HIGHEST

D_MODEL = 2048
EPS = 1e-6
PAST_LEN = 8192
ATT_HEAD_DIM = 64
ATT_HEADS = 8
ATT_KV_HEADS = 2
ATT_GROUP = ATT_HEADS // ATT_KV_HEADS
D_ATT = ATT_HEADS * ATT_HEAD_DIM
D_KV = ATT_KV_HEADS * ATT_HEAD_DIM
WINDOW = 128
ROPE_THETA = 500000.0
ROPE_DIM = ATT_HEAD_DIM // 4
SSM_HEAD_DIM = 64
SSM_HEADS = 16
D_SSM = SSM_HEADS * SSM_HEAD_DIM
SSM_GROUPS = 2
D_STATE = 128
CONV_WIDTH = 4
CONV_DIM = D_SSM + 2 * SSM_GROUPS * D_STATE
CHUNK = 128
MLSTM_HEADS = 4
MLSTM_HEAD_DIM = 128
D_MLSTM = MLSTM_HEADS * MLSTM_HEAD_DIM
D_FF = 4 * D_MODEL

C_XBC = 0
C_Q = 1536
C_Z = 2048
C_MQ = 3072
C_MK = 3584
C_MV = 4096
C_MO = 4608
C_K = 5120
C_V = 5248
C_G = 5376
NP = 5632
GATE_LANE = 16

LANE = 128
VMEM_LIMIT = 56 * 1024 * 1024


def _cparams(sem):
    return pltpu.CompilerParams(dimension_semantics=sem, vmem_limit_bytes=VMEM_LIMIT)


def _silu(x):
    return x * jax.nn.sigmoid(x)


def _dot(a, b):
    return jnp.dot(a.astype(BF16), b.astype(BF16), preferred_element_type=F32)


def _dot_nt(a, b):
    return lax.dot_general(a.astype(BF16), b.astype(BF16), (((1,), (1,)), ((), ())),
                           preferred_element_type=F32)


def _dot_tn(a, b):
    return lax.dot_general(a.astype(BF16), b.astype(BF16), (((0,), (0,)), ((), ())),
                           preferred_element_type=F32)


def _tril(n):
    row = lax.broadcasted_iota(jnp.int32, (n, n), 0)
    col = lax.broadcasted_iota(jnp.int32, (n, n), 1)
    return row >= col


def _cumsum_rows(x):
    t = _tril(x.shape[0]).astype(F32)
    return jnp.dot(t, x, precision=HIGHEST, preferred_element_type=F32)


def _rms_matmul_kernel(x_ref, g_ref, w_ref, o_ref, u_ref):
    @pl.when(pl.program_id(1) == 0)
    def _():
        x = x_ref[...]
        ms = jnp.mean(x * x, axis=-1, keepdims=True)
        u_ref[...] = (x * lax.rsqrt(ms + EPS) * g_ref[...]).astype(BF16)

    o_ref[...] = jnp.dot(u_ref[...], w_ref[...], preferred_element_type=F32)


def _rms_matmul(x, g, w, tm, tn):
    m, k = x.shape
    n = w.shape[1]
    return pl.pallas_call(
        _rms_matmul_kernel,
        grid=(m // tm, n // tn),
        in_specs=[pl.BlockSpec((tm, k), lambda i, j: (i, 0)),
                  pl.BlockSpec((1, k), lambda i, j: (0, 0)),
                  pl.BlockSpec((k, tn), lambda i, j: (0, j))],
        out_specs=pl.BlockSpec((tm, tn), lambda i, j: (i, j)),
        out_shape=jax.ShapeDtypeStruct((m, n), F32),
        scratch_shapes=[pltpu.VMEM((tm, k), BF16)],
        compiler_params=_cparams(("parallel", "arbitrary")),
        name="rms_inproj",
    )(x, g, w)


_SRC_Q, _SRC_K, _SRC_V, _SRC_Z, _SRC_XBC, _SRC_DT, _SRC_MQ = 0, 512, 640, 768, 1792, 3328, 3344
_SRC_MI = _SRC_MQ + 4 * D_MLSTM
IN_WIDTH = _SRC_MI + 2 * MLSTM_HEADS


def _prep_w_in_kernel(w_ref, o_ref):
    def put(dst, src, n):
        o_ref[:, dst:dst + n] = w_ref[:, src:src + n].astype(BF16)

    put(C_XBC, _SRC_XBC, CONV_DIM)
    put(C_Q, _SRC_Q, D_ATT)
    put(C_Z, _SRC_Z, D_SSM)
    put(C_MQ, _SRC_MQ, 4 * D_MLSTM)
    put(C_K, _SRC_K, D_KV)
    put(C_V, _SRC_V, D_KV)
    rows = w_ref.shape[0]
    lane = lax.broadcasted_iota(jnp.int32, (rows, LANE), 1)
    gate = (lane >= GATE_LANE) & (lane < GATE_LANE + MLSTM_HEADS)
    head = w_ref[:, _SRC_DT:_SRC_DT + LANE]
    tail = w_ref[:, IN_WIDTH - LANE:IN_WIDTH]
    mi = pltpu.roll(tail, (GATE_LANE - (LANE - 2 * MLSTM_HEADS)) % LANE, 1)
    mf = pltpu.roll(tail, (GATE_LANE - (LANE - MLSTM_HEADS)) % LANE, 1)
    lo = jnp.where(lane < SSM_HEADS, head, jnp.where(gate, mi, 0.0))
    hi = jnp.where(gate, mf, 0.0)
    o_ref[:, C_G:C_G + LANE] = lo.astype(BF16)
    o_ref[:, C_G + LANE:C_G + 2 * LANE] = hi.astype(BF16)


def _prep_w_in(w_in, layer, tr):
    k = w_in.shape[1]
    w_in = w_in.reshape(w_in.shape[0] * k, IN_WIDTH)
    return pl.pallas_call(
        _prep_w_in_kernel,
        grid=(k // tr,),
        in_specs=[pl.BlockSpec((tr, IN_WIDTH), lambda i: (layer * (k // tr) + i, 0))],
        out_specs=pl.BlockSpec((tr, NP), lambda i: (i, 0)),
        out_shape=jax.ShapeDtypeStruct((k, NP), BF16),
        compiler_params=_cparams(("parallel",)),
        name="prep_w_in",
    )(w_in)


def _outproj_kernel(a_ref, y_ref, h_ref, x_ref, w_ref, o_ref, wb_ref):
    @pl.when(pl.program_id(0) == 0)
    def _():
        wb_ref[...] = w_ref[...].astype(BF16)

    acc = jnp.dot(a_ref[...], wb_ref[0:D_ATT, :], preferred_element_type=F32)
    acc = acc + jnp.dot(y_ref[...], wb_ref[D_ATT:D_ATT + D_SSM, :], preferred_element_type=F32)
    acc = acc + jnp.dot(h_ref[...], wb_ref[D_ATT + D_SSM:, :], preferred_element_type=F32)
    o_ref[...] = x_ref[...] + acc


def _outproj(att, y, h, x, w, layer, tm):
    m, n = x.shape
    k = w.shape[1]
    rows = lambda width: pl.BlockSpec((tm, width), lambda i: (i, 0))
    return pl.pallas_call(
        _outproj_kernel,
        grid=(m // tm,),
        in_specs=[rows(D_ATT), rows(D_SSM), rows(D_MLSTM), rows(n),
                  pl.BlockSpec((None, k, n), lambda i: (layer, 0, 0), pipeline_mode=pl.Buffered(1))],
        out_specs=rows(n),
        out_shape=jax.ShapeDtypeStruct((m, n), F32),
        scratch_shapes=[pltpu.VMEM((k, n), BF16)],
        compiler_params=_cparams(("arbitrary",)),
        name="outproj",
    )(att, y, h, x, w)


N_MLP_IN, N_SSM_SIDE_IN, N_MLSTM_SIDE_IN = 5, 5, 6


def _mlp_kernel(*refs, final_norm, side, n_prev):
    x_ref, g_ref, wu_ref, wd_ref, gf_ref = refs[:N_MLP_IN]
    n_side = N_SSM_SIDE_IN + N_MLSTM_SIDE_IN if side else 0
    side_in = refs[N_MLP_IN:N_MLP_IN + n_side]
    outs = refs[N_MLP_IN + n_side + n_prev:]
    o_ref, u_ref = outs[0], outs[-1]
    j = pl.program_id(1)

    @pl.when(j == 0)
    def _():
        x = x_ref[...]
        ms = jnp.mean(x * x, axis=-1, keepdims=True)
        u_ref[...] = (x * lax.rsqrt(ms + EPS) * g_ref[...]).astype(BF16)
        o_ref[...] = x

    h = _dot(u_ref[...], wu_ref[...])
    h = jnp.square(jnp.maximum(h, 0.0))
    o_ref[...] += _dot(h, wd_ref[...])

    if side:
        seq = pl.program_id(0) * pl.num_programs(1) + j
        so_ref, y_ref, co_ref, qc_ref = outs[1:5]
        _ssm_sample_one(0, seq, *side_in[:N_SSM_SIDE_IN], so_ref, y_ref)
        _mlstm_sample_one(0, seq, *side_in[N_SSM_SIDE_IN:], co_ref, qc_ref)

    if final_norm:
        @pl.when(j == pl.num_programs(1) - 1)
        def _():
            y = o_ref[...]
            ms = jnp.mean(y * y, axis=-1, keepdims=True)
            o_ref[...] = y * lax.rsqrt(ms + EPS) * gf_ref[...]


def _mlp(x, g, wu, wd, gf, layer, final_norm, tm, tf, side=None):
    m, d = x.shape
    ff = wu.shape[2]
    nj = ff // tf
    in_specs = [pl.BlockSpec((tm, d), lambda i, j: (i, 0), pipeline_mode=pl.Buffered(1)),
                pl.BlockSpec((1, d), lambda i, j: (0, 0)),
                pl.BlockSpec((None, d, tf), lambda i, j: (layer, 0, j)),
                pl.BlockSpec((None, tf, d), lambda i, j: (layer, j, 0)),
                pl.BlockSpec((1, d), lambda i, j: (0, 0))]
    out_specs = [pl.BlockSpec((tm, d), lambda i, j: (i, 0))]
    out_shape = [jax.ShapeDtypeStruct((m, d), F32)]
    operands = [x, g, wu, wd, gf]
    aliases = {}
    n_prev = 0
    if side is not None:
        ea, ssm, xdt3, b3, c3, gq, esw, mem, q3, k3, v3, prev_s, prev_c = side
        depth, bd = ssm.shape[0], ssm.shape[1]
        assert (m // tm) * nj == bd
        smem = pl.BlockSpec(memory_space=pltpu.SMEM)
        one = lambda n: pl.BlockSpec((1, 1, n), lambda i, j: (i * nj + j, 0, 0))
        state = lambda r, n: pl.BlockSpec((None, 1, r, n), lambda i, j: (layer, i * nj + j, 0, 0))
        in_specs += [smem, state(D_SSM, D_STATE), one(D_SSM), one(SSM_GROUPS * D_STATE), one(SSM_GROUPS * D_STATE),
                     smem, smem, state(D_MLSTM, MLSTM_HEAD_DIM), one(D_MLSTM), one(D_MLSTM), one(D_MLSTM)]
        operands += [ea, ssm, xdt3, b3, c3, gq, esw, mem, q3, k3, v3]
        out_specs += [state(D_SSM, D_STATE), one(D_SSM), state(D_MLSTM, MLSTM_HEAD_DIM), one(D_MLSTM)]
        out_shape += [jax.ShapeDtypeStruct((depth, bd, D_SSM, D_STATE), F32),
                      jax.ShapeDtypeStruct((bd, 1, D_SSM), F32),
                      jax.ShapeDtypeStruct((depth, bd, D_MLSTM, MLSTM_HEAD_DIM), F32),
                      jax.ShapeDtypeStruct((bd, 1, D_MLSTM), F32)]
        if prev_s is not None:
            aliases = {len(operands): 1, len(operands) + 1: 3}
            operands += [prev_s, prev_c]
            in_specs += [pl.BlockSpec(memory_space=pl.ANY)] * 2
            n_prev = 2
    res = pl.pallas_call(
        functools.partial(_mlp_kernel, final_norm=final_norm, side=side is not None, n_prev=n_prev),
        grid=(m // tm, nj),
        in_specs=in_specs,
        out_specs=out_specs,
        out_shape=out_shape,
        input_output_aliases=aliases,
        scratch_shapes=[pltpu.VMEM((tm, d), BF16)],
        compiler_params=_cparams(("parallel", "arbitrary")),
        name="mlp",
    )(*operands)
    return res[0] if side is None else res


def _rope_tables(pos):
    half = ROPE_DIM // 2
    inv = jnp.power(jnp.float32(ROPE_THETA), -jnp.arange(half, dtype=jnp.float32) / half)
    ang = pos.astype(jnp.float32)[:, None] * inv[None, :]
    cos = jnp.cos(ang)
    sin = jnp.sin(ang)
    n = pos.shape[0]
    rest = ATT_HEAD_DIM - ROPE_DIM
    c = jnp.concatenate([cos, cos, jnp.ones((n, rest), F32)], axis=1)
    s1 = jnp.concatenate([-sin, jnp.zeros((n, half + rest), F32)], axis=1)
    s2 = jnp.concatenate([jnp.zeros((n, half), F32), sin, jnp.zeros((n, rest), F32)], axis=1)
    rep = LANE // ATT_HEAD_DIM
    return jnp.tile(c, (1, rep)), jnp.tile(s1, (1, rep)), jnp.tile(s2, (1, rep))


def _rope(x, c, s1, s2):
    width = x.shape[-1]
    rep = width // LANE
    half = ROPE_DIM // 2
    if rep > 1:
        c = jnp.concatenate([c] * rep, axis=-1)
        s1 = jnp.concatenate([s1] * rep, axis=-1)
        s2 = jnp.concatenate([s2] * rep, axis=-1)
    axis = x.ndim - 1
    return x * c + pltpu.roll(x, width - half, axis) * s1 + pltpu.roll(x, half, axis) * s2


def _attn_prompt_init(ko_ref, vo_ref):
    ko_ref[...] = jnp.zeros_like(ko_ref)
    vo_ref[...] = jnp.zeros_like(vo_ref)


def _attn_prompt_body(j, sink_ref, q_ref, k_ref, v_ref, c_ref, s1_ref, s2_ref, o_ref, ko_ref, vo_ref):
    w = WINDOW
    c = c_ref[...]
    s1 = s1_ref[...]
    s2 = s2_ref[...]
    krot = _rope(k_ref[...], c, s1, s2)
    v = v_ref[...]
    qrot = _rope(q_ref[...], c, s1, s2)
    kk = jnp.concatenate([ko_ref[...], krot], axis=0).astype(BF16)
    vv = jnp.concatenate([vo_ref[...], v], axis=0).astype(BF16)

    row = lax.broadcasted_iota(jnp.int32, (w, 2 * w), 0)
    col = lax.broadcasted_iota(jnp.int32, (w, 2 * w), 1)
    first_col = jnp.where(j > 0, 0, w)
    mask = (col >= row) & (col <= row + w) & (col >= first_col)
    scale = ATT_HEAD_DIM ** -0.5

    for h in range(ATT_KV_HEADS):
        kh = kk[:, h * ATT_HEAD_DIM:(h + 1) * ATT_HEAD_DIM]
        vh = vv[:, h * ATT_HEAD_DIM:(h + 1) * ATT_HEAD_DIM]
        for g in range(ATT_GROUP):
            hg = h * ATT_GROUP + g
            qh = qrot[:, hg * ATT_HEAD_DIM:(hg + 1) * ATT_HEAD_DIM]
            s = _dot_nt(qh, kh) * scale
            s = jnp.where(mask, s, -jnp.inf)
            sink = sink_ref[hg]
            m = jnp.maximum(jnp.max(s, axis=-1, keepdims=True), sink)
            p = jnp.exp(s - m)
            denom = jnp.sum(p, axis=-1, keepdims=True) + jnp.exp(sink - m)
            p = p / denom
            o = _dot(p, vh)
            o_ref[:, hg * ATT_HEAD_DIM:(hg + 1) * ATT_HEAD_DIM] = o.astype(o_ref.dtype)
            yield

    ko_ref[...] = krot
    vo_ref[...] = v


def _group_rmsnorm(y, w, groups):
    width = y.shape[-1] // groups
    outs = []
    for g in range(groups):
        yg = y[:, g * width:(g + 1) * width]
        ms = jnp.mean(yg * yg, axis=-1, keepdims=True)
        outs.append(yg * lax.rsqrt(ms + EPS) * w[:, g * width:(g + 1) * width])
    return jnp.concatenate(outs, axis=-1)


CONV_PAD = 8


def _ssd_prompt_init(s_ref, tail_ref):
    tail_ref[...] = jnp.zeros_like(tail_ref)
    s_ref[...] = jnp.zeros_like(s_ref)


def _ssd_prompt_body(xbc_ref, z_ref, g_ref, cw_ref, cb_ref, dtb_ref, alog_ref, dskip_ref, wn_ref,
                     y_ref, s_ref, tail_ref, ybuf_ref, xw_ref):
    q = CHUNK
    pad = CONV_PAD
    hp = SSM_HEAD_DIM
    heads_per_group = SSM_HEADS // SSM_GROUPS
    gw = heads_per_group * hp

    x = xbc_ref[...]
    cw = cw_ref[...]
    row8 = lax.broadcasted_iota(jnp.int32, (pad, CONV_DIM), 0)
    acc = None
    for j in range(CONV_WIDTH - 1):
        shift = CONV_WIDTH - 1 - j
        rolled = pltpu.roll(x, shift, 0)
        head = jnp.where(row8 < shift, tail_ref[j * pad:(j + 1) * pad, :], rolled[0:pad, :])
        tail_ref[j * pad:(j + 1) * pad, :] = rolled[0:pad, :]
        term = jnp.concatenate([head, rolled[pad:, :]], axis=0) * cw[j:j + 1, :]
        acc = term if acc is None else acc + term
    acc = acc + x * cw[CONV_WIDTH - 1:CONV_WIDTH, :]
    xc = _silu(acc + cb_ref[...])

    xs = xc[:, 0:D_SSM]
    bm = xc[:, D_SSM:D_SSM + SSM_GROUPS * D_STATE]
    cm = xc[:, D_SSM + SSM_GROUPS * D_STATE:]

    dt = jax.nn.softplus(g_ref[:, 0:LANE] + dtb_ref[...])
    a_neg = -jnp.exp(alog_ref[...])
    a_col = _cumsum_rows(dt * a_neg)
    a_row = a_col.T
    dt_row = dt.T
    ea_col = jnp.exp(a_col)
    wk_col = jnp.exp(a_col[q - 1:q, :] - a_col) * dt
    tril = _tril(q)
    yield

    for g in range(SSM_GROUPS):
        bg = bm[:, g * D_STATE:(g + 1) * D_STATE]
        cg = cm[:, g * D_STATE:(g + 1) * D_STATE]
        cb = _dot_nt(cg, bg)
        cs = _dot_nt(cg, s_ref[g * gw:(g + 1) * gw, :])
        for r in range(heads_per_group):
            h = g * heads_per_group + r
            seg = a_col[:, h:h + 1] - a_row[h:h + 1, :]
            wmat = jnp.exp(jnp.where(tril, seg, -jnp.inf)) * cb * dt_row[h:h + 1, :]
            xh = xs[:, h * hp:(h + 1) * hp]
            yh = _dot(wmat, xh)
            yh = yh + cs[:, r * hp:(r + 1) * hp] * ea_col[:, h:h + 1]
            ybuf_ref[:, h * hp:(h + 1) * hp] = yh
            xw_ref[:, r * hp:(r + 1) * hp] = xh * wk_col[:, h:h + 1]
            yield
        upd = _dot_tn(xw_ref[...], bg)
        for r in range(heads_per_group):
            h = g * heads_per_group + r
            decay = jnp.exp(a_row[h:h + 1, q - 1:q])
            s_ref[h * hp:(h + 1) * hp, :] = s_ref[h * hp:(h + 1) * hp, :] * decay + upd[r * hp:(r + 1) * hp, :]
        yield

    y = ybuf_ref[...] + dskip_ref[...] * xs
    y = y * _silu(z_ref[...])
    y_ref[...] = _group_rmsnorm(y, wn_ref[...], SSM_GROUPS).astype(y_ref.dtype)


def _mlstm_prompt_init(c_ref, n_ref, m_ref):
    c_ref[...] = jnp.zeros_like(c_ref)
    n_ref[...] = jnp.zeros_like(n_ref)
    m_ref[...] = jnp.zeros_like(m_ref)


def _mlstm_prompt_body(q_ref, k_ref, v_ref, o_ref, g_ref, bi_ref, bf_ref, wn_ref, h_ref, c_ref, n_ref, m_ref):
    t = CHUNK
    d = MLSTM_HEAD_DIM

    gi = g_ref[:, 0:LANE] + bi_ref[...]
    gf = g_ref[:, LANE:2 * LANE] + bf_ref[...]
    b_col = _cumsum_rows(jax.nn.log_sigmoid(gf))
    b_row = b_col.T
    i_row = gi.T
    tril = _tril(t)
    k_all = k_ref[...] * (d ** -0.5)

    for h in range(MLSTM_HEADS):
        ln = GATE_LANE + h
        bq = b_col[:, ln:ln + 1]
        logw = jnp.where(tril, bq - b_row[ln:ln + 1, :] + i_row[ln:ln + 1, :], -jnp.inf)
        m_prev = m_ref[h:h + 1, 0:1]
        log_inter = bq + m_prev
        mt = jnp.maximum(log_inter, jnp.max(logw, axis=-1, keepdims=True))
        qh = q_ref[:, h * d:(h + 1) * d]
        kh = k_all[:, h * d:(h + 1) * d]
        vh = v_ref[:, h * d:(h + 1) * d]
        ch = c_ref[h * d:(h + 1) * d, :]
        nh = n_ref[h:h + 1, :]
        sw = jnp.exp(logw - mt) * _dot_nt(qh, kh)
        gq = jnp.exp(log_inter - mt)
        num = _dot(sw, vh) + _dot(qh, ch) * gq
        den = jnp.sum(sw, axis=-1, keepdims=True) + jnp.sum(qh * nh, axis=-1, keepdims=True) * gq
        hh = num / jnp.maximum(jnp.abs(den), jnp.exp(-mt))

        m_new = mt[t - 1:t, :]
        b_last = b_col[t - 1:t, ln:ln + 1]
        wk = jnp.exp(b_last - bq + gi[:, ln:ln + 1] - m_new)
        g_end = jnp.exp(b_last + m_prev - m_new)
        kw = kh * wk
        c_ref[h * d:(h + 1) * d, :] = ch * g_end + _dot_tn(kw, vh)
        n_ref[h:h + 1, :] = nh * g_end + jnp.sum(kw, axis=0, keepdims=True)
        m_ref[h:h + 1, :] = jnp.broadcast_to(m_new, (1, LANE))

        ms = jnp.mean(hh * hh, axis=-1, keepdims=True)
        hn = hh * lax.rsqrt(ms + EPS) * wn_ref[:, h * d:(h + 1) * d]
        h_ref[:, h * d:(h + 1) * d] = (hn * jax.nn.sigmoid(o_ref[:, h * d:(h + 1) * d])).astype(h_ref.dtype)
        yield

N_ATT_IN, N_SSD_IN, N_MLSTM_IN = 7, 9, 7


def _interleave(stages):
    live = [[gen, 0, n] for gen, n in stages]
    while live:
        item = min(live, key=lambda s: s[1] / s[2])
        try:
            next(item[0])
            item[1] += 1
        except StopIteration:
            live.remove(item)


def _mixers_prompt_kernel(*refs):
    att_in = refs[0:N_ATT_IN]
    ssd_in = refs[N_ATT_IN:N_ATT_IN + N_SSD_IN]
    ml_in = refs[N_ATT_IN + N_SSD_IN:N_ATT_IN + N_SSD_IN + N_MLSTM_IN]
    outs = refs[N_ATT_IN + N_SSD_IN + N_MLSTM_IN:]
    att_ref, ko_ref, vo_ref, y_ref, s_ref, h_ref, c_ref, n_ref, m_ref, tail_ref, ybuf_ref, xw_ref = outs
    j = pl.program_id(1)

    @pl.when(j == 0)
    def _():
        _attn_prompt_init(ko_ref, vo_ref)
        _ssd_prompt_init(s_ref, tail_ref)
        _mlstm_prompt_init(c_ref, n_ref, m_ref)

    g_ref = ssd_in[2]
    _interleave([
        (_attn_prompt_body(j, *att_in, att_ref, ko_ref, vo_ref), ATT_HEADS + 1),
        (_ssd_prompt_body(*ssd_in, y_ref, s_ref, tail_ref, ybuf_ref, xw_ref), SSM_HEADS + SSM_GROUPS + 2),
        (_mlstm_prompt_body(*ml_in[0:4], g_ref, *ml_in[4:], h_ref, c_ref, n_ref, m_ref), MLSTM_HEADS + 1),
    ])


def _mixers_prompt(proj, sinks, tables, cw, cb, dtb, alog, dskip_rep, wns, bi, bf, wnm, bsz, seq):
    nb = seq // CHUNK
    t = CHUNK
    c, s1, s2 = tables
    row = lambda b, j: b * nb + j
    seg = lambda col, width: pl.BlockSpec((t, width), lambda b, j: (row(b, j), col // width))
    vec = lambda n: pl.BlockSpec((1, n), lambda b, j: (0, 0))
    tab = pl.BlockSpec((t, LANE), lambda b, j: (j, 0))
    rows_out = lambda width: pl.BlockSpec((t, width), lambda b, j: (row(b, j), 0))
    per_seq = lambda r, n: pl.BlockSpec((None, r, n), lambda b, j: (b, 0, 0))
    in_specs = (
        [pl.BlockSpec(memory_space=pltpu.SMEM), seg(C_Q, D_ATT), seg(C_K, D_KV), seg(C_V, D_KV), tab, tab, tab]
        + [seg(C_XBC, CONV_DIM), seg(C_Z, D_SSM), seg(C_G, 2 * LANE),
           pl.BlockSpec((CONV_WIDTH, CONV_DIM), lambda b, j: (0, 0)),
           vec(CONV_DIM), vec(LANE), vec(LANE), vec(D_SSM), vec(D_SSM)]
        + [seg(C_MQ, D_MLSTM), seg(C_MK, D_MLSTM), seg(C_MV, D_MLSTM), seg(C_MO, D_MLSTM),
           vec(LANE), vec(LANE), vec(D_MLSTM)])
    assert len(in_specs) == N_ATT_IN + N_SSD_IN + N_MLSTM_IN
    return pl.pallas_call(
        _mixers_prompt_kernel,
        grid=(bsz, nb),
        in_specs=in_specs,
        out_specs=[rows_out(D_ATT), per_seq(t, D_KV), per_seq(t, D_KV),
                   rows_out(D_SSM), per_seq(D_SSM, D_STATE),
                   rows_out(D_MLSTM), per_seq(D_MLSTM, MLSTM_HEAD_DIM), per_seq(8, LANE), per_seq(8, LANE)],
        out_shape=[jax.ShapeDtypeStruct((bsz * seq, D_ATT), BF16),
                   jax.ShapeDtypeStruct((bsz, t, D_KV), F32),
                   jax.ShapeDtypeStruct((bsz, t, D_KV), F32),
                   jax.ShapeDtypeStruct((bsz * seq, D_SSM), BF16),
                   jax.ShapeDtypeStruct((bsz, D_SSM, D_STATE), F32),
                   jax.ShapeDtypeStruct((bsz * seq, D_MLSTM), BF16),
                   jax.ShapeDtypeStruct((bsz, D_MLSTM, MLSTM_HEAD_DIM), F32),
                   jax.ShapeDtypeStruct((bsz, 8, LANE), F32),
                   jax.ShapeDtypeStruct((bsz, 8, LANE), F32)],
        scratch_shapes=[pltpu.VMEM(((CONV_WIDTH - 1) * CONV_PAD, CONV_DIM), F32),
                        pltpu.VMEM((t, D_SSM), F32),
                        pltpu.VMEM((t, D_SSM // SSM_GROUPS), F32)],
        compiler_params=_cparams(("parallel", "arbitrary")),
        name="mixers_prompt",
    )(sinks, proj, proj, proj, c, s1, s2,
      proj, proj, proj, cw, cb, dtb, alog, dskip_rep, wns,
      proj, proj, proj, proj, bi, bf, wnm)


def _stacked_out(prev, depth, shape, block, layer):
    spec = pl.BlockSpec((None,) + block, lambda i: (layer,) + (i,) + (0,) * (len(block) - 1))
    out_shape = jax.ShapeDtypeStruct((depth,) + shape, F32)
    extra_in = [] if prev is None else [prev]
    extra_spec = [] if prev is None else [pl.BlockSpec(memory_space=pl.ANY)]
    return spec, out_shape, extra_in, extra_spec


def _drop_aliased(kernel_fn, n_in, n_prev):
    def wrapped(*refs):
        return kernel_fn(*refs[:n_in], *refs[n_in + n_prev:])
    return wrapped


def _attn_sample_kernel(q_ref, k_ref, v_ref, kc_ref, vc_ref, sink_ref, c_ref, s1_ref, s2_ref,
                        o_ref, ko_ref, vo_ref):
    w = kc_ref.shape[1]
    c = c_ref[...]
    s1 = s1_ref[...]
    s2 = s2_ref[...]
    qb = _rope(q_ref[...], c, s1, s2)
    kn = _rope(k_ref[...], c, s1, s2)
    vn = v_ref[...]
    kc = kc_ref[...]
    vc = vc_ref[...]
    ko_ref[:, 0:w - 1, :] = kc[:, 1:w, :]
    ko_ref[:, w - 1:w, :] = kn
    vo_ref[:, 0:w - 1, :] = vc[:, 1:w, :]
    vo_ref[:, w - 1:w, :] = vn

    scale = ATT_HEAD_DIM ** -0.5
    s = jnp.einsum("bhd,bkd->bhk", qb.astype(BF16), kc.astype(BF16),
                   preferred_element_type=F32) * scale
    sn = jnp.sum(qb * kn, axis=-1, keepdims=True) * scale
    sink = sink_ref[...][None, :, 0:1]
    m = jnp.maximum(jnp.maximum(jnp.max(s, axis=-1, keepdims=True), sn), sink)
    p = jnp.exp(s - m)
    pn = jnp.exp(sn - m)
    denom = jnp.sum(p, axis=-1, keepdims=True) + pn + jnp.exp(sink - m)
    p = p / denom
    pn = pn / denom
    o = jnp.einsum("bhk,bkd->bhd", p.astype(BF16), vc.astype(BF16), preferred_element_type=F32)
    o = o + pn * vn
    head = lax.broadcasted_iota(jnp.int32, o.shape[:2] + (ATT_HEAD_DIM,), 1)
    o_ref[...] = jnp.where(head < ATT_GROUP, o[:, :, 0:ATT_HEAD_DIM], o[:, :, ATT_HEAD_DIM:]).astype(o_ref.dtype)


def _attn_sample(qb, kn, vn, kc, vc, sink_rows, tables, layer, prev, tb):
    depth, bd, w, _ = kc.shape
    c, s1, s2 = tables
    blk3 = lambda r, n: pl.BlockSpec((tb, r, n), lambda i: (i, 0, 0))
    cache = pl.BlockSpec((None, tb, w, LANE), lambda i: (layer, i, 0, 0))
    vec = pl.BlockSpec((1, LANE), lambda i: (0, 0))
    prev_k, prev_v = (None, None) if prev is None else prev
    k_spec, k_shape, k_in, k_in_spec = _stacked_out(prev_k, depth, (bd, w, LANE), (tb, w, LANE), layer)
    v_spec, v_shape, v_in, v_in_spec = _stacked_out(prev_v, depth, (bd, w, LANE), (tb, w, LANE), layer)
    n_in = 9
    n_prev = len(k_in) + len(v_in)
    return pl.pallas_call(
        _drop_aliased(_attn_sample_kernel, n_in, n_prev),
        grid=(bd // tb,),
        in_specs=[blk3(ATT_HEADS, LANE), blk3(1, LANE), blk3(1, LANE), cache, cache,
                  pl.BlockSpec((ATT_HEADS, LANE), lambda i: (0, 0)), vec, vec, vec] + k_in_spec + v_in_spec,
        out_specs=[blk3(ATT_HEADS, ATT_HEAD_DIM), k_spec, v_spec],
        out_shape=[jax.ShapeDtypeStruct((bd, ATT_HEADS, ATT_HEAD_DIM), BF16), k_shape, v_shape],
        input_output_aliases={n_in + t: 1 + t for t in range(n_prev)},
        compiler_params=_cparams(("parallel",)),
        name="attn_sample",
    )(qb, kn, vn, kc, vc, sink_rows, c, s1, s2, *k_in, *v_in)


def _lane_place(cols, lane0):
    m = cols[0].shape[0]
    lane = lax.broadcasted_iota(jnp.int32, (m, LANE), 1)
    out = jnp.zeros((m, LANE), F32)
    for i, col in enumerate(cols):
        out = jnp.where(lane == lane0 + i, col, out)
    return out


def _sample_pre_kernel(proj_ref, cs_ref, cw_ref, cb_ref, dtb_ref, alog_ref, bi_ref, bf_ref, n_ref, m_ref,
                       conv_ref, xs_ref, xdt_ref, bm_ref, cm_ref, ea_ref,
                       ks_ref, nn_ref, g_ref, esw_ref, sw_ref, den_ref, mt_ref):
    xbc = proj_ref[:, C_XBC:C_XBC + CONV_DIM]
    s0 = cs_ref[:, 0:CONV_DIM]
    s1 = cs_ref[:, CONV_DIM:2 * CONV_DIM]
    s2 = cs_ref[:, 2 * CONV_DIM:3 * CONV_DIM]
    cw = cw_ref[...]
    acc = s0 * cw[0:1, :]
    acc = acc + s1 * cw[1:2, :]
    acc = acc + s2 * cw[2:3, :]
    acc = acc + xbc * cw[3:4, :]
    xc = _silu(acc + cb_ref[...])
    conv_ref[:, 0:CONV_DIM] = s1
    conv_ref[:, CONV_DIM:2 * CONV_DIM] = s2
    conv_ref[:, 2 * CONV_DIM:3 * CONV_DIM] = xbc

    xs = xc[:, 0:D_SSM]
    xs_ref[...] = xs
    bm_ref[...] = xc[:, D_SSM:D_SSM + SSM_GROUPS * D_STATE]
    cm_ref[...] = xc[:, D_SSM + SSM_GROUPS * D_STATE:]
    dt = jax.nn.softplus(proj_ref[:, C_G:C_G + LANE] + dtb_ref[...])
    ea_ref[...] = jnp.exp(dt * (-jnp.exp(alog_ref[...])))
    hp = SSM_HEAD_DIM
    for h in range(SSM_HEADS):
        xdt_ref[:, h * hp:(h + 1) * hp] = xs[:, h * hp:(h + 1) * hp] * dt[:, h:h + 1]

    d = MLSTM_HEAD_DIM
    gi = proj_ref[:, C_G:C_G + LANE] + bi_ref[...]
    gf = proj_ref[:, C_G + LANE:C_G + 2 * LANE] + bf_ref[...]
    log_inter = jax.nn.log_sigmoid(gf) + m_ref[...]
    mt = jnp.maximum(log_inter, gi)
    gq = jnp.exp(log_inter - mt)
    esw = jnp.exp(gi - mt)
    ks = proj_ref[:, C_MK:C_MK + D_MLSTM] * (d ** -0.5)
    ks_ref[...] = ks
    qk_cols = []
    qn_cols = []
    for h in range(MLSTM_HEADS):
        qh = proj_ref[:, C_MQ + h * d:C_MQ + (h + 1) * d]
        kh = ks[:, h * d:(h + 1) * d]
        nh = n_ref[:, h * d:(h + 1) * d]
        qk_cols.append(jnp.sum(qh * kh, axis=-1, keepdims=True))
        qn_cols.append(jnp.sum(qh * nh, axis=-1, keepdims=True))
        ln = GATE_LANE + h
        nn_ref[:, h * d:(h + 1) * d] = nh * gq[:, ln:ln + 1] + kh * esw[:, ln:ln + 1]
    sw = esw * _lane_place(qk_cols, GATE_LANE)
    den = sw + _lane_place(qn_cols, GATE_LANE) * gq
    g_ref[...] = gq
    esw_ref[...] = esw
    sw_ref[...] = sw
    den_ref[...] = jnp.maximum(jnp.abs(den), jnp.exp(-mt))
    mt_ref[...] = mt


def _sample_pre(proj, conv_state, cw, cb, dtb, alog, bi, bf, n_state, m_tile):
    bd = proj.shape[0]
    tile = jax.ShapeDtypeStruct((bd, LANE), F32)
    outs = [jax.ShapeDtypeStruct((bd, (CONV_WIDTH - 1) * CONV_DIM), F32),
            jax.ShapeDtypeStruct((bd, D_SSM), F32),
            jax.ShapeDtypeStruct((bd, D_SSM), F32),
            jax.ShapeDtypeStruct((bd, SSM_GROUPS * D_STATE), F32),
            jax.ShapeDtypeStruct((bd, SSM_GROUPS * D_STATE), F32),
            tile,
            jax.ShapeDtypeStruct((bd, D_MLSTM), F32),
            jax.ShapeDtypeStruct((bd, D_MLSTM), F32),
            tile, tile, tile, tile, tile]
    return pl.pallas_call(
        _sample_pre_kernel,
        out_shape=outs,
        compiler_params=pltpu.CompilerParams(vmem_limit_bytes=VMEM_LIMIT),
        name="sample_pre",
    )(proj, conv_state, cw, cb, dtb, alog, bi, bf, n_state, m_tile)


def _column_tile(row):
    return jnp.broadcast_to(row, (LANE, LANE)).T


def _ssm_sample_one(b, gb, ea_ref, s_ref, xdt_ref, b_ref, c_ref, so_ref, y_ref):
    hp = SSM_HEAD_DIM
    heads_per_tile = LANE // hp
    n_tiles = D_SSM // LANE
    tiles_per_group = n_tiles // SSM_GROUPS
    top = lax.broadcasted_iota(jnp.int32, (LANE, 1), 0) < hp
    xrow = xdt_ref[b]
    brow = b_ref[b]
    crow = c_ref[b]
    for t in range(n_tiles):
        g = t // tiles_per_group
        xcol = _column_tile(xrow[:, t * LANE:(t + 1) * LANE])
        bg = brow[:, g * D_STATE:(g + 1) * D_STATE]
        cg = jnp.broadcast_to(crow[:, g * D_STATE:(g + 1) * D_STATE], (8, D_STATE))
        e0 = ea_ref[gb, heads_per_tile * t]
        e1 = ea_ref[gb, heads_per_tile * t + 1]
        decay = jnp.where(top, e0, e1)
        sn = s_ref[b, t * LANE:(t + 1) * LANE, :] * decay + xcol * bg
        so_ref[b, t * LANE:(t + 1) * LANE, :] = sn
        y_ref[b, :, t * LANE:(t + 1) * LANE] = _dot_nt(cg, sn)[0:1, :]


def _ssm_sample_kernel(ea_ref, s_ref, xdt_ref, b_ref, c_ref, so_ref, y_ref):
    tb = s_ref.shape[0]
    base = pl.program_id(0) * tb

    def body(b, carry):
        _ssm_sample_one(b, base + b, ea_ref, s_ref, xdt_ref, b_ref, c_ref, so_ref, y_ref)
        return carry

    lax.fori_loop(0, tb, body, 0)


def _ssm_sample(ea, s, xdt3, b3, c3, layer, prev, tb):
    depth, bd = s.shape[0], s.shape[1]
    blk3 = lambda r, n: pl.BlockSpec((tb, r, n), lambda i: (i, 0, 0))
    s_spec, s_shape, s_in, s_in_spec = _stacked_out(prev, depth, (bd, D_SSM, D_STATE), (tb, D_SSM, D_STATE), layer)
    n_in = 5
    return pl.pallas_call(
        _drop_aliased(_ssm_sample_kernel, n_in, len(s_in)),
        grid=(bd // tb,),
        in_specs=[pl.BlockSpec(memory_space=pltpu.SMEM),
                  pl.BlockSpec((None, tb, D_SSM, D_STATE), lambda i: (layer, i, 0, 0)),
                  blk3(1, D_SSM), blk3(1, SSM_GROUPS * D_STATE),
                  blk3(1, SSM_GROUPS * D_STATE)] + s_in_spec,
        out_specs=[s_spec, blk3(1, D_SSM)],
        out_shape=[s_shape, jax.ShapeDtypeStruct((bd, 1, D_SSM), F32)],
        input_output_aliases={n_in + t: t for t in range(len(s_in))},
        compiler_params=_cparams(("parallel",)),
        name="ssm_sample",
    )(ea, s, xdt3, b3, c3, *s_in)


def _mlstm_sample_one(b, gb, g_ref, esw_ref, c_ref, q_ref, k_ref, v_ref, co_ref, qc_ref):
    d = MLSTM_HEAD_DIM
    qrow = q_ref[b]
    krow = k_ref[b]
    vrow = v_ref[b]
    for h in range(MLSTM_HEADS):
        qcol = _column_tile(qrow[:, h * d:(h + 1) * d])
        kcol = _column_tile(krow[:, h * d:(h + 1) * d])
        ch = c_ref[b, h * d:(h + 1) * d, :]
        g_end = g_ref[gb, GATE_LANE + h]
        wk = esw_ref[gb, GATE_LANE + h]
        qc_ref[b, :, h * d:(h + 1) * d] = jnp.sum(qcol * ch, axis=0, keepdims=True)
        co_ref[b, h * d:(h + 1) * d, :] = ch * g_end + (kcol * wk) * vrow[:, h * d:(h + 1) * d]


def _mlstm_sample_kernel(g_ref, esw_ref, c_ref, q_ref, k_ref, v_ref, co_ref, qc_ref):
    tb = c_ref.shape[0]
    base = pl.program_id(0) * tb

    def body(b, carry):
        _mlstm_sample_one(b, base + b, g_ref, esw_ref, c_ref, q_ref, k_ref, v_ref, co_ref, qc_ref)
        return carry

    lax.fori_loop(0, tb, body, 0)


def _mlstm_sample(gq, esw, c, q3, k3, v3, layer, prev, tb):
    depth, bd = c.shape[0], c.shape[1]
    blk3 = lambda r, n: pl.BlockSpec((tb, r, n), lambda i: (i, 0, 0))
    smem = pl.BlockSpec(memory_space=pltpu.SMEM)
    c_spec, c_shape, c_in, c_in_spec = _stacked_out(prev, depth, (bd, D_MLSTM, MLSTM_HEAD_DIM),
                                                    (tb, D_MLSTM, MLSTM_HEAD_DIM), layer)
    n_in = 6
    return pl.pallas_call(
        _drop_aliased(_mlstm_sample_kernel, n_in, len(c_in)),
        grid=(bd // tb,),
        in_specs=[smem, smem, pl.BlockSpec((None, tb, D_MLSTM, MLSTM_HEAD_DIM), lambda i: (layer, i, 0, 0)),
                  blk3(1, D_MLSTM), blk3(1, D_MLSTM), blk3(1, D_MLSTM)] + c_in_spec,
        out_specs=[c_spec, blk3(1, D_MLSTM)],
        out_shape=[c_shape, jax.ShapeDtypeStruct((bd, 1, D_MLSTM), F32)],
        input_output_aliases={n_in + t: t for t in range(len(c_in))},
        compiler_params=_cparams(("parallel",)),
        name="mlstm_sample",
    )(gq, esw, c, q3, k3, v3, *c_in)


def _sample_post_kernel(proj_ref, y_ref, xs_ref, dskip_ref, wns_ref, qc_ref, g_ref, sw_ref, den_ref, wnm_ref,
                        ys_ref, hs_ref):
    y = y_ref[...] + dskip_ref[...] * xs_ref[...]
    y = y * _silu(proj_ref[:, C_Z:C_Z + D_SSM])
    ys_ref[...] = _group_rmsnorm(y, wns_ref[...], SSM_GROUPS).astype(ys_ref.dtype)

    d = MLSTM_HEAD_DIM
    for h in range(MLSTM_HEADS):
        ln = GATE_LANE + h
        vh = proj_ref[:, C_MV + h * d:C_MV + (h + 1) * d]
        num = sw_ref[:, ln:ln + 1] * vh + qc_ref[:, h * d:(h + 1) * d] * g_ref[:, ln:ln + 1]
        hh = num / den_ref[:, ln:ln + 1]
        ms = jnp.mean(hh * hh, axis=-1, keepdims=True)
        hn = hh * lax.rsqrt(ms + EPS) * wnm_ref[:, h * d:(h + 1) * d]
        gate = jax.nn.sigmoid(proj_ref[:, C_MO + h * d:C_MO + (h + 1) * d])
        hs_ref[:, h * d:(h + 1) * d] = (hn * gate).astype(hs_ref.dtype)


def _sample_post(proj, y, xs, dskip_rep, wns, qc, gq, sw, den, wnm):
    bd = proj.shape[0]
    return pl.pallas_call(
        _sample_post_kernel,
        out_shape=[jax.ShapeDtypeStruct((bd, D_SSM), BF16), jax.ShapeDtypeStruct((bd, D_MLSTM), BF16)],
        compiler_params=pltpu.CompilerParams(vmem_limit_bytes=VMEM_LIMIT),
        name="sample_post",
    )(proj, y, xs, dskip_rep, wns, qc, gq, sw, den, wnm)


def _lane_vec(v, lane0):
    return jnp.pad(v.astype(F32), (lane0, LANE - lane0 - v.shape[0]))[None, :]


def _pick(n, candidates):
    for c in candidates:
        if n % c == 0:
            return c
    return n


def kernel(x_prompt, x_sample, cache_swa_k, cache_swa_v, state_conv, state_ssm, state_mlstm_C, state_mlstm_n,
           state_mlstm_m, w_norm_mix, w_in, attn_sinks, conv_w, conv_b, dt_bias, a_log, d_skip, w_norm_ssm,
           igate_b, fgate_b, w_norm_mlstm, w_out, w_norm_mlp, w_up, w_down, w_norm_final):
    bsz, seq, d_model = x_prompt.shape
    bd = x_sample.shape[0]
    depth = w_in.shape[0]
    win = cache_swa_k.shape[2]
    assert x_sample.shape[1] == 1 and seq % CHUNK == 0 and d_model == D_MODEL

    mp = bsz * seq
    tm_p = _pick(mp, (1024, 512, 256, 128))
    tm_mlp = _pick(mp, (1024, 512, 256, 128))
    tb = _pick(bd, (8,))

    hp = x_prompt.reshape(mp, d_model)
    hs = x_sample.reshape(bd, d_model)
    tab_p = _rope_tables(jnp.arange(seq, dtype=jnp.int32))
    tab_s = _rope_tables(jnp.arange(1, dtype=jnp.int32) + PAST_LEN)
    gf = w_norm_final[None, :]
    kc_all = cache_swa_k.reshape(depth, bd, win, D_KV)
    vc_all = cache_swa_v.reshape(depth, bd, win, D_KV)
    ssm_all = state_ssm.reshape(depth, bd, D_SSM, D_STATE)
    mem_all = state_mlstm_C.reshape(depth, bd, D_MLSTM, MLSTM_HEAD_DIM)

    st_p, st_s = [], []
    kv_new = s_new = c_new = None
    for l in range(depth):
        w_in_l = _prep_w_in(w_in, l, 256)
        g_mix = w_norm_mix[l][None, :]
        g_mlp = w_norm_mlp[l][None, :]
        cw = conv_w[l]
        cb = conv_b[l][None, :]
        dtb = _lane_vec(dt_bias[l], 0)
        alog = _lane_vec(a_log[l], 0)
        dskip_rep = jnp.repeat(d_skip[l].astype(F32), SSM_HEAD_DIM)[None, :]
        wns = w_norm_ssm[l][None, :]
        bi = _lane_vec(igate_b[l], GATE_LANE)
        bf = _lane_vec(fgate_b[l], GATE_LANE)
        wnm = w_norm_mlstm[l][None, :]
        sinks = attn_sinks[l].reshape(ATT_HEADS).astype(F32)
        last = l == depth - 1

        proj = _rms_matmul(hp, g_mix, w_in_l, tm_p, NP // 4)
        att, pk, pv, y, p_ssm, hm, p_c, p_n, p_m = _mixers_prompt(
            proj, sinks, tab_p, cw, cb, dtb, alog, dskip_rep, wns, bi, bf, wnm, bsz, seq)
        x1 = _outproj(att, y, hm, hp, w_out, l, _pick(mp, (512, 256, 128)))
        p_conv = proj.reshape(bsz, seq, NP)[:, seq - (CONV_WIDTH - 1):, C_XBC:C_XBC + CONV_DIM]
        st_p.append((pk.reshape(bsz, WINDOW, ATT_KV_HEADS, ATT_HEAD_DIM),
                     pv.reshape(bsz, WINDOW, ATT_KV_HEADS, ATT_HEAD_DIM),
                     p_conv,
                     p_ssm.reshape(bsz, SSM_HEADS, SSM_HEAD_DIM, D_STATE),
                     p_c.reshape(bsz, MLSTM_HEADS, MLSTM_HEAD_DIM, MLSTM_HEAD_DIM),
                     p_n[:, 0:MLSTM_HEADS, :],
                     p_m[:, 0:MLSTM_HEADS, 0]))

        proj_s = _rms_matmul(hs, g_mix, w_in_l, bd, NP // 4)
        q8 = proj_s[:, C_Q:C_Q + D_ATT].reshape(bd, ATT_HEADS, ATT_HEAD_DIM)
        zero = jnp.zeros_like(q8)
        first_kv = (jnp.arange(ATT_HEADS) < ATT_GROUP)[None, :, None]
        qb = jnp.where(first_kv, jnp.concatenate([q8, zero], -1), jnp.concatenate([zero, q8], -1))
        kn = proj_s[:, C_K:C_K + D_KV].reshape(bd, 1, D_KV)
        vn = proj_s[:, C_V:C_V + D_KV].reshape(bd, 1, D_KV)
        sink_rows = jnp.broadcast_to(sinks[:, None], (ATT_HEADS, LANE))
        att_s, sk, sv = _attn_sample(qb, kn, vn, kc_all, vc_all, sink_rows, tab_s, l, kv_new, tb)
        kv_new = (sk, sv)

        m_tile = jnp.pad(state_mlstm_m[l], ((0, 0), (GATE_LANE, LANE - GATE_LANE - MLSTM_HEADS)))
        (conv_new, xs, xdt, bm, cm, ea, ks, n_new, gq, esw, sw, den, mt) = _sample_pre(
            proj_s, state_conv[l].reshape(bd, (CONV_WIDTH - 1) * CONV_DIM), cw, cb, dtb, alog, bi, bf,
            state_mlstm_n[l].reshape(bd, D_MLSTM), m_tile)
        xdt3 = xdt.reshape(bd, 1, D_SSM)
        b3 = bm.reshape(bd, 1, SSM_GROUPS * D_STATE)
        c3 = cm.reshape(bd, 1, SSM_GROUPS * D_STATE)
        q3 = proj_s[:, C_MQ:C_MQ + D_MLSTM].reshape(bd, 1, D_MLSTM)
        k3 = ks.reshape(bd, 1, D_MLSTM)
        v3 = proj_s[:, C_MV:C_MV + D_MLSTM].reshape(bd, 1, D_MLSTM)
        tf = 512
        if (mp // tm_mlp) * (w_up.shape[2] // tf) == bd:
            side = (ea, ssm_all, xdt3, b3, c3, gq, esw, mem_all, q3, k3, v3, s_new, c_new)
            hp, s_new, y3, c_new, qc3 = _mlp(x1, g_mlp, w_up, w_down, gf, l, last, tm_mlp, tf, side)
        else:
            hp = _mlp(x1, g_mlp, w_up, w_down, gf, l, last, tm_mlp, tf)
            s_new, y3 = _ssm_sample(ea, ssm_all, xdt3, b3, c3, l, s_new, tb)
            c_new, qc3 = _mlstm_sample(gq, esw, mem_all, q3, k3, v3, l, c_new, tb)
        ys, hms = _sample_post(proj_s, y3.reshape(bd, D_SSM), xs, dskip_rep, wns, qc3.reshape(bd, D_MLSTM),
                               gq, sw, den, wnm)
        x1s = _outproj(att_s.reshape(bd, D_ATT), ys, hms, hs, w_out, l, bd)
        hs = _mlp(x1s, g_mlp, w_up, w_down, gf, l, last, bd, 512)
        st_s.append((conv_new.reshape(bd, CONV_WIDTH - 1, CONV_DIM),
                     n_new.reshape(bd, MLSTM_HEADS, MLSTM_HEAD_DIM),
                     mt[:, GATE_LANE:GATE_LANE + MLSTM_HEADS]))

    y_prompt = hp.reshape(bsz, seq, d_model)
    y_sample = hs.reshape(bd, 1, d_model)
    p_out = [jnp.stack([s[i] for s in st_p]) for i in range(7)]
    s_conv, s_n, s_m = [jnp.stack([s[i] for s in st_s]) for i in range(3)]
    s_k = kv_new[0].reshape(depth, bd, win, ATT_KV_HEADS, ATT_HEAD_DIM)
    s_v = kv_new[1].reshape(depth, bd, win, ATT_KV_HEADS, ATT_HEAD_DIM)
    s_ssm = s_new.reshape(depth, bd, SSM_HEADS, SSM_HEAD_DIM, D_STATE)
    s_c = c_new.reshape(depth, bd, MLSTM_HEADS, MLSTM_HEAD_DIM, MLSTM_HEAD_DIM)
    return (y_prompt, y_sample, *p_out, s_k, s_v, s_conv, s_ssm, s_c, s_n, s_m)
```

```python
import functools
import math

import jax
import jax.numpy as jnp
from jax import lax
from jax.experimental import pallas as pl
from jax.experimental.pallas import tpu as pltpu

F32 = jnp.float32
BF16 = jnp.bfloat16
HIGHEST = lax.Precision.HIGHEST

D_MODEL = 2048
EPS = 1e-6
PAST_LEN = 8192
ATT_HEAD_DIM = 64
ATT_HEADS = 8
ATT_KV_HEADS = 2
ATT_GROUP = ATT_HEADS // ATT_KV_HEADS
D_ATT = ATT_HEADS * ATT_HEAD_DIM
D_KV = ATT_KV_HEADS * ATT_HEAD_DIM
WINDOW = 128
ROPE_THETA = 500000.0
ROPE_DIM = ATT_HEAD_DIM // 4
SSM_HEAD_DIM = 64
SSM_HEADS = 16
D_SSM = SSM_HEADS * SSM_HEAD_DIM
SSM_GROUPS = 2
D_STATE = 128
CONV_WIDTH = 4
CONV_DIM = D_SSM + 2 * SSM_GROUPS * D_STATE
CHUNK = 128
MLSTM_HEADS = 4
MLSTM_HEAD_DIM = 128
D_MLSTM = MLSTM_HEADS * MLSTM_HEAD_DIM
D_FF = 4 * D_MODEL

C_XBC = 0
C_Q = 1536
C_Z = 2048
C_MQ = 3072
C_MK = 3584
C_MV = 4096
C_MO = 4608
C_K = 5120
C_V = 5248
C_G = 5376
NP = 5632
GATE_LANE = 16

LANE = 128
VMEM_LIMIT = 60 * 1024 * 1024


def _cparams(sem):
    return pltpu.CompilerParams(dimension_semantics=sem, vmem_limit_bytes=VMEM_LIMIT)


def _silu(x):
    return x * jax.nn.sigmoid(x)


def _dot(a, b):
    return jnp.dot(a.astype(BF16), b.astype(BF16), preferred_element_type=F32)


def _dot_nt(a, b):
    return lax.dot_general(a.astype(BF16), b.astype(BF16), (((1,), (1,)), ((), ())),
                           preferred_element_type=F32)


def _dot_tn(a, b):
    return lax.dot_general(a.astype(BF16), b.astype(BF16), (((0,), (0,)), ((), ())),
                           preferred_element_type=F32)


def _tril(n):
    row = lax.broadcasted_iota(jnp.int32, (n, n), 0)
    col = lax.broadcasted_iota(jnp.int32, (n, n), 1)
    return row >= col


def _cumsum_rows(x):
    t = _tril(x.shape[0]).astype(F32)
    return jnp.dot(t, x, precision=HIGHEST, preferred_element_type=F32)


def _rms_matmul_kernel(x_ref, g_ref, w_ref, o_ref, u_ref):
    @pl.when(pl.program_id(1) == 0)
    def _():
        x = x_ref[...]
        ms = jnp.mean(x * x, axis=-1, keepdims=True)
        u_ref[...] = (x * lax.rsqrt(ms + EPS) * g_ref[...]).astype(BF16)

    o_ref[...] = jnp.dot(u_ref[...], w_ref[...], preferred_element_type=F32)


def _rms_matmul(x, g, w, tm, tn):
    m, k = x.shape
    n = w.shape[1]
    return pl.pallas_call(
        _rms_matmul_kernel,
        grid=(m // tm, n // tn),
        in_specs=[pl.BlockSpec((tm, k), lambda i, j: (i, 0)),
                  pl.BlockSpec((1, k), lambda i, j: (0, 0)),
                  pl.BlockSpec((k, tn), lambda i, j: (0, j))],
        out_specs=pl.BlockSpec((tm, tn), lambda i, j: (i, j)),
        out_shape=jax.ShapeDtypeStruct((m, n), F32),
        scratch_shapes=[pltpu.VMEM((tm, k), BF16)],
        compiler_params=_cparams(("parallel", "arbitrary")),
        name="rms_inproj",
    )(x, g, w)


_SRC_Q, _SRC_K, _SRC_V, _SRC_Z, _SRC_XBC, _SRC_DT, _SRC_MQ = 0, 512, 640, 768, 1792, 3328, 3344
_SRC_MI = _SRC_MQ + 4 * D_MLSTM
IN_WIDTH = _SRC_MI + 2 * MLSTM_HEADS


def _prep_w_in_kernel(w_ref, o_ref):
    def put(dst, src, n):
        o_ref[:, dst:dst + n] = w_ref[src:src + n, :].T.astype(BF16)

    put(C_XBC, _SRC_XBC, CONV_DIM)
    put(C_Q, _SRC_Q, D_ATT)
    put(C_Z, _SRC_Z, D_SSM)
    put(C_MQ, _SRC_MQ, 4 * D_MLSTM)
    put(C_K, _SRC_K, D_KV)
    put(C_V, _SRC_V, D_KV)
    tk = w_ref.shape[1]
    dt = w_ref[_SRC_DT:_SRC_DT + SSM_HEADS, :]
    gates = w_ref[_SRC_MI:_SRC_MI + 2 * MLSTM_HEADS, :]
    pad = jnp.zeros((LANE - GATE_LANE - 2 * MLSTM_HEADS, tk), F32)
    lo = jnp.concatenate([dt, gates, pad], axis=0)
    hi = jnp.concatenate([jnp.zeros_like(dt), pltpu.roll(gates, MLSTM_HEADS, 0), pad], axis=0)
    o_ref[:, C_G:C_G + LANE] = lo.T.astype(BF16)
    o_ref[:, C_G + LANE:C_G + 2 * LANE] = hi.T.astype(BF16)


def _prep_w_in(w_in, layer, tk):
    depth, k, n = w_in.shape
    assert n == IN_WIDTH and n % 8 == 0
    wt = jnp.swapaxes(w_in, 1, 2).reshape(depth * n, k)
    return pl.pallas_call(
        _prep_w_in_kernel,
        grid=(k // tk,),
        in_specs=[pl.BlockSpec((n, tk), lambda i: (layer, i))],
        out_specs=pl.BlockSpec((tk, NP), lambda i: (i, 0)),
        out_shape=jax.ShapeDtypeStruct((k, NP), BF16),
        compiler_params=_cparams(("parallel",)),
        name="prep_w_in",
    )(wt)


def _outproj_kernel(a_ref, y_ref, h_ref, x_ref, w_ref, o_ref, wb_ref):
    @pl.when(pl.program_id(0) == 0)
    def _():
        wb_ref[...] = w_ref[...].astype(BF16)

    acc = jnp.dot(a_ref[...], wb_ref[0:D_ATT, :], preferred_element_type=F32)
    acc = acc + jnp.dot(y_ref[...], wb_ref[D_ATT:D_ATT + D_SSM, :], preferred_element_type=F32)
    acc = acc + jnp.dot(h_ref[...], wb_ref[D_ATT + D_SSM:, :], preferred_element_type=F32)
    o_ref[...] = x_ref[...] + acc


def _outproj(att, y, h, x, w, layer, tm):
    m, n = x.shape
    k = w.shape[1]
    rows = lambda width: pl.BlockSpec((tm, width), lambda i: (i, 0))
    return pl.pallas_call(
        _outproj_kernel,
        grid=(m // tm,),
        in_specs=[rows(D_ATT), rows(D_SSM), rows(D_MLSTM), rows(n),
                  pl.BlockSpec((None, k, n), lambda i: (layer, 0, 0), pipeline_mode=pl.Buffered(1))],
        out_specs=rows(n),
        out_shape=jax.ShapeDtypeStruct((m, n), F32),
        scratch_shapes=[pltpu.VMEM((k, n), BF16)],
        compiler_params=_cparams(("arbitrary",)),
        name="outproj",
    )(att, y, h, x, w)


N_MLP_IN, N_SSM_SIDE_IN, N_MLSTM_SIDE_IN = 5, 5, 6


def _mlp_kernel(*refs, final_norm, side, n_prev):
    x_ref, g_ref, wu_ref, wd_ref, gf_ref = refs[:N_MLP_IN]
    n_side = N_SSM_SIDE_IN + N_MLSTM_SIDE_IN if side else 0
    side_in = refs[N_MLP_IN:N_MLP_IN + n_side]
    outs = refs[N_MLP_IN + n_side + n_prev:]
    o_ref, u_ref = outs[0], outs[-1]
    j = pl.program_id(1)

    @pl.when(j == 0)
    def _():
        x = x_ref[...]
        ms = jnp.mean(x * x, axis=-1, keepdims=True)
        u_ref[...] = (x * lax.rsqrt(ms + EPS) * g_ref[...]).astype(BF16)
        o_ref[...] = x

    h = _dot(u_ref[...], wu_ref[...])
    h = jnp.square(jnp.maximum(h, 0.0))
    o_ref[...] += _dot(h, wd_ref[...])

    if side:
        seq = pl.program_id(0) * pl.num_programs(1) + j
        so_ref, y_ref, co_ref, qc_ref = outs[1:5]
        _ssm_sample_one(0, seq, *side_in[:N_SSM_SIDE_IN], so_ref, y_ref)
        _mlstm_sample_one(0, seq, *side_in[N_SSM_SIDE_IN:], co_ref, qc_ref)

    if final_norm:
        @pl.when(j == pl.num_programs(1) - 1)
        def _():
            y = o_ref[...]
            ms = jnp.mean(y * y, axis=-1, keepdims=True)
            o_ref[...] = y * lax.rsqrt(ms + EPS) * gf_ref[...]


def _mlp(x, g, wu, wd, gf, layer, final_norm, tm, tf, side=None):
    m, d = x.shape
    ff = wu.shape[2]
    nj = ff // tf
    in_specs = [pl.BlockSpec((tm, d), lambda i, j: (i, 0)),
                pl.BlockSpec((1, d), lambda i, j: (0, 0)),
                pl.BlockSpec((None, d, tf), lambda i, j: (layer, 0, j)),
                pl.BlockSpec((None, tf, d), lambda i, j: (layer, j, 0)),
                pl.BlockSpec((1, d), lambda i, j: (0, 0))]
    out_specs = [pl.BlockSpec((tm, d), lambda i, j: (i, 0))]
    out_shape = [jax.ShapeDtypeStruct((m, d), F32)]
    operands = [x, g, wu, wd, gf]
    aliases = {}
    n_prev = 0
    if side is not None:
        ea, ssm, xdt3, b3, c3, gq, esw, mem, q3, k3, v3, prev_s, prev_c = side
        depth, bd = ssm.shape[0], ssm.shape[1]
        assert (m // tm) * nj == bd
        smem = pl.BlockSpec(memory_space=pltpu.SMEM)
        one = lambda n: pl.BlockSpec((1, 1, n), lambda i, j: (i * nj + j, 0, 0))
        state = lambda r, n: pl.BlockSpec((None, 1, r, n), lambda i, j: (layer, i * nj + j, 0, 0))
        in_specs += [smem, state(D_SSM, D_STATE), one(D_SSM), one(SSM_GROUPS * D_STATE), one(SSM_GROUPS * D_STATE),
                     smem, smem, state(D_MLSTM, MLSTM_HEAD_DIM), one(D_MLSTM), one(D_MLSTM), one(D_MLSTM)]
        operands += [ea, ssm, xdt3, b3, c3, gq, esw, mem, q3, k3, v3]
        out_specs += [state(D_SSM, D_STATE), one(D_SSM), state(D_MLSTM, MLSTM_HEAD_DIM), one(D_MLSTM)]
        out_shape += [jax.ShapeDtypeStruct((depth, bd, D_SSM, D_STATE), F32),
                      jax.ShapeDtypeStruct((bd, 1, D_SSM), F32),
                      jax.ShapeDtypeStruct((depth, bd, D_MLSTM, MLSTM_HEAD_DIM), F32),
                      jax.ShapeDtypeStruct((bd, 1, D_MLSTM), F32)]
        if prev_s is not None:
            aliases = {len(operands): 1, len(operands) + 1: 3}
            operands += [prev_s, prev_c]
            in_specs += [pl.BlockSpec(memory_space=pl.ANY)] * 2
            n_prev = 2
    res = pl.pallas_call(
        functools.partial(_mlp_kernel, final_norm=final_norm, side=side is not None, n_prev=n_prev),
        grid=(m // tm, nj),
        in_specs=in_specs,
        out_specs=out_specs,
        out_shape=out_shape,
        input_output_aliases=aliases,
        scratch_shapes=[pltpu.VMEM((tm, d), BF16)],
        compiler_params=_cparams(("parallel", "arbitrary")),
        name="mlp",
    )(*operands)
    return res[0] if side is None else res


def _rope_tables(pos):
    half = ROPE_DIM // 2
    inv = jnp.power(jnp.float32(ROPE_THETA), -jnp.arange(half, dtype=jnp.float32) / half)
    ang = pos.astype(jnp.float32)[:, None] * inv[None, :]
    cos = jnp.cos(ang)
    sin = jnp.sin(ang)
    n = pos.shape[0]
    rest = ATT_HEAD_DIM - ROPE_DIM
    c = jnp.concatenate([cos, cos, jnp.ones((n, rest), F32)], axis=1)
    s1 = jnp.concatenate([-sin, jnp.zeros((n, half + rest), F32)], axis=1)
    s2 = jnp.concatenate([jnp.zeros((n, half), F32), sin, jnp.zeros((n, rest), F32)], axis=1)
    rep = LANE // ATT_HEAD_DIM
    return jnp.tile(c, (1, rep)), jnp.tile(s1, (1, rep)), jnp.tile(s2, (1, rep))


def _rope(x, c, s1, s2):
    width = x.shape[-1]
    rep = width // LANE
    half = ROPE_DIM // 2
    if rep > 1:
        c = jnp.concatenate([c] * rep, axis=-1)
        s1 = jnp.concatenate([s1] * rep, axis=-1)
        s2 = jnp.concatenate([s2] * rep, axis=-1)
    axis = x.ndim - 1
    return x * c + pltpu.roll(x, width - half, axis) * s1 + pltpu.roll(x, half, axis) * s2


def _attn_prompt_init(ko_ref, vo_ref):
    ko_ref[...] = jnp.zeros_like(ko_ref)
    vo_ref[...] = jnp.zeros_like(vo_ref)


def _attn_prompt_body(j, sink_ref, q_ref, k_ref, v_ref, c_ref, s1_ref, s2_ref, o_ref, ko_ref, vo_ref):
    w = WINDOW
    c = c_ref[...]
    s1 = s1_ref[...]
    s2 = s2_ref[...]
    krot = _rope(k_ref[...], c, s1, s2)
    v = v_ref[...]
    qrot = _rope(q_ref[...], c, s1, s2)
    kk = jnp.concatenate([ko_ref[...], krot], axis=0).astype(BF16)
    vv = jnp.concatenate([vo_ref[...], v], axis=0).astype(BF16)

    row = lax.broadcasted_iota(jnp.int32, (w, 2 * w), 0)
    col = lax.broadcasted_iota(jnp.int32, (w, 2 * w), 1)
    first_col = jnp.where(j > 0, 0, w)
    mask = (col >= row) & (col <= row + w) & (col >= first_col)
    scale = ATT_HEAD_DIM ** -0.5

    for h in range(ATT_KV_HEADS):
        kh = kk[:, h * ATT_HEAD_DIM:(h + 1) * ATT_HEAD_DIM]
        vh = vv[:, h * ATT_HEAD_DIM:(h + 1) * ATT_HEAD_DIM]
        for g in range(ATT_GROUP):
            hg = h * ATT_GROUP + g
            qh = qrot[:, hg * ATT_HEAD_DIM:(hg + 1) * ATT_HEAD_DIM]
            s = _dot_nt(qh, kh) * scale
            s = jnp.where(mask, s, -jnp.inf)
            sink = sink_ref[hg]
            m = jnp.maximum(jnp.max(s, axis=-1, keepdims=True), sink)
            p = jnp.exp(s - m)
            denom = jnp.sum(p, axis=-1, keepdims=True) + jnp.exp(sink - m)
            p = p / denom
            o = _dot(p, vh)
            o_ref[:, hg * ATT_HEAD_DIM:(hg + 1) * ATT_HEAD_DIM] = o.astype(o_ref.dtype)
            yield

    ko_ref[...] = krot
    vo_ref[...] = v


def _group_rmsnorm(y, w, groups):
    width = y.shape[-1] // groups
    outs = []
    for g in range(groups):
        yg = y[:, g * width:(g + 1) * width]
        ms = jnp.mean(yg * yg, axis=-1, keepdims=True)
        outs.append(yg * lax.rsqrt(ms + EPS) * w[:, g * width:(g + 1) * width])
    return jnp.concatenate(outs, axis=-1)


CONV_PAD = 8


def _ssd_prompt_init(s_ref, tail_ref):
    tail_ref[...] = jnp.zeros_like(tail_ref)
    s_ref[...] = jnp.zeros_like(s_ref)


def _ssd_prompt_body(xbc_ref, z_ref, g_ref, cw_ref, cb_ref, dtb_ref, alog_ref, dskip_ref, wn_ref,
                     y_ref, s_ref, tail_ref, ybuf_ref, xw_ref):
    q = CHUNK
    pad = CONV_PAD
    hp = SSM_HEAD_DIM
    heads_per_group = SSM_HEADS // SSM_GROUPS
    gw = heads_per_group * hp

    x = xbc_ref[...]
    cw = cw_ref[...]
    row8 = lax.broadcasted_iota(jnp.int32, (pad, CONV_DIM), 0)
    acc = None
    for j in range(CONV_WIDTH - 1):
        shift = CONV_WIDTH - 1 - j
        rolled = pltpu.roll(x, shift, 0)
        head = jnp.where(row8 < shift, tail_ref[j * pad:(j + 1) * pad, :], rolled[0:pad, :])
        tail_ref[j * pad:(j + 1) * pad, :] = rolled[0:pad, :]
        term = jnp.concatenate([head, rolled[pad:, :]], axis=0) * cw[j:j + 1, :]
        acc = term if acc is None else acc + term
    acc = acc + x * cw[CONV_WIDTH - 1:CONV_WIDTH, :]
    xc = _silu(acc + cb_ref[...])

    xs = xc[:, 0:D_SSM]
    bm = xc[:, D_SSM:D_SSM + SSM_GROUPS * D_STATE]
    cm = xc[:, D_SSM + SSM_GROUPS * D_STATE:]

    dt = jax.nn.softplus(g_ref[:, 0:LANE] + dtb_ref[...])
    a_neg = -jnp.exp(alog_ref[...])
    a_col = _cumsum_rows(dt * a_neg)
    a_row = a_col.T
    dt_row = dt.T
    ea_col = jnp.exp(a_col)
    wk_col = jnp.exp(a_col[q - 1:q, :] - a_col) * dt
    tril = _tril(q)
    yield

    for g in range(SSM_GROUPS):
        bg = bm[:, g * D_STATE:(g + 1) * D_STATE]
        cg = cm[:, g * D_STATE:(g + 1) * D_STATE]
        cb = _dot_nt(cg, bg)
        cs = _dot_nt(cg, s_ref[g * gw:(g + 1) * gw, :])
        for r in range(heads_per_group):
            h = g * heads_per_group + r
            seg = a_col[:, h:h + 1] - a_row[h:h + 1, :]
            wmat = jnp.exp(jnp.where(tril, seg, -jnp.inf)) * cb * dt_row[h:h + 1, :]
            xh = xs[:, h * hp:(h + 1) * hp]
            yh = _dot(wmat, xh)
            yh = yh + cs[:, r * hp:(r + 1) * hp] * ea_col[:, h:h + 1]
            ybuf_ref[:, h * hp:(h + 1) * hp] = yh
            xw_ref[:, r * hp:(r + 1) * hp] = xh * wk_col[:, h:h + 1]
            yield
        upd = _dot_tn(xw_ref[...], bg)
        for r in range(heads_per_group):
            h = g * heads_per_group + r
            decay = jnp.exp(a_row[h:h + 1, q - 1:q])
            s_ref[h * hp:(h + 1) * hp, :] = s_ref[h * hp:(h + 1) * hp, :] * decay + upd[r * hp:(r + 1) * hp, :]
        yield

    y = ybuf_ref[...] + dskip_ref[...] * xs
    y = y * _silu(z_ref[...])
    y_ref[...] = _group_rmsnorm(y, wn_ref[...], SSM_GROUPS).astype(y_ref.dtype)


def _mlstm_prompt_init(c_ref, n_ref, m_ref):
    c_ref[...] = jnp.zeros_like(c_ref)
    n_ref[...] = jnp.zeros_like(n_ref)
    m_ref[...] = jnp.zeros_like(m_ref)


def _mlstm_prompt_body(q_ref, k_ref, v_ref, o_ref, g_ref, bi_ref, bf_ref, wn_ref, h_ref, c_ref, n_ref, m_ref):
    t = CHUNK
    d = MLSTM_HEAD_DIM

    gi = g_ref[:, 0:LANE] + bi_ref[...]
    gf = g_ref[:, LANE:2 * LANE] + bf_ref[...]
    b_col = _cumsum_rows(jax.nn.log_sigmoid(gf))
    b_row = b_col.T
    i_row = gi.T
    tril = _tril(t)
    k_all = k_ref[...] * (d ** -0.5)

    for h in range(MLSTM_HEADS):
        ln = GATE_LANE + h
        bq = b_col[:, ln:ln + 1]
        logw = jnp.where(tril, bq - b_row[ln:ln + 1, :] + i_row[ln:ln + 1, :], -jnp.inf)
        m_prev = m_ref[h:h + 1, 0:1]
        log_inter = bq + m_prev
        mt = jnp.maximum(log_inter, jnp.max(logw, axis=-1, keepdims=True))
        qh = q_ref[:, h * d:(h + 1) * d]
        kh = k_all[:, h * d:(h + 1) * d]
        vh = v_ref[:, h * d:(h + 1) * d]
        ch = c_ref[h * d:(h + 1) * d, :]
        nh = n_ref[h:h + 1, :]
        sw = jnp.exp(logw - mt) * _dot_nt(qh, kh)
        gq = jnp.exp(log_inter - mt)
        num = _dot(sw, vh) + _dot(qh, ch) * gq
        den = jnp.sum(sw, axis=-1, keepdims=True) + jnp.sum(qh * nh, axis=-1, keepdims=True) * gq
        hh = num / jnp.maximum(jnp.abs(den), jnp.exp(-mt))

        m_new = mt[t - 1:t, :]
        b_last = b_col[t - 1:t, ln:ln + 1]
        wk = jnp.exp(b_last - bq + gi[:, ln:ln + 1] - m_new)
        g_end = jnp.exp(b_last + m_prev - m_new)
        kw = kh * wk
        c_ref[h * d:(h + 1) * d, :] = ch * g_end + _dot_tn(kw, vh)
        n_ref[h:h + 1, :] = nh * g_end + jnp.sum(kw, axis=0, keepdims=True)
        m_ref[h:h + 1, :] = jnp.broadcast_to(m_new, (1, LANE))

        ms = jnp.mean(hh * hh, axis=-1, keepdims=True)
        hn = hh * lax.rsqrt(ms + EPS) * wn_ref[:, h * d:(h + 1) * d]
        h_ref[:, h * d:(h + 1) * d] = (hn * jax.nn.sigmoid(o_ref[:, h * d:(h + 1) * d])).astype(h_ref.dtype)
        yield

N_ATT_IN, N_SSD_IN, N_MLSTM_IN = 7, 9, 7


def _interleave(stages):
    live = [[gen, 0, n] for gen, n in stages]
    while live:
        item = min(live, key=lambda s: s[1] / s[2])
        try:
            next(item[0])
            item[1] += 1
        except StopIteration:
            live.remove(item)


def _mixers_prompt_kernel(*refs):
    att_in = refs[0:N_ATT_IN]
    ssd_in = refs[N_ATT_IN:N_ATT_IN + N_SSD_IN]
    ml_in = refs[N_ATT_IN + N_SSD_IN:N_ATT_IN + N_SSD_IN + N_MLSTM_IN]
    outs = refs[N_ATT_IN + N_SSD_IN + N_MLSTM_IN:]
    att_ref, ko_ref, vo_ref, y_ref, s_ref, h_ref, c_ref, n_ref, m_ref, tail_ref, ybuf_ref, xw_ref = outs
    j = pl.program_id(1)

    @pl.when(j == 0)
    def _():
        _attn_prompt_init(ko_ref, vo_ref)
        _ssd_prompt_init(s_ref, tail_ref)
        _mlstm_prompt_init(c_ref, n_ref, m_ref)

    g_ref = ssd_in[2]
    _interleave([
        (_attn_prompt_body(j, *att_in, att_ref, ko_ref, vo_ref), ATT_HEADS + 1),
        (_ssd_prompt_body(*ssd_in, y_ref, s_ref, tail_ref, ybuf_ref, xw_ref), SSM_HEADS + SSM_GROUPS + 2),
        (_mlstm_prompt_body(*ml_in[0:4], g_ref, *ml_in[4:], h_ref, c_ref, n_ref, m_ref), MLSTM_HEADS + 1),
    ])


def _mixers_prompt(proj, sinks, tables, cw, cb, dtb, alog, dskip_rep, wns, bi, bf, wnm, bsz, seq):
    nb = seq // CHUNK
    t = CHUNK
    c, s1, s2 = tables
    row = lambda b, j: b * nb + j
    seg = lambda col, width: pl.BlockSpec((t, width), lambda b, j: (row(b, j), col // width))
    vec = lambda n: pl.BlockSpec((1, n), lambda b, j: (0, 0))
    tab = pl.BlockSpec((t, LANE), lambda b, j: (j, 0))
    rows_out = lambda width: pl.BlockSpec((t, width), lambda b, j: (row(b, j), 0))
    per_seq = lambda r, n: pl.BlockSpec((None, r, n), lambda b, j: (b, 0, 0))
    in_specs = (
        [pl.BlockSpec(memory_space=pltpu.SMEM), seg(C_Q, D_ATT), seg(C_K, D_KV), seg(C_V, D_KV), tab, tab, tab]
        + [seg(C_XBC, CONV_DIM), seg(C_Z, D_SSM), seg(C_G, 2 * LANE),
           pl.BlockSpec((CONV_WIDTH, CONV_DIM), lambda b, j: (0, 0)),
           vec(CONV_DIM), vec(LANE), vec(LANE), vec(D_SSM), vec(D_SSM)]
        + [seg(C_MQ, D_MLSTM), seg(C_MK, D_MLSTM), seg(C_MV, D_MLSTM), seg(C_MO, D_MLSTM),
           vec(LANE), vec(LANE), vec(D_MLSTM)])
    assert len(in_specs) == N_ATT_IN + N_SSD_IN + N_MLSTM_IN
    return pl.pallas_call(
        _mixers_prompt_kernel,
        grid=(bsz, nb),
        in_specs=in_specs,
        out_specs=[rows_out(D_ATT), per_seq(t, D_KV), per_seq(t, D_KV),
                   rows_out(D_SSM), per_seq(D_SSM, D_STATE),
                   rows_out(D_MLSTM), per_seq(D_MLSTM, MLSTM_HEAD_DIM), per_seq(8, LANE), per_seq(8, LANE)],
        out_shape=[jax.ShapeDtypeStruct((bsz * seq, D_ATT), BF16),
                   jax.ShapeDtypeStruct((bsz, t, D_KV), F32),
                   jax.ShapeDtypeStruct((bsz, t, D_KV), F32),
                   jax.ShapeDtypeStruct((bsz * seq, D_SSM), BF16),
                   jax.ShapeDtypeStruct((bsz, D_SSM, D_STATE), F32),
                   jax.ShapeDtypeStruct((bsz * seq, D_MLSTM), BF16),
                   jax.ShapeDtypeStruct((bsz, D_MLSTM, MLSTM_HEAD_DIM), F32),
                   jax.ShapeDtypeStruct((bsz, 8, LANE), F32),
                   jax.ShapeDtypeStruct((bsz, 8, LANE), F32)],
        scratch_shapes=[pltpu.VMEM(((CONV_WIDTH - 1) * CONV_PAD, CONV_DIM), F32),
                        pltpu.VMEM((t, D_SSM), F32),
                        pltpu.VMEM((t, D_SSM // SSM_GROUPS), F32)],
        compiler_params=_cparams(("parallel", "arbitrary")),
        name="mixers_prompt",
    )(sinks, proj, proj, proj, c, s1, s2,
      proj, proj, proj, cw, cb, dtb, alog, dskip_rep, wns,
      proj, proj, proj, proj, bi, bf, wnm)


def _stacked_out(prev, depth, shape, block, layer):
    spec = pl.BlockSpec((None,) + block, lambda i: (layer,) + (i,) + (0,) * (len(block) - 1))
    out_shape = jax.ShapeDtypeStruct((depth,) + shape, F32)
    extra_in = [] if prev is None else [prev]
    extra_spec = [] if prev is None else [pl.BlockSpec(memory_space=pl.ANY)]
    return spec, out_shape, extra_in, extra_spec


def _drop_aliased(kernel_fn, n_in, n_prev):
    def wrapped(*refs):
        return kernel_fn(*refs[:n_in], *refs[n_in + n_prev:])
    return wrapped


def _attn_sample_kernel(q_ref, k_ref, v_ref, kc_ref, vc_ref, sink_ref, c_ref, s1_ref, s2_ref,
                        o_ref, ko_ref, vo_ref):
    w = kc_ref.shape[1]
    c = c_ref[...]
    s1 = s1_ref[...]
    s2 = s2_ref[...]
    qb = _rope(q_ref[...], c, s1, s2)
    kn = _rope(k_ref[...], c, s1, s2)
    vn = v_ref[...]
    kc = kc_ref[...]
    vc = vc_ref[...]
    ko_ref[:, 0:w - 1, :] = kc[:, 1:w, :]
    ko_ref[:, w - 1:w, :] = kn
    vo_ref[:, 0:w - 1, :] = vc[:, 1:w, :]
    vo_ref[:, w - 1:w, :] = vn

    scale = ATT_HEAD_DIM ** -0.5
    s = jnp.einsum("bhd,bkd->bhk", qb.astype(BF16), kc.astype(BF16),
                   preferred_element_type=F32) * scale
    sn = jnp.sum(qb * kn, axis=-1, keepdims=True) * scale
    sink = sink_ref[...][None, :, 0:1]
    m = jnp.maximum(jnp.maximum(jnp.max(s, axis=-1, keepdims=True), sn), sink)
    p = jnp.exp(s - m)
    pn = jnp.exp(sn - m)
    denom = jnp.sum(p, axis=-1, keepdims=True) + pn + jnp.exp(sink - m)
    p = p / denom
    pn = pn / denom
    o = jnp.einsum("bhk,bkd->bhd", p.astype(BF16), vc.astype(BF16), preferred_element_type=F32)
    o = o + pn * vn
    head = lax.broadcasted_iota(jnp.int32, o.shape[:2] + (ATT_HEAD_DIM,), 1)
    o_ref[...] = jnp.where(head < ATT_GROUP, o[:, :, 0:ATT_HEAD_DIM], o[:, :, ATT_HEAD_DIM:]).astype(o_ref.dtype)


def _attn_sample(qb, kn, vn, kc, vc, sink_rows, tables, layer, prev, tb):
    depth, bd, w, _ = kc.shape
    c, s1, s2 = tables
    blk3 = lambda r, n: pl.BlockSpec((tb, r, n), lambda i: (i, 0, 0))
    cache = pl.BlockSpec((None, tb, w, LANE), lambda i: (layer, i, 0, 0))
    vec = pl.BlockSpec((1, LANE), lambda i: (0, 0))
    prev_k, prev_v = (None, None) if prev is None else prev
    k_spec, k_shape, k_in, k_in_spec = _stacked_out(prev_k, depth, (bd, w, LANE), (tb, w, LANE), layer)
    v_spec, v_shape, v_in, v_in_spec = _stacked_out(prev_v, depth, (bd, w, LANE), (tb, w, LANE), layer)
    n_in = 9
    n_prev = len(k_in) + len(v_in)
    return pl.pallas_call(
        _drop_aliased(_attn_sample_kernel, n_in, n_prev),
        grid=(bd // tb,),
        in_specs=[blk3(ATT_HEADS, LANE), blk3(1, LANE), blk3(1, LANE), cache, cache,
                  pl.BlockSpec((ATT_HEADS, LANE), lambda i: (0, 0)), vec, vec, vec] + k_in_spec + v_in_spec,
        out_specs=[blk3(ATT_HEADS, ATT_HEAD_DIM), k_spec, v_spec],
        out_shape=[jax.ShapeDtypeStruct((bd, ATT_HEADS, ATT_HEAD_DIM), BF16), k_shape, v_shape],
        input_output_aliases={n_in + t: 1 + t for t in range(n_prev)},
        compiler_params=_cparams(("parallel",)),
        name="attn_sample",
    )(qb, kn, vn, kc, vc, sink_rows, c, s1, s2, *k_in, *v_in)


def _lane_place(cols, lane0):
    m = cols[0].shape[0]
    lane = lax.broadcasted_iota(jnp.int32, (m, LANE), 1)
    out = jnp.zeros((m, LANE), F32)
    for i, col in enumerate(cols):
        out = jnp.where(lane == lane0 + i, col, out)
    return out


def _sample_pre_kernel(proj_ref, cs_ref, cw_ref, cb_ref, dtb_ref, alog_ref, bi_ref, bf_ref, n_ref, m_ref,
                       conv_ref, xs_ref, xdt_ref, bm_ref, cm_ref, ea_ref,
                       ks_ref, nn_ref, g_ref, esw_ref, sw_ref, den_ref, mt_ref):
    xbc = proj_ref[:, C_XBC:C_XBC + CONV_DIM]
    s0 = cs_ref[:, 0:CONV_DIM]
    s1 = cs_ref[:, CONV_DIM:2 * CONV_DIM]
    s2 = cs_ref[:, 2 * CONV_DIM:3 * CONV_DIM]
    cw = cw_ref[...]
    acc = s0 * cw[0:1, :]
    acc = acc + s1 * cw[1:2, :]
    acc = acc + s2 * cw[2:3, :]
    acc = acc + xbc * cw[3:4, :]
    xc = _silu(acc + cb_ref[...])
    conv_ref[:, 0:CONV_DIM] = s1
    conv_ref[:, CONV_DIM:2 * CONV_DIM] = s2
    conv_ref[:, 2 * CONV_DIM:3 * CONV_DIM] = xbc

    xs = xc[:, 0:D_SSM]
    xs_ref[...] = xs
    bm_ref[...] = xc[:, D_SSM:D_SSM + SSM_GROUPS * D_STATE]
    cm_ref[...] = xc[:, D_SSM + SSM_GROUPS * D_STATE:]
    dt = jax.nn.softplus(proj_ref[:, C_G:C_G + LANE] + dtb_ref[...])
    ea_ref[...] = jnp.exp(dt * (-jnp.exp(alog_ref[...])))
    hp = SSM_HEAD_DIM
    for h in range(SSM_HEADS):
        xdt_ref[:, h * hp:(h + 1) * hp] = xs[:, h * hp:(h + 1) * hp] * dt[:, h:h + 1]

    d = MLSTM_HEAD_DIM
    gi = proj_ref[:, C_G:C_G + LANE] + bi_ref[...]
    gf = proj_ref[:, C_G + LANE:C_G + 2 * LANE] + bf_ref[...]
    log_inter = jax.nn.log_sigmoid(gf) + m_ref[...]
    mt = jnp.maximum(log_inter, gi)
    gq = jnp.exp(log_inter - mt)
    esw = jnp.exp(gi - mt)
    ks = proj_ref[:, C_MK:C_MK + D_MLSTM] * (d ** -0.5)
    ks_ref[...] = ks
    qk_cols = []
    qn_cols = []
    for h in range(MLSTM_HEADS):
        qh = proj_ref[:, C_MQ + h * d:C_MQ + (h + 1) * d]
        kh = ks[:, h * d:(h + 1) * d]
        nh = n_ref[:, h * d:(h + 1) * d]
        qk_cols.append(jnp.sum(qh * kh, axis=-1, keepdims=True))
        qn_cols.append(jnp.sum(qh * nh, axis=-1, keepdims=True))
        ln = GATE_LANE + h
        nn_ref[:, h * d:(h + 1) * d] = nh * gq[:, ln:ln + 1] + kh * esw[:, ln:ln + 1]
    sw = esw * _lane_place(qk_cols, GATE_LANE)
    den = sw + _lane_place(qn_cols, GATE_LANE) * gq
    g_ref[...] = gq
    esw_ref[...] = esw
    sw_ref[...] = sw
    den_ref[...] = jnp.maximum(jnp.abs(den), jnp.exp(-mt))
    mt_ref[...] = mt


def _sample_pre(proj, conv_state, cw, cb, dtb, alog, bi, bf, n_state, m_tile):
    bd = proj.shape[0]
    tile = jax.ShapeDtypeStruct((bd, LANE), F32)
    outs = [jax.ShapeDtypeStruct((bd, (CONV_WIDTH - 1) * CONV_DIM), F32),
            jax.ShapeDtypeStruct((bd, D_SSM), F32),
            jax.ShapeDtypeStruct((bd, D_SSM), F32),
            jax.ShapeDtypeStruct((bd, SSM_GROUPS * D_STATE), F32),
            jax.ShapeDtypeStruct((bd, SSM_GROUPS * D_STATE), F32),
            tile,
            jax.ShapeDtypeStruct((bd, D_MLSTM), F32),
            jax.ShapeDtypeStruct((bd, D_MLSTM), F32),
            tile, tile, tile, tile, tile]
    return pl.pallas_call(
        _sample_pre_kernel,
        out_shape=outs,
        compiler_params=pltpu.CompilerParams(vmem_limit_bytes=VMEM_LIMIT),
        name="sample_pre",
    )(proj, conv_state, cw, cb, dtb, alog, bi, bf, n_state, m_tile)


def _column_tile(row):
    return jnp.broadcast_to(row, (LANE, LANE)).T


def _ssm_sample_one(b, gb, ea_ref, s_ref, xdt_ref, b_ref, c_ref, so_ref, y_ref):
    hp = SSM_HEAD_DIM
    heads_per_tile = LANE // hp
    n_tiles = D_SSM // LANE
    tiles_per_group = n_tiles // SSM_GROUPS
    top = lax.broadcasted_iota(jnp.int32, (LANE, 1), 0) < hp
    xrow = xdt_ref[b]
    brow = b_ref[b]
    crow = c_ref[b]
    for t in range(n_tiles):
        g = t // tiles_per_group
        xcol = _column_tile(xrow[:, t * LANE:(t + 1) * LANE])
        bg = brow[:, g * D_STATE:(g + 1) * D_STATE]
        cg = jnp.broadcast_to(crow[:, g * D_STATE:(g + 1) * D_STATE], (8, D_STATE))
        e0 = ea_ref[gb, heads_per_tile * t]
        e1 = ea_ref[gb, heads_per_tile * t + 1]
        decay = jnp.where(top, e0, e1)
        sn = s_ref[b, t * LANE:(t + 1) * LANE, :] * decay + xcol * bg
        so_ref[b, t * LANE:(t + 1) * LANE, :] = sn
        y_ref[b, :, t * LANE:(t + 1) * LANE] = _dot_nt(cg, sn)[0:1, :]


def _ssm_sample_kernel(ea_ref, s_ref, xdt_ref, b_ref, c_ref, so_ref, y_ref):
    tb = s_ref.shape[0]
    base = pl.program_id(0) * tb

    def body(b, carry):
        _ssm_sample_one(b, base + b, ea_ref, s_ref, xdt_ref, b_ref, c_ref, so_ref, y_ref)
        return carry

    lax.fori_loop(0, tb, body, 0)


def _ssm_sample(ea, s, xdt3, b3, c3, layer, prev, tb):
    depth, bd = s.shape[0], s.shape[1]
    blk3 = lambda r, n: pl.BlockSpec((tb, r, n), lambda i: (i, 0, 0))
    s_spec, s_shape, s_in, s_in_spec = _stacked_out(prev, depth, (bd, D_SSM, D_STATE), (tb, D_SSM, D_STATE), layer)
    n_in = 5
    return pl.pallas_call(
        _drop_aliased(_ssm_sample_kernel, n_in, len(s_in)),
        grid=(bd // tb,),
        in_specs=[pl.BlockSpec(memory_space=pltpu.SMEM),
                  pl.BlockSpec((None, tb, D_SSM, D_STATE), lambda i: (layer, i, 0, 0)),
                  blk3(1, D_SSM), blk3(1, SSM_GROUPS * D_STATE),
                  blk3(1, SSM_GROUPS * D_STATE)] + s_in_spec,
        out_specs=[s_spec, blk3(1, D_SSM)],
        out_shape=[s_shape, jax.ShapeDtypeStruct((bd, 1, D_SSM), F32)],
        input_output_aliases={n_in + t: t for t in range(len(s_in))},
        compiler_params=_cparams(("parallel",)),
        name="ssm_sample",
    )(ea, s, xdt3, b3, c3, *s_in)


def _mlstm_sample_one(b, gb, g_ref, esw_ref, c_ref, q_ref, k_ref, v_ref, co_ref, qc_ref):
    d = MLSTM_HEAD_DIM
    qrow = q_ref[b]
    krow = k_ref[b]
    vrow = v_ref[b]
    for h in range(MLSTM_HEADS):
        qcol = _column_tile(qrow[:, h * d:(h + 1) * d])
        kcol = _column_tile(krow[:, h * d:(h + 1) * d])
        ch = c_ref[b, h * d:(h + 1) * d, :]
        g_end = g_ref[gb, GATE_LANE + h]
        wk = esw_ref[gb, GATE_LANE + h]
        qc_ref[b, :, h * d:(h + 1) * d] = jnp.sum(qcol * ch, axis=0, keepdims=True)
        co_ref[b, h * d:(h + 1) * d, :] = ch * g_end + (kcol * wk) * vrow[:, h * d:(h + 1) * d]


def _mlstm_sample_kernel(g_ref, esw_ref, c_ref, q_ref, k_ref, v_ref, co_ref, qc_ref):
    tb = c_ref.shape[0]
    base = pl.program_id(0) * tb

    def body(b, carry):
        _mlstm_sample_one(b, base + b, g_ref, esw_ref, c_ref, q_ref, k_ref, v_ref, co_ref, qc_ref)
        return carry

    lax.fori_loop(0, tb, body, 0)


def _mlstm_sample(gq, esw, c, q3, k3, v3, layer, prev, tb):
    depth, bd = c.shape[0], c.shape[1]
    blk3 = lambda r, n: pl.BlockSpec((tb, r, n), lambda i: (i, 0, 0))
    smem = pl.BlockSpec(memory_space=pltpu.SMEM)
    c_spec, c_shape, c_in, c_in_spec = _stacked_out(prev, depth, (bd, D_MLSTM, MLSTM_HEAD_DIM),
                                                    (tb, D_MLSTM, MLSTM_HEAD_DIM), layer)
    n_in = 6
    return pl.pallas_call(
        _drop_aliased(_mlstm_sample_kernel, n_in, len(c_in)),
        grid=(bd // tb,),
        in_specs=[smem, smem, pl.BlockSpec((None, tb, D_MLSTM, MLSTM_HEAD_DIM), lambda i: (layer, i, 0, 0)),
                  blk3(1, D_MLSTM), blk3(1, D_MLSTM), blk3(1, D_MLSTM)] + c_in_spec,
        out_specs=[c_spec, blk3(1, D_MLSTM)],
        out_shape=[c_shape, jax.ShapeDtypeStruct((bd, 1, D_MLSTM), F32)],
        input_output_aliases={n_in + t: t for t in range(len(c_in))},
        compiler_params=_cparams(("parallel",)),
        name="mlstm_sample",
    )(gq, esw, c, q3, k3, v3, *c_in)


def _sample_post_kernel(proj_ref, y_ref, xs_ref, dskip_ref, wns_ref, qc_ref, g_ref, sw_ref, den_ref, wnm_ref,
                        ys_ref, hs_ref):
    y = y_ref[...] + dskip_ref[...] * xs_ref[...]
    y = y * _silu(proj_ref[:, C_Z:C_Z + D_SSM])
    ys_ref[...] = _group_rmsnorm(y, wns_ref[...], SSM_GROUPS).astype(ys_ref.dtype)

    d = MLSTM_HEAD_DIM
    for h in range(MLSTM_HEADS):
        ln = GATE_LANE + h
        vh = proj_ref[:, C_MV + h * d:C_MV + (h + 1) * d]
        num = sw_ref[:, ln:ln + 1] * vh + qc_ref[:, h * d:(h + 1) * d] * g_ref[:, ln:ln + 1]
        hh = num / den_ref[:, ln:ln + 1]
        ms = jnp.mean(hh * hh, axis=-1, keepdims=True)
        hn = hh * lax.rsqrt(ms + EPS) * wnm_ref[:, h * d:(h + 1) * d]
        gate = jax.nn.sigmoid(proj_ref[:, C_MO + h * d:C_MO + (h + 1) * d])
        hs_ref[:, h * d:(h + 1) * d] = (hn * gate).astype(hs_ref.dtype)


def _sample_post(proj, y, xs, dskip_rep, wns, qc, gq, sw, den, wnm):
    bd = proj.shape[0]
    return pl.pallas_call(
        _sample_post_kernel,
        out_shape=[jax.ShapeDtypeStruct((bd, D_SSM), BF16), jax.ShapeDtypeStruct((bd, D_MLSTM), BF16)],
        compiler_params=pltpu.CompilerParams(vmem_limit_bytes=VMEM_LIMIT),
        name="sample_post",
    )(proj, y, xs, dskip_rep, wns, qc, gq, sw, den, wnm)


def _lane_vec(v, lane0):
    return jnp.pad(v.astype(F32), (lane0, LANE - lane0 - v.shape[0]))[None, :]


def _pick(n, candidates):
    for c in candidates:
        if n % c == 0:
            return c
    return n


def kernel(x_prompt, x_sample, cache_swa_k, cache_swa_v, state_conv, state_ssm, state_mlstm_C, state_mlstm_n,
           state_mlstm_m, w_norm_mix, w_in, attn_sinks, conv_w, conv_b, dt_bias, a_log, d_skip, w_norm_ssm,
           igate_b, fgate_b, w_norm_mlstm, w_out, w_norm_mlp, w_up, w_down, w_norm_final):
    bsz, seq, d_model = x_prompt.shape
    bd = x_sample.shape[0]
    depth = w_in.shape[0]
    win = cache_swa_k.shape[2]
    assert x_sample.shape[1] == 1 and seq % CHUNK == 0 and d_model == D_MODEL

    mp = bsz * seq
    tm_p = _pick(mp, (1024, 512, 256, 128))
    tm_mlp = _pick(mp, (1024, 512, 256, 128))
    tb = _pick(bd, (8,))

    hp = x_prompt.reshape(mp, d_model)
    hs = x_sample.reshape(bd, d_model)
    tab_p = _rope_tables(jnp.arange(seq, dtype=jnp.int32))
    tab_s = _rope_tables(jnp.arange(1, dtype=jnp.int32) + PAST_LEN)
    gf = w_norm_final[None, :]
    kc_all = cache_swa_k.reshape(depth, bd, win, D_KV)
    vc_all = cache_swa_v.reshape(depth, bd, win, D_KV)
    ssm_all = state_ssm.reshape(depth, bd, D_SSM, D_STATE)
    mem_all = state_mlstm_C.reshape(depth, bd, D_MLSTM, MLSTM_HEAD_DIM)

    st_p, st_s = [], []
    kv_new = s_new = c_new = None
    for l in range(depth):
        w_in_l = _prep_w_in(w_in, l, 256)
        g_mix = w_norm_mix[l][None, :]
        g_mlp = w_norm_mlp[l][None, :]
        cw = conv_w[l]
        cb = conv_b[l][None, :]
        dtb = _lane_vec(dt_bias[l], 0)
        alog = _lane_vec(a_log[l], 0)
        dskip_rep = jnp.repeat(d_skip[l].astype(F32), SSM_HEAD_DIM)[None, :]
        wns = w_norm_ssm[l][None, :]
        bi = _lane_vec(igate_b[l], GATE_LANE)
        bf = _lane_vec(fgate_b[l], GATE_LANE)
        wnm = w_norm_mlstm[l][None, :]
        sinks = attn_sinks[l].reshape(ATT_HEADS).astype(F32)
        last = l == depth - 1

        proj = _rms_matmul(hp, g_mix, w_in_l, tm_p, NP // 4)
        att, pk, pv, y, p_ssm, hm, p_c, p_n, p_m = _mixers_prompt(
            proj, sinks, tab_p, cw, cb, dtb, alog, dskip_rep, wns, bi, bf, wnm, bsz, seq)
        x1 = _outproj(att, y, hm, hp, w_out, l, _pick(mp, (512, 256, 128)))
        p_conv = proj.reshape(bsz, seq, NP)[:, seq - (CONV_WIDTH - 1):, C_XBC:C_XBC + CONV_DIM]
        st_p.append((pk.reshape(bsz, WINDOW, ATT_KV_HEADS, ATT_HEAD_DIM),
                     pv.reshape(bsz, WINDOW, ATT_KV_HEADS, ATT_HEAD_DIM),
                     p_conv,
                     p_ssm.reshape(bsz, SSM_HEADS, SSM_HEAD_DIM, D_STATE),
                     p_c.reshape(bsz, MLSTM_HEADS, MLSTM_HEAD_DIM, MLSTM_HEAD_DIM),
                     p_n[:, 0:MLSTM_HEADS, :],
                     p_m[:, 0:MLSTM_HEADS, 0]))

        proj_s = _rms_matmul(hs, g_mix, w_in_l, bd, NP // 4)
        q8 = proj_s[:, C_Q:C_Q + D_ATT].reshape(bd, ATT_HEADS, ATT_HEAD_DIM)
        zero = jnp.zeros_like(q8)
        first_kv = (jnp.arange(ATT_HEADS) < ATT_GROUP)[None, :, None]
        qb = jnp.where(first_kv, jnp.concatenate([q8, zero], -1), jnp.concatenate([zero, q8], -1))
        kn = proj_s[:, C_K:C_K + D_KV].reshape(bd, 1, D_KV)
        vn = proj_s[:, C_V:C_V + D_KV].reshape(bd, 1, D_KV)
        sink_rows = jnp.broadcast_to(sinks[:, None], (ATT_HEADS, LANE))
        att_s, sk, sv = _attn_sample(qb, kn, vn, kc_all, vc_all, sink_rows, tab_s, l, kv_new, tb)
        kv_new = (sk, sv)

        m_tile = jnp.pad(state_mlstm_m[l], ((0, 0), (GATE_LANE, LANE - GATE_LANE - MLSTM_HEADS)))
        (conv_new, xs, xdt, bm, cm, ea, ks, n_new, gq, esw, sw, den, mt) = _sample_pre(
            proj_s, state_conv[l].reshape(bd, (CONV_WIDTH - 1) * CONV_DIM), cw, cb, dtb, alog, bi, bf,
            state_mlstm_n[l].reshape(bd, D_MLSTM), m_tile)
        xdt3 = xdt.reshape(bd, 1, D_SSM)
        b3 = bm.reshape(bd, 1, SSM_GROUPS * D_STATE)
        c3 = cm.reshape(bd, 1, SSM_GROUPS * D_STATE)
        q3 = proj_s[:, C_MQ:C_MQ + D_MLSTM].reshape(bd, 1, D_MLSTM)
        k3 = ks.reshape(bd, 1, D_MLSTM)
        v3 = proj_s[:, C_MV:C_MV + D_MLSTM].reshape(bd, 1, D_MLSTM)
        tf = 512
        if (mp // tm_mlp) * (w_up.shape[2] // tf) == bd:
            side = (ea, ssm_all, xdt3, b3, c3, gq, esw, mem_all, q3, k3, v3, s_new, c_new)
            hp, s_new, y3, c_new, qc3 = _mlp(x1, g_mlp, w_up, w_down, gf, l, last, tm_mlp, tf, side)
        else:
            hp = _mlp(x1, g_mlp, w_up, w_down, gf, l, last, tm_mlp, tf)
            s_new, y3 = _ssm_sample(ea, ssm_all, xdt3, b3, c3, l, s_new, tb)
            c_new, qc3 = _mlstm_sample(gq, esw, mem_all, q3, k3, v3, l, c_new, tb)
        ys, hms = _sample_post(proj_s, y3.reshape(bd, D_SSM), xs, dskip_rep, wns, qc3.reshape(bd, D_MLSTM),
                               gq, sw, den, wnm)
        x1s = _outproj(att_s.reshape(bd, D_ATT), ys, hms, hs, w_out, l, bd)
        hs = _mlp(x1s, g_mlp, w_up, w_down, gf, l, last, bd, 512)
        st_s.append((conv_new.reshape(bd, CONV_WIDTH - 1, CONV_DIM),
                     n_new.reshape(bd, MLSTM_HEADS, MLSTM_HEAD_DIM),
                     mt[:, GATE_LANE:GATE_LANE + MLSTM_HEADS]))

    y_prompt = hp.reshape(bsz, seq, d_model)
    y_sample = hs.reshape(bd, 1, d_model)
    p_out = [jnp.stack([s[i] for s in st_p]) for i in range(7)]
    s_conv, s_n, s_m = [jnp.stack([s[i] for s in st_s]) for i in range(3)]
    s_k = kv_new[0].reshape(depth, bd, win, ATT_KV_HEADS, ATT_HEAD_DIM)
    s_v = kv_new[1].reshape(depth, bd, win, ATT_KV_HEADS, ATT_HEAD_DIM)
    s_ssm = s_new.reshape(depth, bd, SSM_HEADS, SSM_HEAD_DIM, D_STATE)
    s_c = c_new.reshape(depth, bd, MLSTM_HEADS, MLSTM_HEAD_DIM, MLSTM_HEAD_DIM)
    return (y_prompt, y_sample, *p_out, s_k, s_v, s_conv, s_ssm, s_c, s_n, s_m)
```

```python
import functools
import math

import jax
import jax.numpy as jnp
from jax import lax
from jax.experimental import pallas as pl
from jax.experimental.pallas import tpu as pltpu

F32 = jnp.float32
BF16 = jnp.bfloat16
HIGHEST = lax.Precision.HIGHEST

D_MODEL = 2048
EPS = 1e-6
PAST_LEN = 8192
ATT_HEAD_DIM = 64
ATT_HEADS = 8
ATT_KV_HEADS = 2
ATT_GROUP = ATT_HEADS // ATT_KV_HEADS
D_ATT = ATT_HEADS * ATT_HEAD_DIM
D_KV = ATT_KV_HEADS * ATT_HEAD_DIM
WINDOW = 128
ROPE_THETA = 500000.0
ROPE_DIM = ATT_HEAD_DIM // 4
SSM_HEAD_DIM = 64
SSM_HEADS = 16
D_SSM = SSM_HEADS * SSM_HEAD_DIM
SSM_GROUPS = 2
D_STATE = 128
CONV_WIDTH = 4
CONV_DIM = D_SSM + 2 * SSM_GROUPS * D_STATE
CHUNK = 128
MLSTM_HEADS = 4
MLSTM_HEAD_DIM = 128
D_MLSTM = MLSTM_HEADS * MLSTM_HEAD_DIM
D_FF = 4 * D_MODEL

C_XBC = 0
C_Q = 1536
C_Z = 2048
C_MQ = 3072
C_MK = 3584
C_MV = 4096
C_MO = 4608
C_K = 5120
C_V = 5248
C_G = 5376
NP = 5632
GATE_LANE = 16

LANE = 128
VMEM_LIMIT = 60 * 1024 * 1024


def _cparams(sem):
    return pltpu.CompilerParams(dimension_semantics=sem, vmem_limit_bytes=VMEM_LIMIT)


def _silu(x):
    return x * jax.nn.sigmoid(x)


def _dot(a, b):
    return jnp.dot(a.astype(BF16), b.astype(BF16), preferred_element_type=F32)


def _dot_nt(a, b):
    return lax.dot_general(a.astype(BF16), b.astype(BF16), (((1,), (1,)), ((), ())),
                           preferred_element_type=F32)


def _dot_tn(a, b):
    return lax.dot_general(a.astype(BF16), b.astype(BF16), (((0,), (0,)), ((), ())),
                           preferred_element_type=F32)


def _tril(n):
    row = lax.broadcasted_iota(jnp.int32, (n, n), 0)
    col = lax.broadcasted_iota(jnp.int32, (n, n), 1)
    return row >= col


def _cumsum_rows(x):
    t = _tril(x.shape[0]).astype(F32)
    return jnp.dot(t, x, precision=HIGHEST, preferred_element_type=F32)


def _rms_matmul_kernel(x_ref, g_ref, w_ref, o_ref, u_ref):
    @pl.when(pl.program_id(1) == 0)
    def _():
        x = x_ref[...]
        ms = jnp.mean(x * x, axis=-1, keepdims=True)
        u_ref[...] = (x * lax.rsqrt(ms + EPS) * g_ref[...]).astype(BF16)

    o_ref[...] = jnp.dot(u_ref[...], w_ref[...], preferred_element_type=F32)


def _rms_matmul(x, g, w, tm, tn):
    m, k = x.shape
    n = w.shape[1]
    return pl.pallas_call(
        _rms_matmul_kernel,
        grid=(m // tm, n // tn),
        in_specs=[pl.BlockSpec((tm, k), lambda i, j: (i, 0)),
                  pl.BlockSpec((1, k), lambda i, j: (0, 0)),
                  pl.BlockSpec((k, tn), lambda i, j: (0, j))],
        out_specs=pl.BlockSpec((tm, tn), lambda i, j: (i, j)),
        out_shape=jax.ShapeDtypeStruct((m, n), F32),
        scratch_shapes=[pltpu.VMEM((tm, k), BF16)],
        compiler_params=_cparams(("parallel", "arbitrary")),
        name="rms_inproj",
    )(x, g, w)


_SRC_Q, _SRC_K, _SRC_V, _SRC_Z, _SRC_XBC, _SRC_DT, _SRC_MQ = 0, 512, 640, 768, 1792, 3328, 3344
_SRC_MI = _SRC_MQ + 4 * D_MLSTM
IN_WIDTH = _SRC_MI + 2 * MLSTM_HEADS


def _prep_w_in_kernel(w_ref, o_ref):
    def put(dst, src, n):
        o_ref[:, dst:dst + n] = w_ref[src:src + n, :].T.astype(BF16)

    put(C_XBC, _SRC_XBC, CONV_DIM)
    put(C_Q, _SRC_Q, D_ATT)
    put(C_Z, _SRC_Z, D_SSM)
    put(C_MQ, _SRC_MQ, 4 * D_MLSTM)
    put(C_K, _SRC_K, D_KV)
    put(C_V, _SRC_V, D_KV)
    tk = w_ref.shape[1]
    dt = w_ref[_SRC_DT:_SRC_DT + SSM_HEADS, :]
    gates = w_ref[_SRC_MI:_SRC_MI + 2 * MLSTM_HEADS, :]
    pad = jnp.zeros((LANE - GATE_LANE - 2 * MLSTM_HEADS, tk), F32)
    lo = jnp.concatenate([dt, gates, pad], axis=0)
    hi = jnp.concatenate([jnp.zeros_like(dt), pltpu.roll(gates, MLSTM_HEADS, 0), pad], axis=0)
    o_ref[:, C_G:C_G + LANE] = lo.T.astype(BF16)
    o_ref[:, C_G + LANE:C_G + 2 * LANE] = hi.T.astype(BF16)


def _prep_w_in(w_in, layer, tk):
    depth, k, n = w_in.shape
    assert n == IN_WIDTH and n % 8 == 0
    wt = jnp.swapaxes(w_in, 1, 2).reshape(depth * n, k)
    return pl.pallas_call(
        _prep_w_in_kernel,
        grid=(k // tk,),
        in_specs=[pl.BlockSpec((n, tk), lambda i: (layer, i))],
        out_specs=pl.BlockSpec((tk, NP), lambda i: (i, 0)),
        out_shape=jax.ShapeDtypeStruct((k, NP), BF16),
        compiler_params=_cparams(("parallel",)),
        name="prep_w_in",
    )(wt)


def _outproj_kernel(a_ref, y_ref, h_ref, x_ref, w_ref, o_ref, wb_ref):
    @pl.when(pl.program_id(0) == 0)
    def _():
        wb_ref[...] = w_ref[...].astype(BF16)

    acc = jnp.dot(a_ref[...], wb_ref[0:D_ATT, :], preferred_element_type=F32)
    acc = acc + jnp.dot(y_ref[...], wb_ref[D_ATT:D_ATT + D_SSM, :], preferred_element_type=F32)
    acc = acc + jnp.dot(h_ref[...], wb_ref[D_ATT + D_SSM:, :], preferred_element_type=F32)
    o_ref[...] = x_ref[...] + acc


def _outproj(att, y, h, x, w, layer, tm):
    m, n = x.shape
    k = w.shape[1]
    rows = lambda width: pl.BlockSpec((tm, width), lambda i: (i, 0))
    return pl.pallas_call(
        _outproj_kernel,
        grid=(m // tm,),
        in_specs=[rows(D_ATT), rows(D_SSM), rows(D_MLSTM), rows(n),
                  pl.BlockSpec((None, k, n), lambda i: (layer, 0, 0), pipeline_mode=pl.Buffered(1))],
        out_specs=rows(n),
        out_shape=jax.ShapeDtypeStruct((m, n), F32),
        scratch_shapes=[pltpu.VMEM((k, n), BF16)],
        compiler_params=_cparams(("arbitrary",)),
        name="outproj",
    )(att, y, h, x, w)


N_MLP_IN, N_SSM_SIDE_IN, N_MLSTM_SIDE_IN = 5, 5, 6


def _mlp_kernel(*refs, final_norm, side, n_prev):
    x_ref, g_ref, wu_ref, wd_ref, gf_ref = refs[:N_MLP_IN]
    n_side = N_SSM_SIDE_IN + N_MLSTM_SIDE_IN if side else 0
    side_in = refs[N_MLP_IN:N_MLP_IN + n_side]
    outs = refs[N_MLP_IN + n_side + n_prev:]
    o_ref, u_ref = outs[0], outs[-1]
    j = pl.program_id(1)

    @pl.when(j == 0)
    def _():
        x = x_ref[...]
        ms = jnp.mean(x * x, axis=-1, keepdims=True)
        u_ref[...] = (x * lax.rsqrt(ms + EPS) * g_ref[...]).astype(BF16)
        o_ref[...] = x

    h = _dot(u_ref[...], wu_ref[...])
    h = jnp.square(jnp.maximum(h, 0.0))
    o_ref[...] += _dot(h, wd_ref[...])

    if side:
        seq = pl.program_id(0) * pl.num_programs(1) + j
        so_ref, y_ref, co_ref, qc_ref = outs[1:5]
        _ssm_sample_one(0, seq, *side_in[:N_SSM_SIDE_IN], so_ref, y_ref)
        _mlstm_sample_one(0, seq, *side_in[N_SSM_SIDE_IN:], co_ref, qc_ref)

    if final_norm:
        @pl.when(j == pl.num_programs(1) - 1)
        def _():
            y = o_ref[...]
            ms = jnp.mean(y * y, axis=-1, keepdims=True)
            o_ref[...] = y * lax.rsqrt(ms + EPS) * gf_ref[...]


def _mlp(x, g, wu, wd, gf, layer, final_norm, tm, tf, side=None):
    m, d = x.shape
    ff = wu.shape[2]
    nj = ff // tf
    in_specs = [pl.BlockSpec((tm, d), lambda i, j: (i, 0)),
                pl.BlockSpec((1, d), lambda i, j: (0, 0)),
                pl.BlockSpec((None, d, tf), lambda i, j: (layer, 0, j)),
                pl.BlockSpec((None, tf, d), lambda i, j: (layer, j, 0)),
                pl.BlockSpec((1, d), lambda i, j: (0, 0))]
    out_specs = [pl.BlockSpec((tm, d), lambda i, j: (i, 0))]
    out_shape = [jax.ShapeDtypeStruct((m, d), F32)]
    operands = [x, g, wu, wd, gf]
    aliases = {}
    n_prev = 0
    if side is not None:
        ea, ssm, xdt3, b3, c3, gq, esw, mem, q3, k3, v3, prev_s, prev_c = side
        depth, bd = ssm.shape[0], ssm.shape[1]
        assert (m // tm) * nj == bd
        smem = pl.BlockSpec(memory_space=pltpu.SMEM)
        one = lambda n: pl.BlockSpec((1, 1, n), lambda i, j: (i * nj + j, 0, 0))
        state = lambda r, n: pl.BlockSpec((None, 1, r, n), lambda i, j: (layer, i * nj + j, 0, 0))
        in_specs += [smem, state(D_SSM, D_STATE), one(D_SSM), one(SSM_GROUPS * D_STATE), one(SSM_GROUPS * D_STATE),
                     smem, smem, state(D_MLSTM, MLSTM_HEAD_DIM), one(D_MLSTM), one(D_MLSTM), one(D_MLSTM)]
        operands += [ea, ssm, xdt3, b3, c3, gq, esw, mem, q3, k3, v3]
        out_specs += [state(D_SSM, D_STATE), one(D_SSM), state(D_MLSTM, MLSTM_HEAD_DIM), one(D_MLSTM)]
        out_shape += [jax.ShapeDtypeStruct((depth, bd, D_SSM, D_STATE), F32),
                      jax.ShapeDtypeStruct((bd, 1, D_SSM), F32),
                      jax.ShapeDtypeStruct((depth, bd, D_MLSTM, MLSTM_HEAD_DIM), F32),
                      jax.ShapeDtypeStruct((bd, 1, D_MLSTM), F32)]
        if prev_s is not None:
            aliases = {len(operands): 1, len(operands) + 1: 3}
            operands += [prev_s, prev_c]
            in_specs += [pl.BlockSpec(memory_space=pl.ANY)] * 2
            n_prev = 2
    res = pl.pallas_call(
        functools.partial(_mlp_kernel, final_norm=final_norm, side=side is not None, n_prev=n_prev),
        grid=(m // tm, nj),
        in_specs=in_specs,
        out_specs=out_specs,
        out_shape=out_shape,
        input_output_aliases=aliases,
        scratch_shapes=[pltpu.VMEM((tm, d), BF16)],
        compiler_params=_cparams(("parallel", "arbitrary")),
        name="mlp",
    )(*operands)
    return res[0] if side is None else res


def _rope_tables(pos):
    half = ROPE_DIM // 2
    inv = jnp.power(jnp.float32(ROPE_THETA), -jnp.arange(half, dtype=jnp.float32) / half)
    ang = pos.astype(jnp.float32)[:, None] * inv[None, :]
    cos = jnp.cos(ang)
    sin = jnp.sin(ang)
    n = pos.shape[0]
    rest = ATT_HEAD_DIM - ROPE_DIM
    c = jnp.concatenate([cos, cos, jnp.ones((n, rest), F32)], axis=1)
    s1 = jnp.concatenate([-sin, jnp.zeros((n, half + rest), F32)], axis=1)
    s2 = jnp.concatenate([jnp.zeros((n, half), F32), sin, jnp.zeros((n, rest), F32)], axis=1)
    rep = LANE // ATT_HEAD_DIM
    return jnp.tile(c, (1, rep)), jnp.tile(s1, (1, rep)), jnp.tile(s2, (1, rep))


def _rope(x, c, s1, s2):
    width = x.shape[-1]
    rep = width // LANE
    half = ROPE_DIM // 2
    if rep > 1:
        c = jnp.concatenate([c] * rep, axis=-1)
        s1 = jnp.concatenate([s1] * rep, axis=-1)
        s2 = jnp.concatenate([s2] * rep, axis=-1)
    axis = x.ndim - 1
    return x * c + pltpu.roll(x, width - half, axis) * s1 + pltpu.roll(x, half, axis) * s2


def _attn_prompt_init(ko_ref, vo_ref):
    ko_ref[...] = jnp.zeros_like(ko_ref)
    vo_ref[...] = jnp.zeros_like(vo_ref)


def _attn_prompt_body(j, sink_ref, q_ref, k_ref, v_ref, c_ref, s1_ref, s2_ref, o_ref, ko_ref, vo_ref):
    w = WINDOW
    c = c_ref[...]
    s1 = s1_ref[...]
    s2 = s2_ref[...]
    krot = _rope(k_ref[...], c, s1, s2)
    v = v_ref[...]
    qrot = _rope(q_ref[...], c, s1, s2)
    kk = jnp.concatenate([ko_ref[...], krot], axis=0).astype(BF16)
    vv = jnp.concatenate([vo_ref[...], v], axis=0).astype(BF16)

    row = lax.broadcasted_iota(jnp.int32, (w, 2 * w), 0)
    col = lax.broadcasted_iota(jnp.int32, (w, 2 * w), 1)
    first_col = jnp.where(j > 0, 0, w)
    mask = (col >= row) & (col <= row + w) & (col >= first_col)
    scale = ATT_HEAD_DIM ** -0.5

    for h in range(ATT_KV_HEADS):
        kh = kk[:, h * ATT_HEAD_DIM:(h + 1) * ATT_HEAD_DIM]
        vh = vv[:, h * ATT_HEAD_DIM:(h + 1) * ATT_HEAD_DIM]
        for g in range(ATT_GROUP):
            hg = h * ATT_GROUP + g
            qh = qrot[:, hg * ATT_HEAD_DIM:(hg + 1) * ATT_HEAD_DIM]
            s = _dot_nt(qh, kh) * scale
            s = jnp.where(mask, s, -jnp.inf)
            sink = sink_ref[hg]
            m = jnp.maximum(jnp.max(s, axis=-1, keepdims=True), sink)
            p = jnp.exp(s - m)
            denom = jnp.sum(p, axis=-1, keepdims=True) + jnp.exp(sink - m)
            p = p / denom
            o = _dot(p, vh)
            o_ref[:, hg * ATT_HEAD_DIM:(hg + 1) * ATT_HEAD_DIM] = o.astype(o_ref.dtype)
            yield

    ko_ref[...] = krot
    vo_ref[...] = v


def _group_rmsnorm(y, w, groups):
    width = y.shape[-1] // groups
    outs = []
    for g in range(groups):
        yg = y[:, g * width:(g + 1) * width]
        ms = jnp.mean(yg * yg, axis=-1, keepdims=True)
        outs.append(yg * lax.rsqrt(ms + EPS) * w[:, g * width:(g + 1) * width])
    return jnp.concatenate(outs, axis=-1)


CONV_PAD = 8


def _ssd_prompt_init(s_ref, tail_ref):
    tail_ref[...] = jnp.zeros_like(tail_ref)
    s_ref[...] = jnp.zeros_like(s_ref)


def _ssd_prompt_body(xbc_ref, z_ref, g_ref, cw_ref, cb_ref, dtb_ref, alog_ref, dskip_ref, wn_ref,
                     y_ref, s_ref, tail_ref, ybuf_ref, xw_ref):
    q = CHUNK
    pad = CONV_PAD
    hp = SSM_HEAD_DIM
    heads_per_group = SSM_HEADS // SSM_GROUPS
    gw = heads_per_group * hp

    x = xbc_ref[...]
    cw = cw_ref[...]
    row8 = lax.broadcasted_iota(jnp.int32, (pad, CONV_DIM), 0)
    acc = None
    for j in range(CONV_WIDTH - 1):
        shift = CONV_WIDTH - 1 - j
        rolled = pltpu.roll(x, shift, 0)
        head = jnp.where(row8 < shift, tail_ref[j * pad:(j + 1) * pad, :], rolled[0:pad, :])
        tail_ref[j * pad:(j + 1) * pad, :] = rolled[0:pad, :]
        term = jnp.concatenate([head, rolled[pad:, :]], axis=0) * cw[j:j + 1, :]
        acc = term if acc is None else acc + term
    acc = acc + x * cw[CONV_WIDTH - 1:CONV_WIDTH, :]
    xc = _silu(acc + cb_ref[...])

    xs = xc[:, 0:D_SSM]
    bm = xc[:, D_SSM:D_SSM + SSM_GROUPS * D_STATE]
    cm = xc[:, D_SSM + SSM_GROUPS * D_STATE:]

    dt = jax.nn.softplus(g_ref[:, 0:LANE] + dtb_ref[...])
    a_neg = -jnp.exp(alog_ref[...])
    a_col = _cumsum_rows(dt * a_neg)
    a_row = a_col.T
    dt_row = dt.T
    ea_col = jnp.exp(a_col)
    wk_col = jnp.exp(a_col[q - 1:q, :] - a_col) * dt
    tril = _tril(q)
    yield

    for g in range(SSM_GROUPS):
        bg = bm[:, g * D_STATE:(g + 1) * D_STATE]
        cg = cm[:, g * D_STATE:(g + 1) * D_STATE]
        cb = _dot_nt(cg, bg)
        cs = _dot_nt(cg, s_ref[g * gw:(g + 1) * gw, :])
        for r in range(heads_per_group):
            h = g * heads_per_group + r
            seg = a_col[:, h:h + 1] - a_row[h:h + 1, :]
            wmat = jnp.exp(jnp.where(tril, seg, -jnp.inf)) * cb * dt_row[h:h + 1, :]
            xh = xs[:, h * hp:(h + 1) * hp]
            yh = _dot(wmat, xh)
            yh = yh + cs[:, r * hp:(r + 1) * hp] * ea_col[:, h:h + 1]
            ybuf_ref[:, h * hp:(h + 1) * hp] = yh
            xw_ref[:, r * hp:(r + 1) * hp] = xh * wk_col[:, h:h + 1]
            yield
        upd = _dot_tn(xw_ref[...], bg)
        for r in range(heads_per_group):
            h = g * heads_per_group + r
            decay = jnp.exp(a_row[h:h + 1, q - 1:q])
            s_ref[h * hp:(h + 1) * hp, :] = s_ref[h * hp:(h + 1) * hp, :] * decay + upd[r * hp:(r + 1) * hp, :]
        yield

    y = ybuf_ref[...] + dskip_ref[...] * xs
    y = y * _silu(z_ref[...])
    y_ref[...] = _group_rmsnorm(y, wn_ref[...], SSM_GROUPS).astype(y_ref.dtype)


def _mlstm_prompt_init(c_ref, n_ref, m_ref):
    c_ref[...] = jnp.zeros_like(c_ref)
    n_ref[...] = jnp.zeros_like(n_ref)
    m_ref[...] = jnp.zeros_like(m_ref)


def _mlstm_prompt_body(q_ref, k_ref, v_ref, o_ref, g_ref, bi_ref, bf_ref, wn_ref, h_ref, c_ref, n_ref, m_ref):
    t = CHUNK
    d = MLSTM_HEAD_DIM

    gi = g_ref[:, 0:LANE] + bi_ref[...]
    gf = g_ref[:, LANE:2 * LANE] + bf_ref[...]
    b_col = _cumsum_rows(jax.nn.log_sigmoid(gf))
    b_row = b_col.T
    i_row = gi.T
    tril = _tril(t)
    k_all = k_ref[...] * (d ** -0.5)

    for h in range(MLSTM_HEADS):
        ln = GATE_LANE + h
        bq = b_col[:, ln:ln + 1]
        logw = jnp.where(tril, bq - b_row[ln:ln + 1, :] + i_row[ln:ln + 1, :], -jnp.inf)
        m_prev = m_ref[h:h + 1, 0:1]
        log_inter = bq + m_prev
        mt = jnp.maximum(log_inter, jnp.max(logw, axis=-1, keepdims=True))
        qh = q_ref[:, h * d:(h + 1) * d]
        kh = k_all[:, h * d:(h + 1) * d]
        vh = v_ref[:, h * d:(h + 1) * d]
        ch = c_ref[h * d:(h + 1) * d, :]
        nh = n_ref[h:h + 1, :]
        sw = jnp.exp(logw - mt) * _dot_nt(qh, kh)
        gq = jnp.exp(log_inter - mt)
        num = _dot(sw, vh) + _dot(qh, ch) * gq
        den = jnp.sum(sw, axis=-1, keepdims=True) + jnp.sum(qh * nh, axis=-1, keepdims=True) * gq
        hh = num / jnp.maximum(jnp.abs(den), jnp.exp(-mt))

        m_new = mt[t - 1:t, :]
        b_last = b_col[t - 1:t, ln:ln + 1]
        wk = jnp.exp(b_last - bq + gi[:, ln:ln + 1] - m_new)
        g_end = jnp.exp(b_last + m_prev - m_new)
        kw = kh * wk
        c_ref[h * d:(h + 1) * d, :] = ch * g_end + _dot_tn(kw, vh)
        n_ref[h:h + 1, :] = nh * g_end + jnp.sum(kw, axis=0, keepdims=True)
        m_ref[h:h + 1, :] = jnp.broadcast_to(m_new, (1, LANE))

        ms = jnp.mean(hh * hh, axis=-1, keepdims=True)
        hn = hh * lax.rsqrt(ms + EPS) * wn_ref[:, h * d:(h + 1) * d]
        h_ref[:, h * d:(h + 1) * d] = (hn * jax.nn.sigmoid(o_ref[:, h * d:(h + 1) * d])).astype(h_ref.dtype)
        yield

def _interleave(stages):
    live = [[gen, 0, n] for gen, n in stages]
    while live:
        item = min(live, key=lambda s: s[1] / s[2])
        try:
            next(item[0])
            item[1] += 1
        except StopIteration:
            live.remove(item)


PROJ_TILE = 256


def _project_next(x_ref, g_ref, w_ref, pnext_ref):
    x = x_ref[...]
    u = (x * lax.rsqrt(jnp.mean(x * x, axis=-1, keepdims=True) + EPS) * g_ref[...]).astype(BF16)
    for ct in range(NP // PROJ_TILE):
        cols = slice(ct * PROJ_TILE, (ct + 1) * PROJ_TILE)
        pnext_ref[:, cols] = jnp.dot(u, w_ref[:, cols], preferred_element_type=F32)
        yield


def _front_kernel(sink_ref, x_ref, g_ref, w_ref, p0_ref, c_ref, s1_ref, s2_ref,
                  cw_ref, cb_ref, dtb_ref, alog_ref, dskip_ref, wns_ref, bi_ref, bf_ref, wnm_ref,
                  att_ref, ko_ref, vo_ref, y_ref, s_ref, h_ref, mem_ref, n_ref, m_ref, ptail_ref,
                  pcur_ref, pnext_ref, tail_ref, ybuf_ref, xw_ref):
    j = pl.program_id(1)
    first = (pl.program_id(0) == 0) & (j == 0)

    @pl.when(first)
    def _():
        pcur_ref[...] = p0_ref[...]

    @pl.when(jnp.logical_not(first))
    def _():
        pcur_ref[...] = pnext_ref[...]

    @pl.when(j == 0)
    def _():
        _attn_prompt_init(ko_ref, vo_ref)
        _ssd_prompt_init(s_ref, tail_ref)
        _mlstm_prompt_init(mem_ref, n_ref, m_ref)

    seg = lambda col, width: pcur_ref.at[:, col:col + width]
    g_gate = seg(C_G, 2 * LANE)
    xbc = seg(C_XBC, CONV_DIM)
    ptail_ref[...] = xbc[CHUNK - CONV_PAD:CHUNK, :]
    _interleave([
        (_project_next(x_ref, g_ref, w_ref, pnext_ref), NP // PROJ_TILE + 1),
        (_attn_prompt_body(j, sink_ref, seg(C_Q, D_ATT), seg(C_K, D_KV), seg(C_V, D_KV), c_ref, s1_ref, s2_ref,
                           att_ref, ko_ref, vo_ref), ATT_HEADS + 1),
        (_ssd_prompt_body(xbc, seg(C_Z, D_SSM), g_gate, cw_ref, cb_ref, dtb_ref, alog_ref, dskip_ref, wns_ref,
                          y_ref, s_ref, tail_ref, ybuf_ref, xw_ref), SSM_HEADS + SSM_GROUPS + 2),
        (_mlstm_prompt_body(seg(C_MQ, D_MLSTM), seg(C_MK, D_MLSTM), seg(C_MV, D_MLSTM), seg(C_MO, D_MLSTM),
                            g_gate, bi_ref, bf_ref, wnm_ref, h_ref, mem_ref, n_ref, m_ref), MLSTM_HEADS + 1),
    ])


def _front(x, g_mix, w, proj0, sinks, tables, cw, cb, dtb, alog, dskip_rep, wns, bi, bf, wnm, bsz, seq):
    nb = seq // CHUNK
    t = CHUNK
    total = bsz * nb
    c, s1, s2 = tables
    row = lambda b, j: b * nb + j
    full = lambda shape: pl.BlockSpec(shape, lambda b, j: (0,) * len(shape))
    vec = lambda n: full((1, n))
    tab = pl.BlockSpec((t, LANE), lambda b, j: (j, 0))
    rows_out = lambda width: pl.BlockSpec((t, width), lambda b, j: (row(b, j), 0))
    per_seq = lambda r, n: pl.BlockSpec((None, r, n), lambda b, j: (b, 0, 0))
    return pl.pallas_call(
        _front_kernel,
        grid=(bsz, nb),
        in_specs=[pl.BlockSpec(memory_space=pltpu.SMEM),
                  pl.BlockSpec((t, D_MODEL), lambda b, j: (jnp.minimum(row(b, j) + 1, total - 1), 0)),
                  vec(D_MODEL),
                  pl.BlockSpec((D_MODEL, NP), lambda b, j: (0, 0), pipeline_mode=pl.Buffered(1)),
                  full((t, NP)),
                  tab, tab, tab,
                  full((CONV_WIDTH, CONV_DIM)), vec(CONV_DIM), vec(LANE), vec(LANE), vec(D_SSM), vec(D_SSM),
                  vec(LANE), vec(LANE), vec(D_MLSTM)],
        out_specs=[rows_out(D_ATT), per_seq(t, D_KV), per_seq(t, D_KV),
                   rows_out(D_SSM), per_seq(D_SSM, D_STATE),
                   rows_out(D_MLSTM), per_seq(D_MLSTM, MLSTM_HEAD_DIM), per_seq(8, LANE), per_seq(8, LANE),
                   per_seq(CONV_PAD, CONV_DIM)],
        out_shape=[jax.ShapeDtypeStruct((bsz * seq, D_ATT), BF16),
                   jax.ShapeDtypeStruct((bsz, t, D_KV), F32),
                   jax.ShapeDtypeStruct((bsz, t, D_KV), F32),
                   jax.ShapeDtypeStruct((bsz * seq, D_SSM), BF16),
                   jax.ShapeDtypeStruct((bsz, D_SSM, D_STATE), F32),
                   jax.ShapeDtypeStruct((bsz * seq, D_MLSTM), BF16),
                   jax.ShapeDtypeStruct((bsz, D_MLSTM, MLSTM_HEAD_DIM), F32),
                   jax.ShapeDtypeStruct((bsz, 8, LANE), F32),
                   jax.ShapeDtypeStruct((bsz, 8, LANE), F32),
                   jax.ShapeDtypeStruct((bsz, CONV_PAD, CONV_DIM), F32)],
        scratch_shapes=[pltpu.VMEM((t, NP), F32),
                        pltpu.VMEM((t, NP), F32),
                        pltpu.VMEM(((CONV_WIDTH - 1) * CONV_PAD, CONV_DIM), F32),
                        pltpu.VMEM((t, D_SSM), F32),
                        pltpu.VMEM((t, D_SSM // SSM_GROUPS), F32)],
        compiler_params=_cparams(("arbitrary", "arbitrary")),
        name="front",
    )(sinks, x, g_mix, w, proj0, c, s1, s2, cw, cb, dtb, alog, dskip_rep, wns, bi, bf, wnm)


def _stacked_out(prev, depth, shape, block, layer):
    spec = pl.BlockSpec((None,) + block, lambda i: (layer,) + (i,) + (0,) * (len(block) - 1))
    out_shape = jax.ShapeDtypeStruct((depth,) + shape, F32)
    extra_in = [] if prev is None else [prev]
    extra_spec = [] if prev is None else [pl.BlockSpec(memory_space=pl.ANY)]
    return spec, out_shape, extra_in, extra_spec


def _drop_aliased(kernel_fn, n_in, n_prev):
    def wrapped(*refs):
        return kernel_fn(*refs[:n_in], *refs[n_in + n_prev:])
    return wrapped


def _attn_sample_kernel(q_ref, k_ref, v_ref, kc_ref, vc_ref, sink_ref, c_ref, s1_ref, s2_ref,
                        o_ref, ko_ref, vo_ref):
    w = kc_ref.shape[1]
    c = c_ref[...]
    s1 = s1_ref[...]
    s2 = s2_ref[...]
    qb = _rope(q_ref[...], c, s1, s2)
    kn = _rope(k_ref[...], c, s1, s2)
    vn = v_ref[...]
    kc = kc_ref[...]
    vc = vc_ref[...]
    ko_ref[:, 0:w - 1, :] = kc[:, 1:w, :]
    ko_ref[:, w - 1:w, :] = kn
    vo_ref[:, 0:w - 1, :] = vc[:, 1:w, :]
    vo_ref[:, w - 1:w, :] = vn

    scale = ATT_HEAD_DIM ** -0.5
    s = jnp.einsum("bhd,bkd->bhk", qb.astype(BF16), kc.astype(BF16),
                   preferred_element_type=F32) * scale
    sn = jnp.sum(qb * kn, axis=-1, keepdims=True) * scale
    sink = sink_ref[...][None, :, 0:1]
    m = jnp.maximum(jnp.maximum(jnp.max(s, axis=-1, keepdims=True), sn), sink)
    p = jnp.exp(s - m)
    pn = jnp.exp(sn - m)
    denom = jnp.sum(p, axis=-1, keepdims=True) + pn + jnp.exp(sink - m)
    p = p / denom
    pn = pn / denom
    o = jnp.einsum("bhk,bkd->bhd", p.astype(BF16), vc.astype(BF16), preferred_element_type=F32)
    o = o + pn * vn
    head = lax.broadcasted_iota(jnp.int32, o.shape[:2] + (ATT_HEAD_DIM,), 1)
    o_ref[...] = jnp.where(head < ATT_GROUP, o[:, :, 0:ATT_HEAD_DIM], o[:, :, ATT_HEAD_DIM:]).astype(o_ref.dtype)


def _attn_sample(qb, kn, vn, kc, vc, sink_rows, tables, layer, prev, tb):
    depth, bd, w, _ = kc.shape
    c, s1, s2 = tables
    blk3 = lambda r, n: pl.BlockSpec((tb, r, n), lambda i: (i, 0, 0))
    cache = pl.BlockSpec((None, tb, w, LANE), lambda i: (layer, i, 0, 0))
    vec = pl.BlockSpec((1, LANE), lambda i: (0, 0))
    prev_k, prev_v = (None, None) if prev is None else prev
    k_spec, k_shape, k_in, k_in_spec = _stacked_out(prev_k, depth, (bd, w, LANE), (tb, w, LANE), layer)
    v_spec, v_shape, v_in, v_in_spec = _stacked_out(prev_v, depth, (bd, w, LANE), (tb, w, LANE), layer)
    n_in = 9
    n_prev = len(k_in) + len(v_in)
    return pl.pallas_call(
        _drop_aliased(_attn_sample_kernel, n_in, n_prev),
        grid=(bd // tb,),
        in_specs=[blk3(ATT_HEADS, LANE), blk3(1, LANE), blk3(1, LANE), cache, cache,
                  pl.BlockSpec((ATT_HEADS, LANE), lambda i: (0, 0)), vec, vec, vec] + k_in_spec + v_in_spec,
        out_specs=[blk3(ATT_HEADS, ATT_HEAD_DIM), k_spec, v_spec],
        out_shape=[jax.ShapeDtypeStruct((bd, ATT_HEADS, ATT_HEAD_DIM), BF16), k_shape, v_shape],
        input_output_aliases={n_in + t: 1 + t for t in range(n_prev)},
        compiler_params=_cparams(("parallel",)),
        name="attn_sample",
    )(qb, kn, vn, kc, vc, sink_rows, c, s1, s2, *k_in, *v_in)


def _lane_place(cols, lane0):
    m = cols[0].shape[0]
    lane = lax.broadcasted_iota(jnp.int32, (m, LANE), 1)
    out = jnp.zeros((m, LANE), F32)
    for i, col in enumerate(cols):
        out = jnp.where(lane == lane0 + i, col, out)
    return out


def _sample_pre_kernel(proj_ref, cs_ref, cw_ref, cb_ref, dtb_ref, alog_ref, bi_ref, bf_ref, n_ref, m_ref,
                       conv_ref, xs_ref, xdt_ref, bm_ref, cm_ref, ea_ref,
                       ks_ref, nn_ref, g_ref, esw_ref, sw_ref, den_ref, mt_ref):
    xbc = proj_ref[:, C_XBC:C_XBC + CONV_DIM]
    s0 = cs_ref[:, 0:CONV_DIM]
    s1 = cs_ref[:, CONV_DIM:2 * CONV_DIM]
    s2 = cs_ref[:, 2 * CONV_DIM:3 * CONV_DIM]
    cw = cw_ref[...]
    acc = s0 * cw[0:1, :]
    acc = acc + s1 * cw[1:2, :]
    acc = acc + s2 * cw[2:3, :]
    acc = acc + xbc * cw[3:4, :]
    xc = _silu(acc + cb_ref[...])
    conv_ref[:, 0:CONV_DIM] = s1
    conv_ref[:, CONV_DIM:2 * CONV_DIM] = s2
    conv_ref[:, 2 * CONV_DIM:3 * CONV_DIM] = xbc

    xs = xc[:, 0:D_SSM]
    xs_ref[...] = xs
    bm_ref[...] = xc[:, D_SSM:D_SSM + SSM_GROUPS * D_STATE]
    cm_ref[...] = xc[:, D_SSM + SSM_GROUPS * D_STATE:]
    dt = jax.nn.softplus(proj_ref[:, C_G:C_G + LANE] + dtb_ref[...])
    ea_ref[...] = jnp.exp(dt * (-jnp.exp(alog_ref[...])))
    hp = SSM_HEAD_DIM
    for h in range(SSM_HEADS):
        xdt_ref[:, h * hp:(h + 1) * hp] = xs[:, h * hp:(h + 1) * hp] * dt[:, h:h + 1]

    d = MLSTM_HEAD_DIM
    gi = proj_ref[:, C_G:C_G + LANE] + bi_ref[...]
    gf = proj_ref[:, C_G + LANE:C_G + 2 * LANE] + bf_ref[...]
    log_inter = jax.nn.log_sigmoid(gf) + m_ref[...]
    mt = jnp.maximum(log_inter, gi)
    gq = jnp.exp(log_inter - mt)
    esw = jnp.exp(gi - mt)
    ks = proj_ref[:, C_MK:C_MK + D_MLSTM] * (d ** -0.5)
    ks_ref[...] = ks
    qk_cols = []
    qn_cols = []
    for h in range(MLSTM_HEADS):
        qh = proj_ref[:, C_MQ + h * d:C_MQ + (h + 1) * d]
        kh = ks[:, h * d:(h + 1) * d]
        nh = n_ref[:, h * d:(h + 1) * d]
        qk_cols.append(jnp.sum(qh * kh, axis=-1, keepdims=True))
        qn_cols.append(jnp.sum(qh * nh, axis=-1, keepdims=True))
        ln = GATE_LANE + h
        nn_ref[:, h * d:(h + 1) * d] = nh * gq[:, ln:ln + 1] + kh * esw[:, ln:ln + 1]
    sw = esw * _lane_place(qk_cols, GATE_LANE)
    den = sw + _lane_place(qn_cols, GATE_LANE) * gq
    g_ref[...] = gq
    esw_ref[...] = esw
    sw_ref[...] = sw
    den_ref[...] = jnp.maximum(jnp.abs(den), jnp.exp(-mt))
    mt_ref[...] = mt


def _sample_pre(proj, conv_state, cw, cb, dtb, alog, bi, bf, n_state, m_tile):
    bd = proj.shape[0]
    tile = jax.ShapeDtypeStruct((bd, LANE), F32)
    outs = [jax.ShapeDtypeStruct((bd, (CONV_WIDTH - 1) * CONV_DIM), F32),
            jax.ShapeDtypeStruct((bd, D_SSM), F32),
            jax.ShapeDtypeStruct((bd, D_SSM), F32),
            jax.ShapeDtypeStruct((bd, SSM_GROUPS * D_STATE), F32),
            jax.ShapeDtypeStruct((bd, SSM_GROUPS * D_STATE), F32),
            tile,
            jax.ShapeDtypeStruct((bd, D_MLSTM), F32),
            jax.ShapeDtypeStruct((bd, D_MLSTM), F32),
            tile, tile, tile, tile, tile]
    return pl.pallas_call(
        _sample_pre_kernel,
        out_shape=outs,
        compiler_params=pltpu.CompilerParams(vmem_limit_bytes=VMEM_LIMIT),
        name="sample_pre",
    )(proj, conv_state, cw, cb, dtb, alog, bi, bf, n_state, m_tile)


def _column_tile(row):
    return jnp.broadcast_to(row, (LANE, LANE)).T


def _ssm_sample_one(b, gb, ea_ref, s_ref, xdt_ref, b_ref, c_ref, so_ref, y_ref):
    hp = SSM_HEAD_DIM
    heads_per_tile = LANE // hp
    n_tiles = D_SSM // LANE
    tiles_per_group = n_tiles // SSM_GROUPS
    top = lax.broadcasted_iota(jnp.int32, (LANE, 1), 0) < hp
    xrow = xdt_ref[b]
    brow = b_ref[b]
    crow = c_ref[b]
    for t in range(n_tiles):
        g = t // tiles_per_group
        xcol = _column_tile(xrow[:, t * LANE:(t + 1) * LANE])
        bg = brow[:, g * D_STATE:(g + 1) * D_STATE]
        cg = jnp.broadcast_to(crow[:, g * D_STATE:(g + 1) * D_STATE], (8, D_STATE))
        e0 = ea_ref[gb, heads_per_tile * t]
        e1 = ea_ref[gb, heads_per_tile * t + 1]
        decay = jnp.where(top, e0, e1)
        sn = s_ref[b, t * LANE:(t + 1) * LANE, :] * decay + xcol * bg
        so_ref[b, t * LANE:(t + 1) * LANE, :] = sn
        y_ref[b, :, t * LANE:(t + 1) * LANE] = _dot_nt(cg, sn)[0:1, :]


def _ssm_sample_kernel(ea_ref, s_ref, xdt_ref, b_ref, c_ref, so_ref, y_ref):
    tb = s_ref.shape[0]
    base = pl.program_id(0) * tb

    def body(b, carry):
        _ssm_sample_one(b, base + b, ea_ref, s_ref, xdt_ref, b_ref, c_ref, so_ref, y_ref)
        return carry

    lax.fori_loop(0, tb, body, 0)


def _ssm_sample(ea, s, xdt3, b3, c3, layer, prev, tb):
    depth, bd = s.shape[0], s.shape[1]
    blk3 = lambda r, n: pl.BlockSpec((tb, r, n), lambda i: (i, 0, 0))
    s_spec, s_shape, s_in, s_in_spec = _stacked_out(prev, depth, (bd, D_SSM, D_STATE), (tb, D_SSM, D_STATE), layer)
    n_in = 5
    return pl.pallas_call(
        _drop_aliased(_ssm_sample_kernel, n_in, len(s_in)),
        grid=(bd // tb,),
        in_specs=[pl.BlockSpec(memory_space=pltpu.SMEM),
                  pl.BlockSpec((None, tb, D_SSM, D_STATE), lambda i: (layer, i, 0, 0)),
                  blk3(1, D_SSM), blk3(1, SSM_GROUPS * D_STATE),
                  blk3(1, SSM_GROUPS * D_STATE)] + s_in_spec,
        out_specs=[s_spec, blk3(1, D_SSM)],
        out_shape=[s_shape, jax.ShapeDtypeStruct((bd, 1, D_SSM), F32)],
        input_output_aliases={n_in + t: t for t in range(len(s_in))},
        compiler_params=_cparams(("parallel",)),
        name="ssm_sample",
    )(ea, s, xdt3, b3, c3, *s_in)


def _mlstm_sample_one(b, gb, g_ref, esw_ref, c_ref, q_ref, k_ref, v_ref, co_ref, qc_ref):
    d = MLSTM_HEAD_DIM
    qrow = q_ref[b]
    krow = k_ref[b]
    vrow = v_ref[b]
    for h in range(MLSTM_HEADS):
        qcol = _column_tile(qrow[:, h * d:(h + 1) * d])
        kcol = _column_tile(krow[:, h * d:(h + 1) * d])
        ch = c_ref[b, h * d:(h + 1) * d, :]
        g_end = g_ref[gb, GATE_LANE + h]
        wk = esw_ref[gb, GATE_LANE + h]
        qc_ref[b, :, h * d:(h + 1) * d] = jnp.sum(qcol * ch, axis=0, keepdims=True)
        co_ref[b, h * d:(h + 1) * d, :] = ch * g_end + (kcol * wk) * vrow[:, h * d:(h + 1) * d]


def _mlstm_sample_kernel(g_ref, esw_ref, c_ref, q_ref, k_ref, v_ref, co_ref, qc_ref):
    tb = c_ref.shape[0]
    base = pl.program_id(0) * tb

    def body(b, carry):
        _mlstm_sample_one(b, base + b, g_ref, esw_ref, c_ref, q_ref, k_ref, v_ref, co_ref, qc_ref)
        return carry

    lax.fori_loop(0, tb, body, 0)


def _mlstm_sample(gq, esw, c, q3, k3, v3, layer, prev, tb):
    depth, bd = c.shape[0], c.shape[1]
    blk3 = lambda r, n: pl.BlockSpec((tb, r, n), lambda i: (i, 0, 0))
    smem = pl.BlockSpec(memory_space=pltpu.SMEM)
    c_spec, c_shape, c_in, c_in_spec = _stacked_out(prev, depth, (bd, D_MLSTM, MLSTM_HEAD_DIM),
                                                    (tb, D_MLSTM, MLSTM_HEAD_DIM), layer)
    n_in = 6
    return pl.pallas_call(
        _drop_aliased(_mlstm_sample_kernel, n_in, len(c_in)),
        grid=(bd // tb,),
        in_specs=[smem, smem, pl.BlockSpec((None, tb, D_MLSTM, MLSTM_HEAD_DIM), lambda i: (layer, i, 0, 0)),
                  blk3(1, D_MLSTM), blk3(1, D_MLSTM), blk3(1, D_MLSTM)] + c_in_spec,
        out_specs=[c_spec, blk3(1, D_MLSTM)],
        out_shape=[c_shape, jax.ShapeDtypeStruct((bd, 1, D_MLSTM), F32)],
        input_output_aliases={n_in + t: t for t in range(len(c_in))},
        compiler_params=_cparams(("parallel",)),
        name="mlstm_sample",
    )(gq, esw, c, q3, k3, v3, *c_in)


def _sample_post_kernel(proj_ref, y_ref, xs_ref, dskip_ref, wns_ref, qc_ref, g_ref, sw_ref, den_ref, wnm_ref,
                        ys_ref, hs_ref):
    y = y_ref[...] + dskip_ref[...] * xs_ref[...]
    y = y * _silu(proj_ref[:, C_Z:C_Z + D_SSM])
    ys_ref[...] = _group_rmsnorm(y, wns_ref[...], SSM_GROUPS).astype(ys_ref.dtype)

    d = MLSTM_HEAD_DIM
    for h in range(MLSTM_HEADS):
        ln = GATE_LANE + h
        vh = proj_ref[:, C_MV + h * d:C_MV + (h + 1) * d]
        num = sw_ref[:, ln:ln + 1] * vh + qc_ref[:, h * d:(h + 1) * d] * g_ref[:, ln:ln + 1]
        hh = num / den_ref[:, ln:ln + 1]
        ms = jnp.mean(hh * hh, axis=-1, keepdims=True)
        hn = hh * lax.rsqrt(ms + EPS) * wnm_ref[:, h * d:(h + 1) * d]
        gate = jax.nn.sigmoid(proj_ref[:, C_MO + h * d:C_MO + (h + 1) * d])
        hs_ref[:, h * d:(h + 1) * d] = (hn * gate).astype(hs_ref.dtype)


def _sample_post(proj, y, xs, dskip_rep, wns, qc, gq, sw, den, wnm):
    bd = proj.shape[0]
    return pl.pallas_call(
        _sample_post_kernel,
        out_shape=[jax.ShapeDtypeStruct((bd, D_SSM), BF16), jax.ShapeDtypeStruct((bd, D_MLSTM), BF16)],
        compiler_params=pltpu.CompilerParams(vmem_limit_bytes=VMEM_LIMIT),
        name="sample_post",
    )(proj, y, xs, dskip_rep, wns, qc, gq, sw, den, wnm)


def _lane_vec(v, lane0):
    return jnp.pad(v.astype(F32), (lane0, LANE - lane0 - v.shape[0]))[None, :]


def _pick(n, candidates):
    for c in candidates:
        if n % c == 0:
            return c
    return n


def kernel(x_prompt, x_sample, cache_swa_k, cache_swa_v, state_conv, state_ssm, state_mlstm_C, state_mlstm_n,
           state_mlstm_m, w_norm_mix, w_in, attn_sinks, conv_w, conv_b, dt_bias, a_log, d_skip, w_norm_ssm,
           igate_b, fgate_b, w_norm_mlstm, w_out, w_norm_mlp, w_up, w_down, w_norm_final):
    bsz, seq, d_model = x_prompt.shape
    bd = x_sample.shape[0]
    depth = w_in.shape[0]
    win = cache_swa_k.shape[2]
    assert x_sample.shape[1] == 1 and seq % CHUNK == 0 and d_model == D_MODEL

    mp = bsz * seq
    tm_p = _pick(mp, (1024, 512, 256, 128))
    tm_mlp = _pick(mp, (1024, 512, 256, 128))
    tb = _pick(bd, (8,))

    hp = x_prompt.reshape(mp, d_model)
    hs = x_sample.reshape(bd, d_model)
    tab_p = _rope_tables(jnp.arange(seq, dtype=jnp.int32))
    tab_s = _rope_tables(jnp.arange(1, dtype=jnp.int32) + PAST_LEN)
    gf = w_norm_final[None, :]
    kc_all = cache_swa_k.reshape(depth, bd, win, D_KV)
    vc_all = cache_swa_v.reshape(depth, bd, win, D_KV)
    ssm_all = state_ssm.reshape(depth, bd, D_SSM, D_STATE)
    mem_all = state_mlstm_C.reshape(depth, bd, D_MLSTM, MLSTM_HEAD_DIM)

    st_p, st_s = [], []
    kv_new = s_new = c_new = None
    for l in range(depth):
        w_in_l = _prep_w_in(w_in, l, 256)
        g_mix = w_norm_mix[l][None, :]
        g_mlp = w_norm_mlp[l][None, :]
        cw = conv_w[l]
        cb = conv_b[l][None, :]
        dtb = _lane_vec(dt_bias[l], 0)
        alog = _lane_vec(a_log[l], 0)
        dskip_rep = jnp.repeat(d_skip[l].astype(F32), SSM_HEAD_DIM)[None, :]
        wns = w_norm_ssm[l][None, :]
        bi = _lane_vec(igate_b[l], GATE_LANE)
        bf = _lane_vec(fgate_b[l], GATE_LANE)
        wnm = w_norm_mlstm[l][None, :]
        sinks = attn_sinks[l].reshape(ATT_HEADS).astype(F32)
        last = l == depth - 1

        proj0 = _rms_matmul(hp[0:CHUNK], g_mix, w_in_l, CHUNK, NP // 4)
        att, pk, pv, y, p_ssm, hm, p_c, p_n, p_m, p_tail = _front(
            hp, g_mix, w_in_l, proj0, sinks, tab_p, cw, cb, dtb, alog, dskip_rep, wns, bi, bf, wnm, bsz, seq)
        x1 = _outproj(att, y, hm, hp, w_out, l, _pick(mp, (512, 256, 128)))
        p_conv = p_tail[:, CONV_PAD - (CONV_WIDTH - 1):, :]
        st_p.append((pk.reshape(bsz, WINDOW, ATT_KV_HEADS, ATT_HEAD_DIM),
                     pv.reshape(bsz, WINDOW, ATT_KV_HEADS, ATT_HEAD_DIM),
                     p_conv,
                     p_ssm.reshape(bsz, SSM_HEADS, SSM_HEAD_DIM, D_STATE),
                     p_c.reshape(bsz, MLSTM_HEADS, MLSTM_HEAD_DIM, MLSTM_HEAD_DIM),
                     p_n[:, 0:MLSTM_HEADS, :],
                     p_m[:, 0:MLSTM_HEADS, 0]))

        proj_s = _rms_matmul(hs, g_mix, w_in_l, bd, NP // 4)
        q8 = proj_s[:, C_Q:C_Q + D_ATT].reshape(bd, ATT_HEADS, ATT_HEAD_DIM)
        zero = jnp.zeros_like(q8)
        first_kv = (jnp.arange(ATT_HEADS) < ATT_GROUP)[None, :, None]
        qb = jnp.where(first_kv, jnp.concatenate([q8, zero], -1), jnp.concatenate([zero, q8], -1))
        kn = proj_s[:, C_K:C_K + D_KV].reshape(bd, 1, D_KV)
        vn = proj_s[:, C_V:C_V + D_KV].reshape(bd, 1, D_KV)
        sink_rows = jnp.broadcast_to(sinks[:, None], (ATT_HEADS, LANE))
        att_s, sk, sv = _attn_sample(qb, kn, vn, kc_all, vc_all, sink_rows, tab_s, l, kv_new, tb)
        kv_new = (sk, sv)

        m_tile = jnp.pad(state_mlstm_m[l], ((0, 0), (GATE_LANE, LANE - GATE_LANE - MLSTM_HEADS)))
        (conv_new, xs, xdt, bm, cm, ea, ks, n_new, gq, esw, sw, den, mt) = _sample_pre(
            proj_s, state_conv[l].reshape(bd, (CONV_WIDTH - 1) * CONV_DIM), cw, cb, dtb, alog, bi, bf,
            state_mlstm_n[l].reshape(bd, D_MLSTM), m_tile)
        xdt3 = xdt.reshape(bd, 1, D_SSM)
        b3 = bm.reshape(bd, 1, SSM_GROUPS * D_STATE)
        c3 = cm.reshape(bd, 1, SSM_GROUPS * D_STATE)
        q3 = proj_s[:, C_MQ:C_MQ + D_MLSTM].reshape(bd, 1, D_MLSTM)
        k3 = ks.reshape(bd, 1, D_MLSTM)
        v3 = proj_s[:, C_MV:C_MV + D_MLSTM].reshape(bd, 1, D_MLSTM)
        tf = 512
        if (mp // tm_mlp) * (w_up.shape[2] // tf) == bd:
            side = (ea, ssm_all, xdt3, b3, c3, gq, esw, mem_all, q3, k3, v3, s_new, c_new)
            hp, s_new, y3, c_new, qc3 = _mlp(x1, g_mlp, w_up, w_down, gf, l, last, tm_mlp, tf, side)
        else:
            hp = _mlp(x1, g_mlp, w_up, w_down, gf, l, last, tm_mlp, tf)
            s_new, y3 = _ssm_sample(ea, ssm_all, xdt3, b3, c3, l, s_new, tb)
            c_new, qc3 = _mlstm_sample(gq, esw, mem_all, q3, k3, v3, l, c_new, tb)
        ys, hms = _sample_post(proj_s, y3.reshape(bd, D_SSM), xs, dskip_rep, wns, qc3.reshape(bd, D_MLSTM),
                               gq, sw, den, wnm)
        x1s = _outproj(att_s.reshape(bd, D_ATT), ys, hms, hs, w_out, l, bd)
        hs = _mlp(x1s, g_mlp, w_up, w_down, gf, l, last, bd, 512)
        st_s.append((conv_new.reshape(bd, CONV_WIDTH - 1, CONV_DIM),
                     n_new.reshape(bd, MLSTM_HEADS, MLSTM_HEAD_DIM),
                     mt[:, GATE_LANE:GATE_LANE + MLSTM_HEADS]))

    y_prompt = hp.reshape(bsz, seq, d_model)
    y_sample = hs.reshape(bd, 1, d_model)
    p_out = [jnp.stack([s[i] for s in st_p]) for i in range(7)]
    s_conv, s_n, s_m = [jnp.stack([s[i] for s in st_s]) for i in range(3)]
    s_k = kv_new[0].reshape(depth, bd, win, ATT_KV_HEADS, ATT_HEAD_DIM)
    s_v = kv_new[1].reshape(depth, bd, win, ATT_KV_HEADS, ATT_HEAD_DIM)
    s_ssm = s_new.reshape(depth, bd, SSM_HEADS, SSM_HEAD_DIM, D_STATE)
    s_c = c_new.reshape(depth, bd, MLSTM_HEADS, MLSTM_HEAD_DIM, MLSTM_HEAD_DIM)
    return (y_prompt, y_sample, *p_out, s_k, s_v, s_conv, s_ssm, s_c, s_n, s_m)
```

```python
import functools
import math

import jax
import jax.numpy as jnp
from jax import lax
from jax.experimental import pallas as pl
from jax.experimental.pallas import tpu as pltpu

F32 = jnp.float32
BF16 = jnp.bfloat16
HIGHEST = lax.Precision.HIGHEST

D_MODEL = 2048
EPS = 1e-6
PAST_LEN = 8192
ATT_HEAD_DIM = 64
ATT_HEADS = 8
ATT_KV_HEADS = 2
ATT_GROUP = ATT_HEADS // ATT_KV_HEADS
D_ATT = ATT_HEADS * ATT_HEAD_DIM
D_KV = ATT_KV_HEADS * ATT_HEAD_DIM
WINDOW = 128
ROPE_THETA = 500000.0
ROPE_DIM = ATT_HEAD_DIM // 4
SSM_HEAD_DIM = 64
SSM_HEADS = 16
D_SSM = SSM_HEADS * SSM_HEAD_DIM
SSM_GROUPS = 2
D_STATE = 128
CONV_WIDTH = 4
CONV_DIM = D_SSM + 2 * SSM_GROUPS * D_STATE
CHUNK = 128
MLSTM_HEADS = 4
MLSTM_HEAD_DIM = 128
D_MLSTM = MLSTM_HEADS * MLSTM_HEAD_DIM
D_FF = 4 * D_MODEL

C_XBC = 0
C_Q = 1536
C_Z = 2048
C_MQ = 3072
C_MK = 3584
C_MV = 4096
C_MO = 4608
C_K = 5120
C_V = 5248
C_G = 5376
NP = 5632
GATE_LANE = 16

LANE = 128
VMEM_LIMIT = 60 * 1024 * 1024


def _cparams(sem):
    return pltpu.CompilerParams(dimension_semantics=sem, vmem_limit_bytes=VMEM_LIMIT)


def _silu(x):
    return x * jax.nn.sigmoid(x)


def _dot(a, b):
    return jnp.dot(a.astype(BF16), b.astype(BF16), preferred_element_type=F32)


def _dot_nt(a, b):
    return lax.dot_general(a.astype(BF16), b.astype(BF16), (((1,), (1,)), ((), ())),
                           preferred_element_type=F32)


def _dot_tn(a, b):
    return lax.dot_general(a.astype(BF16), b.astype(BF16), (((0,), (0,)), ((), ())),
                           preferred_element_type=F32)


def _tril(n):
    row = lax.broadcasted_iota(jnp.int32, (n, n), 0)
    col = lax.broadcasted_iota(jnp.int32, (n, n), 1)
    return row >= col


def _cumsum_rows(x):
    t = _tril(x.shape[0]).astype(F32)
    return jnp.dot(t, x, precision=HIGHEST, preferred_element_type=F32)


def _rms_matmul_kernel(x_ref, g_ref, w_ref, o_ref, u_ref):
    @pl.when(pl.program_id(1) == 0)
    def _():
        x = x_ref[...]
        ms = jnp.mean(x * x, axis=-1, keepdims=True)
        u_ref[...] = (x * lax.rsqrt(ms + EPS) * g_ref[...]).astype(BF16)

    o_ref[...] = jnp.dot(u_ref[...], w_ref[...], preferred_element_type=F32)


def _rms_matmul(x, g, w, tm, tn):
    m, k = x.shape
    n = w.shape[1]
    return pl.pallas_call(
        _rms_matmul_kernel,
        grid=(m // tm, n // tn),
        in_specs=[pl.BlockSpec((tm, k), lambda i, j: (i, 0)),
                  pl.BlockSpec((1, k), lambda i, j: (0, 0)),
                  pl.BlockSpec((k, tn), lambda i, j: (0, j))],
        out_specs=pl.BlockSpec((tm, tn), lambda i, j: (i, j)),
        out_shape=jax.ShapeDtypeStruct((m, n), F32),
        scratch_shapes=[pltpu.VMEM((tm, k), BF16)],
        compiler_params=_cparams(("parallel", "arbitrary")),
        name="rms_inproj",
    )(x, g, w)


_SRC_Q, _SRC_K, _SRC_V, _SRC_Z, _SRC_XBC, _SRC_DT, _SRC_MQ = 0, 512, 640, 768, 1792, 3328, 3344
_SRC_MI = _SRC_MQ + 4 * D_MLSTM
IN_WIDTH = _SRC_MI + 2 * MLSTM_HEADS


def _prep_w_in_kernel(w_ref, o_ref):
    def put(dst, src, n):
        o_ref[:, dst:dst + n] = w_ref[src:src + n, :].T.astype(BF16)

    put(C_XBC, _SRC_XBC, CONV_DIM)
    put(C_Q, _SRC_Q, D_ATT)
    put(C_Z, _SRC_Z, D_SSM)
    put(C_MQ, _SRC_MQ, 4 * D_MLSTM)
    put(C_K, _SRC_K, D_KV)
    put(C_V, _SRC_V, D_KV)
    tk = w_ref.shape[1]
    dt = w_ref[_SRC_DT:_SRC_DT + SSM_HEADS, :]
    gates = w_ref[_SRC_MI:_SRC_MI + 2 * MLSTM_HEADS, :]
    pad = jnp.zeros((LANE - GATE_LANE - 2 * MLSTM_HEADS, tk), F32)
    lo = jnp.concatenate([dt, gates, pad], axis=0)
    hi = jnp.concatenate([jnp.zeros_like(dt), pltpu.roll(gates, MLSTM_HEADS, 0), pad], axis=0)
    o_ref[:, C_G:C_G + LANE] = lo.T.astype(BF16)
    o_ref[:, C_G + LANE:C_G + 2 * LANE] = hi.T.astype(BF16)


def _prep_w_in(w_in, layer, tk):
    depth, k, n = w_in.shape
    assert n == IN_WIDTH and n % 8 == 0
    wt = jnp.swapaxes(w_in, 1, 2).reshape(depth * n, k)
    return pl.pallas_call(
        _prep_w_in_kernel,
        grid=(k // tk,),
        in_specs=[pl.BlockSpec((n, tk), lambda i: (layer, i))],
        out_specs=pl.BlockSpec((tk, NP), lambda i: (i, 0)),
        out_shape=jax.ShapeDtypeStruct((k, NP), BF16),
        compiler_params=_cparams(("parallel",)),
        name="prep_w_in",
    )(wt)


def _outproj_kernel(a_ref, y_ref, h_ref, x_ref, w_ref, o_ref, wb_ref):
    @pl.when(pl.program_id(0) == 0)
    def _():
        wb_ref[...] = w_ref[...].astype(BF16)

    acc = jnp.dot(a_ref[...], wb_ref[0:D_ATT, :], preferred_element_type=F32)
    acc = acc + jnp.dot(y_ref[...], wb_ref[D_ATT:D_ATT + D_SSM, :], preferred_element_type=F32)
    acc = acc + jnp.dot(h_ref[...], wb_ref[D_ATT + D_SSM:, :], preferred_element_type=F32)
    o_ref[...] = x_ref[...] + acc


def _outproj(att, y, h, x, w, layer, tm):
    m, n = x.shape
    k = w.shape[1]
    rows = lambda width: pl.BlockSpec((tm, width), lambda i: (i, 0))
    return pl.pallas_call(
        _outproj_kernel,
        grid=(m // tm,),
        in_specs=[rows(D_ATT), rows(D_SSM), rows(D_MLSTM), rows(n),
                  pl.BlockSpec((None, k, n), lambda i: (layer, 0, 0), pipeline_mode=pl.Buffered(1))],
        out_specs=rows(n),
        out_shape=jax.ShapeDtypeStruct((m, n), F32),
        scratch_shapes=[pltpu.VMEM((k, n), BF16)],
        compiler_params=_cparams(("arbitrary",)),
        name="outproj",
    )(att, y, h, x, w)


N_MLP_IN, N_SSM_SIDE_IN, N_MLSTM_SIDE_IN = 5, 5, 6


def _mlp_kernel(*refs, final_norm, side, n_prev):
    x_ref, g_ref, wu_ref, wd_ref, gf_ref = refs[:N_MLP_IN]
    n_side = N_SSM_SIDE_IN + N_MLSTM_SIDE_IN if side else 0
    side_in = refs[N_MLP_IN:N_MLP_IN + n_side]
    outs = refs[N_MLP_IN + n_side + n_prev:]
    o_ref, u_ref = outs[0], outs[-1]
    j = pl.program_id(1)

    @pl.when(j == 0)
    def _():
        x = x_ref[...]
        ms = jnp.mean(x * x, axis=-1, keepdims=True)
        u_ref[...] = (x * lax.rsqrt(ms + EPS) * g_ref[...]).astype(BF16)
        o_ref[...] = x

    h = _dot(u_ref[...], wu_ref[...])
    h = jnp.square(jnp.maximum(h, 0.0))
    o_ref[...] += _dot(h, wd_ref[...])

    if side:
        seq = pl.program_id(0) * pl.num_programs(1) + j
        so_ref, y_ref, co_ref, qc_ref = outs[1:5]
        _ssm_sample_one(0, seq, *side_in[:N_SSM_SIDE_IN], so_ref, y_ref)
        _mlstm_sample_one(0, seq, *side_in[N_SSM_SIDE_IN:], co_ref, qc_ref)

    if final_norm:
        @pl.when(j == pl.num_programs(1) - 1)
        def _():
            y = o_ref[...]
            ms = jnp.mean(y * y, axis=-1, keepdims=True)
            o_ref[...] = y * lax.rsqrt(ms + EPS) * gf_ref[...]


def _mlp(x, g, wu, wd, gf, layer, final_norm, tm, tf, side=None):
    m, d = x.shape
    ff = wu.shape[2]
    nj = ff // tf
    in_specs = [pl.BlockSpec((tm, d), lambda i, j: (i, 0)),
                pl.BlockSpec((1, d), lambda i, j: (0, 0)),
                pl.BlockSpec((None, d, tf), lambda i, j: (layer, 0, j)),
                pl.BlockSpec((None, tf, d), lambda i, j: (layer, j, 0)),
                pl.BlockSpec((1, d), lambda i, j: (0, 0))]
    out_specs = [pl.BlockSpec((tm, d), lambda i, j: (i, 0))]
    out_shape = [jax.ShapeDtypeStruct((m, d), F32)]
    operands = [x, g, wu, wd, gf]
    aliases = {}
    n_prev = 0
    if side is not None:
        ea, ssm, xdt3, b3, c3, gq, esw, mem, q3, k3, v3, prev_s, prev_c = side
        depth, bd = ssm.shape[0], ssm.shape[1]
        assert (m // tm) * nj == bd
        smem = pl.BlockSpec(memory_space=pltpu.SMEM)
        one = lambda n: pl.BlockSpec((1, 1, n), lambda i, j: (i * nj + j, 0, 0))
        state = lambda r, n: pl.BlockSpec((None, 1, r, n), lambda i, j: (layer, i * nj + j, 0, 0))
        in_specs += [smem, state(D_SSM, D_STATE), one(D_SSM), one(SSM_GROUPS * D_STATE), one(SSM_GROUPS * D_STATE),
                     smem, smem, state(D_MLSTM, MLSTM_HEAD_DIM), one(D_MLSTM), one(D_MLSTM), one(D_MLSTM)]
        operands += [ea, ssm, xdt3, b3, c3, gq, esw, mem, q3, k3, v3]
        out_specs += [state(D_SSM, D_STATE), one(D_SSM), state(D_MLSTM, MLSTM_HEAD_DIM), one(D_MLSTM)]
        out_shape += [jax.ShapeDtypeStruct((depth, bd, D_SSM, D_STATE), F32),
                      jax.ShapeDtypeStruct((bd, 1, D_SSM), F32),
                      jax.ShapeDtypeStruct((depth, bd, D_MLSTM, MLSTM_HEAD_DIM), F32),
                      jax.ShapeDtypeStruct((bd, 1, D_MLSTM), F32)]
        if prev_s is not None:
            aliases = {len(operands): 1, len(operands) + 1: 3}
            operands += [prev_s, prev_c]
            in_specs += [pl.BlockSpec(memory_space=pl.ANY)] * 2
            n_prev = 2
    res = pl.pallas_call(
        functools.partial(_mlp_kernel, final_norm=final_norm, side=side is not None, n_prev=n_prev),
        grid=(m // tm, nj),
        in_specs=in_specs,
        out_specs=out_specs,
        out_shape=out_shape,
        input_output_aliases=aliases,
        scratch_shapes=[pltpu.VMEM((tm, d), BF16)],
        compiler_params=_cparams(("parallel", "arbitrary")),
        name="mlp",
    )(*operands)
    return res[0] if side is None else res


def _rope_tables(pos):
    half = ROPE_DIM // 2
    inv = jnp.power(jnp.float32(ROPE_THETA), -jnp.arange(half, dtype=jnp.float32) / half)
    ang = pos.astype(jnp.float32)[:, None] * inv[None, :]
    cos = jnp.cos(ang)
    sin = jnp.sin(ang)
    n = pos.shape[0]
    rest = ATT_HEAD_DIM - ROPE_DIM
    c = jnp.concatenate([cos, cos, jnp.ones((n, rest), F32)], axis=1)
    s1 = jnp.concatenate([-sin, jnp.zeros((n, half + rest), F32)], axis=1)
    s2 = jnp.concatenate([jnp.zeros((n, half), F32), sin, jnp.zeros((n, rest), F32)], axis=1)
    rep = LANE // ATT_HEAD_DIM
    return jnp.tile(c, (1, rep)), jnp.tile(s1, (1, rep)), jnp.tile(s2, (1, rep))


def _rope(x, c, s1, s2):
    width = x.shape[-1]
    rep = width // LANE
    half = ROPE_DIM // 2
    if rep > 1:
        c = jnp.concatenate([c] * rep, axis=-1)
        s1 = jnp.concatenate([s1] * rep, axis=-1)
        s2 = jnp.concatenate([s2] * rep, axis=-1)
    axis = x.ndim - 1
    return x * c + pltpu.roll(x, width - half, axis) * s1 + pltpu.roll(x, half, axis) * s2


def _attn_prompt_init(ko_ref, vo_ref):
    ko_ref[...] = jnp.zeros_like(ko_ref)
    vo_ref[...] = jnp.zeros_like(vo_ref)


def _attn_prompt_body(j, sink_ref, q_ref, k_ref, v_ref, c_ref, s1_ref, s2_ref, o_ref, ko_ref, vo_ref):
    w = WINDOW
    c = c_ref[...]
    s1 = s1_ref[...]
    s2 = s2_ref[...]
    krot = _rope(k_ref[...], c, s1, s2)
    v = v_ref[...]
    qrot = _rope(q_ref[...], c, s1, s2)
    kk = jnp.concatenate([ko_ref[...], krot], axis=0).astype(BF16)
    vv = jnp.concatenate([vo_ref[...], v], axis=0).astype(BF16)

    row = lax.broadcasted_iota(jnp.int32, (w, 2 * w), 0)
    col = lax.broadcasted_iota(jnp.int32, (w, 2 * w), 1)
    first_col = jnp.where(j > 0, 0, w)
    mask = (col >= row) & (col <= row + w) & (col >= first_col)
    scale = ATT_HEAD_DIM ** -0.5

    for h in range(ATT_KV_HEADS):
        kh = kk[:, h * ATT_HEAD_DIM:(h + 1) * ATT_HEAD_DIM]
        vh = vv[:, h * ATT_HEAD_DIM:(h + 1) * ATT_HEAD_DIM]
        for g in range(ATT_GROUP):
            hg = h * ATT_GROUP + g
            qh = qrot[:, hg * ATT_HEAD_DIM:(hg + 1) * ATT_HEAD_DIM]
            s = _dot_nt(qh, kh) * scale
            s = jnp.where(mask, s, -jnp.inf)
            sink = sink_ref[hg]
            m = jnp.maximum(jnp.max(s, axis=-1, keepdims=True), sink)
            p = jnp.exp(s - m)
            denom = jnp.sum(p, axis=-1, keepdims=True) + jnp.exp(sink - m)
            p = p / denom
            o = _dot(p, vh)
            o_ref[:, hg * ATT_HEAD_DIM:(hg + 1) * ATT_HEAD_DIM] = o.astype(o_ref.dtype)
            yield

    ko_ref[...] = krot
    vo_ref[...] = v


def _group_rmsnorm(y, w, groups):
    width = y.shape[-1] // groups
    outs = []
    for g in range(groups):
        yg = y[:, g * width:(g + 1) * width]
        ms = jnp.mean(yg * yg, axis=-1, keepdims=True)
        outs.append(yg * lax.rsqrt(ms + EPS) * w[:, g * width:(g + 1) * width])
    return jnp.concatenate(outs, axis=-1)


CONV_PAD = 8


def _ssd_prompt_init(s_ref, tail_ref):
    tail_ref[...] = jnp.zeros_like(tail_ref)
    s_ref[...] = jnp.zeros_like(s_ref)


def _ssd_prompt_body(xbc_ref, z_ref, g_ref, cw_ref, cb_ref, dtb_ref, alog_ref, dskip_ref, wn_ref, spread_ref,
                     y_ref, s_ref, tail_ref, ybuf_ref, xw_ref):
    q = CHUNK
    pad = CONV_PAD
    hp = SSM_HEAD_DIM
    heads_per_group = SSM_HEADS // SSM_GROUPS
    gw = heads_per_group * hp

    x = xbc_ref[...]
    cw = cw_ref[...]
    row8 = lax.broadcasted_iota(jnp.int32, (pad, CONV_DIM), 0)
    acc = None
    for j in range(CONV_WIDTH - 1):
        shift = CONV_WIDTH - 1 - j
        rolled = pltpu.roll(x, shift, 0)
        head = jnp.where(row8 < shift, tail_ref[j * pad:(j + 1) * pad, :], rolled[0:pad, :])
        tail_ref[j * pad:(j + 1) * pad, :] = rolled[0:pad, :]
        term = jnp.concatenate([head, rolled[pad:, :]], axis=0) * cw[j:j + 1, :]
        acc = term if acc is None else acc + term
    acc = acc + x * cw[CONV_WIDTH - 1:CONV_WIDTH, :]
    xc = _silu(acc + cb_ref[...])

    xs = xc[:, 0:D_SSM]
    bm = xc[:, D_SSM:D_SSM + SSM_GROUPS * D_STATE]
    cm = xc[:, D_SSM + SSM_GROUPS * D_STATE:]

    dt = jax.nn.softplus(g_ref[:, 0:LANE] + dtb_ref[...])
    a_neg = -jnp.exp(alog_ref[...])
    a_col = _cumsum_rows(dt * a_neg)
    a_row = a_col.T
    dt_row = dt.T
    wk_row = jnp.exp(a_row[:, q - 1:q] - a_row) * dt_row
    tril = _tril(q)
    xs_t = xs.T
    ea_col = jnp.exp(a_col)
    ea_hi = ea_col.astype(BF16)
    ea_lo = (ea_col - ea_hi.astype(F32)).astype(BF16)
    spread = spread_ref[...]
    ea_full = (jnp.dot(ea_hi, spread, preferred_element_type=F32)
               + jnp.dot(ea_lo, spread, preferred_element_type=F32))
    yield

    cs_parts = []
    for g in range(SSM_GROUPS):
        bg = bm[:, g * D_STATE:(g + 1) * D_STATE]
        cg = cm[:, g * D_STATE:(g + 1) * D_STATE]
        cb = _dot_nt(cg, bg)
        cs_parts.append(_dot_nt(cg, s_ref[g * gw:(g + 1) * gw, :]))
        for r in range(heads_per_group):
            h = g * heads_per_group + r
            seg = a_col[:, h:h + 1] - a_row[h:h + 1, :]
            wmat = jnp.exp(jnp.where(tril, seg, -jnp.inf)) * cb * dt_row[h:h + 1, :]
            ybuf_ref[:, h * hp:(h + 1) * hp] = _dot(wmat, xs[:, h * hp:(h + 1) * hp])
            xw_ref[r * hp:(r + 1) * hp, :] = xs_t[h * hp:(h + 1) * hp, :] * wk_row[h:h + 1, :]
            yield
        upd = _dot(xw_ref[...], bg)
        for r in range(heads_per_group):
            h = g * heads_per_group + r
            decay = jnp.exp(a_row[h:h + 1, q - 1:q])
            s_ref[h * hp:(h + 1) * hp, :] = s_ref[h * hp:(h + 1) * hp, :] * decay + upd[r * hp:(r + 1) * hp, :]
        yield

    y = ybuf_ref[...] + jnp.concatenate(cs_parts, axis=1) * ea_full
    y = y + dskip_ref[...] * xs
    y = y * _silu(z_ref[...])
    y_ref[...] = _group_rmsnorm(y, wn_ref[...], SSM_GROUPS).astype(y_ref.dtype)


def _mlstm_prompt_init(c_ref, n_ref, m_ref):
    c_ref[...] = jnp.zeros_like(c_ref)
    n_ref[...] = jnp.zeros_like(n_ref)
    m_ref[...] = jnp.zeros_like(m_ref)


def _mlstm_prompt_body(q_ref, k_ref, v_ref, o_ref, g_ref, bi_ref, bf_ref, wn_ref, h_ref, c_ref, n_ref, m_ref):
    t = CHUNK
    d = MLSTM_HEAD_DIM

    gi = g_ref[:, 0:LANE] + bi_ref[...]
    gf = g_ref[:, LANE:2 * LANE] + bf_ref[...]
    b_col = _cumsum_rows(jax.nn.log_sigmoid(gf))
    b_row = b_col.T
    i_row = gi.T
    tril = _tril(t)
    k_all = k_ref[...] * (d ** -0.5)

    for h in range(MLSTM_HEADS):
        ln = GATE_LANE + h
        bq = b_col[:, ln:ln + 1]
        logw = jnp.where(tril, bq - b_row[ln:ln + 1, :] + i_row[ln:ln + 1, :], -jnp.inf)
        m_prev = m_ref[h:h + 1, 0:1]
        log_inter = bq + m_prev
        mt = jnp.maximum(log_inter, jnp.max(logw, axis=-1, keepdims=True))
        qh = q_ref[:, h * d:(h + 1) * d]
        kh = k_all[:, h * d:(h + 1) * d]
        vh = v_ref[:, h * d:(h + 1) * d]
        ch = c_ref[h * d:(h + 1) * d, :]
        nh = n_ref[h:h + 1, :]
        sw = jnp.exp(logw - mt) * _dot_nt(qh, kh)
        gq = jnp.exp(log_inter - mt)
        num = _dot(sw, vh) + _dot(qh, ch) * gq
        den = jnp.sum(sw, axis=-1, keepdims=True) + jnp.sum(qh * nh, axis=-1, keepdims=True) * gq
        hh = num / jnp.maximum(jnp.abs(den), jnp.exp(-mt))

        m_new = mt[t - 1:t, :]
        b_last = b_col[t - 1:t, ln:ln + 1]
        wk = jnp.exp(b_last - bq + gi[:, ln:ln + 1] - m_new)
        g_end = jnp.exp(b_last + m_prev - m_new)
        kw = kh * wk
        c_ref[h * d:(h + 1) * d, :] = ch * g_end + _dot_tn(kw, vh)
        n_ref[h:h + 1, :] = nh * g_end + jnp.sum(kw, axis=0, keepdims=True)
        m_ref[h:h + 1, :] = jnp.broadcast_to(m_new, (1, LANE))

        ms = jnp.mean(hh * hh, axis=-1, keepdims=True)
        hn = hh * lax.rsqrt(ms + EPS) * wn_ref[:, h * d:(h + 1) * d]
        h_ref[:, h * d:(h + 1) * d] = (hn * jax.nn.sigmoid(o_ref[:, h * d:(h + 1) * d])).astype(h_ref.dtype)
        yield

def _interleave(stages):
    live = [[gen, 0, n] for gen, n in stages]
    while live:
        item = min(live, key=lambda s: s[1] / s[2])
        try:
            next(item[0])
            item[1] += 1
        except StopIteration:
            live.remove(item)


PROJ_TILE = 256


def _project_next(x_ref, g_ref, w_ref, pnext_ref):
    x = x_ref[...]
    u = (x * lax.rsqrt(jnp.mean(x * x, axis=-1, keepdims=True) + EPS) * g_ref[...]).astype(BF16)
    for ct in range(NP // PROJ_TILE):
        cols = slice(ct * PROJ_TILE, (ct + 1) * PROJ_TILE)
        pnext_ref[:, cols] = jnp.dot(u, w_ref[:, cols], preferred_element_type=F32)
        yield


def _front_kernel(sink_ref, x_ref, x0_ref, g_ref, w_ref, c_ref, s1_ref, s2_ref,
                  cw_ref, cb_ref, dtb_ref, alog_ref, dskip_ref, wns_ref, spread_ref, bi_ref, bf_ref, wnm_ref,
                  att_ref, ko_ref, vo_ref, y_ref, s_ref, h_ref, mem_ref, n_ref, m_ref, ptail_ref,
                  pcur_ref, pnext_ref, tail_ref, ybuf_ref, xw_ref):
    j = pl.program_id(1)
    first = (pl.program_id(0) == 0) & (j == 0)

    @pl.when(first)
    def _():
        for _ in _project_next(x0_ref, g_ref, w_ref, pcur_ref):
            pass

    @pl.when(jnp.logical_not(first))
    def _():
        pcur_ref[...] = pnext_ref[...]

    @pl.when(j == 0)
    def _():
        _attn_prompt_init(ko_ref, vo_ref)
        _ssd_prompt_init(s_ref, tail_ref)
        _mlstm_prompt_init(mem_ref, n_ref, m_ref)

    seg = lambda col, width: pcur_ref.at[:, col:col + width]
    g_gate = seg(C_G, 2 * LANE)
    xbc = seg(C_XBC, CONV_DIM)
    ptail_ref[...] = xbc[CHUNK - CONV_PAD:CHUNK, :]
    _interleave([
        (_project_next(x_ref, g_ref, w_ref, pnext_ref), NP // PROJ_TILE + 1),
        (_attn_prompt_body(j, sink_ref, seg(C_Q, D_ATT), seg(C_K, D_KV), seg(C_V, D_KV), c_ref, s1_ref, s2_ref,
                           att_ref, ko_ref, vo_ref), ATT_HEADS + 1),
        (_ssd_prompt_body(xbc, seg(C_Z, D_SSM), g_gate, cw_ref, cb_ref, dtb_ref, alog_ref, dskip_ref, wns_ref,
                          spread_ref, y_ref, s_ref, tail_ref, ybuf_ref, xw_ref), SSM_HEADS + SSM_GROUPS + 2),
        (_mlstm_prompt_body(seg(C_MQ, D_MLSTM), seg(C_MK, D_MLSTM), seg(C_MV, D_MLSTM), seg(C_MO, D_MLSTM),
                            g_gate, bi_ref, bf_ref, wnm_ref, h_ref, mem_ref, n_ref, m_ref), MLSTM_HEADS + 1),
    ])


def _front(x, g_mix, w, sinks, tables, cw, cb, dtb, alog, dskip_rep, wns, bi, bf, wnm, bsz, seq):
    nb = seq // CHUNK
    t = CHUNK
    total = bsz * nb
    c, s1, s2 = tables
    head_of_lane = jnp.arange(D_SSM, dtype=jnp.int32) // SSM_HEAD_DIM
    spread = (jnp.arange(LANE, dtype=jnp.int32)[:, None] == head_of_lane[None, :]).astype(BF16)
    row = lambda b, j: b * nb + j
    full = lambda shape: pl.BlockSpec(shape, lambda b, j: (0,) * len(shape))
    vec = lambda n: full((1, n))
    tab = pl.BlockSpec((t, LANE), lambda b, j: (j, 0))
    rows_out = lambda width: pl.BlockSpec((t, width), lambda b, j: (row(b, j), 0))
    per_seq = lambda r, n: pl.BlockSpec((None, r, n), lambda b, j: (b, 0, 0))
    return pl.pallas_call(
        _front_kernel,
        grid=(bsz, nb),
        in_specs=[pl.BlockSpec(memory_space=pltpu.SMEM),
                  pl.BlockSpec((t, D_MODEL), lambda b, j: (jnp.minimum(row(b, j) + 1, total - 1), 0)),
                  full((t, D_MODEL)),
                  vec(D_MODEL),
                  pl.BlockSpec((D_MODEL, NP), lambda b, j: (0, 0), pipeline_mode=pl.Buffered(1)),
                  tab, tab, tab,
                  full((CONV_WIDTH, CONV_DIM)), vec(CONV_DIM), vec(LANE), vec(LANE), vec(D_SSM), vec(D_SSM),
                  full((LANE, D_SSM)),
                  vec(LANE), vec(LANE), vec(D_MLSTM)],
        out_specs=[rows_out(D_ATT), per_seq(t, D_KV), per_seq(t, D_KV),
                   rows_out(D_SSM), per_seq(D_SSM, D_STATE),
                   rows_out(D_MLSTM), per_seq(D_MLSTM, MLSTM_HEAD_DIM), per_seq(8, LANE), per_seq(8, LANE),
                   per_seq(CONV_PAD, CONV_DIM)],
        out_shape=[jax.ShapeDtypeStruct((bsz * seq, D_ATT), BF16),
                   jax.ShapeDtypeStruct((bsz, t, D_KV), F32),
                   jax.ShapeDtypeStruct((bsz, t, D_KV), F32),
                   jax.ShapeDtypeStruct((bsz * seq, D_SSM), BF16),
                   jax.ShapeDtypeStruct((bsz, D_SSM, D_STATE), F32),
                   jax.ShapeDtypeStruct((bsz * seq, D_MLSTM), BF16),
                   jax.ShapeDtypeStruct((bsz, D_MLSTM, MLSTM_HEAD_DIM), F32),
                   jax.ShapeDtypeStruct((bsz, 8, LANE), F32),
                   jax.ShapeDtypeStruct((bsz, 8, LANE), F32),
                   jax.ShapeDtypeStruct((bsz, CONV_PAD, CONV_DIM), F32)],
        scratch_shapes=[pltpu.VMEM((t, NP), F32),
                        pltpu.VMEM((t, NP), F32),
                        pltpu.VMEM(((CONV_WIDTH - 1) * CONV_PAD, CONV_DIM), F32),
                        pltpu.VMEM((t, D_SSM), F32),
                        pltpu.VMEM((D_SSM // SSM_GROUPS, t), F32)],
        compiler_params=_cparams(("arbitrary", "arbitrary")),
        name="front",
    )(sinks, x, x, g_mix, w, c, s1, s2, cw, cb, dtb, alog, dskip_rep, wns, spread, bi, bf, wnm)


def _stacked_out(prev, depth, shape, block, layer):
    spec = pl.BlockSpec((None,) + block, lambda i: (layer,) + (i,) + (0,) * (len(block) - 1))
    out_shape = jax.ShapeDtypeStruct((depth,) + shape, F32)
    extra_in = [] if prev is None else [prev]
    extra_spec = [] if prev is None else [pl.BlockSpec(memory_space=pl.ANY)]
    return spec, out_shape, extra_in, extra_spec


def _drop_aliased(kernel_fn, n_in, n_prev):
    def wrapped(*refs):
        return kernel_fn(*refs[:n_in], *refs[n_in + n_prev:])
    return wrapped


def _attn_sample_kernel(q_ref, k_ref, v_ref, kc_ref, vc_ref, sink_ref, c_ref, s1_ref, s2_ref,
                        o_ref, ko_ref, vo_ref):
    w = kc_ref.shape[1]
    c = c_ref[...]
    s1 = s1_ref[...]
    s2 = s2_ref[...]
    qb = _rope(q_ref[...], c, s1, s2)
    kn = _rope(k_ref[...], c, s1, s2)
    vn = v_ref[...]
    kc = kc_ref[...]
    vc = vc_ref[...]
    ko_ref[:, 0:w - 1, :] = kc[:, 1:w, :]
    ko_ref[:, w - 1:w, :] = kn
    vo_ref[:, 0:w - 1, :] = vc[:, 1:w, :]
    vo_ref[:, w - 1:w, :] = vn

    scale = ATT_HEAD_DIM ** -0.5
    s = jnp.einsum("bhd,bkd->bhk", qb.astype(BF16), kc.astype(BF16),
                   preferred_element_type=F32) * scale
    sn = jnp.sum(qb * kn, axis=-1, keepdims=True) * scale
    sink = sink_ref[...][None, :, 0:1]
    m = jnp.maximum(jnp.maximum(jnp.max(s, axis=-1, keepdims=True), sn), sink)
    p = jnp.exp(s - m)
    pn = jnp.exp(sn - m)
    denom = jnp.sum(p, axis=-1, keepdims=True) + pn + jnp.exp(sink - m)
    p = p / denom
    pn = pn / denom
    o = jnp.einsum("bhk,bkd->bhd", p.astype(BF16), vc.astype(BF16), preferred_element_type=F32)
    o = o + pn * vn
    head = lax.broadcasted_iota(jnp.int32, o.shape[:2] + (ATT_HEAD_DIM,), 1)
    o_ref[...] = jnp.where(head < ATT_GROUP, o[:, :, 0:ATT_HEAD_DIM], o[:, :, ATT_HEAD_DIM:]).astype(o_ref.dtype)


def _attn_sample(qb, kn, vn, kc, vc, sink_rows, tables, layer, prev, tb):
    depth, bd, w, _ = kc.shape
    c, s1, s2 = tables
    blk3 = lambda r, n: pl.BlockSpec((tb, r, n), lambda i: (i, 0, 0))
    cache = pl.BlockSpec((None, tb, w, LANE), lambda i: (layer, i, 0, 0))
    vec = pl.BlockSpec((1, LANE), lambda i: (0, 0))
    prev_k, prev_v = (None, None) if prev is None else prev
    k_spec, k_shape, k_in, k_in_spec = _stacked_out(prev_k, depth, (bd, w, LANE), (tb, w, LANE), layer)
    v_spec, v_shape, v_in, v_in_spec = _stacked_out(prev_v, depth, (bd, w, LANE), (tb, w, LANE), layer)
    n_in = 9
    n_prev = len(k_in) + len(v_in)
    return pl.pallas_call(
        _drop_aliased(_attn_sample_kernel, n_in, n_prev),
        grid=(bd // tb,),
        in_specs=[blk3(ATT_HEADS, LANE), blk3(1, LANE), blk3(1, LANE), cache, cache,
                  pl.BlockSpec((ATT_HEADS, LANE), lambda i: (0, 0)), vec, vec, vec] + k_in_spec + v_in_spec,
        out_specs=[blk3(ATT_HEADS, ATT_HEAD_DIM), k_spec, v_spec],
        out_shape=[jax.ShapeDtypeStruct((bd, ATT_HEADS, ATT_HEAD_DIM), BF16), k_shape, v_shape],
        input_output_aliases={n_in + t: 1 + t for t in range(n_prev)},
        compiler_params=_cparams(("parallel",)),
        name="attn_sample",
    )(qb, kn, vn, kc, vc, sink_rows, c, s1, s2, *k_in, *v_in)


def _lane_place(cols, lane0):
    m = cols[0].shape[0]
    lane = lax.broadcasted_iota(jnp.int32, (m, LANE), 1)
    out = jnp.zeros((m, LANE), F32)
    for i, col in enumerate(cols):
        out = jnp.where(lane == lane0 + i, col, out)
    return out


def _sample_pre_kernel(proj_ref, cs_ref, cw_ref, cb_ref, dtb_ref, alog_ref, bi_ref, bf_ref, n_ref, m_ref,
                       conv_ref, xs_ref, xdt_ref, bm_ref, cm_ref, ea_ref,
                       ks_ref, nn_ref, g_ref, esw_ref, sw_ref, den_ref, mt_ref):
    xbc = proj_ref[:, C_XBC:C_XBC + CONV_DIM]
    s0 = cs_ref[:, 0:CONV_DIM]
    s1 = cs_ref[:, CONV_DIM:2 * CONV_DIM]
    s2 = cs_ref[:, 2 * CONV_DIM:3 * CONV_DIM]
    cw = cw_ref[...]
    acc = s0 * cw[0:1, :]
    acc = acc + s1 * cw[1:2, :]
    acc = acc + s2 * cw[2:3, :]
    acc = acc + xbc * cw[3:4, :]
    xc = _silu(acc + cb_ref[...])
    conv_ref[:, 0:CONV_DIM] = s1
    conv_ref[:, CONV_DIM:2 * CONV_DIM] = s2
    conv_ref[:, 2 * CONV_DIM:3 * CONV_DIM] = xbc

    xs = xc[:, 0:D_SSM]
    xs_ref[...] = xs
    bm_ref[...] = xc[:, D_SSM:D_SSM + SSM_GROUPS * D_STATE]
    cm_ref[...] = xc[:, D_SSM + SSM_GROUPS * D_STATE:]
    dt = jax.nn.softplus(proj_ref[:, C_G:C_G + LANE] + dtb_ref[...])
    ea_ref[...] = jnp.exp(dt * (-jnp.exp(alog_ref[...])))
    hp = SSM_HEAD_DIM
    for h in range(SSM_HEADS):
        xdt_ref[:, h * hp:(h + 1) * hp] = xs[:, h * hp:(h + 1) * hp] * dt[:, h:h + 1]

    d = MLSTM_HEAD_DIM
    gi = proj_ref[:, C_G:C_G + LANE] + bi_ref[...]
    gf = proj_ref[:, C_G + LANE:C_G + 2 * LANE] + bf_ref[...]
    log_inter = jax.nn.log_sigmoid(gf) + m_ref[...]
    mt = jnp.maximum(log_inter, gi)
    gq = jnp.exp(log_inter - mt)
    esw = jnp.exp(gi - mt)
    ks = proj_ref[:, C_MK:C_MK + D_MLSTM] * (d ** -0.5)
    ks_ref[...] = ks
    qk_cols = []
    qn_cols = []
    for h in range(MLSTM_HEADS):
        qh = proj_ref[:, C_MQ + h * d:C_MQ + (h + 1) * d]
        kh = ks[:, h * d:(h + 1) * d]
        nh = n_ref[:, h * d:(h + 1) * d]
        qk_cols.append(jnp.sum(qh * kh, axis=-1, keepdims=True))
        qn_cols.append(jnp.sum(qh * nh, axis=-1, keepdims=True))
        ln = GATE_LANE + h
        nn_ref[:, h * d:(h + 1) * d] = nh * gq[:, ln:ln + 1] + kh * esw[:, ln:ln + 1]
    sw = esw * _lane_place(qk_cols, GATE_LANE)
    den = sw + _lane_place(qn_cols, GATE_LANE) * gq
    g_ref[...] = gq
    esw_ref[...] = esw
    sw_ref[...] = sw
    den_ref[...] = jnp.maximum(jnp.abs(den), jnp.exp(-mt))
    mt_ref[...] = mt


def _sample_pre(proj, conv_state, cw, cb, dtb, alog, bi, bf, n_state, m_tile):
    bd = proj.shape[0]
    tile = jax.ShapeDtypeStruct((bd, LANE), F32)
    outs = [jax.ShapeDtypeStruct((bd, (CONV_WIDTH - 1) * CONV_DIM), F32),
            jax.ShapeDtypeStruct((bd, D_SSM), F32),
            jax.ShapeDtypeStruct((bd, D_SSM), F32),
            jax.ShapeDtypeStruct((bd, SSM_GROUPS * D_STATE), F32),
            jax.ShapeDtypeStruct((bd, SSM_GROUPS * D_STATE), F32),
            tile,
            jax.ShapeDtypeStruct((bd, D_MLSTM), F32),
            jax.ShapeDtypeStruct((bd, D_MLSTM), F32),
            tile, tile, tile, tile, tile]
    return pl.pallas_call(
        _sample_pre_kernel,
        out_shape=outs,
        compiler_params=pltpu.CompilerParams(vmem_limit_bytes=VMEM_LIMIT),
        name="sample_pre",
    )(proj, conv_state, cw, cb, dtb, alog, bi, bf, n_state, m_tile)


def _column_tile(row):
    return jnp.broadcast_to(row, (LANE, LANE)).T


def _ssm_sample_one(b, gb, ea_ref, s_ref, xdt_ref, b_ref, c_ref, so_ref, y_ref):
    hp = SSM_HEAD_DIM
    heads_per_tile = LANE // hp
    n_tiles = D_SSM // LANE
    tiles_per_group = n_tiles // SSM_GROUPS
    top = lax.broadcasted_iota(jnp.int32, (LANE, 1), 0) < hp
    xrow = xdt_ref[b]
    brow = b_ref[b]
    crow = c_ref[b]
    for t in range(n_tiles):
        g = t // tiles_per_group
        xcol = _column_tile(xrow[:, t * LANE:(t + 1) * LANE])
        bg = brow[:, g * D_STATE:(g + 1) * D_STATE]
        cg = jnp.broadcast_to(crow[:, g * D_STATE:(g + 1) * D_STATE], (8, D_STATE))
        e0 = ea_ref[gb, heads_per_tile * t]
        e1 = ea_ref[gb, heads_per_tile * t + 1]
        decay = jnp.where(top, e0, e1)
        sn = s_ref[b, t * LANE:(t + 1) * LANE, :] * decay + xcol * bg
        so_ref[b, t * LANE:(t + 1) * LANE, :] = sn
        y_ref[b, :, t * LANE:(t + 1) * LANE] = _dot_nt(cg, sn)[0:1, :]


def _ssm_sample_kernel(ea_ref, s_ref, xdt_ref, b_ref, c_ref, so_ref, y_ref):
    tb = s_ref.shape[0]
    base = pl.program_id(0) * tb

    def body(b, carry):
        _ssm_sample_one(b, base + b, ea_ref, s_ref, xdt_ref, b_ref, c_ref, so_ref, y_ref)
        return carry

    lax.fori_loop(0, tb, body, 0)


def _ssm_sample(ea, s, xdt3, b3, c3, layer, prev, tb):
    depth, bd = s.shape[0], s.shape[1]
    blk3 = lambda r, n: pl.BlockSpec((tb, r, n), lambda i: (i, 0, 0))
    s_spec, s_shape, s_in, s_in_spec = _stacked_out(prev, depth, (bd, D_SSM, D_STATE), (tb, D_SSM, D_STATE), layer)
    n_in = 5
    return pl.pallas_call(
        _drop_aliased(_ssm_sample_kernel, n_in, len(s_in)),
        grid=(bd // tb,),
        in_specs=[pl.BlockSpec(memory_space=pltpu.SMEM),
                  pl.BlockSpec((None, tb, D_SSM, D_STATE), lambda i: (layer, i, 0, 0)),
                  blk3(1, D_SSM), blk3(1, SSM_GROUPS * D_STATE),
                  blk3(1, SSM_GROUPS * D_STATE)] + s_in_spec,
        out_specs=[s_spec, blk3(1, D_SSM)],
        out_shape=[s_shape, jax.ShapeDtypeStruct((bd, 1, D_SSM), F32)],
        input_output_aliases={n_in + t: t for t in range(len(s_in))},
        compiler_params=_cparams(("parallel",)),
        name="ssm_sample",
    )(ea, s, xdt3, b3, c3, *s_in)


def _mlstm_sample_one(b, gb, g_ref, esw_ref, c_ref, q_ref, k_ref, v_ref, co_ref, qc_ref):
    d = MLSTM_HEAD_DIM
    qrow = q_ref[b]
    krow = k_ref[b]
    vrow = v_ref[b]
    for h in range(MLSTM_HEADS):
        qcol = _column_tile(qrow[:, h * d:(h + 1) * d])
        kcol = _column_tile(krow[:, h * d:(h + 1) * d])
        ch = c_ref[b, h * d:(h + 1) * d, :]
        g_end = g_ref[gb, GATE_LANE + h]
        wk = esw_ref[gb, GATE_LANE + h]
        qc_ref[b, :, h * d:(h + 1) * d] = jnp.sum(qcol * ch, axis=0, keepdims=True)
        co_ref[b, h * d:(h + 1) * d, :] = ch * g_end + (kcol * wk) * vrow[:, h * d:(h + 1) * d]


def _mlstm_sample_kernel(g_ref, esw_ref, c_ref, q_ref, k_ref, v_ref, co_ref, qc_ref):
    tb = c_ref.shape[0]
    base = pl.program_id(0) * tb

    def body(b, carry):
        _mlstm_sample_one(b, base + b, g_ref, esw_ref, c_ref, q_ref, k_ref, v_ref, co_ref, qc_ref)
        return carry

    lax.fori_loop(0, tb, body, 0)


def _mlstm_sample(gq, esw, c, q3, k3, v3, layer, prev, tb):
    depth, bd = c.shape[0], c.shape[1]
    blk3 = lambda r, n: pl.BlockSpec((tb, r, n), lambda i: (i, 0, 0))
    smem = pl.BlockSpec(memory_space=pltpu.SMEM)
    c_spec, c_shape, c_in, c_in_spec = _stacked_out(prev, depth, (bd, D_MLSTM, MLSTM_HEAD_DIM),
                                                    (tb, D_MLSTM, MLSTM_HEAD_DIM), layer)
    n_in = 6
    return pl.pallas_call(
        _drop_aliased(_mlstm_sample_kernel, n_in, len(c_in)),
        grid=(bd // tb,),
        in_specs=[smem, smem, pl.BlockSpec((None, tb, D_MLSTM, MLSTM_HEAD_DIM), lambda i: (layer, i, 0, 0)),
                  blk3(1, D_MLSTM), blk3(1, D_MLSTM), blk3(1, D_MLSTM)] + c_in_spec,
        out_specs=[c_spec, blk3(1, D_MLSTM)],
        out_shape=[c_shape, jax.ShapeDtypeStruct((bd, 1, D_MLSTM), F32)],
        input_output_aliases={n_in + t: t for t in range(len(c_in))},
        compiler_params=_cparams(("parallel",)),
        name="mlstm_sample",
    )(gq, esw, c, q3, k3, v3, *c_in)


def _sample_post_kernel(proj_ref, y_ref, xs_ref, dskip_ref, wns_ref, qc_ref, g_ref, sw_ref, den_ref, wnm_ref,
                        ys_ref, hs_ref):
    y = y_ref[...] + dskip_ref[...] * xs_ref[...]
    y = y * _silu(proj_ref[:, C_Z:C_Z + D_SSM])
    ys_ref[...] = _group_rmsnorm(y, wns_ref[...], SSM_GROUPS).astype(ys_ref.dtype)

    d = MLSTM_HEAD_DIM
    for h in range(MLSTM_HEADS):
        ln = GATE_LANE + h
        vh = proj_ref[:, C_MV + h * d:C_MV + (h + 1) * d]
        num = sw_ref[:, ln:ln + 1] * vh + qc_ref[:, h * d:(h + 1) * d] * g_ref[:, ln:ln + 1]
        hh = num / den_ref[:, ln:ln + 1]
        ms = jnp.mean(hh * hh, axis=-1, keepdims=True)
        hn = hh * lax.rsqrt(ms + EPS) * wnm_ref[:, h * d:(h + 1) * d]
        gate = jax.nn.sigmoid(proj_ref[:, C_MO + h * d:C_MO + (h + 1) * d])
        hs_ref[:, h * d:(h + 1) * d] = (hn * gate).astype(hs_ref.dtype)


def _sample_post(proj, y, xs, dskip_rep, wns, qc, gq, sw, den, wnm):
    bd = proj.shape[0]
    return pl.pallas_call(
        _sample_post_kernel,
        out_shape=[jax.ShapeDtypeStruct((bd, D_SSM), BF16), jax.ShapeDtypeStruct((bd, D_MLSTM), BF16)],
        compiler_params=pltpu.CompilerParams(vmem_limit_bytes=VMEM_LIMIT),
        name="sample_post",
    )(proj, y, xs, dskip_rep, wns, qc, gq, sw, den, wnm)


def _lane_vec(v, lane0):
    return jnp.pad(v.astype(F32), (lane0, LANE - lane0 - v.shape[0]))[None, :]


def _pick(n, candidates):
    for c in candidates:
        if n % c == 0:
            return c
    return n


def kernel(x_prompt, x_sample, cache_swa_k, cache_swa_v, state_conv, state_ssm, state_mlstm_C, state_mlstm_n,
           state_mlstm_m, w_norm_mix, w_in, attn_sinks, conv_w, conv_b, dt_bias, a_log, d_skip, w_norm_ssm,
           igate_b, fgate_b, w_norm_mlstm, w_out, w_norm_mlp, w_up, w_down, w_norm_final):
    bsz, seq, d_model = x_prompt.shape
    bd = x_sample.shape[0]
    depth = w_in.shape[0]
    win = cache_swa_k.shape[2]
    assert x_sample.shape[1] == 1 and seq % CHUNK == 0 and d_model == D_MODEL

    mp = bsz * seq
    tm_p = _pick(mp, (1024, 512, 256, 128))
    tm_mlp = _pick(mp, (1024, 512, 256, 128))
    tb = _pick(bd, (8,))

    hp = x_prompt.reshape(mp, d_model)
    hs = x_sample.reshape(bd, d_model)
    tab_p = _rope_tables(jnp.arange(seq, dtype=jnp.int32))
    tab_s = _rope_tables(jnp.arange(1, dtype=jnp.int32) + PAST_LEN)
    gf = w_norm_final[None, :]
    kc_all = cache_swa_k.reshape(depth, bd, win, D_KV)
    vc_all = cache_swa_v.reshape(depth, bd, win, D_KV)
    ssm_all = state_ssm.reshape(depth, bd, D_SSM, D_STATE)
    mem_all = state_mlstm_C.reshape(depth, bd, D_MLSTM, MLSTM_HEAD_DIM)

    st_p, st_s = [], []
    kv_new = s_new = c_new = None
    for l in range(depth):
        w_in_l = _prep_w_in(w_in, l, 256)
        g_mix = w_norm_mix[l][None, :]
        g_mlp = w_norm_mlp[l][None, :]
        cw = conv_w[l]
        cb = conv_b[l][None, :]
        dtb = _lane_vec(dt_bias[l], 0)
        alog = _lane_vec(a_log[l], 0)
        dskip_rep = jnp.repeat(d_skip[l].astype(F32), SSM_HEAD_DIM)[None, :]
        wns = w_norm_ssm[l][None, :]
        bi = _lane_vec(igate_b[l], GATE_LANE)
        bf = _lane_vec(fgate_b[l], GATE_LANE)
        wnm = w_norm_mlstm[l][None, :]
        sinks = attn_sinks[l].reshape(ATT_HEADS).astype(F32)
        last = l == depth - 1

        att, pk, pv, y, p_ssm, hm, p_c, p_n, p_m, p_tail = _front(
            hp, g_mix, w_in_l, sinks, tab_p, cw, cb, dtb, alog, dskip_rep, wns, bi, bf, wnm, bsz, seq)
        x1 = _outproj(att, y, hm, hp, w_out, l, _pick(mp, (512, 256, 128)))
        p_conv = p_tail[:, CONV_PAD - (CONV_WIDTH - 1):, :]
        st_p.append((pk.reshape(bsz, WINDOW, ATT_KV_HEADS, ATT_HEAD_DIM),
                     pv.reshape(bsz, WINDOW, ATT_KV_HEADS, ATT_HEAD_DIM),
                     p_conv,
                     p_ssm.reshape(bsz, SSM_HEADS, SSM_HEAD_DIM, D_STATE),
                     p_c.reshape(bsz, MLSTM_HEADS, MLSTM_HEAD_DIM, MLSTM_HEAD_DIM),
                     p_n[:, 0:MLSTM_HEADS, :],
                     p_m[:, 0:MLSTM_HEADS, 0]))

        proj_s = _rms_matmul(hs, g_mix, w_in_l, bd, NP // 4)
        q8 = proj_s[:, C_Q:C_Q + D_ATT].reshape(bd, ATT_HEADS, ATT_HEAD_DIM)
        zero = jnp.zeros_like(q8)
        first_kv = (jnp.arange(ATT_HEADS) < ATT_GROUP)[None, :, None]
        qb = jnp.where(first_kv, jnp.concatenate([q8, zero], -1), jnp.concatenate([zero, q8], -1))
        kn = proj_s[:, C_K:C_K + D_KV].reshape(bd, 1, D_KV)
        vn = proj_s[:, C_V:C_V + D_KV].reshape(bd, 1, D_KV)
        sink_rows = jnp.broadcast_to(sinks[:, None], (ATT_HEADS, LANE))
        att_s, sk, sv = _attn_sample(qb, kn, vn, kc_all, vc_all, sink_rows, tab_s, l, kv_new, tb)
        kv_new = (sk, sv)

        m_tile = jnp.pad(state_mlstm_m[l], ((0, 0), (GATE_LANE, LANE - GATE_LANE - MLSTM_HEADS)))
        (conv_new, xs, xdt, bm, cm, ea, ks, n_new, gq, esw, sw, den, mt) = _sample_pre(
            proj_s, state_conv[l].reshape(bd, (CONV_WIDTH - 1) * CONV_DIM), cw, cb, dtb, alog, bi, bf,
            state_mlstm_n[l].reshape(bd, D_MLSTM), m_tile)
        xdt3 = xdt.reshape(bd, 1, D_SSM)
        b3 = bm.reshape(bd, 1, SSM_GROUPS * D_STATE)
        c3 = cm.reshape(bd, 1, SSM_GROUPS * D_STATE)
        q3 = proj_s[:, C_MQ:C_MQ + D_MLSTM].reshape(bd, 1, D_MLSTM)
        k3 = ks.reshape(bd, 1, D_MLSTM)
        v3 = proj_s[:, C_MV:C_MV + D_MLSTM].reshape(bd, 1, D_MLSTM)
        tf = 512
        if (mp // tm_mlp) * (w_up.shape[2] // tf) == bd:
            side = (ea, ssm_all, xdt3, b3, c3, gq, esw, mem_all, q3, k3, v3, s_new, c_new)
            hp, s_new, y3, c_new, qc3 = _mlp(x1, g_mlp, w_up, w_down, gf, l, last, tm_mlp, tf, side)
        else:
            hp = _mlp(x1, g_mlp, w_up, w_down, gf, l, last, tm_mlp, tf)
            s_new, y3 = _ssm_sample(ea, ssm_all, xdt3, b3, c3, l, s_new, tb)
            c_new, qc3 = _mlstm_sample(gq, esw, mem_all, q3, k3, v3, l, c_new, tb)
        ys, hms = _sample_post(proj_s, y3.reshape(bd, D_SSM), xs, dskip_rep, wns, qc3.reshape(bd, D_MLSTM),
                               gq, sw, den, wnm)
        x1s = _outproj(att_s.reshape(bd, D_ATT), ys, hms, hs, w_out, l, bd)
        hs = _mlp(x1s, g_mlp, w_up, w_down, gf, l, last, bd, 512)
        st_s.append((conv_new.reshape(bd, CONV_WIDTH - 1, CONV_DIM),
                     n_new.reshape(bd, MLSTM_HEADS, MLSTM_HEAD_DIM),
                     mt[:, GATE_LANE:GATE_LANE + MLSTM_HEADS]))

    y_prompt = hp.reshape(bsz, seq, d_model)
    y_sample = hs.reshape(bd, 1, d_model)
    p_out = [jnp.stack([s[i] for s in st_p]) for i in range(7)]
    s_conv, s_n, s_m = [jnp.stack([s[i] for s in st_s]) for i in range(3)]
    s_k = kv_new[0].reshape(depth, bd, win, ATT_KV_HEADS, ATT_HEAD_DIM)
    s_v = kv_new[1].reshape(depth, bd, win, ATT_KV_HEADS, ATT_HEAD_DIM)
    s_ssm = s_new.reshape(depth, bd, SSM_HEADS, SSM_HEAD_DIM, D_STATE)
    s_c = c_new.reshape(depth, bd, MLSTM_HEADS, MLSTM_HEAD_DIM, MLSTM_HEAD_DIM)
    return (y_prompt, y_sample, *p_out, s_k, s_v, s_conv, s_ssm, s_c, s_n, s_m)
```

```python
import functools
import math

import jax
import jax.numpy as jnp
from jax import lax
from jax.experimental import pallas as pl
from jax.experimental.pallas import tpu as pltpu

F32 = jnp.float32
BF16 = jnp.bfloat16

D_MODEL = 2048
EPS = 1e-6
PAST_LEN = 8192
ATT_HEAD_DIM = 64
ATT_HEADS = 8
ATT_KV_HEADS = 2
ATT_GROUP = ATT_HEADS // ATT_KV_HEADS
D_ATT = ATT_HEADS * ATT_HEAD_DIM
D_KV = ATT_KV_HEADS * ATT_HEAD_DIM
WINDOW = 128
ROPE_THETA = 500000.0
ROPE_DIM = ATT_HEAD_DIM // 4
SSM_HEAD_DIM = 64
SSM_HEADS = 16
D_SSM = SSM_HEADS * SSM_HEAD_DIM
SSM_GROUPS = 2
D_STATE = 128
CONV_WIDTH = 4
CONV_DIM = D_SSM + 2 * SSM_GROUPS * D_STATE
CHUNK = 128
MLSTM_HEADS = 4
MLSTM_HEAD_DIM = 128
D_MLSTM = MLSTM_HEADS * MLSTM_HEAD_DIM
D_FF = 4 * D_MODEL

C_XBC = 0
C_Q = 1536
C_Z = 2048
C_MQ = 3072
C_MK = 3584
C_MV = 4096
C_MO = 4608
C_K = 5120
C_V = 5248
C_G = 5376
NP = 5632
GATE_LANE = 16

LANE = 128
VMEM_LIMIT = 60 * 1024 * 1024


def _cparams(sem):
    return pltpu.CompilerParams(dimension_semantics=sem, vmem_limit_bytes=VMEM_LIMIT)


def _silu(x):
    return x * jax.nn.sigmoid(x)


def _dot(a, b):
    return jnp.dot(a.astype(BF16), b.astype(BF16), preferred_element_type=F32)


def _dot_nt(a, b):
    return lax.dot_general(a.astype(BF16), b.astype(BF16), (((1,), (1,)), ((), ())),
                           preferred_element_type=F32)


def _dot_tn(a, b):
    return lax.dot_general(a.astype(BF16), b.astype(BF16), (((0,), (0,)), ((), ())),
                           preferred_element_type=F32)


def _tril(n):
    row = lax.broadcasted_iota(jnp.int32, (n, n), 0)
    col = lax.broadcasted_iota(jnp.int32, (n, n), 1)
    return row >= col


def _cumsum_rows(x):
    t = _tril(x.shape[0]).astype(BF16)
    hi = x.astype(BF16)
    r1 = x - hi.astype(F32)
    mid = r1.astype(BF16)
    lo = (r1 - mid.astype(F32)).astype(BF16)
    dot = lambda piece: jnp.dot(t, piece, preferred_element_type=F32)
    return dot(hi) + dot(mid) + dot(lo)


def _rms_matmul_kernel(x_ref, g_ref, w_ref, o_ref, u_ref):
    @pl.when(pl.program_id(1) == 0)
    def _():
        x = x_ref[...]
        ms = jnp.mean(x * x, axis=-1, keepdims=True)
        u_ref[...] = (x * lax.rsqrt(ms + EPS) * g_ref[...]).astype(BF16)

    o_ref[...] = jnp.dot(u_ref[...], w_ref[...], preferred_element_type=F32)


def _rms_matmul(x, g, w, tm, tn):
    m, k = x.shape
    n = w.shape[1]
    return pl.pallas_call(
        _rms_matmul_kernel,
        grid=(m // tm, n // tn),
        in_specs=[pl.BlockSpec((tm, k), lambda i, j: (i, 0)),
                  pl.BlockSpec((1, k), lambda i, j: (0, 0)),
                  pl.BlockSpec((k, tn), lambda i, j: (0, j))],
        out_specs=pl.BlockSpec((tm, tn), lambda i, j: (i, j)),
        out_shape=jax.ShapeDtypeStruct((m, n), F32),
        scratch_shapes=[pltpu.VMEM((tm, k), BF16)],
        compiler_params=_cparams(("parallel", "arbitrary")),
        name="rms_inproj",
    )(x, g, w)


_SRC_Q, _SRC_K, _SRC_V, _SRC_Z, _SRC_XBC, _SRC_DT, _SRC_MQ = 0, 512, 640, 768, 1792, 3328, 3344
_SRC_MI = _SRC_MQ + 4 * D_MLSTM
IN_WIDTH = _SRC_MI + 2 * MLSTM_HEADS


def _prep_w_in_kernel(w_ref, o_ref):
    def put(dst, src, n):
        o_ref[:, dst:dst + n] = w_ref[src:src + n, :].T.astype(BF16)

    put(C_XBC, _SRC_XBC, CONV_DIM)
    put(C_Q, _SRC_Q, D_ATT)
    put(C_Z, _SRC_Z, D_SSM)
    put(C_MQ, _SRC_MQ, 4 * D_MLSTM)
    put(C_K, _SRC_K, D_KV)
    put(C_V, _SRC_V, D_KV)
    tk = w_ref.shape[1]
    dt = w_ref[_SRC_DT:_SRC_DT + SSM_HEADS, :]
    gates = w_ref[_SRC_MI:_SRC_MI + 2 * MLSTM_HEADS, :]
    pad = jnp.zeros((LANE - GATE_LANE - 2 * MLSTM_HEADS, tk), F32)
    lo = jnp.concatenate([dt, gates, pad], axis=0)
    hi = jnp.concatenate([jnp.zeros_like(dt), pltpu.roll(gates, MLSTM_HEADS, 0), pad], axis=0)
    o_ref[:, C_G:C_G + LANE] = lo.T.astype(BF16)
    o_ref[:, C_G + LANE:C_G + 2 * LANE] = hi.T.astype(BF16)


def _prep_w_in(w_in, layer, tk):
    depth, k, n = w_in.shape
    assert n == IN_WIDTH and n % 8 == 0
    wt = jnp.swapaxes(w_in, 1, 2).reshape(depth * n, k)
    return pl.pallas_call(
        _prep_w_in_kernel,
        grid=(k // tk,),
        in_specs=[pl.BlockSpec((n, tk), lambda i: (layer, i))],
        out_specs=pl.BlockSpec((tk, NP), lambda i: (i, 0)),
        out_shape=jax.ShapeDtypeStruct((k, NP), BF16),
        compiler_params=_cparams(("parallel",)),
        name="prep_w_in",
    )(wt)


def _outproj_kernel(a_ref, y_ref, h_ref, x_ref, w_ref, o_ref, wb_ref):
    @pl.when(pl.program_id(0) == 0)
    def _():
        wb_ref[...] = w_ref[...].astype(BF16)

    acc = jnp.dot(a_ref[...], wb_ref[0:D_ATT, :], preferred_element_type=F32)
    acc = acc + jnp.dot(y_ref[...], wb_ref[D_ATT:D_ATT + D_SSM, :], preferred_element_type=F32)
    acc = acc + jnp.dot(h_ref[...], wb_ref[D_ATT + D_SSM:, :], preferred_element_type=F32)
    o_ref[...] = x_ref[...] + acc


def _outproj(att, y, h, x, w, layer, tm):
    m, n = x.shape
    k = w.shape[1]
    rows = lambda width: pl.BlockSpec((tm, width), lambda i: (i, 0))
    return pl.pallas_call(
        _outproj_kernel,
        grid=(m // tm,),
        in_specs=[rows(D_ATT), rows(D_SSM), rows(D_MLSTM), rows(n),
                  pl.BlockSpec((None, k, n), lambda i: (layer, 0, 0), pipeline_mode=pl.Buffered(1))],
        out_specs=rows(n),
        out_shape=jax.ShapeDtypeStruct((m, n), F32),
        scratch_shapes=[pltpu.VMEM((k, n), BF16)],
        compiler_params=_cparams(("arbitrary",)),
        name="outproj",
    )(att, y, h, x, w)


N_MLP_IN, N_SSM_SIDE_IN, N_MLSTM_SIDE_IN = 5, 5, 6


def _mlp_kernel(*refs, final_norm, side, n_prev):
    x_ref, g_ref, wu_ref, wd_ref, gf_ref = refs[:N_MLP_IN]
    n_side = N_SSM_SIDE_IN + N_MLSTM_SIDE_IN if side else 0
    side_in = refs[N_MLP_IN:N_MLP_IN + n_side]
    outs = refs[N_MLP_IN + n_side + n_prev:]
    o_ref, u_ref = outs[0], outs[-1]
    j = pl.program_id(1)

    @pl.when(j == 0)
    def _():
        x = x_ref[...]
        ms = jnp.mean(x * x, axis=-1, keepdims=True)
        u_ref[...] = (x * lax.rsqrt(ms + EPS) * g_ref[...]).astype(BF16)
        o_ref[...] = x

    h = _dot(u_ref[...], wu_ref[...])
    h = jnp.square(jnp.maximum(h, 0.0))
    o_ref[...] += _dot(h, wd_ref[...])

    if side:
        seq = pl.program_id(0) * pl.num_programs(1) + j
        so_ref, y_ref, co_ref, qc_ref = outs[1:5]
        _ssm_sample_one(0, seq, *side_in[:N_SSM_SIDE_IN], so_ref, y_ref)
        _mlstm_sample_one(0, seq, *side_in[N_SSM_SIDE_IN:], co_ref, qc_ref)

    if final_norm:
        @pl.when(j == pl.num_programs(1) - 1)
        def _():
            y = o_ref[...]
            ms = jnp.mean(y * y, axis=-1, keepdims=True)
            o_ref[...] = y * lax.rsqrt(ms + EPS) * gf_ref[...]


def _mlp(x, g, wu, wd, gf, layer, final_norm, tm, tf, side=None):
    m, d = x.shape
    ff = wu.shape[2]
    nj = ff // tf
    in_specs = [pl.BlockSpec((tm, d), lambda i, j: (i, 0)),
                pl.BlockSpec((1, d), lambda i, j: (0, 0)),
                pl.BlockSpec((None, d, tf), lambda i, j: (layer, 0, j)),
                pl.BlockSpec((None, tf, d), lambda i, j: (layer, j, 0)),
                pl.BlockSpec((1, d), lambda i, j: (0, 0))]
    out_specs = [pl.BlockSpec((tm, d), lambda i, j: (i, 0))]
    out_shape = [jax.ShapeDtypeStruct((m, d), F32)]
    operands = [x, g, wu, wd, gf]
    aliases = {}
    n_prev = 0
    if side is not None:
        ea, ssm, xdt3, b3, c3, gq, esw, mem, q3, k3, v3, prev_s, prev_c = side
        depth, bd = ssm.shape[0], ssm.shape[1]
        assert (m // tm) * nj == bd
        smem = pl.BlockSpec(memory_space=pltpu.SMEM)
        one = lambda n: pl.BlockSpec((1, 1, n), lambda i, j: (i * nj + j, 0, 0))
        state = lambda r, n: pl.BlockSpec((None, 1, r, n), lambda i, j: (layer, i * nj + j, 0, 0))
        in_specs += [smem, state(D_SSM, D_STATE), one(D_SSM), one(SSM_GROUPS * D_STATE), one(SSM_GROUPS * D_STATE),
                     smem, smem, state(D_MLSTM, MLSTM_HEAD_DIM), one(D_MLSTM), one(D_MLSTM), one(D_MLSTM)]
        operands += [ea, ssm, xdt3, b3, c3, gq, esw, mem, q3, k3, v3]
        out_specs += [state(D_SSM, D_STATE), one(D_SSM), state(D_MLSTM, MLSTM_HEAD_DIM), one(D_MLSTM)]
        out_shape += [jax.ShapeDtypeStruct((depth, bd, D_SSM, D_STATE), F32),
                      jax.ShapeDtypeStruct((bd, 1, D_SSM), F32),
                      jax.ShapeDtypeStruct((depth, bd, D_MLSTM, MLSTM_HEAD_DIM), F32),
                      jax.ShapeDtypeStruct((bd, 1, D_MLSTM), F32)]
        if prev_s is not None:
            aliases = {len(operands): 1, len(operands) + 1: 3}
            operands += [prev_s, prev_c]
            in_specs += [pl.BlockSpec(memory_space=pl.ANY)] * 2
            n_prev = 2
    res = pl.pallas_call(
        functools.partial(_mlp_kernel, final_norm=final_norm, side=side is not None, n_prev=n_prev),
        grid=(m // tm, nj),
        in_specs=in_specs,
        out_specs=out_specs,
        out_shape=out_shape,
        input_output_aliases=aliases,
        scratch_shapes=[pltpu.VMEM((tm, d), BF16)],
        compiler_params=_cparams(("parallel", "arbitrary")),
        name="mlp",
    )(*operands)
    return res[0] if side is None else res


def _rope_tables(pos):
    half = ROPE_DIM // 2
    inv = jnp.power(jnp.float32(ROPE_THETA), -jnp.arange(half, dtype=jnp.float32) / half)
    ang = pos.astype(jnp.float32)[:, None] * inv[None, :]
    cos = jnp.cos(ang)
    sin = jnp.sin(ang)
    n = pos.shape[0]
    rest = ATT_HEAD_DIM - ROPE_DIM
    c = jnp.concatenate([cos, cos, jnp.ones((n, rest), F32)], axis=1)
    s1 = jnp.concatenate([-sin, jnp.zeros((n, half + rest), F32)], axis=1)
    s2 = jnp.concatenate([jnp.zeros((n, half), F32), sin, jnp.zeros((n, rest), F32)], axis=1)
    rep = LANE // ATT_HEAD_DIM
    return jnp.tile(c, (1, rep)), jnp.tile(s1, (1, rep)), jnp.tile(s2, (1, rep))


def _rope(x, c, s1, s2):
    width = x.shape[-1]
    rep = width // LANE
    half = ROPE_DIM // 2
    if rep > 1:
        c = jnp.concatenate([c] * rep, axis=-1)
        s1 = jnp.concatenate([s1] * rep, axis=-1)
        s2 = jnp.concatenate([s2] * rep, axis=-1)
    axis = x.ndim - 1
    return x * c + pltpu.roll(x, width - half, axis) * s1 + pltpu.roll(x, half, axis) * s2


def _attn_prompt_init(ko_ref, vo_ref):
    ko_ref[...] = jnp.zeros_like(ko_ref)
    vo_ref[...] = jnp.zeros_like(vo_ref)


def _attn_prompt_body(j, sink_ref, q_ref, k_ref, v_ref, c_ref, s1_ref, s2_ref, o_ref, ko_ref, vo_ref):
    w = WINDOW
    c = c_ref[...]
    s1 = s1_ref[...]
    s2 = s2_ref[...]
    krot = _rope(k_ref[...], c, s1, s2)
    v = v_ref[...]
    qrot = _rope(q_ref[...], c, s1, s2)
    kk = jnp.concatenate([ko_ref[...], krot], axis=0).astype(BF16)
    vv = jnp.concatenate([vo_ref[...], v], axis=0).astype(BF16)

    row = lax.broadcasted_iota(jnp.int32, (w, 2 * w), 0)
    col = lax.broadcasted_iota(jnp.int32, (w, 2 * w), 1)
    first_col = jnp.where(j > 0, 0, w)
    mask = (col >= row) & (col <= row + w) & (col >= first_col)
    scale = ATT_HEAD_DIM ** -0.5

    for h in range(ATT_KV_HEADS):
        kh = kk[:, h * ATT_HEAD_DIM:(h + 1) * ATT_HEAD_DIM]
        vh = vv[:, h * ATT_HEAD_DIM:(h + 1) * ATT_HEAD_DIM]
        for g in range(ATT_GROUP):
            hg = h * ATT_GROUP + g
            qh = qrot[:, hg * ATT_HEAD_DIM:(hg + 1) * ATT_HEAD_DIM]
            s = _dot_nt(qh, kh) * scale
            s = jnp.where(mask, s, -jnp.inf)
            sink = sink_ref[hg]
            m = jnp.maximum(jnp.max(s, axis=-1, keepdims=True), sink)
            p = jnp.exp(s - m)
            denom = jnp.sum(p, axis=-1, keepdims=True) + jnp.exp(sink - m)
            o = _dot(p, vh) / denom
            o_ref[:, hg * ATT_HEAD_DIM:(hg + 1) * ATT_HEAD_DIM] = o.astype(o_ref.dtype)
            yield

    ko_ref[...] = krot
    vo_ref[...] = v


def _group_rmsnorm(y, w, groups):
    width = y.shape[-1] // groups
    outs = []
    for g in range(groups):
        yg = y[:, g * width:(g + 1) * width]
        ms = jnp.mean(yg * yg, axis=-1, keepdims=True)
        outs.append(yg * lax.rsqrt(ms + EPS) * w[:, g * width:(g + 1) * width])
    return jnp.concatenate(outs, axis=-1)


CONV_PAD = 8


def _ssd_prompt_init(s_ref, tail_ref):
    tail_ref[...] = jnp.zeros_like(tail_ref)
    s_ref[...] = jnp.zeros_like(s_ref)


def _ssd_prompt_body(xbc_ref, z_ref, g_ref, cw_ref, cb_ref, dtb_ref, alog_ref, dskip_ref, wn_ref, spread_ref,
                     y_ref, s_ref, tail_ref, ybuf_ref, xw_ref):
    q = CHUNK
    pad = CONV_PAD
    hp = SSM_HEAD_DIM
    heads_per_group = SSM_HEADS // SSM_GROUPS
    gw = heads_per_group * hp

    x = xbc_ref[...]
    cw = cw_ref[...]
    row8 = lax.broadcasted_iota(jnp.int32, (pad, CONV_DIM), 0)
    acc = None
    for j in range(CONV_WIDTH - 1):
        shift = CONV_WIDTH - 1 - j
        rolled = pltpu.roll(x, shift, 0)
        head = jnp.where(row8 < shift, tail_ref[j * pad:(j + 1) * pad, :], rolled[0:pad, :])
        tail_ref[j * pad:(j + 1) * pad, :] = rolled[0:pad, :]
        term = jnp.concatenate([head, rolled[pad:, :]], axis=0) * cw[j:j + 1, :]
        acc = term if acc is None else acc + term
    acc = acc + x * cw[CONV_WIDTH - 1:CONV_WIDTH, :]
    xc = _silu(acc + cb_ref[...])

    xs = xc[:, 0:D_SSM]
    bm = xc[:, D_SSM:D_SSM + SSM_GROUPS * D_STATE]
    cm = xc[:, D_SSM + SSM_GROUPS * D_STATE:]

    dt = jax.nn.softplus(g_ref[:, 0:LANE] + dtb_ref[...])
    a_neg = -jnp.exp(alog_ref[...])
    a_col = _cumsum_rows(dt * a_neg)
    a_row = a_col.T
    dt_row = dt.T
    wk_row = jnp.exp(a_row[:, q - 1:q] - a_row) * dt_row
    tril = _tril(q)
    xs_t = xs.T
    ea_col = jnp.exp(a_col)
    ea_hi = ea_col.astype(BF16)
    ea_lo = (ea_col - ea_hi.astype(F32)).astype(BF16)
    spread = spread_ref[...]
    ea_full = (jnp.dot(ea_hi, spread, preferred_element_type=F32)
               + jnp.dot(ea_lo, spread, preferred_element_type=F32))
    yield

    cs_parts = []
    for g in range(SSM_GROUPS):
        bg = bm[:, g * D_STATE:(g + 1) * D_STATE]
        cg = cm[:, g * D_STATE:(g + 1) * D_STATE]
        cb = _dot_nt(cg, bg)
        cs_parts.append(_dot_nt(cg, s_ref[g * gw:(g + 1) * gw, :]))
        for r in range(heads_per_group):
            h = g * heads_per_group + r
            seg = a_col[:, h:h + 1] - a_row[h:h + 1, :]
            wmat = jnp.exp(jnp.where(tril, seg, -jnp.inf)) * cb * dt_row[h:h + 1, :]
            ybuf_ref[:, h * hp:(h + 1) * hp] = _dot(wmat, xs[:, h * hp:(h + 1) * hp])
            xw_ref[r * hp:(r + 1) * hp, :] = xs_t[h * hp:(h + 1) * hp, :] * wk_row[h:h + 1, :]
            yield
        upd = _dot(xw_ref[...], bg)
        for r in range(heads_per_group):
            h = g * heads_per_group + r
            decay = jnp.exp(a_row[h:h + 1, q - 1:q])
            s_ref[h * hp:(h + 1) * hp, :] = s_ref[h * hp:(h + 1) * hp, :] * decay + upd[r * hp:(r + 1) * hp, :]
        yield

    y = ybuf_ref[...] + jnp.concatenate(cs_parts, axis=1) * ea_full
    y = y + dskip_ref[...] * xs
    y = y * _silu(z_ref[...])
    y_ref[...] = _group_rmsnorm(y, wn_ref[...], SSM_GROUPS).astype(y_ref.dtype)


def _mlstm_prompt_init(c_ref, n_ref, m_ref):
    c_ref[...] = jnp.zeros_like(c_ref)
    n_ref[...] = jnp.zeros_like(n_ref)
    m_ref[...] = jnp.zeros_like(m_ref)


def _mlstm_prompt_body(q_ref, k_ref, v_ref, o_ref, g_ref, bi_ref, bf_ref, wn_ref, h_ref, c_ref, n_ref, m_ref):
    t = CHUNK
    d = MLSTM_HEAD_DIM

    gi = g_ref[:, 0:LANE] + bi_ref[...]
    gf = g_ref[:, LANE:2 * LANE] + bf_ref[...]
    b_col = _cumsum_rows(jax.nn.log_sigmoid(gf))
    b_row = b_col.T
    i_row = gi.T
    tril = _tril(t)
    k_all = k_ref[...] * (d ** -0.5)

    for h in range(MLSTM_HEADS):
        ln = GATE_LANE + h
        bq = b_col[:, ln:ln + 1]
        logw = jnp.where(tril, bq - b_row[ln:ln + 1, :] + i_row[ln:ln + 1, :], -jnp.inf)
        m_prev = m_ref[h:h + 1, 0:1]
        log_inter = bq + m_prev
        mt = jnp.maximum(log_inter, jnp.max(logw, axis=-1, keepdims=True))
        qh = q_ref[:, h * d:(h + 1) * d]
        kh = k_all[:, h * d:(h + 1) * d]
        vh = v_ref[:, h * d:(h + 1) * d]
        ch = c_ref[h * d:(h + 1) * d, :]
        nh = n_ref[h:h + 1, :]
        sw = jnp.exp(logw - mt) * _dot_nt(qh, kh)
        gq = jnp.exp(log_inter - mt)
        num = _dot(sw, vh) + _dot(qh, ch) * gq
        den = jnp.sum(sw, axis=-1, keepdims=True) + jnp.sum(qh * nh, axis=-1, keepdims=True) * gq
        hh = num / jnp.maximum(jnp.abs(den), jnp.exp(-mt))

        m_new = mt[t - 1:t, :]
        b_last = b_col[t - 1:t, ln:ln + 1]
        wk = jnp.exp(b_last - bq + gi[:, ln:ln + 1] - m_new)
        g_end = jnp.exp(b_last + m_prev - m_new)
        kw = kh * wk
        c_ref[h * d:(h + 1) * d, :] = ch * g_end + _dot_tn(kw, vh)
        n_ref[h:h + 1, :] = nh * g_end + jnp.sum(kw, axis=0, keepdims=True)
        m_ref[h:h + 1, :] = jnp.broadcast_to(m_new, (1, LANE))

        ms = jnp.mean(hh * hh, axis=-1, keepdims=True)
        hn = hh * lax.rsqrt(ms + EPS) * wn_ref[:, h * d:(h + 1) * d]
        h_ref[:, h * d:(h + 1) * d] = (hn * jax.nn.sigmoid(o_ref[:, h * d:(h + 1) * d])).astype(h_ref.dtype)
        yield

def _interleave(stages):
    live = [[gen, 0, n] for gen, n in stages]
    while live:
        item = min(live, key=lambda s: s[1] / s[2])
        try:
            next(item[0])
            item[1] += 1
        except StopIteration:
            live.remove(item)


PROJ_TILE = 256


def _project_next(x_ref, g_ref, w_ref, pnext_ref):
    x = x_ref[...]
    u = (x * lax.rsqrt(jnp.mean(x * x, axis=-1, keepdims=True) + EPS) * g_ref[...]).astype(BF16)
    for ct in range(NP // PROJ_TILE):
        cols = slice(ct * PROJ_TILE, (ct + 1) * PROJ_TILE)
        pnext_ref[:, cols] = jnp.dot(u, w_ref[:, cols], preferred_element_type=F32)
        yield


def _front_kernel(sink_ref, x_ref, x0_ref, g_ref, w_ref, c_ref, s1_ref, s2_ref,
                  cw_ref, cb_ref, dtb_ref, alog_ref, dskip_ref, wns_ref, spread_ref, bi_ref, bf_ref, wnm_ref,
                  att_ref, ko_ref, vo_ref, y_ref, s_ref, h_ref, mem_ref, n_ref, m_ref, ptail_ref,
                  pcur_ref, pnext_ref, tail_ref, ybuf_ref, xw_ref, *, nblk):
    j = pl.program_id(1)
    first = (pl.program_id(0) == 0) & (j == 0)
    t = CHUNK

    @pl.when(first)
    def _():
        for _ in _project_next(x0_ref, g_ref, w_ref, pcur_ref):
            pass

    @pl.when(jnp.logical_not(first))
    def _():
        pcur_ref[...] = pnext_ref[...]

    @pl.when(j == 0)
    def _():
        _attn_prompt_init(ko_ref, vo_ref)
        _ssd_prompt_init(s_ref, tail_ref)
        _mlstm_prompt_init(mem_ref, n_ref, m_ref)

    def attn(i):
        seg = lambda col, width: pcur_ref.at[i * t:(i + 1) * t, col:col + width]
        rows = lambda ref: ref.at[i * t:(i + 1) * t, :]
        return _attn_prompt_body(j * nblk + i, sink_ref, seg(C_Q, D_ATT), seg(C_K, D_KV), seg(C_V, D_KV),
                                 rows(c_ref), rows(s1_ref), rows(s2_ref), rows(att_ref), ko_ref, vo_ref)

    def ssd(i):
        seg = lambda col, width: pcur_ref.at[i * t:(i + 1) * t, col:col + width]
        return _ssd_prompt_body(seg(C_XBC, CONV_DIM), seg(C_Z, D_SSM), seg(C_G, 2 * LANE), cw_ref, cb_ref, dtb_ref,
                                alog_ref, dskip_ref, wns_ref, spread_ref, y_ref.at[i * t:(i + 1) * t, :], s_ref,
                                tail_ref, ybuf_ref, xw_ref)

    def mlstm(i):
        seg = lambda col, width: pcur_ref.at[i * t:(i + 1) * t, col:col + width]
        return _mlstm_prompt_body(seg(C_MQ, D_MLSTM), seg(C_MK, D_MLSTM), seg(C_MV, D_MLSTM), seg(C_MO, D_MLSTM),
                                  seg(C_G, 2 * LANE), bi_ref, bf_ref, wnm_ref, h_ref.at[i * t:(i + 1) * t, :],
                                  mem_ref, n_ref, m_ref)

    def blocks(body):
        for i in range(nblk):
            yield from body(i)

    ptail_ref[...] = pcur_ref[nblk * t - CONV_PAD:nblk * t, C_XBC:C_XBC + CONV_DIM]
    _interleave([
        (_project_next(x_ref, g_ref, w_ref, pnext_ref), NP // PROJ_TILE + 1),
        (blocks(attn), nblk * (ATT_HEADS + 1)),
        (blocks(ssd), nblk * (SSM_HEADS + SSM_GROUPS + 2)),
        (blocks(mlstm), nblk * (MLSTM_HEADS + 1)),
    ])


def _front(x, g_mix, w, sinks, tables, cw, cb, dtb, alog, dskip_rep, wns, bi, bf, wnm, bsz, seq):
    nblk = 2 if (seq // CHUNK) % 2 == 0 else 1
    t = nblk * CHUNK
    nb = seq // t
    total = bsz * nb
    c, s1, s2 = tables
    head_of_lane = jnp.arange(D_SSM, dtype=jnp.int32) // SSM_HEAD_DIM
    spread = (jnp.arange(LANE, dtype=jnp.int32)[:, None] == head_of_lane[None, :]).astype(BF16)
    row = lambda b, j: b * nb + j
    full = lambda shape: pl.BlockSpec(shape, lambda b, j: (0,) * len(shape))
    vec = lambda n: full((1, n))
    tab = pl.BlockSpec((t, LANE), lambda b, j: (j, 0))
    rows_out = lambda width: pl.BlockSpec((t, width), lambda b, j: (row(b, j), 0))
    per_seq = lambda r, n: pl.BlockSpec((None, r, n), lambda b, j: (b, 0, 0))
    return pl.pallas_call(
        functools.partial(_front_kernel, nblk=nblk),
        grid=(bsz, nb),
        in_specs=[pl.BlockSpec(memory_space=pltpu.SMEM),
                  pl.BlockSpec((t, D_MODEL), lambda b, j: (jnp.minimum(row(b, j) + 1, total - 1), 0)),
                  full((t, D_MODEL)),
                  vec(D_MODEL),
                  pl.BlockSpec((D_MODEL, NP), lambda b, j: (0, 0), pipeline_mode=pl.Buffered(1)),
                  tab, tab, tab,
                  full((CONV_WIDTH, CONV_DIM)), vec(CONV_DIM), vec(LANE), vec(LANE), vec(D_SSM), vec(D_SSM),
                  full((LANE, D_SSM)),
                  vec(LANE), vec(LANE), vec(D_MLSTM)],
        out_specs=[rows_out(D_ATT), per_seq(CHUNK, D_KV), per_seq(CHUNK, D_KV),
                   rows_out(D_SSM), per_seq(D_SSM, D_STATE),
                   rows_out(D_MLSTM), per_seq(D_MLSTM, MLSTM_HEAD_DIM), per_seq(8, LANE), per_seq(8, LANE),
                   per_seq(CONV_PAD, CONV_DIM)],
        out_shape=[jax.ShapeDtypeStruct((bsz * seq, D_ATT), BF16),
                   jax.ShapeDtypeStruct((bsz, CHUNK, D_KV), F32),
                   jax.ShapeDtypeStruct((bsz, CHUNK, D_KV), F32),
                   jax.ShapeDtypeStruct((bsz * seq, D_SSM), BF16),
                   jax.ShapeDtypeStruct((bsz, D_SSM, D_STATE), F32),
                   jax.ShapeDtypeStruct((bsz * seq, D_MLSTM), BF16),
                   jax.ShapeDtypeStruct((bsz, D_MLSTM, MLSTM_HEAD_DIM), F32),
                   jax.ShapeDtypeStruct((bsz, 8, LANE), F32),
                   jax.ShapeDtypeStruct((bsz, 8, LANE), F32),
                   jax.ShapeDtypeStruct((bsz, CONV_PAD, CONV_DIM), F32)],
        scratch_shapes=[pltpu.VMEM((t, NP), F32),
                        pltpu.VMEM((t, NP), F32),
                        pltpu.VMEM(((CONV_WIDTH - 1) * CONV_PAD, CONV_DIM), F32),
                        pltpu.VMEM((CHUNK, D_SSM), F32),
                        pltpu.VMEM((D_SSM // SSM_GROUPS, CHUNK), F32)],
        compiler_params=_cparams(("arbitrary", "arbitrary")),
        name="front",
    )(sinks, x, x, g_mix, w, c, s1, s2, cw, cb, dtb, alog, dskip_rep, wns, spread, bi, bf, wnm)


def _stacked_out(prev, depth, shape, block, layer):
    spec = pl.BlockSpec((None,) + block, lambda i: (layer,) + (i,) + (0,) * (len(block) - 1))
    out_shape = jax.ShapeDtypeStruct((depth,) + shape, F32)
    extra_in = [] if prev is None else [prev]
    extra_spec = [] if prev is None else [pl.BlockSpec(memory_space=pl.ANY)]
    return spec, out_shape, extra_in, extra_spec


def _drop_aliased(kernel_fn, n_in, n_prev):
    def wrapped(*refs):
        return kernel_fn(*refs[:n_in], *refs[n_in + n_prev:])
    return wrapped


def _attn_sample_kernel(q_ref, k_ref, v_ref, kc_ref, vc_ref, sink_ref, c_ref, s1_ref, s2_ref,
                        o_ref, ko_ref, vo_ref):
    w = kc_ref.shape[1]
    c = c_ref[...]
    s1 = s1_ref[...]
    s2 = s2_ref[...]
    qb = _rope(q_ref[...], c, s1, s2)
    kn = _rope(k_ref[...], c, s1, s2)
    vn = v_ref[...]
    kc = kc_ref[...]
    vc = vc_ref[...]
    ko_ref[:, 0:w - 1, :] = kc[:, 1:w, :]
    ko_ref[:, w - 1:w, :] = kn
    vo_ref[:, 0:w - 1, :] = vc[:, 1:w, :]
    vo_ref[:, w - 1:w, :] = vn

    scale = ATT_HEAD_DIM ** -0.5
    s = jnp.einsum("bhd,bkd->bhk", qb.astype(BF16), kc.astype(BF16),
                   preferred_element_type=F32) * scale
    sn = jnp.sum(qb * kn, axis=-1, keepdims=True) * scale
    sink = sink_ref[...][None, :, 0:1]
    m = jnp.maximum(jnp.maximum(jnp.max(s, axis=-1, keepdims=True), sn), sink)
    p = jnp.exp(s - m)
    pn = jnp.exp(sn - m)
    denom = jnp.sum(p, axis=-1, keepdims=True) + pn + jnp.exp(sink - m)
    p = p / denom
    pn = pn / denom
    o = jnp.einsum("bhk,bkd->bhd", p.astype(BF16), vc.astype(BF16), preferred_element_type=F32)
    o = o + pn * vn
    head = lax.broadcasted_iota(jnp.int32, o.shape[:2] + (ATT_HEAD_DIM,), 1)
    o_ref[...] = jnp.where(head < ATT_GROUP, o[:, :, 0:ATT_HEAD_DIM], o[:, :, ATT_HEAD_DIM:]).astype(o_ref.dtype)


def _attn_sample(qb, kn, vn, kc, vc, sink_rows, tables, layer, prev, tb):
    depth, bd, w, _ = kc.shape
    c, s1, s2 = tables
    blk3 = lambda r, n: pl.BlockSpec((tb, r, n), lambda i: (i, 0, 0))
    cache = pl.BlockSpec((None, tb, w, LANE), lambda i: (layer, i, 0, 0))
    vec = pl.BlockSpec((1, LANE), lambda i: (0, 0))
    prev_k, prev_v = (None, None) if prev is None else prev
    k_spec, k_shape, k_in, k_in_spec = _stacked_out(prev_k, depth, (bd, w, LANE), (tb, w, LANE), layer)
    v_spec, v_shape, v_in, v_in_spec = _stacked_out(prev_v, depth, (bd, w, LANE), (tb, w, LANE), layer)
    n_in = 9
    n_prev = len(k_in) + len(v_in)
    return pl.pallas_call(
        _drop_aliased(_attn_sample_kernel, n_in, n_prev),
        grid=(bd // tb,),
        in_specs=[blk3(ATT_HEADS, LANE), blk3(1, LANE), blk3(1, LANE), cache, cache,
                  pl.BlockSpec((ATT_HEADS, LANE), lambda i: (0, 0)), vec, vec, vec] + k_in_spec + v_in_spec,
        out_specs=[blk3(ATT_HEADS, ATT_HEAD_DIM), k_spec, v_spec],
        out_shape=[jax.ShapeDtypeStruct((bd, ATT_HEADS, ATT_HEAD_DIM), BF16), k_shape, v_shape],
        input_output_aliases={n_in + t: 1 + t for t in range(n_prev)},
        compiler_params=_cparams(("parallel",)),
        name="attn_sample",
    )(qb, kn, vn, kc, vc, sink_rows, c, s1, s2, *k_in, *v_in)


def _lane_place(cols, lane0):
    m = cols[0].shape[0]
    lane = lax.broadcasted_iota(jnp.int32, (m, LANE), 1)
    out = jnp.zeros((m, LANE), F32)
    for i, col in enumerate(cols):
        out = jnp.where(lane == lane0 + i, col, out)
    return out


def _sample_pre_kernel(proj_ref, cs_ref, cw_ref, cb_ref, dtb_ref, alog_ref, bi_ref, bf_ref, n_ref, m_ref,
                       conv_ref, xs_ref, xdt_ref, bm_ref, cm_ref, ea_ref,
                       ks_ref, nn_ref, g_ref, esw_ref, sw_ref, den_ref, mt_ref):
    xbc = proj_ref[:, C_XBC:C_XBC + CONV_DIM]
    s0 = cs_ref[:, 0:CONV_DIM]
    s1 = cs_ref[:, CONV_DIM:2 * CONV_DIM]
    s2 = cs_ref[:, 2 * CONV_DIM:3 * CONV_DIM]
    cw = cw_ref[...]
    acc = s0 * cw[0:1, :]
    acc = acc + s1 * cw[1:2, :]
    acc = acc + s2 * cw[2:3, :]
    acc = acc + xbc * cw[3:4, :]
    xc = _silu(acc + cb_ref[...])
    conv_ref[:, 0:CONV_DIM] = s1
    conv_ref[:, CONV_DIM:2 * CONV_DIM] = s2
    conv_ref[:, 2 * CONV_DIM:3 * CONV_DIM] = xbc

    xs = xc[:, 0:D_SSM]
    xs_ref[...] = xs
    bm_ref[...] = xc[:, D_SSM:D_SSM + SSM_GROUPS * D_STATE]
    cm_ref[...] = xc[:, D_SSM + SSM_GROUPS * D_STATE:]
    dt = jax.nn.softplus(proj_ref[:, C_G:C_G + LANE] + dtb_ref[...])
    ea_ref[...] = jnp.exp(dt * (-jnp.exp(alog_ref[...])))
    hp = SSM_HEAD_DIM
    for h in range(SSM_HEADS):
        xdt_ref[:, h * hp:(h + 1) * hp] = xs[:, h * hp:(h + 1) * hp] * dt[:, h:h + 1]

    d = MLSTM_HEAD_DIM
    gi = proj_ref[:, C_G:C_G + LANE] + bi_ref[...]
    gf = proj_ref[:, C_G + LANE:C_G + 2 * LANE] + bf_ref[...]
    log_inter = jax.nn.log_sigmoid(gf) + m_ref[...]
    mt = jnp.maximum(log_inter, gi)
    gq = jnp.exp(log_inter - mt)
    esw = jnp.exp(gi - mt)
    ks = proj_ref[:, C_MK:C_MK + D_MLSTM] * (d ** -0.5)
    ks_ref[...] = ks
    qk_cols = []
    qn_cols = []
    for h in range(MLSTM_HEADS):
        qh = proj_ref[:, C_MQ + h * d:C_MQ + (h + 1) * d]
        kh = ks[:, h * d:(h + 1) * d]
        nh = n_ref[:, h * d:(h + 1) * d]
        qk_cols.append(jnp.sum(qh * kh, axis=-1, keepdims=True))
        qn_cols.append(jnp.sum(qh * nh, axis=-1, keepdims=True))
        ln = GATE_LANE + h
        nn_ref[:, h * d:(h + 1) * d] = nh * gq[:, ln:ln + 1] + kh * esw[:, ln:ln + 1]
    sw = esw * _lane_place(qk_cols, GATE_LANE)
    den = sw + _lane_place(qn_cols, GATE_LANE) * gq
    g_ref[...] = gq
    esw_ref[...] = esw
    sw_ref[...] = sw
    den_ref[...] = jnp.maximum(jnp.abs(den), jnp.exp(-mt))
    mt_ref[...] = mt


def _sample_pre(proj, conv_state, cw, cb, dtb, alog, bi, bf, n_state, m_tile):
    bd = proj.shape[0]
    tile = jax.ShapeDtypeStruct((bd, LANE), F32)
    outs = [jax.ShapeDtypeStruct((bd, (CONV_WIDTH - 1) * CONV_DIM), F32),
            jax.ShapeDtypeStruct((bd, D_SSM), F32),
            jax.ShapeDtypeStruct((bd, D_SSM), F32),
            jax.ShapeDtypeStruct((bd, SSM_GROUPS * D_STATE), F32),
            jax.ShapeDtypeStruct((bd, SSM_GROUPS * D_STATE), F32),
            tile,
            jax.ShapeDtypeStruct((bd, D_MLSTM), F32),
            jax.ShapeDtypeStruct((bd, D_MLSTM), F32),
            tile, tile, tile, tile, tile]
    return pl.pallas_call(
        _sample_pre_kernel,
        out_shape=outs,
        compiler_params=pltpu.CompilerParams(vmem_limit_bytes=VMEM_LIMIT),
        name="sample_pre",
    )(proj, conv_state, cw, cb, dtb, alog, bi, bf, n_state, m_tile)


def _column_tile(row):
    return jnp.broadcast_to(row, (LANE, LANE)).T


def _ssm_sample_one(b, gb, ea_ref, s_ref, xdt_ref, b_ref, c_ref, so_ref, y_ref):
    hp = SSM_HEAD_DIM
    heads_per_tile = LANE // hp
    n_tiles = D_SSM // LANE
    tiles_per_group = n_tiles // SSM_GROUPS
    top = lax.broadcasted_iota(jnp.int32, (LANE, 1), 0) < hp
    xrow = xdt_ref[b]
    brow = b_ref[b]
    crow = c_ref[b]
    for t in range(n_tiles):
        g = t // tiles_per_group
        xcol = _column_tile(xrow[:, t * LANE:(t + 1) * LANE])
        bg = brow[:, g * D_STATE:(g + 1) * D_STATE]
        cg = jnp.broadcast_to(crow[:, g * D_STATE:(g + 1) * D_STATE], (8, D_STATE))
        e0 = ea_ref[gb, heads_per_tile * t]
        e1 = ea_ref[gb, heads_per_tile * t + 1]
        decay = jnp.where(top, e0, e1)
        sn = s_ref[b, t * LANE:(t + 1) * LANE, :] * decay + xcol * bg
        so_ref[b, t * LANE:(t + 1) * LANE, :] = sn
        y_ref[b, :, t * LANE:(t + 1) * LANE] = _dot_nt(cg, sn)[0:1, :]


def _ssm_sample_kernel(ea_ref, s_ref, xdt_ref, b_ref, c_ref, so_ref, y_ref):
    tb = s_ref.shape[0]
    base = pl.program_id(0) * tb

    def body(b, carry):
        _ssm_sample_one(b, base + b, ea_ref, s_ref, xdt_ref, b_ref, c_ref, so_ref, y_ref)
        return carry

    lax.fori_loop(0, tb, body, 0)


def _ssm_sample(ea, s, xdt3, b3, c3, layer, prev, tb):
    depth, bd = s.shape[0], s.shape[1]
    blk3 = lambda r, n: pl.BlockSpec((tb, r, n), lambda i: (i, 0, 0))
    s_spec, s_shape, s_in, s_in_spec = _stacked_out(prev, depth, (bd, D_SSM, D_STATE), (tb, D_SSM, D_STATE), layer)
    n_in = 5
    return pl.pallas_call(
        _drop_aliased(_ssm_sample_kernel, n_in, len(s_in)),
        grid=(bd // tb,),
        in_specs=[pl.BlockSpec(memory_space=pltpu.SMEM),
                  pl.BlockSpec((None, tb, D_SSM, D_STATE), lambda i: (layer, i, 0, 0)),
                  blk3(1, D_SSM), blk3(1, SSM_GROUPS * D_STATE),
                  blk3(1, SSM_GROUPS * D_STATE)] + s_in_spec,
        out_specs=[s_spec, blk3(1, D_SSM)],
        out_shape=[s_shape, jax.ShapeDtypeStruct((bd, 1, D_SSM), F32)],
        input_output_aliases={n_in + t: t for t in range(len(s_in))},
        compiler_params=_cparams(("parallel",)),
        name="ssm_sample",
    )(ea, s, xdt3, b3, c3, *s_in)


def _mlstm_sample_one(b, gb, g_ref, esw_ref, c_ref, q_ref, k_ref, v_ref, co_ref, qc_ref):
    d = MLSTM_HEAD_DIM
    qrow = q_ref[b]
    krow = k_ref[b]
    vrow = v_ref[b]
    for h in range(MLSTM_HEADS):
        qcol = _column_tile(qrow[:, h * d:(h + 1) * d])
        kcol = _column_tile(krow[:, h * d:(h + 1) * d])
        ch = c_ref[b, h * d:(h + 1) * d, :]
        g_end = g_ref[gb, GATE_LANE + h]
        wk = esw_ref[gb, GATE_LANE + h]
        qc_ref[b, :, h * d:(h + 1) * d] = jnp.sum(qcol * ch, axis=0, keepdims=True)
        co_ref[b, h * d:(h + 1) * d, :] = ch * g_end + (kcol * wk) * vrow[:, h * d:(h + 1) * d]


def _mlstm_sample_kernel(g_ref, esw_ref, c_ref, q_ref, k_ref, v_ref, co_ref, qc_ref):
    tb = c_ref.shape[0]
    base = pl.program_id(0) * tb

    def body(b, carry):
        _mlstm_sample_one(b, base + b, g_ref, esw_ref, c_ref, q_ref, k_ref, v_ref, co_ref, qc_ref)
        return carry

    lax.fori_loop(0, tb, body, 0)


def _mlstm_sample(gq, esw, c, q3, k3, v3, layer, prev, tb):
    depth, bd = c.shape[0], c.shape[1]
    blk3 = lambda r, n: pl.BlockSpec((tb, r, n), lambda i: (i, 0, 0))
    smem = pl.BlockSpec(memory_space=pltpu.SMEM)
    c_spec, c_shape, c_in, c_in_spec = _stacked_out(prev, depth, (bd, D_MLSTM, MLSTM_HEAD_DIM),
                                                    (tb, D_MLSTM, MLSTM_HEAD_DIM), layer)
    n_in = 6
    return pl.pallas_call(
        _drop_aliased(_mlstm_sample_kernel, n_in, len(c_in)),
        grid=(bd // tb,),
        in_specs=[smem, smem, pl.BlockSpec((None, tb, D_MLSTM, MLSTM_HEAD_DIM), lambda i: (layer, i, 0, 0)),
                  blk3(1, D_MLSTM), blk3(1, D_MLSTM), blk3(1, D_MLSTM)] + c_in_spec,
        out_specs=[c_spec, blk3(1, D_MLSTM)],
        out_shape=[c_shape, jax.ShapeDtypeStruct((bd, 1, D_MLSTM), F32)],
        input_output_aliases={n_in + t: t for t in range(len(c_in))},
        compiler_params=_cparams(("parallel",)),
        name="mlstm_sample",
    )(gq, esw, c, q3, k3, v3, *c_in)


def _sample_post_kernel(proj_ref, y_ref, xs_ref, dskip_ref, wns_ref, qc_ref, g_ref, sw_ref, den_ref, wnm_ref,
                        ys_ref, hs_ref):
    y = y_ref[...] + dskip_ref[...] * xs_ref[...]
    y = y * _silu(proj_ref[:, C_Z:C_Z + D_SSM])
    ys_ref[...] = _group_rmsnorm(y, wns_ref[...], SSM_GROUPS).astype(ys_ref.dtype)

    d = MLSTM_HEAD_DIM
    for h in range(MLSTM_HEADS):
        ln = GATE_LANE + h
        vh = proj_ref[:, C_MV + h * d:C_MV + (h + 1) * d]
        num = sw_ref[:, ln:ln + 1] * vh + qc_ref[:, h * d:(h + 1) * d] * g_ref[:, ln:ln + 1]
        hh = num / den_ref[:, ln:ln + 1]
        ms = jnp.mean(hh * hh, axis=-1, keepdims=True)
        hn = hh * lax.rsqrt(ms + EPS) * wnm_ref[:, h * d:(h + 1) * d]
        gate = jax.nn.sigmoid(proj_ref[:, C_MO + h * d:C_MO + (h + 1) * d])
        hs_ref[:, h * d:(h + 1) * d] = (hn * gate).astype(hs_ref.dtype)


def _sample_post(proj, y, xs, dskip_rep, wns, qc, gq, sw, den, wnm):
    bd = proj.shape[0]
    return pl.pallas_call(
        _sample_post_kernel,
        out_shape=[jax.ShapeDtypeStruct((bd, D_SSM), BF16), jax.ShapeDtypeStruct((bd, D_MLSTM), BF16)],
        compiler_params=pltpu.CompilerParams(vmem_limit_bytes=VMEM_LIMIT),
        name="sample_post",
    )(proj, y, xs, dskip_rep, wns, qc, gq, sw, den, wnm)


def _lane_vec(v, lane0):
    return jnp.pad(v.astype(F32), (lane0, LANE - lane0 - v.shape[0]))[None, :]


def _pick(n, candidates):
    for c in candidates:
        if n % c == 0:
            return c
    return n


def kernel(x_prompt, x_sample, cache_swa_k, cache_swa_v, state_conv, state_ssm, state_mlstm_C, state_mlstm_n,
           state_mlstm_m, w_norm_mix, w_in, attn_sinks, conv_w, conv_b, dt_bias, a_log, d_skip, w_norm_ssm,
           igate_b, fgate_b, w_norm_mlstm, w_out, w_norm_mlp, w_up, w_down, w_norm_final):
    bsz, seq, d_model = x_prompt.shape
    bd = x_sample.shape[0]
    depth = w_in.shape[0]
    win = cache_swa_k.shape[2]
    assert x_sample.shape[1] == 1 and seq % CHUNK == 0 and d_model == D_MODEL

    mp = bsz * seq
    tm_p = _pick(mp, (1024, 512, 256, 128))
    tm_mlp = _pick(mp, (1024, 512, 256, 128))
    tb = _pick(bd, (8,))

    hp = x_prompt.reshape(mp, d_model)
    hs = x_sample.reshape(bd, d_model)
    tab_p = _rope_tables(jnp.arange(seq, dtype=jnp.int32))
    tab_s = _rope_tables(jnp.arange(1, dtype=jnp.int32) + PAST_LEN)
    gf = w_norm_final[None, :]
    kc_all = cache_swa_k.reshape(depth, bd, win, D_KV)
    vc_all = cache_swa_v.reshape(depth, bd, win, D_KV)
    ssm_all = state_ssm.reshape(depth, bd, D_SSM, D_STATE)
    mem_all = state_mlstm_C.reshape(depth, bd, D_MLSTM, MLSTM_HEAD_DIM)

    st_p, st_s = [], []
    kv_new = s_new = c_new = None
    for l in range(depth):
        w_in_l = _prep_w_in(w_in, l, 256)
        g_mix = w_norm_mix[l][None, :]
        g_mlp = w_norm_mlp[l][None, :]
        cw = conv_w[l]
        cb = conv_b[l][None, :]
        dtb = _lane_vec(dt_bias[l], 0)
        alog = _lane_vec(a_log[l], 0)
        dskip_rep = jnp.repeat(d_skip[l].astype(F32), SSM_HEAD_DIM)[None, :]
        wns = w_norm_ssm[l][None, :]
        bi = _lane_vec(igate_b[l], GATE_LANE)
        bf = _lane_vec(fgate_b[l], GATE_LANE)
        wnm = w_norm_mlstm[l][None, :]
        sinks = attn_sinks[l].reshape(ATT_HEADS).astype(F32)
        last = l == depth - 1

        att, pk, pv, y, p_ssm, hm, p_c, p_n, p_m, p_tail = _front(
            hp, g_mix, w_in_l, sinks, tab_p, cw, cb, dtb, alog, dskip_rep, wns, bi, bf, wnm, bsz, seq)
        x1 = _outproj(att, y, hm, hp, w_out, l, _pick(mp, (512, 256, 128)))
        p_conv = p_tail[:, CONV_PAD - (CONV_WIDTH - 1):, :]
        st_p.append((pk.reshape(bsz, WINDOW, ATT_KV_HEADS, ATT_HEAD_DIM),
                     pv.reshape(bsz, WINDOW, ATT_KV_HEADS, ATT_HEAD_DIM),
                     p_conv,
                     p_ssm.reshape(bsz, SSM_HEADS, SSM_HEAD_DIM, D_STATE),
                     p_c.reshape(bsz, MLSTM_HEADS, MLSTM_HEAD_DIM, MLSTM_HEAD_DIM),
                     p_n[:, 0:MLSTM_HEADS, :],
                     p_m[:, 0:MLSTM_HEADS, 0]))

        proj_s = _rms_matmul(hs, g_mix, w_in_l, bd, NP // 4)
        q8 = proj_s[:, C_Q:C_Q + D_ATT].reshape(bd, ATT_HEADS, ATT_HEAD_DIM)
        zero = jnp.zeros_like(q8)
        first_kv = (jnp.arange(ATT_HEADS) < ATT_GROUP)[None, :, None]
        qb = jnp.where(first_kv, jnp.concatenate([q8, zero], -1), jnp.concatenate([zero, q8], -1))
        kn = proj_s[:, C_K:C_K + D_KV].reshape(bd, 1, D_KV)
        vn = proj_s[:, C_V:C_V + D_KV].reshape(bd, 1, D_KV)
        sink_rows = jnp.broadcast_to(sinks[:, None], (ATT_HEADS, LANE))
        att_s, sk, sv = _attn_sample(qb, kn, vn, kc_all, vc_all, sink_rows, tab_s, l, kv_new, tb)
        kv_new = (sk, sv)

        m_tile = jnp.pad(state_mlstm_m[l], ((0, 0), (GATE_LANE, LANE - GATE_LANE - MLSTM_HEADS)))
        (conv_new, xs, xdt, bm, cm, ea, ks, n_new, gq, esw, sw, den, mt) = _sample_pre(
            proj_s, state_conv[l].reshape(bd, (CONV_WIDTH - 1) * CONV_DIM), cw, cb, dtb, alog, bi, bf,
            state_mlstm_n[l].reshape(bd, D_MLSTM), m_tile)
        xdt3 = xdt.reshape(bd, 1, D_SSM)
        b3 = bm.reshape(bd, 1, SSM_GROUPS * D_STATE)
        c3 = cm.reshape(bd, 1, SSM_GROUPS * D_STATE)
        q3 = proj_s[:, C_MQ:C_MQ + D_MLSTM].reshape(bd, 1, D_MLSTM)
        k3 = ks.reshape(bd, 1, D_MLSTM)
        v3 = proj_s[:, C_MV:C_MV + D_MLSTM].reshape(bd, 1, D_MLSTM)
        tf = 512
        if (mp // tm_mlp) * (w_up.shape[2] // tf) == bd:
            side = (ea, ssm_all, xdt3, b3, c3, gq, esw, mem_all, q3, k3, v3, s_new, c_new)
            hp, s_new, y3, c_new, qc3 = _mlp(x1, g_mlp, w_up, w_down, gf, l, last, tm_mlp, tf, side)
        else:
            hp = _mlp(x1, g_mlp, w_up, w_down, gf, l, last, tm_mlp, tf)
            s_new, y3 = _ssm_sample(ea, ssm_all, xdt3, b3, c3, l, s_new, tb)
            c_new, qc3 = _mlstm_sample(gq, esw, mem_all, q3, k3, v3, l, c_new, tb)
        ys, hms = _sample_post(proj_s, y3.reshape(bd, D_SSM), xs, dskip_rep, wns, qc3.reshape(bd, D_MLSTM),
                               gq, sw, den, wnm)
        x1s = _outproj(att_s.reshape(bd, D_ATT), ys, hms, hs, w_out, l, bd)
        hs = _mlp(x1s, g_mlp, w_up, w_down, gf, l, last, bd, 512)
        st_s.append((conv_new.reshape(bd, CONV_WIDTH - 1, CONV_DIM),
                     n_new.reshape(bd, MLSTM_HEADS, MLSTM_HEAD_DIM),
                     mt[:, GATE_LANE:GATE_LANE + MLSTM_HEADS]))

    y_prompt = hp.reshape(bsz, seq, d_model)
    y_sample = hs.reshape(bd, 1, d_model)
    p_out = [jnp.stack([s[i] for s in st_p]) for i in range(7)]
    s_conv, s_n, s_m = [jnp.stack([s[i] for s in st_s]) for i in range(3)]
    s_k = kv_new[0].reshape(depth, bd, win, ATT_KV_HEADS, ATT_HEAD_DIM)
    s_v = kv_new[1].reshape(depth, bd, win, ATT_KV_HEADS, ATT_HEAD_DIM)
    s_ssm = s_new.reshape(depth, bd, SSM_HEADS, SSM_HEAD_DIM, D_STATE)
    s_c = c_new.reshape(depth, bd, MLSTM_HEADS, MLSTM_HEAD_DIM, MLSTM_HEAD_DIM)
    return (y_prompt, y_sample, *p_out, s_k, s_v, s_conv, s_ssm, s_c, s_n, s_m)
```

```python
import functools
import math

import jax
import jax.numpy as jnp
from jax import lax
from jax.experimental import pallas as pl
from jax.experimental.pallas import tpu as pltpu

F32 = jnp.float32
BF16 = jnp.bfloat16

D_MODEL = 2048
EPS = 1e-6
PAST_LEN = 8192
ATT_HEAD_DIM = 64
ATT_HEADS = 8
ATT_KV_HEADS = 2
ATT_GROUP = ATT_HEADS // ATT_KV_HEADS
D_ATT = ATT_HEADS * ATT_HEAD_DIM
D_KV = ATT_KV_HEADS * ATT_HEAD_DIM
WINDOW = 128
ROPE_THETA = 500000.0
ROPE_DIM = ATT_HEAD_DIM // 4
SSM_HEAD_DIM = 64
SSM_HEADS = 16
D_SSM = SSM_HEADS * SSM_HEAD_DIM
SSM_GROUPS = 2
D_STATE = 128
CONV_WIDTH = 4
CONV_DIM = D_SSM + 2 * SSM_GROUPS * D_STATE
CHUNK = 128
MLSTM_HEADS = 4
MLSTM_HEAD_DIM = 128
D_MLSTM = MLSTM_HEADS * MLSTM_HEAD_DIM
D_FF = 4 * D_MODEL

C_XBC = 0
C_Q = 1536
C_Z = 2048
C_MQ = 3072
C_MK = 3584
C_MV = 4096
C_MO = 4608
C_K = 5120
C_V = 5248
C_G = 5376
NP = 5632
GATE_LANE = 16

LANE = 128
VMEM_LIMIT = 60 * 1024 * 1024


def _cparams(sem):
    return pltpu.CompilerParams(dimension_semantics=sem, vmem_limit_bytes=VMEM_LIMIT)


def _silu(x):
    return x * jax.nn.sigmoid(x)


def _dot(a, b):
    return jnp.dot(a.astype(BF16), b.astype(BF16), preferred_element_type=F32)


def _dot_nt(a, b):
    return lax.dot_general(a.astype(BF16), b.astype(BF16), (((1,), (1,)), ((), ())),
                           preferred_element_type=F32)


def _dot_tn(a, b):
    return lax.dot_general(a.astype(BF16), b.astype(BF16), (((0,), (0,)), ((), ())),
                           preferred_element_type=F32)


def _tril(n):
    row = lax.broadcasted_iota(jnp.int32, (n, n), 0)
    col = lax.broadcasted_iota(jnp.int32, (n, n), 1)
    return row >= col


def _cumsum_rows(x):
    t = _tril(x.shape[0]).astype(BF16)
    hi = x.astype(BF16)
    r1 = x - hi.astype(F32)
    mid = r1.astype(BF16)
    lo = (r1 - mid.astype(F32)).astype(BF16)
    dot = lambda piece: jnp.dot(t, piece, preferred_element_type=F32)
    return dot(hi) + dot(mid) + dot(lo)


_SRC_Q, _SRC_K, _SRC_V, _SRC_Z, _SRC_XBC, _SRC_DT, _SRC_MQ = 0, 512, 640, 768, 1792, 3328, 3344
_SRC_MI = _SRC_MQ + 4 * D_MLSTM
IN_WIDTH = _SRC_MI + 2 * MLSTM_HEADS


def _prep_w_in_kernel(w_ref, o_ref):
    def put(dst, src, n):
        o_ref[:, dst:dst + n] = w_ref[src:src + n, :].T.astype(BF16)

    put(C_XBC, _SRC_XBC, CONV_DIM)
    put(C_Q, _SRC_Q, D_ATT)
    put(C_Z, _SRC_Z, D_SSM)
    put(C_MQ, _SRC_MQ, 4 * D_MLSTM)
    put(C_K, _SRC_K, D_KV)
    put(C_V, _SRC_V, D_KV)
    tk = w_ref.shape[1]
    dt = w_ref[_SRC_DT:_SRC_DT + SSM_HEADS, :]
    gates = w_ref[_SRC_MI:_SRC_MI + 2 * MLSTM_HEADS, :]
    pad = jnp.zeros((LANE - GATE_LANE - 2 * MLSTM_HEADS, tk), F32)
    lo = jnp.concatenate([dt, gates, pad], axis=0)
    hi = jnp.concatenate([jnp.zeros_like(dt), pltpu.roll(gates, MLSTM_HEADS, 0), pad], axis=0)
    o_ref[:, C_G:C_G + LANE] = lo.T.astype(BF16)
    o_ref[:, C_G + LANE:C_G + 2 * LANE] = hi.T.astype(BF16)


def _prep_w_in(w_in, layer, tk):
    depth, k, n = w_in.shape
    assert n == IN_WIDTH and n % 8 == 0
    wt = jnp.swapaxes(w_in, 1, 2).reshape(depth * n, k)
    return pl.pallas_call(
        _prep_w_in_kernel,
        grid=(k // tk,),
        in_specs=[pl.BlockSpec((n, tk), lambda i: (layer, i))],
        out_specs=pl.BlockSpec((tk, NP), lambda i: (i, 0)),
        out_shape=jax.ShapeDtypeStruct((k, NP), BF16),
        compiler_params=_cparams(("parallel",)),
        name="prep_w_in",
    )(wt)


def _outproj_kernel(a_ref, y_ref, h_ref, x_ref, w_ref, o_ref, wb_ref):
    @pl.when(pl.program_id(0) == 0)
    def _():
        wb_ref[...] = w_ref[...].astype(BF16)

    acc = jnp.dot(a_ref[...], wb_ref[0:D_ATT, :], preferred_element_type=F32)
    acc = acc + jnp.dot(y_ref[...], wb_ref[D_ATT:D_ATT + D_SSM, :], preferred_element_type=F32)
    acc = acc + jnp.dot(h_ref[...], wb_ref[D_ATT + D_SSM:, :], preferred_element_type=F32)
    o_ref[...] = x_ref[...] + acc


def _outproj(att, y, h, x, w, layer, tm):
    m, n = x.shape
    k = w.shape[1]
    rows = lambda width: pl.BlockSpec((tm, width), lambda i: (i, 0))
    return pl.pallas_call(
        _outproj_kernel,
        grid=(m // tm,),
        in_specs=[rows(D_ATT), rows(D_SSM), rows(D_MLSTM), rows(n),
                  pl.BlockSpec((None, k, n), lambda i: (layer, 0, 0), pipeline_mode=pl.Buffered(1))],
        out_specs=rows(n),
        out_shape=jax.ShapeDtypeStruct((m, n), F32),
        scratch_shapes=[pltpu.VMEM((k, n), BF16)],
        compiler_params=_cparams(("arbitrary",)),
        name="outproj",
    )(att, y, h, x, w)


N_MLP_IN, N_SSM_SIDE_IN, N_MLSTM_SIDE_IN = 5, 5, 6


def _mlp_kernel(*refs, final_norm, side, n_prev):
    x_ref, g_ref, wu_ref, wd_ref, gf_ref = refs[:N_MLP_IN]
    n_side = N_SSM_SIDE_IN + N_MLSTM_SIDE_IN if side else 0
    side_in = refs[N_MLP_IN:N_MLP_IN + n_side]
    outs = refs[N_MLP_IN + n_side + n_prev:]
    o_ref = outs[0]
    u_ref = outs[5] if side else outs[1]
    j = pl.program_id(1)

    @pl.when(j == 0)
    def _():
        x = x_ref[...]
        ms = jnp.mean(x * x, axis=-1, keepdims=True)
        u_ref[...] = (x * lax.rsqrt(ms + EPS) * g_ref[...]).astype(BF16)
        o_ref[...] = x
        if side:
            outs[6][...] = jnp.zeros_like(outs[6])
            outs[7][...] = jnp.zeros_like(outs[7])

    h = _dot(u_ref[...], wu_ref[...])
    h = jnp.square(jnp.maximum(h, 0.0))
    o_ref[...] += _dot(h, wd_ref[...])

    if side:
        seq = pl.program_id(0) * pl.num_programs(1) + j
        row = seq % SEQ_ROWS
        so_ref, y_ref, co_ref, qc_ref = outs[1:5]
        yacc_ref, qacc_ref = outs[6:8]
        _ssm_sample_one(0, row, seq, *side_in[:N_SSM_SIDE_IN], so_ref, y_ref, yacc_ref)
        _mlstm_sample_one(0, row, seq, *side_in[N_SSM_SIDE_IN:], co_ref, qc_ref, qacc_ref)

    if final_norm:
        @pl.when(j == pl.num_programs(1) - 1)
        def _():
            y = o_ref[...]
            ms = jnp.mean(y * y, axis=-1, keepdims=True)
            o_ref[...] = y * lax.rsqrt(ms + EPS) * gf_ref[...]


def _mlp(x, g, wu, wd, gf, layer, final_norm, tm, tf, side=None):
    m, d = x.shape
    ff = wu.shape[2]
    nj = ff // tf
    in_specs = [pl.BlockSpec((tm, d), lambda i, j: (i, 0)),
                pl.BlockSpec((1, d), lambda i, j: (0, 0)),
                pl.BlockSpec((None, d, tf), lambda i, j: (layer, 0, j)),
                pl.BlockSpec((None, tf, d), lambda i, j: (layer, j, 0)),
                pl.BlockSpec((1, d), lambda i, j: (0, 0))]
    out_specs = [pl.BlockSpec((tm, d), lambda i, j: (i, 0))]
    out_shape = [jax.ShapeDtypeStruct((m, d), F32)]
    operands = [x, g, wu, wd, gf]
    aliases = {}
    n_prev = 0
    if side is not None:
        ea, ssm, xdt, bvec, cvec, gq, esw, mem, proj_s, ks, prev_s, prev_c = side
        depth, bd = ssm.shape[0], ssm.shape[1]
        assert (m // tm) * nj == bd and nj % SEQ_ROWS == 0
        smem = pl.BlockSpec(memory_space=pltpu.SMEM)
        rows = lambda n, col=0: pl.BlockSpec((SEQ_ROWS, n), lambda i, j: ((i * nj + j) // SEQ_ROWS, col // n))
        state = lambda r, n: pl.BlockSpec((None, 1, r, n), lambda i, j: (layer, i * nj + j, 0, 0))
        in_specs += [smem, state(D_SSM, D_STATE), rows(D_SSM), rows(SSM_GROUPS * D_STATE),
                     rows(SSM_GROUPS * D_STATE),
                     smem, smem, state(D_MLSTM, MLSTM_HEAD_DIM), rows(D_MLSTM, C_MQ), rows(D_MLSTM),
                     rows(D_MLSTM, C_MV)]
        operands += [ea, ssm, xdt, bvec, cvec, gq, esw, mem, proj_s, ks, proj_s]
        out_specs += [state(D_SSM, D_STATE), rows(D_SSM), state(D_MLSTM, MLSTM_HEAD_DIM), rows(D_MLSTM)]
        out_shape += [jax.ShapeDtypeStruct((depth, bd, D_SSM, D_STATE), F32),
                      jax.ShapeDtypeStruct((bd, D_SSM), F32),
                      jax.ShapeDtypeStruct((depth, bd, D_MLSTM, MLSTM_HEAD_DIM), F32),
                      jax.ShapeDtypeStruct((bd, D_MLSTM), F32)]
        if prev_s is not None:
            aliases = {len(operands): 1, len(operands) + 1: 3}
            operands += [prev_s, prev_c]
            in_specs += [pl.BlockSpec(memory_space=pl.ANY)] * 2
            n_prev = 2
    res = pl.pallas_call(
        functools.partial(_mlp_kernel, final_norm=final_norm, side=side is not None, n_prev=n_prev),
        grid=(m // tm, nj),
        in_specs=in_specs,
        out_specs=out_specs,
        out_shape=out_shape,
        input_output_aliases=aliases,
        scratch_shapes=[pltpu.VMEM((tm, d), BF16)] + ([] if side is None else [
            pltpu.VMEM((SEQ_ROWS, D_SSM), F32), pltpu.VMEM((SEQ_ROWS, D_MLSTM), F32)]),
        compiler_params=_cparams(("parallel", "arbitrary")),
        name="mlp",
    )(*operands)
    return res[0] if side is None else res


def _rope_tables(pos):
    half = ROPE_DIM // 2
    inv = jnp.power(jnp.float32(ROPE_THETA), -jnp.arange(half, dtype=jnp.float32) / half)
    ang = pos.astype(jnp.float32)[:, None] * inv[None, :]
    cos = jnp.cos(ang)
    sin = jnp.sin(ang)
    n = pos.shape[0]
    rest = ATT_HEAD_DIM - ROPE_DIM
    c = jnp.concatenate([cos, cos, jnp.ones((n, rest), F32)], axis=1)
    s1 = jnp.concatenate([-sin, jnp.zeros((n, half + rest), F32)], axis=1)
    s2 = jnp.concatenate([jnp.zeros((n, half), F32), sin, jnp.zeros((n, rest), F32)], axis=1)
    rep = LANE // ATT_HEAD_DIM
    return jnp.tile(c, (1, rep)), jnp.tile(s1, (1, rep)), jnp.tile(s2, (1, rep))


def _rope(x, c, s1, s2):
    width = x.shape[-1]
    rep = width // LANE
    half = ROPE_DIM // 2
    if rep > 1:
        c = jnp.concatenate([c] * rep, axis=-1)
        s1 = jnp.concatenate([s1] * rep, axis=-1)
        s2 = jnp.concatenate([s2] * rep, axis=-1)
    axis = x.ndim - 1
    return x * c + pltpu.roll(x, width - half, axis) * s1 + pltpu.roll(x, half, axis) * s2


def _attn_prompt_init(ko_ref, vo_ref):
    ko_ref[...] = jnp.zeros_like(ko_ref)
    vo_ref[...] = jnp.zeros_like(vo_ref)


def _attn_prompt_body(j, sink_ref, q_ref, k_ref, v_ref, c_ref, s1_ref, s2_ref, o_ref, ko_ref, vo_ref):
    w = WINDOW
    c = c_ref[...]
    s1 = s1_ref[...]
    s2 = s2_ref[...]
    krot = _rope(k_ref[...], c, s1, s2)
    v = v_ref[...]
    qrot = _rope(q_ref[...], c, s1, s2)
    kk = jnp.concatenate([ko_ref[...], krot], axis=0).astype(BF16)
    vv = jnp.concatenate([vo_ref[...], v], axis=0).astype(BF16)

    row = lax.broadcasted_iota(jnp.int32, (w, 2 * w), 0)
    col = lax.broadcasted_iota(jnp.int32, (w, 2 * w), 1)
    first_col = jnp.where(j > 0, 0, w)
    mask = (col >= row) & (col <= row + w) & (col >= first_col)
    scale = ATT_HEAD_DIM ** -0.5

    for h in range(ATT_KV_HEADS):
        kh = kk[:, h * ATT_HEAD_DIM:(h + 1) * ATT_HEAD_DIM]
        vh = vv[:, h * ATT_HEAD_DIM:(h + 1) * ATT_HEAD_DIM]
        for g in range(ATT_GROUP):
            hg = h * ATT_GROUP + g
            qh = qrot[:, hg * ATT_HEAD_DIM:(hg + 1) * ATT_HEAD_DIM]
            s = _dot_nt(qh, kh) * scale
            s = jnp.where(mask, s, -jnp.inf)
            sink = sink_ref[hg]
            m = jnp.maximum(jnp.max(s, axis=-1, keepdims=True), sink)
            p = jnp.exp(s - m)
            denom = jnp.sum(p, axis=-1, keepdims=True) + jnp.exp(sink - m)
            o = _dot(p, vh) / denom
            o_ref[:, hg * ATT_HEAD_DIM:(hg + 1) * ATT_HEAD_DIM] = o.astype(o_ref.dtype)
            yield

    ko_ref[...] = krot
    vo_ref[...] = v


def _group_rmsnorm(y, w, groups):
    width = y.shape[-1] // groups
    outs = []
    for g in range(groups):
        yg = y[:, g * width:(g + 1) * width]
        ms = jnp.mean(yg * yg, axis=-1, keepdims=True)
        outs.append(yg * lax.rsqrt(ms + EPS) * w[:, g * width:(g + 1) * width])
    return jnp.concatenate(outs, axis=-1)


CONV_PAD = 8


def _ssd_prompt_init(s_ref, tail_ref):
    tail_ref[...] = jnp.zeros_like(tail_ref)
    s_ref[...] = jnp.zeros_like(s_ref)


def _ssd_prompt_body(xbc_ref, z_ref, g_ref, cw_ref, cb_ref, dtb_ref, alog_ref, dskip_ref, wn_ref, spread_ref,
                     y_ref, s_ref, tail_ref, ybuf_ref, xw_ref):
    q = CHUNK
    pad = CONV_PAD
    hp = SSM_HEAD_DIM
    heads_per_group = SSM_HEADS // SSM_GROUPS
    gw = heads_per_group * hp

    x = xbc_ref[...]
    cw = cw_ref[...]
    row8 = lax.broadcasted_iota(jnp.int32, (pad, CONV_DIM), 0)
    acc = None
    for j in range(CONV_WIDTH - 1):
        shift = CONV_WIDTH - 1 - j
        rolled = pltpu.roll(x, shift, 0)
        head = jnp.where(row8 < shift, tail_ref[j * pad:(j + 1) * pad, :], rolled[0:pad, :])
        tail_ref[j * pad:(j + 1) * pad, :] = rolled[0:pad, :]
        term = jnp.concatenate([head, rolled[pad:, :]], axis=0) * cw[j:j + 1, :]
        acc = term if acc is None else acc + term
    acc = acc + x * cw[CONV_WIDTH - 1:CONV_WIDTH, :]
    xc = _silu(acc + cb_ref[...])

    xs = xc[:, 0:D_SSM]
    bm = xc[:, D_SSM:D_SSM + SSM_GROUPS * D_STATE]
    cm = xc[:, D_SSM + SSM_GROUPS * D_STATE:]

    dt = jax.nn.softplus(g_ref[:, 0:LANE] + dtb_ref[...])
    a_neg = -jnp.exp(alog_ref[...])
    a_col = _cumsum_rows(dt * a_neg)
    a_row = a_col.T
    dt_row = dt.T
    wk_row = jnp.exp(a_row[:, q - 1:q] - a_row) * dt_row
    tril = _tril(q)
    xs_t = xs.T
    ea_col = jnp.exp(a_col)
    ea_hi = ea_col.astype(BF16)
    ea_lo = (ea_col - ea_hi.astype(F32)).astype(BF16)
    spread = spread_ref[...]
    ea_full = (jnp.dot(ea_hi, spread, preferred_element_type=F32)
               + jnp.dot(ea_lo, spread, preferred_element_type=F32))
    yield

    cs_parts = []
    for g in range(SSM_GROUPS):
        bg = bm[:, g * D_STATE:(g + 1) * D_STATE]
        cg = cm[:, g * D_STATE:(g + 1) * D_STATE]
        cb = _dot_nt(cg, bg)
        cs_parts.append(_dot_nt(cg, s_ref[g * gw:(g + 1) * gw, :]))
        for r in range(heads_per_group):
            h = g * heads_per_group + r
            seg = a_col[:, h:h + 1] - a_row[h:h + 1, :]
            wmat = jnp.exp(jnp.where(tril, seg, -jnp.inf)) * cb * dt_row[h:h + 1, :]
            ybuf_ref[:, h * hp:(h + 1) * hp] = _dot(wmat, xs[:, h * hp:(h + 1) * hp])
            xw_ref[r * hp:(r + 1) * hp, :] = xs_t[h * hp:(h + 1) * hp, :] * wk_row[h:h + 1, :]
            yield
        upd = _dot(xw_ref[...], bg)
        for r in range(heads_per_group):
            h = g * heads_per_group + r
            decay = jnp.exp(a_row[h:h + 1, q - 1:q])
            s_ref[h * hp:(h + 1) * hp, :] = s_ref[h * hp:(h + 1) * hp, :] * decay + upd[r * hp:(r + 1) * hp, :]
        yield

    y = ybuf_ref[...] + jnp.concatenate(cs_parts, axis=1) * ea_full
    y = y + dskip_ref[...] * xs
    y = y * _silu(z_ref[...])
    y_ref[...] = _group_rmsnorm(y, wn_ref[...], SSM_GROUPS).astype(y_ref.dtype)


def _mlstm_prompt_init(c_ref, n_ref, m_ref):
    c_ref[...] = jnp.zeros_like(c_ref)
    n_ref[...] = jnp.zeros_like(n_ref)
    m_ref[...] = jnp.zeros_like(m_ref)


def _mlstm_prompt_body(q_ref, k_ref, v_ref, o_ref, g_ref, bi_ref, bf_ref, wn_ref, h_ref, c_ref, n_ref, m_ref):
    t = CHUNK
    d = MLSTM_HEAD_DIM

    gi = g_ref[:, 0:LANE] + bi_ref[...]
    gf = g_ref[:, LANE:2 * LANE] + bf_ref[...]
    b_col = _cumsum_rows(jax.nn.log_sigmoid(gf))
    b_row = b_col.T
    i_row = gi.T
    tril = _tril(t)
    k_all = k_ref[...] * (d ** -0.5)

    for h in range(MLSTM_HEADS):
        ln = GATE_LANE + h
        bq = b_col[:, ln:ln + 1]
        logw = jnp.where(tril, bq - b_row[ln:ln + 1, :] + i_row[ln:ln + 1, :], -jnp.inf)
        m_prev = m_ref[h:h + 1, 0:1]
        log_inter = bq + m_prev
        mt = jnp.maximum(log_inter, jnp.max(logw, axis=-1, keepdims=True))
        qh = q_ref[:, h * d:(h + 1) * d]
        kh = k_all[:, h * d:(h + 1) * d]
        vh = v_ref[:, h * d:(h + 1) * d]
        ch = c_ref[h * d:(h + 1) * d, :]
        nh = n_ref[h:h + 1, :]
        sw = jnp.exp(logw - mt) * _dot_nt(qh, kh)
        gq = jnp.exp(log_inter - mt)
        num = _dot(sw, vh) + _dot(qh, ch) * gq
        den = jnp.sum(sw, axis=-1, keepdims=True) + jnp.sum(qh * nh, axis=-1, keepdims=True) * gq
        hh = num / jnp.maximum(jnp.abs(den), jnp.exp(-mt))

        m_new = mt[t - 1:t, :]
        b_last = b_col[t - 1:t, ln:ln + 1]
        wk = jnp.exp(b_last - bq + gi[:, ln:ln + 1] - m_new)
        g_end = jnp.exp(b_last + m_prev - m_new)
        kw = kh * wk
        c_ref[h * d:(h + 1) * d, :] = ch * g_end + _dot_tn(kw, vh)
        n_ref[h:h + 1, :] = nh * g_end + jnp.sum(kw, axis=0, keepdims=True)
        m_ref[h:h + 1, :] = jnp.broadcast_to(m_new, (1, LANE))

        ms = jnp.mean(hh * hh, axis=-1, keepdims=True)
        hn = hh * lax.rsqrt(ms + EPS) * wn_ref[:, h * d:(h + 1) * d]
        h_ref[:, h * d:(h + 1) * d] = (hn * jax.nn.sigmoid(o_ref[:, h * d:(h + 1) * d])).astype(h_ref.dtype)
        yield

def _interleave(stages):
    live = [[gen, 0, n] for gen, n in stages]
    while live:
        item = min(live, key=lambda s: s[1] / s[2])
        try:
            next(item[0])
            item[1] += 1
        except StopIteration:
            live.remove(item)


PROJ_TILE = 256


def _project_next(x_ref, g_ref, w_ref, pnext_ref):
    x = x_ref[...]
    u = (x * lax.rsqrt(jnp.mean(x * x, axis=-1, keepdims=True) + EPS) * g_ref[...]).astype(BF16)
    for ct in range(NP // PROJ_TILE):
        cols = slice(ct * PROJ_TILE, (ct + 1) * PROJ_TILE)
        pnext_ref[:, cols] = jnp.dot(u, w_ref[:, cols], preferred_element_type=F32)
        yield


def _front_kernel(sink_ref, x_ref, x0_ref, g_ref, w_ref, c_ref, s1_ref, s2_ref,
                  cw_ref, cb_ref, dtb_ref, alog_ref, dskip_ref, wns_ref, spread_ref, bi_ref, bf_ref, wnm_ref,
                  att_ref, ko_ref, vo_ref, y_ref, s_ref, h_ref, mem_ref, n_ref, m_ref, ptail_ref,
                  pcur_ref, pnext_ref, tail_ref, ybuf_ref, xw_ref, *, nblk):
    j = pl.program_id(1)
    first = (pl.program_id(0) == 0) & (j == 0)
    t = CHUNK

    @pl.when(first)
    def _():
        for _ in _project_next(x0_ref, g_ref, w_ref, pcur_ref):
            pass

    @pl.when(jnp.logical_not(first))
    def _():
        pcur_ref[...] = pnext_ref[...]

    @pl.when(j == 0)
    def _():
        _attn_prompt_init(ko_ref, vo_ref)
        _ssd_prompt_init(s_ref, tail_ref)
        _mlstm_prompt_init(mem_ref, n_ref, m_ref)

    def attn(i):
        seg = lambda col, width: pcur_ref.at[i * t:(i + 1) * t, col:col + width]
        rows = lambda ref: ref.at[i * t:(i + 1) * t, :]
        return _attn_prompt_body(j * nblk + i, sink_ref, seg(C_Q, D_ATT), seg(C_K, D_KV), seg(C_V, D_KV),
                                 rows(c_ref), rows(s1_ref), rows(s2_ref), rows(att_ref), ko_ref, vo_ref)

    def ssd(i):
        seg = lambda col, width: pcur_ref.at[i * t:(i + 1) * t, col:col + width]
        return _ssd_prompt_body(seg(C_XBC, CONV_DIM), seg(C_Z, D_SSM), seg(C_G, 2 * LANE), cw_ref, cb_ref, dtb_ref,
                                alog_ref, dskip_ref, wns_ref, spread_ref, y_ref.at[i * t:(i + 1) * t, :], s_ref,
                                tail_ref, ybuf_ref, xw_ref)

    def mlstm(i):
        seg = lambda col, width: pcur_ref.at[i * t:(i + 1) * t, col:col + width]
        return _mlstm_prompt_body(seg(C_MQ, D_MLSTM), seg(C_MK, D_MLSTM), seg(C_MV, D_MLSTM), seg(C_MO, D_MLSTM),
                                  seg(C_G, 2 * LANE), bi_ref, bf_ref, wnm_ref, h_ref.at[i * t:(i + 1) * t, :],
                                  mem_ref, n_ref, m_ref)

    def blocks(body):
        for i in range(nblk):
            yield from body(i)

    ptail_ref[...] = pcur_ref[nblk * t - CONV_PAD:nblk * t, C_XBC:C_XBC + CONV_DIM]
    _interleave([
        (_project_next(x_ref, g_ref, w_ref, pnext_ref), NP // PROJ_TILE + 1),
        (blocks(attn), nblk * (ATT_HEADS + 1)),
        (blocks(ssd), nblk * (SSM_HEADS + SSM_GROUPS + 2)),
        (blocks(mlstm), nblk * (MLSTM_HEADS + 1)),
    ])


def _front(x, g_mix, w, sinks, tables, cw, cb, dtb, alog, dskip_rep, wns, bi, bf, wnm, bsz, seq):
    nblk = 2 if (seq // CHUNK) % 2 == 0 else 1
    t = nblk * CHUNK
    nb = seq // t
    total = bsz * nb
    c, s1, s2 = tables
    head_of_lane = jnp.arange(D_SSM, dtype=jnp.int32) // SSM_HEAD_DIM
    spread = (jnp.arange(LANE, dtype=jnp.int32)[:, None] == head_of_lane[None, :]).astype(BF16)
    row = lambda b, j: b * nb + j
    full = lambda shape: pl.BlockSpec(shape, lambda b, j: (0,) * len(shape))
    vec = lambda n: full((1, n))
    tab = pl.BlockSpec((t, LANE), lambda b, j: (j, 0))
    rows_out = lambda width: pl.BlockSpec((t, width), lambda b, j: (row(b, j), 0))
    per_seq = lambda r, n: pl.BlockSpec((None, r, n), lambda b, j: (b, 0, 0))
    return pl.pallas_call(
        functools.partial(_front_kernel, nblk=nblk),
        grid=(bsz, nb),
        in_specs=[pl.BlockSpec(memory_space=pltpu.SMEM),
                  pl.BlockSpec((t, D_MODEL), lambda b, j: (jnp.minimum(row(b, j) + 1, total - 1), 0)),
                  pl.BlockSpec((t, D_MODEL), lambda b, j: (0, 0), pipeline_mode=pl.Buffered(1)),
                  vec(D_MODEL),
                  pl.BlockSpec((D_MODEL, NP), lambda b, j: (0, 0), pipeline_mode=pl.Buffered(1)),
                  tab, tab, tab,
                  full((CONV_WIDTH, CONV_DIM)), vec(CONV_DIM), vec(LANE), vec(LANE), vec(D_SSM), vec(D_SSM),
                  full((LANE, D_SSM)),
                  vec(LANE), vec(LANE), vec(D_MLSTM)],
        out_specs=[rows_out(D_ATT), per_seq(CHUNK, D_KV), per_seq(CHUNK, D_KV),
                   rows_out(D_SSM), per_seq(D_SSM, D_STATE),
                   rows_out(D_MLSTM), per_seq(D_MLSTM, MLSTM_HEAD_DIM), per_seq(8, LANE), per_seq(8, LANE),
                   per_seq(CONV_PAD, CONV_DIM)],
        out_shape=[jax.ShapeDtypeStruct((bsz * seq, D_ATT), BF16),
                   jax.ShapeDtypeStruct((bsz, CHUNK, D_KV), F32),
                   jax.ShapeDtypeStruct((bsz, CHUNK, D_KV), F32),
                   jax.ShapeDtypeStruct((bsz * seq, D_SSM), BF16),
                   jax.ShapeDtypeStruct((bsz, D_SSM, D_STATE), F32),
                   jax.ShapeDtypeStruct((bsz * seq, D_MLSTM), BF16),
                   jax.ShapeDtypeStruct((bsz, D_MLSTM, MLSTM_HEAD_DIM), F32),
                   jax.ShapeDtypeStruct((bsz, 8, LANE), F32),
                   jax.ShapeDtypeStruct((bsz, 8, LANE), F32),
                   jax.ShapeDtypeStruct((bsz, CONV_PAD, CONV_DIM), F32)],
        scratch_shapes=[pltpu.VMEM((t, NP), F32),
                        pltpu.VMEM((t, NP), F32),
                        pltpu.VMEM(((CONV_WIDTH - 1) * CONV_PAD, CONV_DIM), F32),
                        pltpu.VMEM((CHUNK, D_SSM), F32),
                        pltpu.VMEM((D_SSM // SSM_GROUPS, CHUNK), F32)],
        compiler_params=_cparams(("arbitrary", "arbitrary")),
        name="front",
    )(sinks, x, x, g_mix, w, c, s1, s2, cw, cb, dtb, alog, dskip_rep, wns, spread, bi, bf, wnm)


def _stacked_out(prev, depth, shape, block, layer):
    spec = pl.BlockSpec((None,) + block, lambda i: (layer,) + (i,) + (0,) * (len(block) - 1))
    out_shape = jax.ShapeDtypeStruct((depth,) + shape, F32)
    extra_in = [] if prev is None else [prev]
    extra_spec = [] if prev is None else [pl.BlockSpec(memory_space=pl.ANY)]
    return spec, out_shape, extra_in, extra_spec


def _drop_aliased(kernel_fn, n_in, n_prev):
    def wrapped(*refs):
        return kernel_fn(*refs[:n_in], *refs[n_in + n_prev:])
    return wrapped


def _attn_sample_kernel(q_ref, k_ref, v_ref, kc_ref, vc_ref, sink_ref, c_ref, s1_ref, s2_ref,
                        o_ref, ko_ref, vo_ref):
    w = kc_ref.shape[1]
    c = c_ref[...]
    s1 = s1_ref[...]
    s2 = s2_ref[...]
    qb = _rope(q_ref[...], c, s1, s2)
    kn = _rope(k_ref[...], c, s1, s2)
    vn = v_ref[...]
    kc = kc_ref[...]
    vc = vc_ref[...]
    ko_ref[:, 0:w - 1, :] = kc[:, 1:w, :]
    ko_ref[:, w - 1:w, :] = kn
    vo_ref[:, 0:w - 1, :] = vc[:, 1:w, :]
    vo_ref[:, w - 1:w, :] = vn

    scale = ATT_HEAD_DIM ** -0.5
    s = jnp.einsum("bhd,bkd->bhk", qb.astype(BF16), kc.astype(BF16),
                   preferred_element_type=F32) * scale
    sn = jnp.sum(qb * kn, axis=-1, keepdims=True) * scale
    sink = sink_ref[...][None, :, 0:1]
    m = jnp.maximum(jnp.maximum(jnp.max(s, axis=-1, keepdims=True), sn), sink)
    p = jnp.exp(s - m)
    pn = jnp.exp(sn - m)
    denom = jnp.sum(p, axis=-1, keepdims=True) + pn + jnp.exp(sink - m)
    p = p / denom
    pn = pn / denom
    o = jnp.einsum("bhk,bkd->bhd", p.astype(BF16), vc.astype(BF16), preferred_element_type=F32)
    o = o + pn * vn
    head = lax.broadcasted_iota(jnp.int32, o.shape[:2] + (ATT_HEAD_DIM,), 1)
    o_ref[...] = jnp.where(head < ATT_GROUP, o[:, :, 0:ATT_HEAD_DIM], o[:, :, ATT_HEAD_DIM:]).astype(o_ref.dtype)


def _attn_sample(qb, kn, vn, kc, vc, sink_rows, tables, layer, prev, tb):
    depth, bd, w, _ = kc.shape
    c, s1, s2 = tables
    blk3 = lambda r, n: pl.BlockSpec((tb, r, n), lambda i: (i, 0, 0))
    cache = pl.BlockSpec((None, tb, w, LANE), lambda i: (layer, i, 0, 0))
    vec = pl.BlockSpec((1, LANE), lambda i: (0, 0))
    prev_k, prev_v = (None, None) if prev is None else prev
    k_spec, k_shape, k_in, k_in_spec = _stacked_out(prev_k, depth, (bd, w, LANE), (tb, w, LANE), layer)
    v_spec, v_shape, v_in, v_in_spec = _stacked_out(prev_v, depth, (bd, w, LANE), (tb, w, LANE), layer)
    n_in = 9
    n_prev = len(k_in) + len(v_in)
    return pl.pallas_call(
        _drop_aliased(_attn_sample_kernel, n_in, n_prev),
        grid=(bd // tb,),
        in_specs=[blk3(ATT_HEADS, LANE), blk3(1, LANE), blk3(1, LANE), cache, cache,
                  pl.BlockSpec((ATT_HEADS, LANE), lambda i: (0, 0)), vec, vec, vec] + k_in_spec + v_in_spec,
        out_specs=[blk3(ATT_HEADS, ATT_HEAD_DIM), k_spec, v_spec],
        out_shape=[jax.ShapeDtypeStruct((bd, ATT_HEADS, ATT_HEAD_DIM), BF16), k_shape, v_shape],
        input_output_aliases={n_in + t: 1 + t for t in range(n_prev)},
        compiler_params=_cparams(("parallel",)),
        name="attn_sample",
    )(qb, kn, vn, kc, vc, sink_rows, c, s1, s2, *k_in, *v_in)


def _lane_place(cols, lane0):
    m = cols[0].shape[0]
    lane = lax.broadcasted_iota(jnp.int32, (m, LANE), 1)
    out = jnp.zeros((m, LANE), F32)
    for i, col in enumerate(cols):
        out = jnp.where(lane == lane0 + i, col, out)
    return out


def _sample_pre_kernel(x_ref, gmix_ref, w_ref, cs_ref, cw_ref, cb_ref, dtb_ref, alog_ref, bi_ref, bf_ref, n_ref,
                       m_ref, proj_ref, conv_ref, xs_ref, xdt_ref, bm_ref, cm_ref, ea_ref,
                       ks_ref, nn_ref, g_ref, esw_ref, sw_ref, den_ref, mt_ref):
    for _ in _project_next(x_ref, gmix_ref, w_ref, proj_ref):
        pass
    xbc = proj_ref[:, C_XBC:C_XBC + CONV_DIM]
    s0 = cs_ref[0]
    s1 = cs_ref[1]
    s2 = cs_ref[2]
    cw = cw_ref[...]
    acc = s0 * cw[0:1, :]
    acc = acc + s1 * cw[1:2, :]
    acc = acc + s2 * cw[2:3, :]
    acc = acc + xbc * cw[3:4, :]
    xc = _silu(acc + cb_ref[...])
    conv_ref[0] = s1
    conv_ref[1] = s2
    conv_ref[2] = xbc

    xs = xc[:, 0:D_SSM]
    xs_ref[...] = xs
    bm_ref[...] = xc[:, D_SSM:D_SSM + SSM_GROUPS * D_STATE]
    cm_ref[...] = xc[:, D_SSM + SSM_GROUPS * D_STATE:]
    dt = jax.nn.softplus(proj_ref[:, C_G:C_G + LANE] + dtb_ref[...])
    ea_ref[...] = jnp.exp(dt * (-jnp.exp(alog_ref[...])))
    hp = SSM_HEAD_DIM
    for h in range(SSM_HEADS):
        xdt_ref[:, h * hp:(h + 1) * hp] = xs[:, h * hp:(h + 1) * hp] * dt[:, h:h + 1]

    d = MLSTM_HEAD_DIM
    gi = proj_ref[:, C_G:C_G + LANE] + bi_ref[...]
    gf = proj_ref[:, C_G + LANE:C_G + 2 * LANE] + bf_ref[...]
    log_inter = jax.nn.log_sigmoid(gf) + m_ref[...]
    mt = jnp.maximum(log_inter, gi)
    gq = jnp.exp(log_inter - mt)
    esw = jnp.exp(gi - mt)
    ks = proj_ref[:, C_MK:C_MK + D_MLSTM] * (d ** -0.5)
    ks_ref[...] = ks
    qk_cols = []
    qn_cols = []
    for h in range(MLSTM_HEADS):
        qh = proj_ref[:, C_MQ + h * d:C_MQ + (h + 1) * d]
        kh = ks[:, h * d:(h + 1) * d]
        nh = n_ref[:, h * d:(h + 1) * d]
        qk_cols.append(jnp.sum(qh * kh, axis=-1, keepdims=True))
        qn_cols.append(jnp.sum(qh * nh, axis=-1, keepdims=True))
        ln = GATE_LANE + h
        nn_ref[:, h * d:(h + 1) * d] = nh * gq[:, ln:ln + 1] + kh * esw[:, ln:ln + 1]
    sw = esw * _lane_place(qk_cols, GATE_LANE)
    den = sw + _lane_place(qn_cols, GATE_LANE) * gq
    g_ref[...] = gq
    esw_ref[...] = esw
    sw_ref[...] = sw
    den_ref[...] = jnp.maximum(jnp.abs(den), jnp.exp(-mt))
    mt_ref[...] = mt


def _sample_pre(x, g_mix, w, conv_state, layer, cw, cb, dtb, alog, bi, bf, n_state, m_tile):
    bd = x.shape[0]
    taps = CONV_WIDTH - 1
    tile = jax.ShapeDtypeStruct((bd, LANE), F32)
    whole = lambda a: pl.BlockSpec(a.shape, lambda i: (0,) * a.ndim)
    in_specs = [whole(x), whole(g_mix), pl.BlockSpec(w.shape, lambda i: (0, 0), pipeline_mode=pl.Buffered(1)),
                pl.BlockSpec((None, taps, bd, CONV_DIM), lambda i: (layer, 0, 0, 0))]
    in_specs += [whole(a) for a in (cw, cb, dtb, alog, bi, bf, n_state, m_tile)]
    outs = [jax.ShapeDtypeStruct((bd, NP), F32),
            jax.ShapeDtypeStruct((taps, bd, CONV_DIM), F32),
            jax.ShapeDtypeStruct((bd, D_SSM), F32),
            jax.ShapeDtypeStruct((bd, D_SSM), F32),
            jax.ShapeDtypeStruct((bd, SSM_GROUPS * D_STATE), F32),
            jax.ShapeDtypeStruct((bd, SSM_GROUPS * D_STATE), F32),
            tile,
            jax.ShapeDtypeStruct((bd, D_MLSTM), F32),
            jax.ShapeDtypeStruct((bd, D_MLSTM), F32),
            tile, tile, tile, tile, tile]
    return pl.pallas_call(
        _sample_pre_kernel,
        grid=(1,),
        in_specs=in_specs,
        out_specs=[pl.BlockSpec(o.shape, lambda i, nd=len(o.shape): (0,) * nd) for o in outs],
        out_shape=outs,
        compiler_params=_cparams(("arbitrary",)),
        name="sample_pre",
    )(x, g_mix, w, conv_state, cw, cb, dtb, alog, bi, bf, n_state, m_tile)


def _column_tile(row):
    return jnp.broadcast_to(row, (LANE, LANE)).T


SEQ_ROWS = 8


def _put_row(out_ref, acc_ref, vr, col, row):
    width = row.shape[1]
    rid = lax.broadcasted_iota(jnp.int32, (SEQ_ROWS, width), 0)
    blk = jnp.where(rid == vr, row, acc_ref[:, col:col + width])
    acc_ref[:, col:col + width] = blk
    out_ref[:, col:col + width] = blk


def _ssm_sample_one(b, vr, gb, ea_ref, s_ref, xdt_ref, b_ref, c_ref, so_ref, y_ref, yacc_ref):
    hp = SSM_HEAD_DIM
    heads_per_tile = LANE // hp
    n_tiles = D_SSM // LANE
    tiles_per_group = n_tiles // SSM_GROUPS
    top = lax.broadcasted_iota(jnp.int32, (LANE, 1), 0) < hp
    xrow = xdt_ref[pl.ds(vr, 1), :]
    brow = b_ref[pl.ds(vr, 1), :]
    crow = c_ref[pl.ds(vr, 1), :]
    for t in range(n_tiles):
        g = t // tiles_per_group
        xcol = _column_tile(xrow[:, t * LANE:(t + 1) * LANE])
        bg = brow[:, g * D_STATE:(g + 1) * D_STATE]
        cg = jnp.broadcast_to(crow[:, g * D_STATE:(g + 1) * D_STATE], (8, D_STATE))
        e0 = ea_ref[gb, heads_per_tile * t]
        e1 = ea_ref[gb, heads_per_tile * t + 1]
        decay = jnp.where(top, e0, e1)
        sn = s_ref[b, t * LANE:(t + 1) * LANE, :] * decay + xcol * bg
        so_ref[b, t * LANE:(t + 1) * LANE, :] = sn
        _put_row(y_ref, yacc_ref, vr, t * LANE, _dot_nt(cg, sn)[0:1, :])


def _mlstm_sample_one(b, vr, gb, g_ref, esw_ref, c_ref, q_ref, k_ref, v_ref, co_ref, qc_ref, qacc_ref):
    d = MLSTM_HEAD_DIM
    qrow = q_ref[pl.ds(vr, 1), :]
    krow = k_ref[pl.ds(vr, 1), :]
    vrow = v_ref[pl.ds(vr, 1), :]
    for h in range(MLSTM_HEADS):
        qcol = _column_tile(qrow[:, h * d:(h + 1) * d])
        kcol = _column_tile(krow[:, h * d:(h + 1) * d])
        ch = c_ref[b, h * d:(h + 1) * d, :]
        g_end = g_ref[gb, GATE_LANE + h]
        wk = esw_ref[gb, GATE_LANE + h]
        _put_row(qc_ref, qacc_ref, vr, h * d, jnp.sum(qcol * ch, axis=0, keepdims=True))
        co_ref[b, h * d:(h + 1) * d, :] = ch * g_end + (kcol * wk) * vrow[:, h * d:(h + 1) * d]


def _sample_post_kernel(proj_ref, y_ref, xs_ref, dskip_ref, wns_ref, qc_ref, g_ref, sw_ref, den_ref, wnm_ref,
                        ys_ref, hs_ref):
    y = y_ref[...] + dskip_ref[...] * xs_ref[...]
    y = y * _silu(proj_ref[:, C_Z:C_Z + D_SSM])
    ys_ref[...] = _group_rmsnorm(y, wns_ref[...], SSM_GROUPS).astype(ys_ref.dtype)

    d = MLSTM_HEAD_DIM
    for h in range(MLSTM_HEADS):
        ln = GATE_LANE + h
        vh = proj_ref[:, C_MV + h * d:C_MV + (h + 1) * d]
        num = sw_ref[:, ln:ln + 1] * vh + qc_ref[:, h * d:(h + 1) * d] * g_ref[:, ln:ln + 1]
        hh = num / den_ref[:, ln:ln + 1]
        ms = jnp.mean(hh * hh, axis=-1, keepdims=True)
        hn = hh * lax.rsqrt(ms + EPS) * wnm_ref[:, h * d:(h + 1) * d]
        gate = jax.nn.sigmoid(proj_ref[:, C_MO + h * d:C_MO + (h + 1) * d])
        hs_ref[:, h * d:(h + 1) * d] = (hn * gate).astype(hs_ref.dtype)


def _sample_post(proj, y, xs, dskip_rep, wns, qc, gq, sw, den, wnm):
    bd = proj.shape[0]
    return pl.pallas_call(
        _sample_post_kernel,
        out_shape=[jax.ShapeDtypeStruct((bd, D_SSM), BF16), jax.ShapeDtypeStruct((bd, D_MLSTM), BF16)],
        compiler_params=pltpu.CompilerParams(vmem_limit_bytes=VMEM_LIMIT),
        name="sample_post",
    )(proj, y, xs, dskip_rep, wns, qc, gq, sw, den, wnm)


def _lane_vec(v, lane0):
    return jnp.pad(v.astype(F32), (lane0, LANE - lane0 - v.shape[0]))[None, :]


def _pick(n, candidates):
    for c in candidates:
        if n % c == 0:
            return c
    return n


def kernel(x_prompt, x_sample, cache_swa_k, cache_swa_v, state_conv, state_ssm, state_mlstm_C, state_mlstm_n,
           state_mlstm_m, w_norm_mix, w_in, attn_sinks, conv_w, conv_b, dt_bias, a_log, d_skip, w_norm_ssm,
           igate_b, fgate_b, w_norm_mlstm, w_out, w_norm_mlp, w_up, w_down, w_norm_final):
    bsz, seq, d_model = x_prompt.shape
    bd = x_sample.shape[0]
    depth = w_in.shape[0]
    win = cache_swa_k.shape[2]
    assert x_sample.shape[1] == 1 and seq % CHUNK == 0 and d_model == D_MODEL

    mp = bsz * seq
    tm_mlp = _pick(mp, (1024, 512, 256, 128))
    tb = _pick(bd, (8,))

    hp = x_prompt.reshape(mp, d_model)
    hs = x_sample.reshape(bd, d_model)
    tab_p = _rope_tables(jnp.arange(seq, dtype=jnp.int32))
    tab_s = _rope_tables(jnp.arange(1, dtype=jnp.int32) + PAST_LEN)
    gf = w_norm_final[None, :]
    kc_all = cache_swa_k.reshape(depth, bd, win, D_KV)
    vc_all = cache_swa_v.reshape(depth, bd, win, D_KV)
    ssm_all = state_ssm.reshape(depth, bd, D_SSM, D_STATE)
    mem_all = state_mlstm_C.reshape(depth, bd, D_MLSTM, MLSTM_HEAD_DIM)
    conv_all = jnp.swapaxes(state_conv, 1, 2)

    st_p, st_s = [], []
    kv_new = s_new = c_new = None
    for l in range(depth):
        w_in_l = _prep_w_in(w_in, l, 256)
        g_mix = w_norm_mix[l][None, :]
        g_mlp = w_norm_mlp[l][None, :]
        cw = conv_w[l]
        cb = conv_b[l][None, :]
        dtb = _lane_vec(dt_bias[l], 0)
        alog = _lane_vec(a_log[l], 0)
        dskip_rep = jnp.repeat(d_skip[l].astype(F32), SSM_HEAD_DIM)[None, :]
        wns = w_norm_ssm[l][None, :]
        bi = _lane_vec(igate_b[l], GATE_LANE)
        bf = _lane_vec(fgate_b[l], GATE_LANE)
        wnm = w_norm_mlstm[l][None, :]
        sinks = attn_sinks[l].reshape(ATT_HEADS).astype(F32)
        last = l == depth - 1

        att, pk, pv, y, p_ssm, hm, p_c, p_n, p_m, p_tail = _front(
            hp, g_mix, w_in_l, sinks, tab_p, cw, cb, dtb, alog, dskip_rep, wns, bi, bf, wnm, bsz, seq)
        x1 = _outproj(att, y, hm, hp, w_out, l, _pick(mp, (512, 256, 128)))
        p_conv = p_tail[:, CONV_PAD - (CONV_WIDTH - 1):, :]
        st_p.append((pk.reshape(bsz, WINDOW, ATT_KV_HEADS, ATT_HEAD_DIM),
                     pv.reshape(bsz, WINDOW, ATT_KV_HEADS, ATT_HEAD_DIM),
                     p_conv,
                     p_ssm.reshape(bsz, SSM_HEADS, SSM_HEAD_DIM, D_STATE),
                     p_c.reshape(bsz, MLSTM_HEADS, MLSTM_HEAD_DIM, MLSTM_HEAD_DIM),
                     p_n[:, 0:MLSTM_HEADS, :],
                     p_m[:, 0:MLSTM_HEADS, 0]))

        m_tile = jnp.pad(state_mlstm_m[l], ((0, 0), (GATE_LANE, LANE - GATE_LANE - MLSTM_HEADS)))
        (proj_s, conv_new, xs, xdt, bm, cm, ea, ks, n_new, gq, esw, sw, den, mt) = _sample_pre(
            hs, g_mix, w_in_l, conv_all, l, cw, cb, dtb, alog, bi, bf, state_mlstm_n[l].reshape(bd, D_MLSTM),
            m_tile)
        q8 = proj_s[:, C_Q:C_Q + D_ATT].reshape(bd, ATT_HEADS, ATT_HEAD_DIM)
        zero = jnp.zeros_like(q8)
        first_kv = (jnp.arange(ATT_HEADS) < ATT_GROUP)[None, :, None]
        qb = jnp.where(first_kv, jnp.concatenate([q8, zero], -1), jnp.concatenate([zero, q8], -1))
        kn = proj_s[:, C_K:C_K + D_KV].reshape(bd, 1, D_KV)
        vn = proj_s[:, C_V:C_V + D_KV].reshape(bd, 1, D_KV)
        sink_rows = jnp.broadcast_to(sinks[:, None], (ATT_HEADS, LANE))
        att_s, sk, sv = _attn_sample(qb, kn, vn, kc_all, vc_all, sink_rows, tab_s, l, kv_new, tb)
        kv_new = (sk, sv)
        side = (ea, ssm_all, xdt, bm, cm, gq, esw, mem_all, proj_s, ks, s_new, c_new)
        hp, s_new, y_read, c_new, qc = _mlp(x1, g_mlp, w_up, w_down, gf, l, last, tm_mlp, 512, side)
        ys, hms = _sample_post(proj_s, y_read, xs, dskip_rep, wns, qc, gq, sw, den, wnm)
        x1s = _outproj(att_s.reshape(bd, D_ATT), ys, hms, hs, w_out, l, bd)
        hs = _mlp(x1s, g_mlp, w_up, w_down, gf, l, last, bd, 512)
        st_s.append((conv_new,
                     n_new.reshape(bd, MLSTM_HEADS, MLSTM_HEAD_DIM),
                     mt[:, GATE_LANE:GATE_LANE + MLSTM_HEADS]))

    y_prompt = hp.reshape(bsz, seq, d_model)
    y_sample = hs.reshape(bd, 1, d_model)
    p_out = [jnp.stack([s[i] for s in st_p]) for i in range(7)]
    s_conv, s_n, s_m = [jnp.stack([s[i] for s in st_s]) for i in range(3)]
    s_conv = jnp.swapaxes(s_conv, 1, 2)
    s_k = kv_new[0].reshape(depth, bd, win, ATT_KV_HEADS, ATT_HEAD_DIM)
    s_v = kv_new[1].reshape(depth, bd, win, ATT_KV_HEADS, ATT_HEAD_DIM)
    s_ssm = s_new.reshape(depth, bd, SSM_HEADS, SSM_HEAD_DIM, D_STATE)
    s_c = c_new.reshape(depth, bd, MLSTM_HEADS, MLSTM_HEAD_DIM, MLSTM_HEAD_DIM)
    return (y_prompt, y_sample, *p_out, s_k, s_v, s_conv, s_ssm, s_c, s_n, s_m)
```

```python
import functools
import math

import jax
import jax.numpy as jnp
from jax import lax
from jax.experimental import pallas as pl
from jax.experimental.pallas import tpu as pltpu

F32 = jnp.float32
BF16 = jnp.bfloat16

D_MODEL = 2048
EPS = 1e-6
PAST_LEN = 8192
ATT_HEAD_DIM = 64
ATT_HEADS = 8
ATT_KV_HEADS = 2
ATT_GROUP = ATT_HEADS // ATT_KV_HEADS
D_ATT = ATT_HEADS * ATT_HEAD_DIM
D_KV = ATT_KV_HEADS * ATT_HEAD_DIM
WINDOW = 128
ROPE_THETA = 500000.0
ROPE_DIM = ATT_HEAD_DIM // 4
SSM_HEAD_DIM = 64
SSM_HEADS = 16
D_SSM = SSM_HEADS * SSM_HEAD_DIM
SSM_GROUPS = 2
D_STATE = 128
CONV_WIDTH = 4
CONV_DIM = D_SSM + 2 * SSM_GROUPS * D_STATE
CHUNK = 128
MLSTM_HEADS = 4
MLSTM_HEAD_DIM = 128
D_MLSTM = MLSTM_HEADS * MLSTM_HEAD_DIM
D_FF = 4 * D_MODEL

C_XBC = 0
C_Q = 1536
C_Z = 2048
C_MQ = 3072
C_MK = 3584
C_MV = 4096
C_MO = 4608
C_K = 5120
C_V = 5248
C_G = 5376
NP = 5632
GATE_LANE = 16

LANE = 128
VMEM_LIMIT = 60 * 1024 * 1024


def _cparams(sem):
    return pltpu.CompilerParams(dimension_semantics=sem, vmem_limit_bytes=VMEM_LIMIT)


def _silu(x):
    return x * jax.nn.sigmoid(x)


def _dot(a, b):
    return jnp.dot(a.astype(BF16), b.astype(BF16), preferred_element_type=F32)


def _dot_nt(a, b):
    return lax.dot_general(a.astype(BF16), b.astype(BF16), (((1,), (1,)), ((), ())),
                           preferred_element_type=F32)


def _dot_tn(a, b):
    return lax.dot_general(a.astype(BF16), b.astype(BF16), (((0,), (0,)), ((), ())),
                           preferred_element_type=F32)


def _tril(n):
    row = lax.broadcasted_iota(jnp.int32, (n, n), 0)
    col = lax.broadcasted_iota(jnp.int32, (n, n), 1)
    return row >= col


def _cumsum_rows(x):
    t = _tril(x.shape[0]).astype(BF16)
    hi = x.astype(BF16)
    r1 = x - hi.astype(F32)
    mid = r1.astype(BF16)
    lo = (r1 - mid.astype(F32)).astype(BF16)
    dot = lambda piece: jnp.dot(t, piece, preferred_element_type=F32)
    return dot(hi) + dot(mid) + dot(lo)


_SRC_Q, _SRC_K, _SRC_V, _SRC_Z, _SRC_XBC, _SRC_DT, _SRC_MQ = 0, 512, 640, 768, 1792, 3328, 3344
_SRC_MI = _SRC_MQ + 4 * D_MLSTM
IN_WIDTH = _SRC_MI + 2 * MLSTM_HEADS


def _prep_w_in_kernel(w_ref, o_ref):
    def put(dst, src, n):
        o_ref[:, dst:dst + n] = w_ref[src:src + n, :].T.astype(BF16)

    put(C_XBC, _SRC_XBC, CONV_DIM)
    put(C_Q, _SRC_Q, D_ATT)
    put(C_Z, _SRC_Z, D_SSM)
    put(C_MQ, _SRC_MQ, 4 * D_MLSTM)
    put(C_K, _SRC_K, D_KV)
    put(C_V, _SRC_V, D_KV)
    tk = w_ref.shape[1]
    dt = w_ref[_SRC_DT:_SRC_DT + SSM_HEADS, :]
    gates = w_ref[_SRC_MI:_SRC_MI + 2 * MLSTM_HEADS, :]
    pad = jnp.zeros((LANE - GATE_LANE - 2 * MLSTM_HEADS, tk), F32)
    lo = jnp.concatenate([dt, gates, pad], axis=0)
    hi = jnp.concatenate([jnp.zeros_like(dt), pltpu.roll(gates, MLSTM_HEADS, 0), pad], axis=0)
    o_ref[:, C_G:C_G + LANE] = lo.T.astype(BF16)
    o_ref[:, C_G + LANE:C_G + 2 * LANE] = hi.T.astype(BF16)


def _prep_w_in(w_in, layer, tk):
    depth, k, n = w_in.shape
    assert n == IN_WIDTH and n % 8 == 0
    wt = jnp.swapaxes(w_in, 1, 2).reshape(depth * n, k)
    return pl.pallas_call(
        _prep_w_in_kernel,
        grid=(k // tk,),
        in_specs=[pl.BlockSpec((n, tk), lambda i: (layer, i))],
        out_specs=pl.BlockSpec((tk, NP), lambda i: (i, 0)),
        out_shape=jax.ShapeDtypeStruct((k, NP), BF16),
        compiler_params=_cparams(("parallel",)),
        name="prep_w_in",
    )(wt)


def _outproj_kernel(a_ref, y_ref, h_ref, x_ref, w_ref, o_ref, wb_ref):
    @pl.when(pl.program_id(0) == 0)
    def _():
        wb_ref[...] = w_ref[...].astype(BF16)

    acc = jnp.dot(a_ref[...], wb_ref[0:D_ATT, :], preferred_element_type=F32)
    acc = acc + jnp.dot(y_ref[...], wb_ref[D_ATT:D_ATT + D_SSM, :], preferred_element_type=F32)
    acc = acc + jnp.dot(h_ref[...], wb_ref[D_ATT + D_SSM:, :], preferred_element_type=F32)
    o_ref[...] = x_ref[...] + acc


def _outproj(att, y, h, x, w, layer, tm):
    m, n = x.shape
    k = w.shape[1]
    rows = lambda width: pl.BlockSpec((tm, width), lambda i: (i, 0))
    return pl.pallas_call(
        _outproj_kernel,
        grid=(m // tm,),
        in_specs=[rows(D_ATT), rows(D_SSM), rows(D_MLSTM), rows(n),
                  pl.BlockSpec((None, k, n), lambda i: (layer, 0, 0), pipeline_mode=pl.Buffered(1))],
        out_specs=rows(n),
        out_shape=jax.ShapeDtypeStruct((m, n), F32),
        scratch_shapes=[pltpu.VMEM((k, n), BF16)],
        compiler_params=_cparams(("arbitrary",)),
        name="outproj",
    )(att, y, h, x, w)


N_MLP_IN, N_SSM_SIDE_IN, N_MLSTM_SIDE_IN = 5, 5, 6


def _mlp_kernel(*refs, final_norm, side, n_prev):
    x_ref, g_ref, wu_ref, wd_ref, gf_ref = refs[:N_MLP_IN]
    n_side = N_SSM_SIDE_IN + N_MLSTM_SIDE_IN if side else 0
    side_in = refs[N_MLP_IN:N_MLP_IN + n_side]
    outs = refs[N_MLP_IN + n_side + n_prev:]
    o_ref = outs[0]
    u_ref = outs[5] if side else outs[1]
    j = pl.program_id(1)

    @pl.when(j == 0)
    def _():
        x = x_ref[...]
        ms = jnp.mean(x * x, axis=-1, keepdims=True)
        u_ref[...] = (x * lax.rsqrt(ms + EPS) * g_ref[...]).astype(BF16)
        o_ref[...] = x
        if side:
            outs[6][...] = jnp.zeros_like(outs[6])
            outs[7][...] = jnp.zeros_like(outs[7])

    h = _dot(u_ref[...], wu_ref[...])
    h = jnp.square(jnp.maximum(h, 0.0))
    o_ref[...] += _dot(h, wd_ref[...])

    if side:
        seq = pl.program_id(0) * pl.num_programs(1) + j
        row = seq % SEQ_ROWS
        so_ref, y_ref, co_ref, qc_ref = outs[1:5]
        yacc_ref, qacc_ref = outs[6:8]
        _ssm_sample_one(0, row, seq, *side_in[:N_SSM_SIDE_IN], so_ref, y_ref, yacc_ref)
        _mlstm_sample_one(0, row, seq, *side_in[N_SSM_SIDE_IN:], co_ref, qc_ref, qacc_ref)

    if final_norm:
        @pl.when(j == pl.num_programs(1) - 1)
        def _():
            y = o_ref[...]
            ms = jnp.mean(y * y, axis=-1, keepdims=True)
            o_ref[...] = y * lax.rsqrt(ms + EPS) * gf_ref[...]


def _mlp(x, g, wu, wd, gf, layer, final_norm, tm, tf, side=None):
    m, d = x.shape
    ff = wu.shape[2]
    nj = ff // tf
    in_specs = [pl.BlockSpec((tm, d), lambda i, j: (i, 0)),
                pl.BlockSpec((1, d), lambda i, j: (0, 0)),
                pl.BlockSpec((None, d, tf), lambda i, j: (layer, 0, j)),
                pl.BlockSpec((None, tf, d), lambda i, j: (layer, j, 0)),
                pl.BlockSpec((1, d), lambda i, j: (0, 0))]
    out_specs = [pl.BlockSpec((tm, d), lambda i, j: (i, 0))]
    out_shape = [jax.ShapeDtypeStruct((m, d), F32)]
    operands = [x, g, wu, wd, gf]
    aliases = {}
    n_prev = 0
    if side is not None:
        ea, ssm, xdt, bvec, cvec, gq, esw, mem, proj_s, ks, prev_s, prev_c = side
        depth, bd = ssm.shape[0], ssm.shape[1]
        assert (m // tm) * nj == bd and nj % SEQ_ROWS == 0
        smem = pl.BlockSpec(memory_space=pltpu.SMEM)
        rows = lambda n, col=0: pl.BlockSpec((SEQ_ROWS, n), lambda i, j: ((i * nj + j) // SEQ_ROWS, col // n))
        state = lambda r, n: pl.BlockSpec((None, 1, r, n), lambda i, j: (layer, i * nj + j, 0, 0))
        in_specs += [smem, state(D_SSM, D_STATE), rows(D_SSM), rows(SSM_GROUPS * D_STATE),
                     rows(SSM_GROUPS * D_STATE),
                     smem, smem, state(D_MLSTM, MLSTM_HEAD_DIM), rows(D_MLSTM, C_MQ), rows(D_MLSTM),
                     rows(D_MLSTM, C_MV)]
        operands += [ea, ssm, xdt, bvec, cvec, gq, esw, mem, proj_s, ks, proj_s]
        out_specs += [state(D_SSM, D_STATE), rows(D_SSM), state(D_MLSTM, MLSTM_HEAD_DIM), rows(D_MLSTM)]
        out_shape += [jax.ShapeDtypeStruct((depth, bd, D_SSM, D_STATE), F32),
                      jax.ShapeDtypeStruct((bd, D_SSM), F32),
                      jax.ShapeDtypeStruct((depth, bd, D_MLSTM, MLSTM_HEAD_DIM), F32),
                      jax.ShapeDtypeStruct((bd, D_MLSTM), F32)]
        if prev_s is not None:
            aliases = {len(operands): 1, len(operands) + 1: 3}
            operands += [prev_s, prev_c]
            in_specs += [pl.BlockSpec(memory_space=pl.ANY)] * 2
            n_prev = 2
    res = pl.pallas_call(
        functools.partial(_mlp_kernel, final_norm=final_norm, side=side is not None, n_prev=n_prev),
        grid=(m // tm, nj),
        in_specs=in_specs,
        out_specs=out_specs,
        out_shape=out_shape,
        input_output_aliases=aliases,
        scratch_shapes=[pltpu.VMEM((tm, d), BF16)] + ([] if side is None else [
            pltpu.VMEM((SEQ_ROWS, D_SSM), F32), pltpu.VMEM((SEQ_ROWS, D_MLSTM), F32)]),
        compiler_params=_cparams(("parallel", "arbitrary")),
        name="mlp",
    )(*operands)
    return res[0] if side is None else res


def _rope_tables(pos):
    half = ROPE_DIM // 2
    inv = jnp.power(jnp.float32(ROPE_THETA), -jnp.arange(half, dtype=jnp.float32) / half)
    ang = pos.astype(jnp.float32)[:, None] * inv[None, :]
    cos = jnp.cos(ang)
    sin = jnp.sin(ang)
    n = pos.shape[0]
    rest = ATT_HEAD_DIM - ROPE_DIM
    c = jnp.concatenate([cos, cos, jnp.ones((n, rest), F32)], axis=1)
    s1 = jnp.concatenate([-sin, jnp.zeros((n, half + rest), F32)], axis=1)
    s2 = jnp.concatenate([jnp.zeros((n, half), F32), sin, jnp.zeros((n, rest), F32)], axis=1)
    rep = LANE // ATT_HEAD_DIM
    return jnp.tile(c, (1, rep)), jnp.tile(s1, (1, rep)), jnp.tile(s2, (1, rep))


def _rope(x, c, s1, s2):
    width = x.shape[-1]
    rep = width // LANE
    half = ROPE_DIM // 2
    if rep > 1:
        c = jnp.concatenate([c] * rep, axis=-1)
        s1 = jnp.concatenate([s1] * rep, axis=-1)
        s2 = jnp.concatenate([s2] * rep, axis=-1)
    axis = x.ndim - 1
    return x * c + pltpu.roll(x, width - half, axis) * s1 + pltpu.roll(x, half, axis) * s2


def _attn_prompt_init(ko_ref, vo_ref):
    ko_ref[...] = jnp.zeros_like(ko_ref)
    vo_ref[...] = jnp.zeros_like(vo_ref)


def _attn_prompt_body(j, sink_ref, q_ref, k_ref, v_ref, c_ref, s1_ref, s2_ref, o_ref, ko_ref, vo_ref):
    w = WINDOW
    c = c_ref[...]
    s1 = s1_ref[...]
    s2 = s2_ref[...]
    krot = _rope(k_ref[...], c, s1, s2)
    v = v_ref[...]
    qrot = _rope(q_ref[...], c, s1, s2)
    kk = jnp.concatenate([ko_ref[...], krot], axis=0).astype(BF16)
    vv = jnp.concatenate([vo_ref[...], v], axis=0).astype(BF16)

    row = lax.broadcasted_iota(jnp.int32, (w, 2 * w), 0)
    col = lax.broadcasted_iota(jnp.int32, (w, 2 * w), 1)
    first_col = jnp.where(j > 0, 0, w)
    mask = (col >= row) & (col <= row + w) & (col >= first_col)
    scale = ATT_HEAD_DIM ** -0.5

    for h in range(ATT_KV_HEADS):
        kh = kk[:, h * ATT_HEAD_DIM:(h + 1) * ATT_HEAD_DIM]
        vh = vv[:, h * ATT_HEAD_DIM:(h + 1) * ATT_HEAD_DIM]
        for g in range(ATT_GROUP):
            hg = h * ATT_GROUP + g
            qh = qrot[:, hg * ATT_HEAD_DIM:(hg + 1) * ATT_HEAD_DIM]
            s = _dot_nt(qh, kh) * scale
            s = jnp.where(mask, s, -jnp.inf)
            sink = sink_ref[hg]
            m = jnp.maximum(jnp.max(s, axis=-1, keepdims=True), sink)
            p = jnp.exp(s - m)
            denom = jnp.sum(p, axis=-1, keepdims=True) + jnp.exp(sink - m)
            o = _dot(p, vh) / denom
            o_ref[:, hg * ATT_HEAD_DIM:(hg + 1) * ATT_HEAD_DIM] = o.astype(o_ref.dtype)
            yield

    ko_ref[...] = krot
    vo_ref[...] = v


def _group_rmsnorm(y, w, groups):
    width = y.shape[-1] // groups
    outs = []
    for g in range(groups):
        yg = y[:, g * width:(g + 1) * width]
        ms = jnp.mean(yg * yg, axis=-1, keepdims=True)
        outs.append(yg * lax.rsqrt(ms + EPS) * w[:, g * width:(g + 1) * width])
    return jnp.concatenate(outs, axis=-1)


CONV_PAD = 8


def _ssd_prompt_init(s_ref, tail_ref):
    tail_ref[...] = jnp.zeros_like(tail_ref)
    s_ref[...] = jnp.zeros_like(s_ref)


def _ssd_prompt_body(xbc_ref, z_ref, g_ref, cw_ref, cb_ref, dtb_ref, alog_ref, dskip_ref, wn_ref, spread_ref,
                     y_ref, s_ref, tail_ref, ybuf_ref, xw_ref):
    q = CHUNK
    pad = CONV_PAD
    hp = SSM_HEAD_DIM
    heads_per_group = SSM_HEADS // SSM_GROUPS
    gw = heads_per_group * hp

    x = xbc_ref[...]
    cw = cw_ref[...]
    row8 = lax.broadcasted_iota(jnp.int32, (pad, CONV_DIM), 0)
    acc = None
    for j in range(CONV_WIDTH - 1):
        shift = CONV_WIDTH - 1 - j
        rolled = pltpu.roll(x, shift, 0)
        head = jnp.where(row8 < shift, tail_ref[j * pad:(j + 1) * pad, :], rolled[0:pad, :])
        tail_ref[j * pad:(j + 1) * pad, :] = rolled[0:pad, :]
        term = jnp.concatenate([head, rolled[pad:, :]], axis=0) * cw[j:j + 1, :]
        acc = term if acc is None else acc + term
    acc = acc + x * cw[CONV_WIDTH - 1:CONV_WIDTH, :]
    xc = _silu(acc + cb_ref[...])

    xs = xc[:, 0:D_SSM]
    bm = xc[:, D_SSM:D_SSM + SSM_GROUPS * D_STATE]
    cm = xc[:, D_SSM + SSM_GROUPS * D_STATE:]

    dt = jax.nn.softplus(g_ref[:, 0:LANE] + dtb_ref[...])
    a_neg = -jnp.exp(alog_ref[...])
    a_col = _cumsum_rows(dt * a_neg)
    a_row = a_col.T
    dt_row = dt.T
    wk_row = jnp.exp(a_row[:, q - 1:q] - a_row) * dt_row
    tril = _tril(q)
    xs_t = xs.T
    ea_col = jnp.exp(a_col)
    ea_hi = ea_col.astype(BF16)
    ea_lo = (ea_col - ea_hi.astype(F32)).astype(BF16)
    spread = spread_ref[...]
    ea_full = (jnp.dot(ea_hi, spread, preferred_element_type=F32)
               + jnp.dot(ea_lo, spread, preferred_element_type=F32))
    yield

    cs_parts = []
    for g in range(SSM_GROUPS):
        bg = bm[:, g * D_STATE:(g + 1) * D_STATE]
        cg = cm[:, g * D_STATE:(g + 1) * D_STATE]
        cb = _dot_nt(cg, bg)
        cs_parts.append(_dot_nt(cg, s_ref[g * gw:(g + 1) * gw, :]))
        for r in range(heads_per_group):
            h = g * heads_per_group + r
            seg = a_col[:, h:h + 1] - a_row[h:h + 1, :]
            wmat = jnp.exp(jnp.where(tril, seg, -jnp.inf)) * cb * dt_row[h:h + 1, :]
            ybuf_ref[:, h * hp:(h + 1) * hp] = _dot(wmat, xs[:, h * hp:(h + 1) * hp])
            xw_ref[r * hp:(r + 1) * hp, :] = xs_t[h * hp:(h + 1) * hp, :] * wk_row[h:h + 1, :]
            yield
        upd = _dot(xw_ref[...], bg)
        for r in range(heads_per_group):
            h = g * heads_per_group + r
            decay = jnp.exp(a_row[h:h + 1, q - 1:q])
            s_ref[h * hp:(h + 1) * hp, :] = s_ref[h * hp:(h + 1) * hp, :] * decay + upd[r * hp:(r + 1) * hp, :]
        yield

    y = ybuf_ref[...] + jnp.concatenate(cs_parts, axis=1) * ea_full
    y = y + dskip_ref[...] * xs
    y = y * _silu(z_ref[...])
    y_ref[...] = _group_rmsnorm(y, wn_ref[...], SSM_GROUPS).astype(y_ref.dtype)


def _mlstm_prompt_init(c_ref, n_ref, m_ref):
    c_ref[...] = jnp.zeros_like(c_ref)
    n_ref[...] = jnp.zeros_like(n_ref)
    m_ref[...] = jnp.zeros_like(m_ref)


def _mlstm_prompt_body(q_ref, k_ref, v_ref, o_ref, g_ref, bi_ref, bf_ref, wn_ref, h_ref, c_ref, n_ref, m_ref):
    t = CHUNK
    d = MLSTM_HEAD_DIM

    gi = g_ref[:, 0:LANE] + bi_ref[...]
    gf = g_ref[:, LANE:2 * LANE] + bf_ref[...]
    b_col = _cumsum_rows(jax.nn.log_sigmoid(gf))
    b_row = b_col.T
    i_row = gi.T
    tril = _tril(t)
    k_all = k_ref[...] * (d ** -0.5)

    for h in range(MLSTM_HEADS):
        ln = GATE_LANE + h
        bq = b_col[:, ln:ln + 1]
        logw = jnp.where(tril, bq - b_row[ln:ln + 1, :] + i_row[ln:ln + 1, :], -jnp.inf)
        m_prev = m_ref[h:h + 1, 0:1]
        log_inter = bq + m_prev
        mt = jnp.maximum(log_inter, jnp.max(logw, axis=-1, keepdims=True))
        qh = q_ref[:, h * d:(h + 1) * d]
        kh = k_all[:, h * d:(h + 1) * d]
        vh = v_ref[:, h * d:(h + 1) * d]
        ch = c_ref[h * d:(h + 1) * d, :]
        nh = n_ref[h:h + 1, :]
        sw = jnp.exp(logw - mt) * _dot_nt(qh, kh)
        gq = jnp.exp(log_inter - mt)
        num = _dot(sw, vh) + _dot(qh, ch) * gq
        den = jnp.sum(sw, axis=-1, keepdims=True) + jnp.sum(qh * nh, axis=-1, keepdims=True) * gq
        hh = num / jnp.maximum(jnp.abs(den), jnp.exp(-mt))

        m_new = mt[t - 1:t, :]
        b_last = b_col[t - 1:t, ln:ln + 1]
        wk = jnp.exp(b_last - bq + gi[:, ln:ln + 1] - m_new)
        g_end = jnp.exp(b_last + m_prev - m_new)
        kw = kh * wk
        c_ref[h * d:(h + 1) * d, :] = ch * g_end + _dot_tn(kw, vh)
        n_ref[h:h + 1, :] = nh * g_end + jnp.sum(kw, axis=0, keepdims=True)
        m_ref[h:h + 1, :] = jnp.broadcast_to(m_new, (1, LANE))

        ms = jnp.mean(hh * hh, axis=-1, keepdims=True)
        hn = hh * lax.rsqrt(ms + EPS) * wn_ref[:, h * d:(h + 1) * d]
        h_ref[:, h * d:(h + 1) * d] = (hn * jax.nn.sigmoid(o_ref[:, h * d:(h + 1) * d])).astype(h_ref.dtype)
        yield

def _interleave(stages):
    live = [[gen, 0, n] for gen, n in stages]
    while live:
        item = min(live, key=lambda s: s[1] / s[2])
        try:
            next(item[0])
            item[1] += 1
        except StopIteration:
            live.remove(item)


PROJ_TILE = 256


def _project_next(x_ref, g_ref, w_ref, pnext_ref):
    x = x_ref[...]
    u = (x * lax.rsqrt(jnp.mean(x * x, axis=-1, keepdims=True) + EPS) * g_ref[...]).astype(BF16)
    for ct in range(NP // PROJ_TILE):
        cols = slice(ct * PROJ_TILE, (ct + 1) * PROJ_TILE)
        pnext_ref[:, cols] = jnp.dot(u, w_ref[:, cols], preferred_element_type=F32)
        yield


def _front_kernel(sink_ref, x_ref, x0_ref, g_ref, w_ref, c_ref, s1_ref, s2_ref,
                  cw_ref, cb_ref, dtb_ref, alog_ref, dskip_ref, wns_ref, spread_ref, bi_ref, bf_ref, wnm_ref,
                  att_ref, ko_ref, vo_ref, y_ref, s_ref, h_ref, mem_ref, n_ref, m_ref, ptail_ref,
                  pcur_ref, pnext_ref, tail_ref, ybuf_ref, xw_ref, *, nblk):
    j = pl.program_id(1)
    first = (pl.program_id(0) == 0) & (j == 0)
    t = CHUNK

    @pl.when(first)
    def _():
        for _ in _project_next(x0_ref, g_ref, w_ref, pcur_ref):
            pass

    @pl.when(jnp.logical_not(first))
    def _():
        pcur_ref[...] = pnext_ref[...]

    @pl.when(j == 0)
    def _():
        _attn_prompt_init(ko_ref, vo_ref)
        _ssd_prompt_init(s_ref, tail_ref)
        _mlstm_prompt_init(mem_ref, n_ref, m_ref)

    def attn(i):
        seg = lambda col, width: pcur_ref.at[i * t:(i + 1) * t, col:col + width]
        rows = lambda ref: ref.at[i * t:(i + 1) * t, :]
        return _attn_prompt_body(j * nblk + i, sink_ref, seg(C_Q, D_ATT), seg(C_K, D_KV), seg(C_V, D_KV),
                                 rows(c_ref), rows(s1_ref), rows(s2_ref), rows(att_ref), ko_ref, vo_ref)

    def ssd(i):
        seg = lambda col, width: pcur_ref.at[i * t:(i + 1) * t, col:col + width]
        return _ssd_prompt_body(seg(C_XBC, CONV_DIM), seg(C_Z, D_SSM), seg(C_G, 2 * LANE), cw_ref, cb_ref, dtb_ref,
                                alog_ref, dskip_ref, wns_ref, spread_ref, y_ref.at[i * t:(i + 1) * t, :], s_ref,
                                tail_ref, ybuf_ref, xw_ref)

    def mlstm(i):
        seg = lambda col, width: pcur_ref.at[i * t:(i + 1) * t, col:col + width]
        return _mlstm_prompt_body(seg(C_MQ, D_MLSTM), seg(C_MK, D_MLSTM), seg(C_MV, D_MLSTM), seg(C_MO, D_MLSTM),
                                  seg(C_G, 2 * LANE), bi_ref, bf_ref, wnm_ref, h_ref.at[i * t:(i + 1) * t, :],
                                  mem_ref, n_ref, m_ref)

    def blocks(body):
        for i in range(nblk):
            yield from body(i)

    ptail_ref[...] = pcur_ref[nblk * t - CONV_PAD:nblk * t, C_XBC:C_XBC + CONV_DIM]
    _interleave([
        (_project_next(x_ref, g_ref, w_ref, pnext_ref), NP // PROJ_TILE + 1),
        (blocks(attn), nblk * (ATT_HEADS + 1)),
        (blocks(ssd), nblk * (SSM_HEADS + SSM_GROUPS + 2)),
        (blocks(mlstm), nblk * (MLSTM_HEADS + 1)),
    ])


def _front(x, g_mix, w, sinks, tables, cw, cb, dtb, alog, dskip_rep, wns, bi, bf, wnm, bsz, seq):
    nblk = 2 if (seq // CHUNK) % 2 == 0 else 1
    t = nblk * CHUNK
    nb = seq // t
    total = bsz * nb
    c, s1, s2 = tables
    head_of_lane = jnp.arange(D_SSM, dtype=jnp.int32) // SSM_HEAD_DIM
    spread = (jnp.arange(LANE, dtype=jnp.int32)[:, None] == head_of_lane[None, :]).astype(BF16)
    row = lambda b, j: b * nb + j
    full = lambda shape: pl.BlockSpec(shape, lambda b, j: (0,) * len(shape))
    vec = lambda n: full((1, n))
    tab = pl.BlockSpec((t, LANE), lambda b, j: (j, 0))
    rows_out = lambda width: pl.BlockSpec((t, width), lambda b, j: (row(b, j), 0))
    per_seq = lambda r, n: pl.BlockSpec((None, r, n), lambda b, j: (b, 0, 0))
    return pl.pallas_call(
        functools.partial(_front_kernel, nblk=nblk),
        grid=(bsz, nb),
        in_specs=[pl.BlockSpec(memory_space=pltpu.SMEM),
                  pl.BlockSpec((t, D_MODEL), lambda b, j: (jnp.minimum(row(b, j) + 1, total - 1), 0)),
                  pl.BlockSpec((t, D_MODEL), lambda b, j: (0, 0), pipeline_mode=pl.Buffered(1)),
                  vec(D_MODEL),
                  pl.BlockSpec((D_MODEL, NP), lambda b, j: (0, 0), pipeline_mode=pl.Buffered(1)),
                  tab, tab, tab,
                  full((CONV_WIDTH, CONV_DIM)), vec(CONV_DIM), vec(LANE), vec(LANE), vec(D_SSM), vec(D_SSM),
                  full((LANE, D_SSM)),
                  vec(LANE), vec(LANE), vec(D_MLSTM)],
        out_specs=[rows_out(D_ATT), per_seq(CHUNK, D_KV), per_seq(CHUNK, D_KV),
                   rows_out(D_SSM), per_seq(D_SSM, D_STATE),
                   rows_out(D_MLSTM), per_seq(D_MLSTM, MLSTM_HEAD_DIM), per_seq(8, LANE), per_seq(8, LANE),
                   per_seq(CONV_PAD, CONV_DIM)],
        out_shape=[jax.ShapeDtypeStruct((bsz * seq, D_ATT), BF16),
                   jax.ShapeDtypeStruct((bsz, CHUNK, D_KV), F32),
                   jax.ShapeDtypeStruct((bsz, CHUNK, D_KV), F32),
                   jax.ShapeDtypeStruct((bsz * seq, D_SSM), BF16),
                   jax.ShapeDtypeStruct((bsz, D_SSM, D_STATE), F32),
                   jax.ShapeDtypeStruct((bsz * seq, D_MLSTM), BF16),
                   jax.ShapeDtypeStruct((bsz, D_MLSTM, MLSTM_HEAD_DIM), F32),
                   jax.ShapeDtypeStruct((bsz, 8, LANE), F32),
                   jax.ShapeDtypeStruct((bsz, 8, LANE), F32),
                   jax.ShapeDtypeStruct((bsz, CONV_PAD, CONV_DIM), F32)],
        scratch_shapes=[pltpu.VMEM((t, NP), F32),
                        pltpu.VMEM((t, NP), F32),
                        pltpu.VMEM(((CONV_WIDTH - 1) * CONV_PAD, CONV_DIM), F32),
                        pltpu.VMEM((CHUNK, D_SSM), F32),
                        pltpu.VMEM((D_SSM // SSM_GROUPS, CHUNK), F32)],
        compiler_params=_cparams(("arbitrary", "arbitrary")),
        name="front",
    )(sinks, x, x, g_mix, w, c, s1, s2, cw, cb, dtb, alog, dskip_rep, wns, spread, bi, bf, wnm)


def _stacked_out(prev, depth, shape, block, layer):
    spec = pl.BlockSpec((None,) + block, lambda i: (layer,) + (i,) + (0,) * (len(block) - 1))
    out_shape = jax.ShapeDtypeStruct((depth,) + shape, F32)
    extra_in = [] if prev is None else [prev]
    extra_spec = [] if prev is None else [pl.BlockSpec(memory_space=pl.ANY)]
    return spec, out_shape, extra_in, extra_spec


def _drop_aliased(kernel_fn, n_in, n_prev):
    def wrapped(*refs):
        return kernel_fn(*refs[:n_in], *refs[n_in + n_prev:])
    return wrapped


def _attn_sample_kernel(q_ref, k_ref, v_ref, kc_ref, vc_ref, sink_ref, c_ref, s1_ref, s2_ref,
                        o_ref, ko_ref, vo_ref):
    tb, _, w = kc_ref.shape
    c = c_ref[...]
    s1 = s1_ref[...]
    s2 = s2_ref[...]
    qb = _rope(q_ref[...], c, s1, s2)
    kn = _rope(k_ref[...], c, s1, s2)
    vn = v_ref[...]
    kc = kc_ref[...]
    vc = vc_ref[...]
    newest = lax.broadcasted_iota(jnp.int32, (D_KV, w), 1) == w - 1
    for b in range(tb):
        ko_ref[b] = jnp.where(newest, _column_tile(kn[b]), pltpu.roll(kc[b], w - 1, 1))
        vo_ref[b] = jnp.where(newest, _column_tile(vn[b]), pltpu.roll(vc[b], w - 1, 1))

    scale = ATT_HEAD_DIM ** -0.5
    s = jnp.einsum("bhd,bdk->bhk", qb.astype(BF16), kc.astype(BF16),
                   preferred_element_type=F32) * scale
    sn = jnp.sum(qb * kn, axis=-1, keepdims=True) * scale
    sink = sink_ref[...][None, :, 0:1]
    m = jnp.maximum(jnp.maximum(jnp.max(s, axis=-1, keepdims=True), sn), sink)
    p = jnp.exp(s - m)
    pn = jnp.exp(sn - m)
    denom = jnp.sum(p, axis=-1, keepdims=True) + pn + jnp.exp(sink - m)
    p = p / denom
    pn = pn / denom
    o = jnp.einsum("bhk,bdk->bhd", p.astype(BF16), vc.astype(BF16), preferred_element_type=F32)
    o = o + pn * vn
    head = lax.broadcasted_iota(jnp.int32, o.shape[:2] + (ATT_HEAD_DIM,), 1)
    o_ref[...] = jnp.where(head < ATT_GROUP, o[:, :, 0:ATT_HEAD_DIM], o[:, :, ATT_HEAD_DIM:]).astype(o_ref.dtype)


def _attn_sample(qb, kn, vn, kc, vc, sink_rows, tables, layer, prev, tb):
    depth, bd, _, w = kc.shape
    c, s1, s2 = tables
    blk3 = lambda r, n: pl.BlockSpec((tb, r, n), lambda i: (i, 0, 0))
    cache = pl.BlockSpec((None, tb, D_KV, w), lambda i: (layer, i, 0, 0))
    vec = pl.BlockSpec((1, LANE), lambda i: (0, 0))
    prev_k, prev_v = (None, None) if prev is None else prev
    k_spec, k_shape, k_in, k_in_spec = _stacked_out(prev_k, depth, (bd, D_KV, w), (tb, D_KV, w), layer)
    v_spec, v_shape, v_in, v_in_spec = _stacked_out(prev_v, depth, (bd, D_KV, w), (tb, D_KV, w), layer)
    n_in = 9
    n_prev = len(k_in) + len(v_in)
    return pl.pallas_call(
        _drop_aliased(_attn_sample_kernel, n_in, n_prev),
        grid=(bd // tb,),
        in_specs=[blk3(ATT_HEADS, LANE), blk3(1, LANE), blk3(1, LANE), cache, cache,
                  pl.BlockSpec((ATT_HEADS, LANE), lambda i: (0, 0)), vec, vec, vec] + k_in_spec + v_in_spec,
        out_specs=[blk3(ATT_HEADS, ATT_HEAD_DIM), k_spec, v_spec],
        out_shape=[jax.ShapeDtypeStruct((bd, ATT_HEADS, ATT_HEAD_DIM), BF16), k_shape, v_shape],
        input_output_aliases={n_in + t: 1 + t for t in range(n_prev)},
        compiler_params=_cparams(("parallel",)),
        name="attn_sample",
    )(qb, kn, vn, kc, vc, sink_rows, c, s1, s2, *k_in, *v_in)


def _lane_place(cols, lane0):
    m = cols[0].shape[0]
    lane = lax.broadcasted_iota(jnp.int32, (m, LANE), 1)
    out = jnp.zeros((m, LANE), F32)
    for i, col in enumerate(cols):
        out = jnp.where(lane == lane0 + i, col, out)
    return out


def _sample_pre_kernel(x_ref, gmix_ref, w_ref, cs_ref, cw_ref, cb_ref, dtb_ref, alog_ref, bi_ref, bf_ref, n_ref,
                       m_ref, proj_ref, conv_ref, xs_ref, xdt_ref, bm_ref, cm_ref, ea_ref,
                       ks_ref, nn_ref, g_ref, esw_ref, sw_ref, den_ref, mt_ref):
    for _ in _project_next(x_ref, gmix_ref, w_ref, proj_ref):
        pass
    xbc = proj_ref[:, C_XBC:C_XBC + CONV_DIM]
    s0 = cs_ref[0]
    s1 = cs_ref[1]
    s2 = cs_ref[2]
    cw = cw_ref[...]
    acc = s0 * cw[0:1, :]
    acc = acc + s1 * cw[1:2, :]
    acc = acc + s2 * cw[2:3, :]
    acc = acc + xbc * cw[3:4, :]
    xc = _silu(acc + cb_ref[...])
    conv_ref[0] = s1
    conv_ref[1] = s2
    conv_ref[2] = xbc

    xs = xc[:, 0:D_SSM]
    xs_ref[...] = xs
    bm_ref[...] = xc[:, D_SSM:D_SSM + SSM_GROUPS * D_STATE]
    cm_ref[...] = xc[:, D_SSM + SSM_GROUPS * D_STATE:]
    dt = jax.nn.softplus(proj_ref[:, C_G:C_G + LANE] + dtb_ref[...])
    ea_ref[...] = jnp.exp(dt * (-jnp.exp(alog_ref[...])))
    hp = SSM_HEAD_DIM
    for h in range(SSM_HEADS):
        xdt_ref[:, h * hp:(h + 1) * hp] = xs[:, h * hp:(h + 1) * hp] * dt[:, h:h + 1]

    d = MLSTM_HEAD_DIM
    gi = proj_ref[:, C_G:C_G + LANE] + bi_ref[...]
    gf = proj_ref[:, C_G + LANE:C_G + 2 * LANE] + bf_ref[...]
    log_inter = jax.nn.log_sigmoid(gf) + m_ref[...]
    mt = jnp.maximum(log_inter, gi)
    gq = jnp.exp(log_inter - mt)
    esw = jnp.exp(gi - mt)
    ks = proj_ref[:, C_MK:C_MK + D_MLSTM] * (d ** -0.5)
    ks_ref[...] = ks
    qk_cols = []
    qn_cols = []
    for h in range(MLSTM_HEADS):
        qh = proj_ref[:, C_MQ + h * d:C_MQ + (h + 1) * d]
        kh = ks[:, h * d:(h + 1) * d]
        nh = n_ref[:, h * d:(h + 1) * d]
        qk_cols.append(jnp.sum(qh * kh, axis=-1, keepdims=True))
        qn_cols.append(jnp.sum(qh * nh, axis=-1, keepdims=True))
        ln = GATE_LANE + h
        nn_ref[:, h * d:(h + 1) * d] = nh * gq[:, ln:ln + 1] + kh * esw[:, ln:ln + 1]
    sw = esw * _lane_place(qk_cols, GATE_LANE)
    den = sw + _lane_place(qn_cols, GATE_LANE) * gq
    g_ref[...] = gq
    esw_ref[...] = esw
    sw_ref[...] = sw
    den_ref[...] = jnp.maximum(jnp.abs(den), jnp.exp(-mt))
    mt_ref[...] = mt


def _sample_pre(x, g_mix, w, conv_state, layer, cw, cb, dtb, alog, bi, bf, n_state, m_tile):
    bd = x.shape[0]
    taps = CONV_WIDTH - 1
    tile = jax.ShapeDtypeStruct((bd, LANE), F32)
    whole = lambda a: pl.BlockSpec(a.shape, lambda i: (0,) * a.ndim)
    in_specs = [whole(x), whole(g_mix), pl.BlockSpec(w.shape, lambda i: (0, 0), pipeline_mode=pl.Buffered(1)),
                pl.BlockSpec((None, taps, bd, CONV_DIM), lambda i: (layer, 0, 0, 0))]
    in_specs += [whole(a) for a in (cw, cb, dtb, alog, bi, bf, n_state, m_tile)]
    outs = [jax.ShapeDtypeStruct((bd, NP), F32),
            jax.ShapeDtypeStruct((taps, bd, CONV_DIM), F32),
            jax.ShapeDtypeStruct((bd, D_SSM), F32),
            jax.ShapeDtypeStruct((bd, D_SSM), F32),
            jax.ShapeDtypeStruct((bd, SSM_GROUPS * D_STATE), F32),
            jax.ShapeDtypeStruct((bd, SSM_GROUPS * D_STATE), F32),
            tile,
            jax.ShapeDtypeStruct((bd, D_MLSTM), F32),
            jax.ShapeDtypeStruct((bd, D_MLSTM), F32),
            tile, tile, tile, tile, tile]
    return pl.pallas_call(
        _sample_pre_kernel,
        grid=(1,),
        in_specs=in_specs,
        out_specs=[pl.BlockSpec(o.shape, lambda i, nd=len(o.shape): (0,) * nd) for o in outs],
        out_shape=outs,
        compiler_params=_cparams(("arbitrary",)),
        name="sample_pre",
    )(x, g_mix, w, conv_state, cw, cb, dtb, alog, bi, bf, n_state, m_tile)


def _column_tile(row):
    return jnp.broadcast_to(row, (LANE, LANE)).T


SEQ_ROWS = 8


def _put_row(out_ref, acc_ref, vr, col, row):
    width = row.shape[1]
    rid = lax.broadcasted_iota(jnp.int32, (SEQ_ROWS, width), 0)
    blk = jnp.where(rid == vr, row, acc_ref[:, col:col + width])
    acc_ref[:, col:col + width] = blk
    out_ref[:, col:col + width] = blk


def _ssm_sample_one(b, vr, gb, ea_ref, s_ref, xdt_ref, b_ref, c_ref, so_ref, y_ref, yacc_ref):
    hp = SSM_HEAD_DIM
    heads_per_tile = LANE // hp
    n_tiles = D_SSM // LANE
    tiles_per_group = n_tiles // SSM_GROUPS
    top = lax.broadcasted_iota(jnp.int32, (LANE, 1), 0) < hp
    xrow = xdt_ref[pl.ds(vr, 1), :]
    brow = b_ref[pl.ds(vr, 1), :]
    crow = c_ref[pl.ds(vr, 1), :]
    for t in range(n_tiles):
        g = t // tiles_per_group
        xcol = _column_tile(xrow[:, t * LANE:(t + 1) * LANE])
        bg = brow[:, g * D_STATE:(g + 1) * D_STATE]
        cg = jnp.broadcast_to(crow[:, g * D_STATE:(g + 1) * D_STATE], (8, D_STATE))
        e0 = ea_ref[gb, heads_per_tile * t]
        e1 = ea_ref[gb, heads_per_tile * t + 1]
        decay = jnp.where(top, e0, e1)
        sn = s_ref[b, t * LANE:(t + 1) * LANE, :] * decay + xcol * bg
        so_ref[b, t * LANE:(t + 1) * LANE, :] = sn
        _put_row(y_ref, yacc_ref, vr, t * LANE, _dot_nt(cg, sn)[0:1, :])


def _mlstm_sample_one(b, vr, gb, g_ref, esw_ref, c_ref, q_ref, k_ref, v_ref, co_ref, qc_ref, qacc_ref):
    d = MLSTM_HEAD_DIM
    qrow = q_ref[pl.ds(vr, 1), :]
    krow = k_ref[pl.ds(vr, 1), :]
    vrow = v_ref[pl.ds(vr, 1), :]
    for h in range(MLSTM_HEADS):
        qcol = _column_tile(qrow[:, h * d:(h + 1) * d])
        kcol = _column_tile(krow[:, h * d:(h + 1) * d])
        ch = c_ref[b, h * d:(h + 1) * d, :]
        g_end = g_ref[gb, GATE_LANE + h]
        wk = esw_ref[gb, GATE_LANE + h]
        _put_row(qc_ref, qacc_ref, vr, h * d, jnp.sum(qcol * ch, axis=0, keepdims=True))
        co_ref[b, h * d:(h + 1) * d, :] = ch * g_end + (kcol * wk) * vrow[:, h * d:(h + 1) * d]


def _sample_post_kernel(proj_ref, y_ref, xs_ref, dskip_ref, wns_ref, qc_ref, g_ref, sw_ref, den_ref, wnm_ref,
                        ys_ref, hs_ref):
    y = y_ref[...] + dskip_ref[...] * xs_ref[...]
    y = y * _silu(proj_ref[:, C_Z:C_Z + D_SSM])
    ys_ref[...] = _group_rmsnorm(y, wns_ref[...], SSM_GROUPS).astype(ys_ref.dtype)

    d = MLSTM_HEAD_DIM
    for h in range(MLSTM_HEADS):
        ln = GATE_LANE + h
        vh = proj_ref[:, C_MV + h * d:C_MV + (h + 1) * d]
        num = sw_ref[:, ln:ln + 1] * vh + qc_ref[:, h * d:(h + 1) * d] * g_ref[:, ln:ln + 1]
        hh = num / den_ref[:, ln:ln + 1]
        ms = jnp.mean(hh * hh, axis=-1, keepdims=True)
        hn = hh * lax.rsqrt(ms + EPS) * wnm_ref[:, h * d:(h + 1) * d]
        gate = jax.nn.sigmoid(proj_ref[:, C_MO + h * d:C_MO + (h + 1) * d])
        hs_ref[:, h * d:(h + 1) * d] = (hn * gate).astype(hs_ref.dtype)


def _sample_post(proj, y, xs, dskip_rep, wns, qc, gq, sw, den, wnm):
    bd = proj.shape[0]
    return pl.pallas_call(
        _sample_post_kernel,
        out_shape=[jax.ShapeDtypeStruct((bd, D_SSM), BF16), jax.ShapeDtypeStruct((bd, D_MLSTM), BF16)],
        compiler_params=pltpu.CompilerParams(vmem_limit_bytes=VMEM_LIMIT),
        name="sample_post",
    )(proj, y, xs, dskip_rep, wns, qc, gq, sw, den, wnm)


def _lane_vec(v, lane0):
    return jnp.pad(v.astype(F32), (lane0, LANE - lane0 - v.shape[0]))[None, :]


def _pick(n, candidates):
    for c in candidates:
        if n % c == 0:
            return c
    return n


def kernel(x_prompt, x_sample, cache_swa_k, cache_swa_v, state_conv, state_ssm, state_mlstm_C, state_mlstm_n,
           state_mlstm_m, w_norm_mix, w_in, attn_sinks, conv_w, conv_b, dt_bias, a_log, d_skip, w_norm_ssm,
           igate_b, fgate_b, w_norm_mlstm, w_out, w_norm_mlp, w_up, w_down, w_norm_final):
    bsz, seq, d_model = x_prompt.shape
    bd = x_sample.shape[0]
    depth = w_in.shape[0]
    win = cache_swa_k.shape[2]
    assert x_sample.shape[1] == 1 and seq % CHUNK == 0 and d_model == D_MODEL

    mp = bsz * seq
    tm_mlp = _pick(mp, (1024, 512, 256, 128))
    tb = _pick(bd, (8,))

    hp = x_prompt.reshape(mp, d_model)
    hs = x_sample.reshape(bd, d_model)
    tab_p = _rope_tables(jnp.arange(seq, dtype=jnp.int32))
    tab_s = _rope_tables(jnp.arange(1, dtype=jnp.int32) + PAST_LEN)
    gf = w_norm_final[None, :]
    kc_all = jnp.transpose(cache_swa_k, (0, 1, 3, 4, 2)).reshape(depth, bd, D_KV, win)
    vc_all = jnp.transpose(cache_swa_v, (0, 1, 3, 4, 2)).reshape(depth, bd, D_KV, win)
    ssm_all = state_ssm.reshape(depth, bd, D_SSM, D_STATE)
    mem_all = state_mlstm_C.reshape(depth, bd, D_MLSTM, MLSTM_HEAD_DIM)
    conv_all = jnp.swapaxes(state_conv, 1, 2)

    st_p, st_s = [], []
    kv_new = s_new = c_new = None
    for l in range(depth):
        w_in_l = _prep_w_in(w_in, l, 256)
        g_mix = w_norm_mix[l][None, :]
        g_mlp = w_norm_mlp[l][None, :]
        cw = conv_w[l]
        cb = conv_b[l][None, :]
        dtb = _lane_vec(dt_bias[l], 0)
        alog = _lane_vec(a_log[l], 0)
        dskip_rep = jnp.repeat(d_skip[l].astype(F32), SSM_HEAD_DIM)[None, :]
        wns = w_norm_ssm[l][None, :]
        bi = _lane_vec(igate_b[l], GATE_LANE)
        bf = _lane_vec(fgate_b[l], GATE_LANE)
        wnm = w_norm_mlstm[l][None, :]
        sinks = attn_sinks[l].reshape(ATT_HEADS).astype(F32)
        last = l == depth - 1

        att, pk, pv, y, p_ssm, hm, p_c, p_n, p_m, p_tail = _front(
            hp, g_mix, w_in_l, sinks, tab_p, cw, cb, dtb, alog, dskip_rep, wns, bi, bf, wnm, bsz, seq)
        x1 = _outproj(att, y, hm, hp, w_out, l, _pick(mp, (512, 256, 128)))
        p_conv = p_tail[:, CONV_PAD - (CONV_WIDTH - 1):, :]
        st_p.append((pk.reshape(bsz, WINDOW, ATT_KV_HEADS, ATT_HEAD_DIM),
                     pv.reshape(bsz, WINDOW, ATT_KV_HEADS, ATT_HEAD_DIM),
                     p_conv,
                     p_ssm.reshape(bsz, SSM_HEADS, SSM_HEAD_DIM, D_STATE),
                     p_c.reshape(bsz, MLSTM_HEADS, MLSTM_HEAD_DIM, MLSTM_HEAD_DIM),
                     p_n[:, 0:MLSTM_HEADS, :],
                     p_m[:, 0:MLSTM_HEADS, 0]))

        m_tile = jnp.pad(state_mlstm_m[l], ((0, 0), (GATE_LANE, LANE - GATE_LANE - MLSTM_HEADS)))
        (proj_s, conv_new, xs, xdt, bm, cm, ea, ks, n_new, gq, esw, sw, den, mt) = _sample_pre(
            hs, g_mix, w_in_l, conv_all, l, cw, cb, dtb, alog, bi, bf, state_mlstm_n[l].reshape(bd, D_MLSTM),
            m_tile)
        q8 = proj_s[:, C_Q:C_Q + D_ATT].reshape(bd, ATT_HEADS, ATT_HEAD_DIM)
        zero = jnp.zeros_like(q8)
        first_kv = (jnp.arange(ATT_HEADS) < ATT_GROUP)[None, :, None]
        qb = jnp.where(first_kv, jnp.concatenate([q8, zero], -1), jnp.concatenate([zero, q8], -1))
        kn = proj_s[:, C_K:C_K + D_KV].reshape(bd, 1, D_KV)
        vn = proj_s[:, C_V:C_V + D_KV].reshape(bd, 1, D_KV)
        sink_rows = jnp.broadcast_to(sinks[:, None], (ATT_HEADS, LANE))
        att_s, sk, sv = _attn_sample(qb, kn, vn, kc_all, vc_all, sink_rows, tab_s, l, kv_new, tb)
        kv_new = (sk, sv)
        side = (ea, ssm_all, xdt, bm, cm, gq, esw, mem_all, proj_s, ks, s_new, c_new)
        hp, s_new, y_read, c_new, qc = _mlp(x1, g_mlp, w_up, w_down, gf, l, last, tm_mlp, 512, side)
        ys, hms = _sample_post(proj_s, y_read, xs, dskip_rep, wns, qc, gq, sw, den, wnm)
        x1s = _outproj(att_s.reshape(bd, D_ATT), ys, hms, hs, w_out, l, bd)
        hs = _mlp(x1s, g_mlp, w_up, w_down, gf, l, last, bd, 512)
        st_s.append((conv_new,
                     n_new.reshape(bd, MLSTM_HEADS, MLSTM_HEAD_DIM),
                     mt[:, GATE_LANE:GATE_LANE + MLSTM_HEADS]))

    y_prompt = hp.reshape(bsz, seq, d_model)
    y_sample = hs.reshape(bd, 1, d_model)
    p_out = [jnp.stack([s[i] for s in st_p]) for i in range(7)]
    s_conv, s_n, s_m = [jnp.stack([s[i] for s in st_s]) for i in range(3)]
    s_conv = jnp.swapaxes(s_conv, 1, 2)
    to_cache = lambda a: jnp.transpose(a.reshape(depth, bd, ATT_KV_HEADS, ATT_HEAD_DIM, win), (0, 1, 4, 2, 3))
    s_k = to_cache(kv_new[0])
    s_v = to_cache(kv_new[1])
    s_ssm = s_new.reshape(depth, bd, SSM_HEADS, SSM_HEAD_DIM, D_STATE)
    s_c = c_new.reshape(depth, bd, MLSTM_HEADS, MLSTM_HEAD_DIM, MLSTM_HEAD_DIM)
    return (y_prompt, y_sample, *p_out, s_k, s_v, s_conv, s_ssm, s_c, s_n, s_m)
```

```python
import functools

import jax
import jax.numpy as jnp
from jax import lax
from jax.experimental import pallas as pl
from jax.experimental.pallas import tpu as pltpu

F32 = jnp.float32
BF16 = jnp.bfloat16

D_MODEL = 2048
EPS = 1e-6
PAST_LEN = 8192
ATT_HEAD_DIM = 64
ATT_HEADS = 8
ATT_KV_HEADS = 2
ATT_GROUP = ATT_HEADS // ATT_KV_HEADS
D_ATT = ATT_HEADS * ATT_HEAD_DIM
D_KV = ATT_KV_HEADS * ATT_HEAD_DIM
WINDOW = 128
ROPE_THETA = 500000.0
ROPE_DIM = ATT_HEAD_DIM // 4
SSM_HEAD_DIM = 64
SSM_HEADS = 16
D_SSM = SSM_HEADS * SSM_HEAD_DIM
SSM_GROUPS = 2
D_STATE = 128
CONV_WIDTH = 4
CONV_DIM = D_SSM + 2 * SSM_GROUPS * D_STATE
CHUNK = 128
MLSTM_HEADS = 4
MLSTM_HEAD_DIM = 128
D_MLSTM = MLSTM_HEADS * MLSTM_HEAD_DIM
D_FF = 4 * D_MODEL

C_XBC = 0
C_Q = 1536
C_Z = 2048
C_MQ = 3072
C_MK = 3584
C_MV = 4096
C_MO = 4608
C_K = 5120
C_V = 5248
C_G = 5376
NP = 5632
GATE_LANE = 16

LANE = 128
VMEM_LIMIT = 60 * 1024 * 1024


def _cparams(sem):
    return pltpu.CompilerParams(dimension_semantics=sem, vmem_limit_bytes=VMEM_LIMIT)


def _silu(x):
    return x * jax.nn.sigmoid(x)


def _dot(a, b):
    return jnp.dot(a.astype(BF16), b.astype(BF16), preferred_element_type=F32)


def _dot_nt(a, b):
    return lax.dot_general(a.astype(BF16), b.astype(BF16), (((1,), (1,)), ((), ())),
                           preferred_element_type=F32)


def _dot_tn(a, b):
    return lax.dot_general(a.astype(BF16), b.astype(BF16), (((0,), (0,)), ((), ())),
                           preferred_element_type=F32)


def _tril(n):
    row = lax.broadcasted_iota(jnp.int32, (n, n), 0)
    col = lax.broadcasted_iota(jnp.int32, (n, n), 1)
    return row >= col


def _cumsum_rows(x):
    t = _tril(x.shape[0]).astype(BF16)
    hi = x.astype(BF16)
    r1 = x - hi.astype(F32)
    mid = r1.astype(BF16)
    lo = (r1 - mid.astype(F32)).astype(BF16)
    dot = lambda piece: jnp.dot(t, piece, preferred_element_type=F32)
    return dot(hi) + dot(mid) + dot(lo)


_SRC_Q, _SRC_K, _SRC_V, _SRC_Z, _SRC_XBC, _SRC_DT, _SRC_MQ = 0, 512, 640, 768, 1792, 3328, 3344
_SRC_MI = _SRC_MQ + 4 * D_MLSTM
IN_WIDTH = _SRC_MI + 2 * MLSTM_HEADS


def _prep_w_in_kernel(w_ref, o_ref):
    def put(dst, src, n):
        o_ref[:, dst:dst + n] = w_ref[src:src + n, :].T.astype(BF16)

    put(C_XBC, _SRC_XBC, CONV_DIM)
    put(C_Q, _SRC_Q, D_ATT)
    put(C_Z, _SRC_Z, D_SSM)
    put(C_MQ, _SRC_MQ, 4 * D_MLSTM)
    put(C_K, _SRC_K, D_KV)
    put(C_V, _SRC_V, D_KV)
    tk = w_ref.shape[1]
    dt = w_ref[_SRC_DT:_SRC_DT + SSM_HEADS, :]
    gates = w_ref[_SRC_MI:_SRC_MI + 2 * MLSTM_HEADS, :]
    pad = jnp.zeros((LANE - GATE_LANE - 2 * MLSTM_HEADS, tk), F32)
    lo = jnp.concatenate([dt, gates, pad], axis=0)
    hi = jnp.concatenate([jnp.zeros_like(dt), pltpu.roll(gates, MLSTM_HEADS, 0), pad], axis=0)
    o_ref[:, C_G:C_G + LANE] = lo.T.astype(BF16)
    o_ref[:, C_G + LANE:C_G + 2 * LANE] = hi.T.astype(BF16)


def _prep_w_in(w_in, layer, tk):
    depth, k, n = w_in.shape
    assert n == IN_WIDTH and n % 8 == 0
    wt = jnp.swapaxes(w_in, 1, 2).reshape(depth * n, k)
    return pl.pallas_call(
        _prep_w_in_kernel,
        grid=(k // tk,),
        in_specs=[pl.BlockSpec((n, tk), lambda i: (layer, i))],
        out_specs=pl.BlockSpec((tk, NP), lambda i: (i, 0)),
        out_shape=jax.ShapeDtypeStruct((k, NP), BF16),
        compiler_params=_cparams(("parallel",)),
        name="prep_w_in",
    )(wt)


def _outproj_kernel(a_ref, y_ref, h_ref, x_ref, w_ref, o_ref, wb_ref):
    @pl.when(pl.program_id(0) == 0)
    def _():
        wb_ref[...] = w_ref[...].astype(BF16)

    acc = jnp.dot(a_ref[...], wb_ref[0:D_ATT, :], preferred_element_type=F32)
    acc = acc + jnp.dot(y_ref[...], wb_ref[D_ATT:D_ATT + D_SSM, :], preferred_element_type=F32)
    acc = acc + jnp.dot(h_ref[...], wb_ref[D_ATT + D_SSM:, :], preferred_element_type=F32)
    o_ref[...] = x_ref[...] + acc


def _outproj(att, y, h, x, w, layer, tm):
    m, n = x.shape
    k = w.shape[1]
    rows = lambda width: pl.BlockSpec((tm, width), lambda i: (i, 0))
    return pl.pallas_call(
        _outproj_kernel,
        grid=(m // tm,),
        in_specs=[rows(D_ATT), rows(D_SSM), rows(D_MLSTM), rows(n),
                  pl.BlockSpec((None, k, n), lambda i: (layer, 0, 0), pipeline_mode=pl.Buffered(1))],
        out_specs=rows(n),
        out_shape=jax.ShapeDtypeStruct((m, n), F32),
        scratch_shapes=[pltpu.VMEM((k, n), BF16)],
        compiler_params=_cparams(("arbitrary",)),
        name="outproj",
    )(att, y, h, x, w)


N_MLP_IN, N_SSM_SIDE_IN, N_MLSTM_SIDE_IN = 5, 5, 6


def _mlp_kernel(*refs, final_norm, side, n_prev):
    x_ref, g_ref, wu_ref, wd_ref, gf_ref = refs[:N_MLP_IN]
    n_side = N_SSM_SIDE_IN + N_MLSTM_SIDE_IN if side else 0
    side_in = refs[N_MLP_IN:N_MLP_IN + n_side]
    outs = refs[N_MLP_IN + n_side + n_prev:]
    o_ref = outs[0]
    u_ref = outs[5] if side else outs[1]
    j = pl.program_id(1)

    @pl.when(j == 0)
    def _():
        x = x_ref[...]
        ms = jnp.mean(x * x, axis=-1, keepdims=True)
        u_ref[...] = (x * lax.rsqrt(ms + EPS) * g_ref[...]).astype(BF16)
        o_ref[...] = x
        if side:
            outs[6][...] = jnp.zeros_like(outs[6])
            outs[7][...] = jnp.zeros_like(outs[7])

    h = _dot(u_ref[...], wu_ref[...])
    h = jnp.square(jnp.maximum(h, 0.0))
    o_ref[...] += _dot(h, wd_ref[...])

    if side:
        seq = pl.program_id(0) * pl.num_programs(1) + j
        row = seq % SEQ_ROWS
        so_ref, y_ref, co_ref, qc_ref = outs[1:5]
        yacc_ref, qacc_ref = outs[6:8]
        _ssm_sample_one(0, row, seq, *side_in[:N_SSM_SIDE_IN], so_ref, y_ref, yacc_ref)
        _mlstm_sample_one(0, row, seq, *side_in[N_SSM_SIDE_IN:], co_ref, qc_ref, qacc_ref)

    if final_norm:
        @pl.when(j == pl.num_programs(1) - 1)
        def _():
            y = o_ref[...]
            ms = jnp.mean(y * y, axis=-1, keepdims=True)
            o_ref[...] = y * lax.rsqrt(ms + EPS) * gf_ref[...]


def _mlp(x, g, wu, wd, gf, layer, final_norm, tm, tf, side=None):
    m, d = x.shape
    ff = wu.shape[2]
    nj = ff // tf
    in_specs = [pl.BlockSpec((tm, d), lambda i, j: (i, 0)),
                pl.BlockSpec((1, d), lambda i, j: (0, 0)),
                pl.BlockSpec((None, d, tf), lambda i, j: (layer, 0, j)),
                pl.BlockSpec((None, tf, d), lambda i, j: (layer, j, 0)),
                pl.BlockSpec((1, d), lambda i, j: (0, 0))]
    out_specs = [pl.BlockSpec((tm, d), lambda i, j: (i, 0))]
    out_shape = [jax.ShapeDtypeStruct((m, d), F32)]
    operands = [x, g, wu, wd, gf]
    aliases = {}
    n_prev = 0
    if side is not None:
        ea, ssm, xdt, bvec, cvec, gq, esw, mem, proj_s, ks, prev_s, prev_c = side
        depth, bd = ssm.shape[0], ssm.shape[1]
        assert (m // tm) * nj == bd and nj % SEQ_ROWS == 0
        smem = pl.BlockSpec(memory_space=pltpu.SMEM)
        rows = lambda n, col=0: pl.BlockSpec((SEQ_ROWS, n), lambda i, j: ((i * nj + j) // SEQ_ROWS, col // n))
        state = lambda r, n: pl.BlockSpec((None, 1, r, n), lambda i, j: (layer, i * nj + j, 0, 0))
        in_specs += [smem, state(D_SSM, D_STATE), rows(D_SSM), rows(SSM_GROUPS * D_STATE),
                     rows(SSM_GROUPS * D_STATE),
                     smem, smem, state(D_MLSTM, MLSTM_HEAD_DIM), rows(D_MLSTM, C_MQ), rows(D_MLSTM),
                     rows(D_MLSTM, C_MV)]
        operands += [ea, ssm, xdt, bvec, cvec, gq, esw, mem, proj_s, ks, proj_s]
        out_specs += [state(D_SSM, D_STATE), rows(D_SSM), state(D_MLSTM, MLSTM_HEAD_DIM), rows(D_MLSTM)]
        out_shape += [jax.ShapeDtypeStruct((depth, bd, D_SSM, D_STATE), F32),
                      jax.ShapeDtypeStruct((bd, D_SSM), F32),
                      jax.ShapeDtypeStruct((depth, bd, D_MLSTM, MLSTM_HEAD_DIM), F32),
                      jax.ShapeDtypeStruct((bd, D_MLSTM), F32)]
        if prev_s is not None:
            aliases = {len(operands): 1, len(operands) + 1: 3}
            operands += [prev_s, prev_c]
            in_specs += [pl.BlockSpec(memory_space=pl.ANY)] * 2
            n_prev = 2
    res = pl.pallas_call(
        functools.partial(_mlp_kernel, final_norm=final_norm, side=side is not None, n_prev=n_prev),
        grid=(m // tm, nj),
        in_specs=in_specs,
        out_specs=out_specs,
        out_shape=out_shape,
        input_output_aliases=aliases,
        scratch_shapes=[pltpu.VMEM((tm, d), BF16)] + ([] if side is None else [
            pltpu.VMEM((SEQ_ROWS, D_SSM), F32), pltpu.VMEM((SEQ_ROWS, D_MLSTM), F32)]),
        compiler_params=_cparams(("parallel", "arbitrary")),
        name="mlp",
    )(*operands)
    return res[0] if side is None else res


def _rope_tables(pos):
    half = ROPE_DIM // 2
    inv = jnp.power(jnp.float32(ROPE_THETA), -jnp.arange(half, dtype=jnp.float32) / half)
    ang = pos.astype(jnp.float32)[:, None] * inv[None, :]
    cos = jnp.cos(ang)
    sin = jnp.sin(ang)
    n = pos.shape[0]
    rest = ATT_HEAD_DIM - ROPE_DIM
    c = jnp.concatenate([cos, cos, jnp.ones((n, rest), F32)], axis=1)
    s1 = jnp.concatenate([-sin, jnp.zeros((n, half + rest), F32)], axis=1)
    s2 = jnp.concatenate([jnp.zeros((n, half), F32), sin, jnp.zeros((n, rest), F32)], axis=1)
    rep = LANE // ATT_HEAD_DIM
    return jnp.tile(c, (1, rep)), jnp.tile(s1, (1, rep)), jnp.tile(s2, (1, rep))


def _rope(x, c, s1, s2):
    width = x.shape[-1]
    rep = width // LANE
    half = ROPE_DIM // 2
    if rep > 1:
        c = jnp.concatenate([c] * rep, axis=-1)
        s1 = jnp.concatenate([s1] * rep, axis=-1)
        s2 = jnp.concatenate([s2] * rep, axis=-1)
    axis = x.ndim - 1
    return x * c + pltpu.roll(x, width - half, axis) * s1 + pltpu.roll(x, half, axis) * s2


def _attn_prompt_init(ko_ref, vo_ref):
    ko_ref[...] = jnp.zeros_like(ko_ref)
    vo_ref[...] = jnp.zeros_like(vo_ref)


def _attn_prompt_body(j, sink_ref, q_ref, k_ref, v_ref, c_ref, s1_ref, s2_ref, o_ref, ko_ref, vo_ref):
    w = WINDOW
    c = c_ref[...]
    s1 = s1_ref[...]
    s2 = s2_ref[...]
    krot = _rope(k_ref[...], c, s1, s2)
    v = v_ref[...]
    qrot = _rope(q_ref[...], c, s1, s2)
    kk = jnp.concatenate([ko_ref[...], krot], axis=0).astype(BF16)
    vv = jnp.concatenate([vo_ref[...], v], axis=0).astype(BF16)

    row = lax.broadcasted_iota(jnp.int32, (w, 2 * w), 0)
    col = lax.broadcasted_iota(jnp.int32, (w, 2 * w), 1)
    first_col = jnp.where(j > 0, 0, w)
    mask = (col >= row) & (col <= row + w) & (col >= first_col)
    scale = ATT_HEAD_DIM ** -0.5

    for h in range(ATT_KV_HEADS):
        kh = kk[:, h * ATT_HEAD_DIM:(h + 1) * ATT_HEAD_DIM]
        vh = vv[:, h * ATT_HEAD_DIM:(h + 1) * ATT_HEAD_DIM]
        for g in range(ATT_GROUP):
            hg = h * ATT_GROUP + g
            qh = qrot[:, hg * ATT_HEAD_DIM:(hg + 1) * ATT_HEAD_DIM]
            s = _dot_nt(qh, kh) * scale
            s = jnp.where(mask, s, -jnp.inf)
            sink = sink_ref[hg]
            m = jnp.maximum(jnp.max(s, axis=-1, keepdims=True), sink)
            p = jnp.exp(s - m)
            denom = jnp.sum(p, axis=-1, keepdims=True) + jnp.exp(sink - m)
            o = _dot(p, vh) / denom
            o_ref[:, hg * ATT_HEAD_DIM:(hg + 1) * ATT_HEAD_DIM] = o.astype(o_ref.dtype)
            yield

    ko_ref[...] = krot
    vo_ref[...] = v


def _group_rmsnorm(y, w, groups):
    width = y.shape[-1] // groups
    outs = []
    for g in range(groups):
        yg = y[:, g * width:(g + 1) * width]
        ms = jnp.mean(yg * yg, axis=-1, keepdims=True)
        outs.append(yg * lax.rsqrt(ms + EPS) * w[:, g * width:(g + 1) * width])
    return jnp.concatenate(outs, axis=-1)


CONV_PAD = 8


def _ssd_prompt_init(s_ref, tail_ref):
    tail_ref[...] = jnp.zeros_like(tail_ref)
    s_ref[...] = jnp.zeros_like(s_ref)


def _ssd_prompt_body(xbc_ref, z_ref, g_ref, cw_ref, cb_ref, dtb_ref, alog_ref, dskip_ref, wn_ref,
                     y_ref, s_ref, tail_ref, ybuf_ref, xw_ref):
    q = CHUNK
    pad = CONV_PAD
    hp = SSM_HEAD_DIM
    heads_per_group = SSM_HEADS // SSM_GROUPS
    gw = heads_per_group * hp

    x = xbc_ref[...]
    cw = cw_ref[...]
    row8 = lax.broadcasted_iota(jnp.int32, (pad, CONV_DIM), 0)
    acc = None
    for j in range(CONV_WIDTH - 1):
        shift = CONV_WIDTH - 1 - j
        rolled = pltpu.roll(x, shift, 0)
        head = jnp.where(row8 < shift, tail_ref[j * pad:(j + 1) * pad, :], rolled[0:pad, :])
        tail_ref[j * pad:(j + 1) * pad, :] = rolled[0:pad, :]
        term = jnp.concatenate([head, rolled[pad:, :]], axis=0) * cw[j:j + 1, :]
        acc = term if acc is None else acc + term
    acc = acc + x * cw[CONV_WIDTH - 1:CONV_WIDTH, :]
    xc = _silu(acc + cb_ref[...])

    xs = xc[:, 0:D_SSM]
    bm = xc[:, D_SSM:D_SSM + SSM_GROUPS * D_STATE]
    cm = xc[:, D_SSM + SSM_GROUPS * D_STATE:]

    dt = jax.nn.softplus(g_ref[:, 0:LANE] + dtb_ref[...])
    a_neg = -jnp.exp(alog_ref[...])
    a_col = _cumsum_rows(dt * a_neg)
    a_row = a_col.T
    dt_row = dt.T
    wk_row = jnp.exp(a_row[:, q - 1:q] - a_row) * dt_row
    tril = _tril(q)
    xs_t = xs.T
    ea_col = jnp.exp(a_col)
    left = lax.broadcasted_iota(jnp.int32, (q, LANE), 1) < hp
    ea_full = jnp.concatenate(
        [jnp.where(left, ea_col[:, 2 * t:2 * t + 1], ea_col[:, 2 * t + 1:2 * t + 2]) for t in range(SSM_HEADS // 2)],
        axis=1)
    yield

    cs_parts = []
    for g in range(SSM_GROUPS):
        bg = bm[:, g * D_STATE:(g + 1) * D_STATE]
        cg = cm[:, g * D_STATE:(g + 1) * D_STATE]
        cb = _dot_nt(cg, bg)
        cs_parts.append(_dot_nt(cg, s_ref[g * gw:(g + 1) * gw, :]))
        for r in range(heads_per_group):
            h = g * heads_per_group + r
            seg = a_col[:, h:h + 1] - a_row[h:h + 1, :]
            wmat = jnp.exp(jnp.where(tril, seg, -jnp.inf)) * cb * dt_row[h:h + 1, :]
            ybuf_ref[:, h * hp:(h + 1) * hp] = _dot(wmat, xs[:, h * hp:(h + 1) * hp])
            xw_ref[r * hp:(r + 1) * hp, :] = xs_t[h * hp:(h + 1) * hp, :] * wk_row[h:h + 1, :]
            yield
        upd = _dot(xw_ref[...], bg)
        for r in range(heads_per_group):
            h = g * heads_per_group + r
            decay = jnp.exp(a_row[h:h + 1, q - 1:q])
            s_ref[h * hp:(h + 1) * hp, :] = s_ref[h * hp:(h + 1) * hp, :] * decay + upd[r * hp:(r + 1) * hp, :]
        yield

    y = ybuf_ref[...] + jnp.concatenate(cs_parts, axis=1) * ea_full
    y = y + dskip_ref[...] * xs
    y = y * _silu(z_ref[...])
    y_ref[...] = _group_rmsnorm(y, wn_ref[...], SSM_GROUPS).astype(y_ref.dtype)


def _mlstm_prompt_init(c_ref, n_ref, m_ref):
    c_ref[...] = jnp.zeros_like(c_ref)
    n_ref[...] = jnp.zeros_like(n_ref)
    m_ref[...] = jnp.zeros_like(m_ref)


def _mlstm_prompt_body(q_ref, k_ref, v_ref, o_ref, g_ref, bi_ref, bf_ref, wn_ref, h_ref, c_ref, n_ref, m_ref):
    t = CHUNK
    d = MLSTM_HEAD_DIM

    gi = g_ref[:, 0:LANE] + bi_ref[...]
    gf = g_ref[:, LANE:2 * LANE] + bf_ref[...]
    b_col = _cumsum_rows(jax.nn.log_sigmoid(gf))
    b_row = b_col.T
    i_row = gi.T
    tril = _tril(t)
    k_all = k_ref[...] * (d ** -0.5)

    for h in range(MLSTM_HEADS):
        ln = GATE_LANE + h
        bq = b_col[:, ln:ln + 1]
        logw = jnp.where(tril, bq - b_row[ln:ln + 1, :] + i_row[ln:ln + 1, :], -jnp.inf)
        m_prev = m_ref[h:h + 1, 0:1]
        log_inter = bq + m_prev
        mt = jnp.maximum(log_inter, jnp.max(logw, axis=-1, keepdims=True))
        qh = q_ref[:, h * d:(h + 1) * d]
        kh = k_all[:, h * d:(h + 1) * d]
        vh = v_ref[:, h * d:(h + 1) * d]
        ch = c_ref[h * d:(h + 1) * d, :]
        nh = n_ref[h:h + 1, :]
        sw = jnp.exp(logw - mt) * _dot_nt(qh, kh)
        gq = jnp.exp(log_inter - mt)
        num = _dot(sw, vh) + _dot(qh, ch) * gq
        den = jnp.sum(sw, axis=-1, keepdims=True) + jnp.sum(qh * nh, axis=-1, keepdims=True) * gq
        hh = num / jnp.maximum(jnp.abs(den), jnp.exp(-mt))

        m_new = mt[t - 1:t, :]
        b_last = b_col[t - 1:t, ln:ln + 1]
        wk = jnp.exp(b_last - bq + gi[:, ln:ln + 1] - m_new)
        g_end = jnp.exp(b_last + m_prev - m_new)
        kw = kh * wk
        c_ref[h * d:(h + 1) * d, :] = ch * g_end + _dot_tn(kw, vh)
        n_ref[h:h + 1, :] = nh * g_end + jnp.sum(kw, axis=0, keepdims=True)
        m_ref[h:h + 1, :] = jnp.broadcast_to(m_new, (1, LANE))

        ms = jnp.mean(hh * hh, axis=-1, keepdims=True)
        hn = hh * lax.rsqrt(ms + EPS) * wn_ref[:, h * d:(h + 1) * d]
        h_ref[:, h * d:(h + 1) * d] = (hn * jax.nn.sigmoid(o_ref[:, h * d:(h + 1) * d])).astype(h_ref.dtype)
        yield

def _interleave(stages):
    live = [[gen, 0, n] for gen, n in stages]
    while live:
        item = min(live, key=lambda s: s[1] / s[2])
        try:
            next(item[0])
            item[1] += 1
        except StopIteration:
            live.remove(item)


PROJ_TILE = 256


def _project_next(x_ref, g_ref, w_ref, pnext_ref):
    x = x_ref[...]
    u = (x * lax.rsqrt(jnp.mean(x * x, axis=-1, keepdims=True) + EPS) * g_ref[...]).astype(BF16)
    for ct in range(NP // PROJ_TILE):
        cols = slice(ct * PROJ_TILE, (ct + 1) * PROJ_TILE)
        pnext_ref[:, cols] = jnp.dot(u, w_ref[:, cols], preferred_element_type=F32)
        yield


def _front_kernel(sink_ref, x_ref, x0_ref, g_ref, w_ref, c_ref, s1_ref, s2_ref,
                  cw_ref, cb_ref, dtb_ref, alog_ref, dskip_ref, wns_ref, bi_ref, bf_ref, wnm_ref,
                  att_ref, ko_ref, vo_ref, y_ref, s_ref, h_ref, mem_ref, n_ref, m_ref, ptail_ref,
                  pcur_ref, pnext_ref, tail_ref, ybuf_ref, xw_ref, *, nblk):
    j = pl.program_id(1)
    first = (pl.program_id(0) == 0) & (j == 0)
    t = CHUNK

    @pl.when(first)
    def _():
        for _ in _project_next(x0_ref, g_ref, w_ref, pcur_ref):
            pass

    @pl.when(jnp.logical_not(first))
    def _():
        pcur_ref[...] = pnext_ref[...]

    @pl.when(j == 0)
    def _():
        _attn_prompt_init(ko_ref, vo_ref)
        _ssd_prompt_init(s_ref, tail_ref)
        _mlstm_prompt_init(mem_ref, n_ref, m_ref)

    def attn(i):
        seg = lambda col, width: pcur_ref.at[i * t:(i + 1) * t, col:col + width]
        rows = lambda ref: ref.at[i * t:(i + 1) * t, :]
        return _attn_prompt_body(j * nblk + i, sink_ref, seg(C_Q, D_ATT), seg(C_K, D_KV), seg(C_V, D_KV),
                                 rows(c_ref), rows(s1_ref), rows(s2_ref), rows(att_ref), ko_ref, vo_ref)

    def ssd(i):
        seg = lambda col, width: pcur_ref.at[i * t:(i + 1) * t, col:col + width]
        return _ssd_prompt_body(seg(C_XBC, CONV_DIM), seg(C_Z, D_SSM), seg(C_G, 2 * LANE), cw_ref, cb_ref, dtb_ref,
                                alog_ref, dskip_ref, wns_ref, y_ref.at[i * t:(i + 1) * t, :], s_ref,
                                tail_ref, ybuf_ref, xw_ref)

    def mlstm(i):
        seg = lambda col, width: pcur_ref.at[i * t:(i + 1) * t, col:col + width]
        return _mlstm_prompt_body(seg(C_MQ, D_MLSTM), seg(C_MK, D_MLSTM), seg(C_MV, D_MLSTM), seg(C_MO, D_MLSTM),
                                  seg(C_G, 2 * LANE), bi_ref, bf_ref, wnm_ref, h_ref.at[i * t:(i + 1) * t, :],
                                  mem_ref, n_ref, m_ref)

    def blocks(body):
        for i in range(nblk):
            yield from body(i)

    ptail_ref[...] = pcur_ref[nblk * t - CONV_PAD:nblk * t, C_XBC:C_XBC + CONV_DIM]
    _interleave([
        (_project_next(x_ref, g_ref, w_ref, pnext_ref), NP // PROJ_TILE + 1),
        (blocks(attn), nblk * (ATT_HEADS + 1)),
        (blocks(ssd), nblk * (SSM_HEADS + SSM_GROUPS + 2)),
        (blocks(mlstm), nblk * (MLSTM_HEADS + 1)),
    ])


def _front(x, g_mix, w, sinks, tables, cw, cb, dtb, alog, dskip_rep, wns, bi, bf, wnm, bsz, seq):
    nblk = 2 if (seq // CHUNK) % 2 == 0 else 1
    t = nblk * CHUNK
    nb = seq // t
    total = bsz * nb
    c, s1, s2 = tables
    row = lambda b, j: b * nb + j
    full = lambda shape: pl.BlockSpec(shape, lambda b, j: (0,) * len(shape))
    vec = lambda n: full((1, n))
    tab = pl.BlockSpec((t, LANE), lambda b, j: (j, 0))
    rows_out = lambda width: pl.BlockSpec((t, width), lambda b, j: (row(b, j), 0))
    per_seq = lambda r, n: pl.BlockSpec((None, r, n), lambda b, j: (b, 0, 0))
    return pl.pallas_call(
        functools.partial(_front_kernel, nblk=nblk),
        grid=(bsz, nb),
        in_specs=[pl.BlockSpec(memory_space=pltpu.SMEM),
                  pl.BlockSpec((t, D_MODEL), lambda b, j: (jnp.minimum(row(b, j) + 1, total - 1), 0)),
                  pl.BlockSpec((t, D_MODEL), lambda b, j: (0, 0), pipeline_mode=pl.Buffered(1)),
                  vec(D_MODEL),
                  pl.BlockSpec((D_MODEL, NP), lambda b, j: (0, 0), pipeline_mode=pl.Buffered(1)),
                  tab, tab, tab,
                  full((CONV_WIDTH, CONV_DIM)), vec(CONV_DIM), vec(LANE), vec(LANE), vec(D_SSM), vec(D_SSM),
                  vec(LANE), vec(LANE), vec(D_MLSTM)],
        out_specs=[rows_out(D_ATT), per_seq(CHUNK, D_KV), per_seq(CHUNK, D_KV),
                   rows_out(D_SSM), per_seq(D_SSM, D_STATE),
                   rows_out(D_MLSTM), per_seq(D_MLSTM, MLSTM_HEAD_DIM), per_seq(8, LANE), per_seq(8, LANE),
                   per_seq(CONV_PAD, CONV_DIM)],
        out_shape=[jax.ShapeDtypeStruct((bsz * seq, D_ATT), BF16),
                   jax.ShapeDtypeStruct((bsz, CHUNK, D_KV), F32),
                   jax.ShapeDtypeStruct((bsz, CHUNK, D_KV), F32),
                   jax.ShapeDtypeStruct((bsz * seq, D_SSM), BF16),
                   jax.ShapeDtypeStruct((bsz, D_SSM, D_STATE), F32),
                   jax.ShapeDtypeStruct((bsz * seq, D_MLSTM), BF16),
                   jax.ShapeDtypeStruct((bsz, D_MLSTM, MLSTM_HEAD_DIM), F32),
                   jax.ShapeDtypeStruct((bsz, 8, LANE), F32),
                   jax.ShapeDtypeStruct((bsz, 8, LANE), F32),
                   jax.ShapeDtypeStruct((bsz, CONV_PAD, CONV_DIM), F32)],
        scratch_shapes=[pltpu.VMEM((t, NP), F32),
                        pltpu.VMEM((t, NP), F32),
                        pltpu.VMEM(((CONV_WIDTH - 1) * CONV_PAD, CONV_DIM), F32),
                        pltpu.VMEM((CHUNK, D_SSM), F32),
                        pltpu.VMEM((D_SSM // SSM_GROUPS, CHUNK), F32)],
        compiler_params=_cparams(("arbitrary", "arbitrary")),
        name="front",
    )(sinks, x, x, g_mix, w, c, s1, s2, cw, cb, dtb, alog, dskip_rep, wns, bi, bf, wnm)


def _stacked_out(prev, depth, shape, block, layer):
    spec = pl.BlockSpec((None,) + block, lambda i: (layer,) + (i,) + (0,) * (len(block) - 1))
    out_shape = jax.ShapeDtypeStruct((depth,) + shape, F32)
    extra_in = [] if prev is None else [prev]
    extra_spec = [] if prev is None else [pl.BlockSpec(memory_space=pl.ANY)]
    return spec, out_shape, extra_in, extra_spec


def _drop_aliased(kernel_fn, n_in, n_prev):
    def wrapped(*refs):
        return kernel_fn(*refs[:n_in], *refs[n_in + n_prev:])
    return wrapped


def _attn_sample_kernel(q_ref, k_ref, v_ref, kc_ref, vc_ref, sink_ref, c_ref, s1_ref, s2_ref,
                        o_ref, ko_ref, vo_ref):
    tb, _, w = kc_ref.shape
    c = c_ref[...]
    s1 = s1_ref[...]
    s2 = s2_ref[...]
    qb = _rope(q_ref[...], c, s1, s2)
    kn = _rope(k_ref[...], c, s1, s2)
    vn = v_ref[...]
    kc = kc_ref[...]
    vc = vc_ref[...]
    newest = lax.broadcasted_iota(jnp.int32, (D_KV, w), 1) == w - 1
    for b in range(tb):
        ko_ref[b] = jnp.where(newest, _column_tile(kn[b]), pltpu.roll(kc[b], w - 1, 1))
        vo_ref[b] = jnp.where(newest, _column_tile(vn[b]), pltpu.roll(vc[b], w - 1, 1))

    scale = ATT_HEAD_DIM ** -0.5
    s = jnp.einsum("bhd,bdk->bhk", qb.astype(BF16), kc.astype(BF16),
                   preferred_element_type=F32) * scale
    sn = jnp.sum(qb * kn, axis=-1, keepdims=True) * scale
    sink = sink_ref[...][None, :, 0:1]
    m = jnp.maximum(jnp.maximum(jnp.max(s, axis=-1, keepdims=True), sn), sink)
    p = jnp.exp(s - m)
    pn = jnp.exp(sn - m)
    denom = jnp.sum(p, axis=-1, keepdims=True) + pn + jnp.exp(sink - m)
    p = p / denom
    pn = pn / denom
    o = jnp.einsum("bhk,bdk->bhd", p.astype(BF16), vc.astype(BF16), preferred_element_type=F32)
    o = o + pn * vn
    head = lax.broadcasted_iota(jnp.int32, o.shape[:2] + (ATT_HEAD_DIM,), 1)
    o_ref[...] = jnp.where(head < ATT_GROUP, o[:, :, 0:ATT_HEAD_DIM], o[:, :, ATT_HEAD_DIM:]).astype(o_ref.dtype)


def _attn_sample(qb, kn, vn, kc, vc, sink_rows, tables, layer, prev, tb):
    depth, bd, _, w = kc.shape
    c, s1, s2 = tables
    blk3 = lambda r, n: pl.BlockSpec((tb, r, n), lambda i: (i, 0, 0))
    cache = pl.BlockSpec((None, tb, D_KV, w), lambda i: (layer, i, 0, 0))
    vec = pl.BlockSpec((1, LANE), lambda i: (0, 0))
    prev_k, prev_v = (None, None) if prev is None else prev
    k_spec, k_shape, k_in, k_in_spec = _stacked_out(prev_k, depth, (bd, D_KV, w), (tb, D_KV, w), layer)
    v_spec, v_shape, v_in, v_in_spec = _stacked_out(prev_v, depth, (bd, D_KV, w), (tb, D_KV, w), layer)
    n_in = 9
    n_prev = len(k_in) + len(v_in)
    return pl.pallas_call(
        _drop_aliased(_attn_sample_kernel, n_in, n_prev),
        grid=(bd // tb,),
        in_specs=[blk3(ATT_HEADS, LANE), blk3(1, LANE), blk3(1, LANE), cache, cache,
                  pl.BlockSpec((ATT_HEADS, LANE), lambda i: (0, 0)), vec, vec, vec] + k_in_spec + v_in_spec,
        out_specs=[blk3(ATT_HEADS, ATT_HEAD_DIM), k_spec, v_spec],
        out_shape=[jax.ShapeDtypeStruct((bd, ATT_HEADS, ATT_HEAD_DIM), BF16), k_shape, v_shape],
        input_output_aliases={n_in + t: 1 + t for t in range(n_prev)},
        compiler_params=_cparams(("parallel",)),
        name="attn_sample",
    )(qb, kn, vn, kc, vc, sink_rows, c, s1, s2, *k_in, *v_in)


def _lane_place(cols, lane0):
    m = cols[0].shape[0]
    lane = lax.broadcasted_iota(jnp.int32, (m, LANE), 1)
    out = jnp.zeros((m, LANE), F32)
    for i, col in enumerate(cols):
        out = jnp.where(lane == lane0 + i, col, out)
    return out


def _sample_pre_kernel(x_ref, gmix_ref, w_ref, cs_ref, cw_ref, cb_ref, dtb_ref, alog_ref, bi_ref, bf_ref, n_ref,
                       m_ref, proj_ref, conv_ref, xs_ref, xdt_ref, bm_ref, cm_ref, ea_ref,
                       ks_ref, nn_ref, g_ref, esw_ref, sw_ref, den_ref, mt_ref):
    for _ in _project_next(x_ref, gmix_ref, w_ref, proj_ref):
        pass
    xbc = proj_ref[:, C_XBC:C_XBC + CONV_DIM]
    s0 = cs_ref[0]
    s1 = cs_ref[1]
    s2 = cs_ref[2]
    cw = cw_ref[...]
    acc = s0 * cw[0:1, :]
    acc = acc + s1 * cw[1:2, :]
    acc = acc + s2 * cw[2:3, :]
    acc = acc + xbc * cw[3:4, :]
    xc = _silu(acc + cb_ref[...])
    conv_ref[0] = s1
    conv_ref[1] = s2
    conv_ref[2] = xbc

    xs = xc[:, 0:D_SSM]
    xs_ref[...] = xs
    bm_ref[...] = xc[:, D_SSM:D_SSM + SSM_GROUPS * D_STATE]
    cm_ref[...] = xc[:, D_SSM + SSM_GROUPS * D_STATE:]
    dt = jax.nn.softplus(proj_ref[:, C_G:C_G + LANE] + dtb_ref[...])
    ea_ref[...] = jnp.exp(dt * (-jnp.exp(alog_ref[...])))
    hp = SSM_HEAD_DIM
    for h in range(SSM_HEADS):
        xdt_ref[:, h * hp:(h + 1) * hp] = xs[:, h * hp:(h + 1) * hp] * dt[:, h:h + 1]

    d = MLSTM_HEAD_DIM
    gi = proj_ref[:, C_G:C_G + LANE] + bi_ref[...]
    gf = proj_ref[:, C_G + LANE:C_G + 2 * LANE] + bf_ref[...]
    log_inter = jax.nn.log_sigmoid(gf) + m_ref[...]
    mt = jnp.maximum(log_inter, gi)
    gq = jnp.exp(log_inter - mt)
    esw = jnp.exp(gi - mt)
    ks = proj_ref[:, C_MK:C_MK + D_MLSTM] * (d ** -0.5)
    ks_ref[...] = ks
    qk_cols = []
    qn_cols = []
    for h in range(MLSTM_HEADS):
        qh = proj_ref[:, C_MQ + h * d:C_MQ + (h + 1) * d]
        kh = ks[:, h * d:(h + 1) * d]
        nh = n_ref[:, h * d:(h + 1) * d]
        qk_cols.append(jnp.sum(qh * kh, axis=-1, keepdims=True))
        qn_cols.append(jnp.sum(qh * nh, axis=-1, keepdims=True))
        ln = GATE_LANE + h
        nn_ref[:, h * d:(h + 1) * d] = nh * gq[:, ln:ln + 1] + kh * esw[:, ln:ln + 1]
    sw = esw * _lane_place(qk_cols, GATE_LANE)
    den = sw + _lane_place(qn_cols, GATE_LANE) * gq
    g_ref[...] = gq
    esw_ref[...] = esw
    sw_ref[...] = sw
    den_ref[...] = jnp.maximum(jnp.abs(den), jnp.exp(-mt))
    mt_ref[...] = mt


def _sample_pre(x, g_mix, w, conv_state, layer, cw, cb, dtb, alog, bi, bf, n_state, m_tile):
    bd = x.shape[0]
    taps = CONV_WIDTH - 1
    tile = jax.ShapeDtypeStruct((bd, LANE), F32)
    whole = lambda a: pl.BlockSpec(a.shape, lambda i: (0,) * a.ndim)
    in_specs = [whole(x), whole(g_mix), pl.BlockSpec(w.shape, lambda i: (0, 0), pipeline_mode=pl.Buffered(1)),
                pl.BlockSpec((None, taps, bd, CONV_DIM), lambda i: (layer, 0, 0, 0))]
    in_specs += [whole(a) for a in (cw, cb, dtb, alog, bi, bf, n_state, m_tile)]
    outs = [jax.ShapeDtypeStruct((bd, NP), F32),
            jax.ShapeDtypeStruct((taps, bd, CONV_DIM), F32),
            jax.ShapeDtypeStruct((bd, D_SSM), F32),
            jax.ShapeDtypeStruct((bd, D_SSM), F32),
            jax.ShapeDtypeStruct((bd, SSM_GROUPS * D_STATE), F32),
            jax.ShapeDtypeStruct((bd, SSM_GROUPS * D_STATE), F32),
            tile,
            jax.ShapeDtypeStruct((bd, D_MLSTM), F32),
            jax.ShapeDtypeStruct((bd, D_MLSTM), F32),
            tile, tile, tile, tile, tile]
    return pl.pallas_call(
        _sample_pre_kernel,
        grid=(1,),
        in_specs=in_specs,
        out_specs=[pl.BlockSpec(o.shape, lambda i, nd=len(o.shape): (0,) * nd) for o in outs],
        out_shape=outs,
        compiler_params=_cparams(("arbitrary",)),
        name="sample_pre",
    )(x, g_mix, w, conv_state, cw, cb, dtb, alog, bi, bf, n_state, m_tile)


def _column_tile(row):
    return jnp.broadcast_to(row, (LANE, LANE)).T


SEQ_ROWS = 8


def _put_row(out_ref, acc_ref, vr, col, row):
    width = row.shape[1]
    rid = lax.broadcasted_iota(jnp.int32, (SEQ_ROWS, width), 0)
    blk = jnp.where(rid == vr, row, acc_ref[:, col:col + width])
    acc_ref[:, col:col + width] = blk
    out_ref[:, col:col + width] = blk


def _ssm_sample_one(b, vr, gb, ea_ref, s_ref, xdt_ref, b_ref, c_ref, so_ref, y_ref, yacc_ref):
    hp = SSM_HEAD_DIM
    heads_per_tile = LANE // hp
    n_tiles = D_SSM // LANE
    tiles_per_group = n_tiles // SSM_GROUPS
    top = lax.broadcasted_iota(jnp.int32, (LANE, 1), 0) < hp
    xrow = xdt_ref[pl.ds(vr, 1), :]
    brow = b_ref[pl.ds(vr, 1), :]
    crow = c_ref[pl.ds(vr, 1), :]
    for t in range(n_tiles):
        g = t // tiles_per_group
        xcol = _column_tile(xrow[:, t * LANE:(t + 1) * LANE])
        bg = brow[:, g * D_STATE:(g + 1) * D_STATE]
        cg = jnp.broadcast_to(crow[:, g * D_STATE:(g + 1) * D_STATE], (8, D_STATE))
        e0 = ea_ref[gb, heads_per_tile * t]
        e1 = ea_ref[gb, heads_per_tile * t + 1]
        decay = jnp.where(top, e0, e1)
        sn = s_ref[b, t * LANE:(t + 1) * LANE, :] * decay + xcol * bg
        so_ref[b, t * LANE:(t + 1) * LANE, :] = sn
        _put_row(y_ref, yacc_ref, vr, t * LANE, _dot_nt(cg, sn)[0:1, :])


def _mlstm_sample_one(b, vr, gb, g_ref, esw_ref, c_ref, q_ref, k_ref, v_ref, co_ref, qc_ref, qacc_ref):
    d = MLSTM_HEAD_DIM
    qrow = q_ref[pl.ds(vr, 1), :]
    krow = k_ref[pl.ds(vr, 1), :]
    vrow = v_ref[pl.ds(vr, 1), :]
    for h in range(MLSTM_HEADS):
        qcol = _column_tile(qrow[:, h * d:(h + 1) * d])
        kcol = _column_tile(krow[:, h * d:(h + 1) * d])
        ch = c_ref[b, h * d:(h + 1) * d, :]
        g_end = g_ref[gb, GATE_LANE + h]
        wk = esw_ref[gb, GATE_LANE + h]
        _put_row(qc_ref, qacc_ref, vr, h * d, jnp.sum(qcol * ch, axis=0, keepdims=True))
        co_ref[b, h * d:(h + 1) * d, :] = ch * g_end + (kcol * wk) * vrow[:, h * d:(h + 1) * d]


def _sample_post_kernel(proj_ref, y_ref, xs_ref, dskip_ref, wns_ref, qc_ref, g_ref, sw_ref, den_ref, wnm_ref,
                        ys_ref, hs_ref):
    y = y_ref[...] + dskip_ref[...] * xs_ref[...]
    y = y * _silu(proj_ref[:, C_Z:C_Z + D_SSM])
    ys_ref[...] = _group_rmsnorm(y, wns_ref[...], SSM_GROUPS).astype(ys_ref.dtype)

    d = MLSTM_HEAD_DIM
    for h in range(MLSTM_HEADS):
        ln = GATE_LANE + h
        vh = proj_ref[:, C_MV + h * d:C_MV + (h + 1) * d]
        num = sw_ref[:, ln:ln + 1] * vh + qc_ref[:, h * d:(h + 1) * d] * g_ref[:, ln:ln + 1]
        hh = num / den_ref[:, ln:ln + 1]
        ms = jnp.mean(hh * hh, axis=-1, keepdims=True)
        hn = hh * lax.rsqrt(ms + EPS) * wnm_ref[:, h * d:(h + 1) * d]
        gate = jax.nn.sigmoid(proj_ref[:, C_MO + h * d:C_MO + (h + 1) * d])
        hs_ref[:, h * d:(h + 1) * d] = (hn * gate).astype(hs_ref.dtype)


def _sample_post(proj, y, xs, dskip_rep, wns, qc, gq, sw, den, wnm):
    bd = proj.shape[0]
    return pl.pallas_call(
        _sample_post_kernel,
        out_shape=[jax.ShapeDtypeStruct((bd, D_SSM), BF16), jax.ShapeDtypeStruct((bd, D_MLSTM), BF16)],
        compiler_params=pltpu.CompilerParams(vmem_limit_bytes=VMEM_LIMIT),
        name="sample_post",
    )(proj, y, xs, dskip_rep, wns, qc, gq, sw, den, wnm)


def _lane_vec(v, lane0):
    return jnp.pad(v.astype(F32), (lane0, LANE - lane0 - v.shape[0]))[None, :]


def _pick(n, candidates):
    for c in candidates:
        if n % c == 0:
            return c
    return n


def kernel(x_prompt, x_sample, cache_swa_k, cache_swa_v, state_conv, state_ssm, state_mlstm_C, state_mlstm_n,
           state_mlstm_m, w_norm_mix, w_in, attn_sinks, conv_w, conv_b, dt_bias, a_log, d_skip, w_norm_ssm,
           igate_b, fgate_b, w_norm_mlstm, w_out, w_norm_mlp, w_up, w_down, w_norm_final):
    bsz, seq, d_model = x_prompt.shape
    bd = x_sample.shape[0]
    depth = w_in.shape[0]
    win = cache_swa_k.shape[2]
    assert x_sample.shape[1] == 1 and seq % CHUNK == 0 and d_model == D_MODEL

    mp = bsz * seq
    tm_mlp = _pick(mp, (1024, 512, 256, 128))
    tb = _pick(bd, (16, 8))

    hp = x_prompt.reshape(mp, d_model)
    hs = x_sample.reshape(bd, d_model)
    tab_p = _rope_tables(jnp.arange(seq, dtype=jnp.int32))
    tab_s = _rope_tables(jnp.arange(1, dtype=jnp.int32) + PAST_LEN)
    gf = w_norm_final[None, :]
    kc_all = jnp.transpose(cache_swa_k, (0, 1, 3, 4, 2)).reshape(depth, bd, D_KV, win)
    vc_all = jnp.transpose(cache_swa_v, (0, 1, 3, 4, 2)).reshape(depth, bd, D_KV, win)
    ssm_all = state_ssm.reshape(depth, bd, D_SSM, D_STATE)
    mem_all = state_mlstm_C.reshape(depth, bd, D_MLSTM, MLSTM_HEAD_DIM)
    conv_all = jnp.swapaxes(state_conv, 1, 2)

    st_p, st_s = [], []
    kv_new = s_new = c_new = None
    for l in range(depth):
        w_in_l = _prep_w_in(w_in, l, 256)
        g_mix = w_norm_mix[l][None, :]
        g_mlp = w_norm_mlp[l][None, :]
        cw = conv_w[l]
        cb = conv_b[l][None, :]
        dtb = _lane_vec(dt_bias[l], 0)
        alog = _lane_vec(a_log[l], 0)
        dskip_rep = jnp.repeat(d_skip[l].astype(F32), SSM_HEAD_DIM)[None, :]
        wns = w_norm_ssm[l][None, :]
        bi = _lane_vec(igate_b[l], GATE_LANE)
        bf = _lane_vec(fgate_b[l], GATE_LANE)
        wnm = w_norm_mlstm[l][None, :]
        sinks = attn_sinks[l].reshape(ATT_HEADS).astype(F32)
        last = l == depth - 1

        att, pk, pv, y, p_ssm, hm, p_c, p_n, p_m, p_tail = _front(
            hp, g_mix, w_in_l, sinks, tab_p, cw, cb, dtb, alog, dskip_rep, wns, bi, bf, wnm, bsz, seq)
        x1 = _outproj(att, y, hm, hp, w_out, l, _pick(mp, (512, 256, 128)))
        p_conv = p_tail[:, CONV_PAD - (CONV_WIDTH - 1):, :]
        st_p.append((pk.reshape(bsz, WINDOW, ATT_KV_HEADS, ATT_HEAD_DIM),
                     pv.reshape(bsz, WINDOW, ATT_KV_HEADS, ATT_HEAD_DIM),
                     p_conv,
                     p_ssm.reshape(bsz, SSM_HEADS, SSM_HEAD_DIM, D_STATE),
                     p_c.reshape(bsz, MLSTM_HEADS, MLSTM_HEAD_DIM, MLSTM_HEAD_DIM),
                     p_n[:, 0:MLSTM_HEADS, :],
                     p_m[:, 0:MLSTM_HEADS, 0]))

        m_tile = jnp.pad(state_mlstm_m[l], ((0, 0), (GATE_LANE, LANE - GATE_LANE - MLSTM_HEADS)))
        (proj_s, conv_new, xs, xdt, bm, cm, ea, ks, n_new, gq, esw, sw, den, mt) = _sample_pre(
            hs, g_mix, w_in_l, conv_all, l, cw, cb, dtb, alog, bi, bf, state_mlstm_n[l].reshape(bd, D_MLSTM),
            m_tile)
        q8 = proj_s[:, C_Q:C_Q + D_ATT].reshape(bd, ATT_HEADS, ATT_HEAD_DIM)
        zero = jnp.zeros_like(q8)
        first_kv = (jnp.arange(ATT_HEADS) < ATT_GROUP)[None, :, None]
        qb = jnp.where(first_kv, jnp.concatenate([q8, zero], -1), jnp.concatenate([zero, q8], -1))
        kn = proj_s[:, C_K:C_K + D_KV].reshape(bd, 1, D_KV)
        vn = proj_s[:, C_V:C_V + D_KV].reshape(bd, 1, D_KV)
        sink_rows = jnp.broadcast_to(sinks[:, None], (ATT_HEADS, LANE))
        att_s, sk, sv = _attn_sample(qb, kn, vn, kc_all, vc_all, sink_rows, tab_s, l, kv_new, tb)
        kv_new = (sk, sv)
        side = (ea, ssm_all, xdt, bm, cm, gq, esw, mem_all, proj_s, ks, s_new, c_new)
        hp, s_new, y_read, c_new, qc = _mlp(x1, g_mlp, w_up, w_down, gf, l, last, tm_mlp, 512, side)
        ys, hms = _sample_post(proj_s, y_read, xs, dskip_rep, wns, qc, gq, sw, den, wnm)
        x1s = _outproj(att_s.reshape(bd, D_ATT), ys, hms, hs, w_out, l, bd)
        hs = _mlp(x1s, g_mlp, w_up, w_down, gf, l, last, bd, 512)
        st_s.append((conv_new,
                     n_new.reshape(bd, MLSTM_HEADS, MLSTM_HEAD_DIM),
                     mt[:, GATE_LANE:GATE_LANE + MLSTM_HEADS]))

    y_prompt = hp.reshape(bsz, seq, d_model)
    y_sample = hs.reshape(bd, 1, d_model)
    p_out = [jnp.stack([s[i] for s in st_p]) for i in range(7)]
    s_conv, s_n, s_m = [jnp.stack([s[i] for s in st_s]) for i in range(3)]
    s_conv = jnp.swapaxes(s_conv, 1, 2)
    to_cache = lambda a: jnp.transpose(a.reshape(depth, bd, ATT_KV_HEADS, ATT_HEAD_DIM, win), (0, 1, 4, 2, 3))
    s_k = to_cache(kv_new[0])
    s_v = to_cache(kv_new[1])
    s_ssm = s_new.reshape(depth, bd, SSM_HEADS, SSM_HEAD_DIM, D_STATE)
    s_c = c_new.reshape(depth, bd, MLSTM_HEADS, MLSTM_HEAD_DIM, MLSTM_HEAD_DIM)
    return (y_prompt, y_sample, *p_out, s_k, s_v, s_conv, s_ssm, s_c, s_n, s_m)
```

```python
import functools

import jax
import jax.numpy as jnp
from jax import lax
from jax.experimental import pallas as pl
from jax.experimental.pallas import tpu as pltpu

F32 = jnp.float32
BF16 = jnp.bfloat16

D_MODEL = 2048
EPS = 1e-6
PAST_LEN = 8192
ATT_HEAD_DIM = 64
ATT_HEADS = 8
ATT_KV_HEADS = 2
ATT_GROUP = ATT_HEADS // ATT_KV_HEADS
D_ATT = ATT_HEADS * ATT_HEAD_DIM
D_KV = ATT_KV_HEADS * ATT_HEAD_DIM
WINDOW = 128
ROPE_THETA = 500000.0
ROPE_DIM = ATT_HEAD_DIM // 4
SSM_HEAD_DIM = 64
SSM_HEADS = 16
D_SSM = SSM_HEADS * SSM_HEAD_DIM
SSM_GROUPS = 2
D_STATE = 128
CONV_WIDTH = 4
CONV_DIM = D_SSM + 2 * SSM_GROUPS * D_STATE
CHUNK = 128
MLSTM_HEADS = 4
MLSTM_HEAD_DIM = 128
D_MLSTM = MLSTM_HEADS * MLSTM_HEAD_DIM
D_FF = 4 * D_MODEL

C_XBC = 0
C_Q = 1536
C_Z = 2048
C_MQ = 3072
C_MK = 3584
C_MV = 4096
C_MO = 4608
C_K = 5120
C_V = 5248
C_G = 5376
NP = 5632
GATE_LANE = 16

LANE = 128
VMEM_LIMIT = 60 * 1024 * 1024


def _cparams(sem):
    return pltpu.CompilerParams(dimension_semantics=sem, vmem_limit_bytes=VMEM_LIMIT)


def _silu(x):
    return x * jax.nn.sigmoid(x)


def _dot(a, b):
    return jnp.dot(a.astype(BF16), b.astype(BF16), preferred_element_type=F32)


def _dot_nt(a, b):
    return lax.dot_general(a.astype(BF16), b.astype(BF16), (((1,), (1,)), ((), ())),
                           preferred_element_type=F32)


def _dot_tn(a, b):
    return lax.dot_general(a.astype(BF16), b.astype(BF16), (((0,), (0,)), ((), ())),
                           preferred_element_type=F32)


def _tril(n):
    row = lax.broadcasted_iota(jnp.int32, (n, n), 0)
    col = lax.broadcasted_iota(jnp.int32, (n, n), 1)
    return row >= col


def _cumsum_rows(x):
    row = lax.broadcasted_iota(jnp.int32, x.shape, 0)
    step = 1
    while step < x.shape[0]:
        x = x + jnp.where(row >= step, pltpu.roll(x, step, 0), 0.0)
        step *= 2
    return x


_SRC_Q, _SRC_K, _SRC_V, _SRC_Z, _SRC_XBC, _SRC_DT, _SRC_MQ = 0, 512, 640, 768, 1792, 3328, 3344
_SRC_MI = _SRC_MQ + 4 * D_MLSTM
IN_WIDTH = _SRC_MI + 2 * MLSTM_HEADS


def _prep_w_in_kernel(w_ref, o_ref):
    def put(dst, src, n):
        o_ref[:, dst:dst + n] = w_ref[src:src + n, :].T.astype(BF16)

    put(C_XBC, _SRC_XBC, CONV_DIM)
    put(C_Q, _SRC_Q, D_ATT)
    put(C_Z, _SRC_Z, D_SSM)
    put(C_MQ, _SRC_MQ, 4 * D_MLSTM)
    put(C_K, _SRC_K, D_KV)
    put(C_V, _SRC_V, D_KV)
    tk = w_ref.shape[1]
    dt = w_ref[_SRC_DT:_SRC_DT + SSM_HEADS, :]
    gates = w_ref[_SRC_MI:_SRC_MI + 2 * MLSTM_HEADS, :]
    pad = jnp.zeros((LANE - GATE_LANE - 2 * MLSTM_HEADS, tk), F32)
    lo = jnp.concatenate([dt, gates, pad], axis=0)
    hi = jnp.concatenate([jnp.zeros_like(dt), pltpu.roll(gates, MLSTM_HEADS, 0), pad], axis=0)
    o_ref[:, C_G:C_G + LANE] = lo.T.astype(BF16)
    o_ref[:, C_G + LANE:C_G + 2 * LANE] = hi.T.astype(BF16)


def _prep_w_in(w_in, layer, tk):
    depth, k, n = w_in.shape
    assert n == IN_WIDTH and n % 8 == 0
    wt = jnp.swapaxes(w_in, 1, 2).reshape(depth * n, k)
    return pl.pallas_call(
        _prep_w_in_kernel,
        grid=(k // tk,),
        in_specs=[pl.BlockSpec((n, tk), lambda i: (layer, i))],
        out_specs=pl.BlockSpec((tk, NP), lambda i: (i, 0)),
        out_shape=jax.ShapeDtypeStruct((k, NP), BF16),
        compiler_params=_cparams(("parallel",)),
        name="prep_w_in",
    )(wt)


def _outproj_kernel(a_ref, y_ref, h_ref, x_ref, w_ref, o_ref, wb_ref):
    @pl.when(pl.program_id(0) == 0)
    def _():
        wb_ref[...] = w_ref[...].astype(BF16)

    acc = jnp.dot(a_ref[...], wb_ref[0:D_ATT, :], preferred_element_type=F32)
    acc = acc + jnp.dot(y_ref[...], wb_ref[D_ATT:D_ATT + D_SSM, :], preferred_element_type=F32)
    acc = acc + jnp.dot(h_ref[...], wb_ref[D_ATT + D_SSM:, :], preferred_element_type=F32)
    o_ref[...] = x_ref[...] + acc


def _outproj(att, y, h, x, w, layer, tm):
    m, n = x.shape
    k = w.shape[1]
    rows = lambda width: pl.BlockSpec((tm, width), lambda i: (i, 0))
    return pl.pallas_call(
        _outproj_kernel,
        grid=(m // tm,),
        in_specs=[rows(D_ATT), rows(D_SSM), rows(D_MLSTM), rows(n),
                  pl.BlockSpec((None, k, n), lambda i: (layer, 0, 0), pipeline_mode=pl.Buffered(1))],
        out_specs=rows(n),
        out_shape=jax.ShapeDtypeStruct((m, n), F32),
        scratch_shapes=[pltpu.VMEM((k, n), BF16)],
        compiler_params=_cparams(("arbitrary",)),
        name="outproj",
    )(att, y, h, x, w)


N_MLP_IN, N_SSM_SIDE_IN, N_MLSTM_SIDE_IN = 5, 5, 6


def _mlp_kernel(*refs, final_norm, side, n_prev):
    x_ref, g_ref, wu_ref, wd_ref, gf_ref = refs[:N_MLP_IN]
    n_side = N_SSM_SIDE_IN + N_MLSTM_SIDE_IN if side else 0
    side_in = refs[N_MLP_IN:N_MLP_IN + n_side]
    outs = refs[N_MLP_IN + n_side + n_prev:]
    o_ref = outs[0]
    u_ref = outs[5] if side else outs[1]
    j = pl.program_id(1)

    @pl.when(j == 0)
    def _():
        x = x_ref[...]
        ms = jnp.mean(x * x, axis=-1, keepdims=True)
        u_ref[...] = (x * lax.rsqrt(ms + EPS) * g_ref[...]).astype(BF16)
        o_ref[...] = x
        if side:
            outs[6][...] = jnp.zeros_like(outs[6])
            outs[7][...] = jnp.zeros_like(outs[7])

    h = _dot(u_ref[...], wu_ref[...])
    h = jnp.square(jnp.maximum(h, 0.0))
    o_ref[...] += _dot(h, wd_ref[...])

    if side:
        seq = pl.program_id(0) * pl.num_programs(1) + j
        row = seq % SEQ_ROWS
        so_ref, y_ref, co_ref, qc_ref = outs[1:5]
        yacc_ref, qacc_ref = outs[6:8]
        _ssm_sample_one(0, row, seq, *side_in[:N_SSM_SIDE_IN], so_ref, y_ref, yacc_ref)
        _mlstm_sample_one(0, row, seq, *side_in[N_SSM_SIDE_IN:], co_ref, qc_ref, qacc_ref)

    if final_norm:
        @pl.when(j == pl.num_programs(1) - 1)
        def _():
            y = o_ref[...]
            ms = jnp.mean(y * y, axis=-1, keepdims=True)
            o_ref[...] = y * lax.rsqrt(ms + EPS) * gf_ref[...]


def _mlp(x, g, wu, wd, gf, layer, final_norm, tm, tf, side=None):
    m, d = x.shape
    ff = wu.shape[2]
    nj = ff // tf
    in_specs = [pl.BlockSpec((tm, d), lambda i, j: (i, 0)),
                pl.BlockSpec((1, d), lambda i, j: (0, 0)),
                pl.BlockSpec((None, d, tf), lambda i, j: (layer, 0, j)),
                pl.BlockSpec((None, tf, d), lambda i, j: (layer, j, 0)),
                pl.BlockSpec((1, d), lambda i, j: (0, 0))]
    out_specs = [pl.BlockSpec((tm, d), lambda i, j: (i, 0))]
    out_shape = [jax.ShapeDtypeStruct((m, d), F32)]
    operands = [x, g, wu, wd, gf]
    aliases = {}
    n_prev = 0
    if side is not None:
        ea, ssm, xdt, bvec, cvec, gq, esw, mem, proj_s, ks, prev_s, prev_c = side
        depth, bd = ssm.shape[0], ssm.shape[1]
        assert (m // tm) * nj == bd and nj % SEQ_ROWS == 0
        smem = pl.BlockSpec(memory_space=pltpu.SMEM)
        rows = lambda n, col=0: pl.BlockSpec((SEQ_ROWS, n), lambda i, j: ((i * nj + j) // SEQ_ROWS, col // n))
        state = lambda r, n: pl.BlockSpec((None, 1, r, n), lambda i, j: (layer, i * nj + j, 0, 0))
        in_specs += [smem, state(D_SSM, D_STATE), rows(D_SSM), rows(SSM_GROUPS * D_STATE),
                     rows(SSM_GROUPS * D_STATE),
                     smem, smem, state(D_MLSTM, MLSTM_HEAD_DIM), rows(D_MLSTM, C_MQ), rows(D_MLSTM),
                     rows(D_MLSTM, C_MV)]
        operands += [ea, ssm, xdt, bvec, cvec, gq, esw, mem, proj_s, ks, proj_s]
        out_specs += [state(D_SSM, D_STATE), rows(D_SSM), state(D_MLSTM, MLSTM_HEAD_DIM), rows(D_MLSTM)]
        out_shape += [jax.ShapeDtypeStruct((depth, bd, D_SSM, D_STATE), F32),
                      jax.ShapeDtypeStruct((bd, D_SSM), F32),
                      jax.ShapeDtypeStruct((depth, bd, D_MLSTM, MLSTM_HEAD_DIM), F32),
                      jax.ShapeDtypeStruct((bd, D_MLSTM), F32)]
        if prev_s is not None:
            aliases = {len(operands): 1, len(operands) + 1: 3}
            operands += [prev_s, prev_c]
            in_specs += [pl.BlockSpec(memory_space=pl.ANY)] * 2
            n_prev = 2
    res = pl.pallas_call(
        functools.partial(_mlp_kernel, final_norm=final_norm, side=side is not None, n_prev=n_prev),
        grid=(m // tm, nj),
        in_specs=in_specs,
        out_specs=out_specs,
        out_shape=out_shape,
        input_output_aliases=aliases,
        scratch_shapes=[pltpu.VMEM((tm, d), BF16)] + ([] if side is None else [
            pltpu.VMEM((SEQ_ROWS, D_SSM), F32), pltpu.VMEM((SEQ_ROWS, D_MLSTM), F32)]),
        compiler_params=_cparams(("parallel", "arbitrary")),
        name="mlp",
    )(*operands)
    return res[0] if side is None else res


def _rope_tables(pos):
    half = ROPE_DIM // 2
    inv = jnp.power(jnp.float32(ROPE_THETA), -jnp.arange(half, dtype=jnp.float32) / half)
    ang = pos.astype(jnp.float32)[:, None] * inv[None, :]
    cos = jnp.cos(ang)
    sin = jnp.sin(ang)
    n = pos.shape[0]
    rest = ATT_HEAD_DIM - ROPE_DIM
    c = jnp.concatenate([cos, cos, jnp.ones((n, rest), F32)], axis=1)
    s1 = jnp.concatenate([-sin, jnp.zeros((n, half + rest), F32)], axis=1)
    s2 = jnp.concatenate([jnp.zeros((n, half), F32), sin, jnp.zeros((n, rest), F32)], axis=1)
    rep = LANE // ATT_HEAD_DIM
    return jnp.tile(c, (1, rep)), jnp.tile(s1, (1, rep)), jnp.tile(s2, (1, rep))


def _rope(x, c, s1, s2):
    width = x.shape[-1]
    rep = width // LANE
    half = ROPE_DIM // 2
    if rep > 1:
        c = jnp.concatenate([c] * rep, axis=-1)
        s1 = jnp.concatenate([s1] * rep, axis=-1)
        s2 = jnp.concatenate([s2] * rep, axis=-1)
    axis = x.ndim - 1
    return x * c + pltpu.roll(x, width - half, axis) * s1 + pltpu.roll(x, half, axis) * s2


def _attn_prompt_init(ko_ref, vo_ref):
    ko_ref[...] = jnp.zeros_like(ko_ref)
    vo_ref[...] = jnp.zeros_like(vo_ref)


def _attn_prompt_body(j, sink_ref, q_ref, k_ref, v_ref, c_ref, s1_ref, s2_ref, o_ref, ko_ref, vo_ref):
    w = WINDOW
    c = c_ref[...]
    s1 = s1_ref[...]
    s2 = s2_ref[...]
    krot = _rope(k_ref[...], c, s1, s2)
    v = v_ref[...]
    qrot = _rope(q_ref[...], c, s1, s2)
    kk = jnp.concatenate([ko_ref[...], krot], axis=0).astype(BF16)
    vv = jnp.concatenate([vo_ref[...], v], axis=0).astype(BF16)

    row = lax.broadcasted_iota(jnp.int32, (w, 2 * w), 0)
    col = lax.broadcasted_iota(jnp.int32, (w, 2 * w), 1)
    first_col = jnp.where(j > 0, 0, w)
    mask = (col >= row) & (col <= row + w) & (col >= first_col)
    scale = ATT_HEAD_DIM ** -0.5

    for h in range(ATT_KV_HEADS):
        kh = kk[:, h * ATT_HEAD_DIM:(h + 1) * ATT_HEAD_DIM]
        vh = vv[:, h * ATT_HEAD_DIM:(h + 1) * ATT_HEAD_DIM]
        for g in range(ATT_GROUP):
            hg = h * ATT_GROUP + g
            qh = qrot[:, hg * ATT_HEAD_DIM:(hg + 1) * ATT_HEAD_DIM]
            s = _dot_nt(qh, kh) * scale
            s = jnp.where(mask, s, -jnp.inf)
            sink = sink_ref[hg]
            m = jnp.maximum(jnp.max(s, axis=-1, keepdims=True), sink)
            p = jnp.exp(s - m)
            denom = jnp.sum(p, axis=-1, keepdims=True) + jnp.exp(sink - m)
            o = _dot(p, vh) / denom
            o_ref[:, hg * ATT_HEAD_DIM:(hg + 1) * ATT_HEAD_DIM] = o.astype(o_ref.dtype)
            yield

    ko_ref[...] = krot
    vo_ref[...] = v


def _group_rmsnorm(y, w, groups):
    width = y.shape[-1] // groups
    outs = []
    for g in range(groups):
        yg = y[:, g * width:(g + 1) * width]
        ms = jnp.mean(yg * yg, axis=-1, keepdims=True)
        outs.append(yg * lax.rsqrt(ms + EPS) * w[:, g * width:(g + 1) * width])
    return jnp.concatenate(outs, axis=-1)


CONV_PAD = 8


def _ssd_prompt_init(s_ref, tail_ref):
    tail_ref[...] = jnp.zeros_like(tail_ref)
    s_ref[...] = jnp.zeros_like(s_ref)


def _ssd_prompt_body(xbc_ref, z_ref, g_ref, cw_ref, cb_ref, dtb_ref, alog_ref, dskip_ref, wn_ref,
                     y_ref, s_ref, tail_ref, ybuf_ref, xw_ref):
    q = CHUNK
    pad = CONV_PAD
    hp = SSM_HEAD_DIM
    heads_per_group = SSM_HEADS // SSM_GROUPS
    gw = heads_per_group * hp

    x = xbc_ref[...]
    cw = cw_ref[...]
    row8 = lax.broadcasted_iota(jnp.int32, (pad, CONV_DIM), 0)
    acc = None
    for j in range(CONV_WIDTH - 1):
        shift = CONV_WIDTH - 1 - j
        rolled = pltpu.roll(x, shift, 0)
        head = jnp.where(row8 < shift, tail_ref[j * pad:(j + 1) * pad, :], rolled[0:pad, :])
        tail_ref[j * pad:(j + 1) * pad, :] = rolled[0:pad, :]
        term = jnp.concatenate([head, rolled[pad:, :]], axis=0) * cw[j:j + 1, :]
        acc = term if acc is None else acc + term
    acc = acc + x * cw[CONV_WIDTH - 1:CONV_WIDTH, :]
    xc = _silu(acc + cb_ref[...])

    xs = xc[:, 0:D_SSM]
    bm = xc[:, D_SSM:D_SSM + SSM_GROUPS * D_STATE]
    cm = xc[:, D_SSM + SSM_GROUPS * D_STATE:]

    dt = jax.nn.softplus(g_ref[:, 0:LANE] + dtb_ref[...])
    a_neg = -jnp.exp(alog_ref[...])
    a_col = _cumsum_rows(dt * a_neg)
    a_row = a_col.T
    dt_row = dt.T
    wk_row = jnp.exp(a_row[:, q - 1:q] - a_row) * dt_row
    tril = _tril(q)
    xs_t = xs.T
    ea_col = jnp.exp(a_col)
    left = lax.broadcasted_iota(jnp.int32, (q, LANE), 1) < hp
    ea_full = jnp.concatenate(
        [jnp.where(left, ea_col[:, 2 * t:2 * t + 1], ea_col[:, 2 * t + 1:2 * t + 2]) for t in range(SSM_HEADS // 2)],
        axis=1)
    yield

    cs_parts = []
    for g in range(SSM_GROUPS):
        bg = bm[:, g * D_STATE:(g + 1) * D_STATE]
        cg = cm[:, g * D_STATE:(g + 1) * D_STATE]
        cb = _dot_nt(cg, bg)
        cs_parts.append(_dot_nt(cg, s_ref[g * gw:(g + 1) * gw, :]))
        for r in range(heads_per_group):
            h = g * heads_per_group + r
            seg = a_col[:, h:h + 1] - a_row[h:h + 1, :]
            wmat = jnp.exp(jnp.where(tril, seg, -jnp.inf)) * cb * dt_row[h:h + 1, :]
            ybuf_ref[:, h * hp:(h + 1) * hp] = _dot(wmat, xs[:, h * hp:(h + 1) * hp])
            xw_ref[r * hp:(r + 1) * hp, :] = xs_t[h * hp:(h + 1) * hp, :] * wk_row[h:h + 1, :]
            yield
        upd = _dot(xw_ref[...], bg)
        for r in range(heads_per_group):
            h = g * heads_per_group + r
            decay = jnp.exp(a_row[h:h + 1, q - 1:q])
            s_ref[h * hp:(h + 1) * hp, :] = s_ref[h * hp:(h + 1) * hp, :] * decay + upd[r * hp:(r + 1) * hp, :]
        yield

    y = ybuf_ref[...] + jnp.concatenate(cs_parts, axis=1) * ea_full
    y = y + dskip_ref[...] * xs
    y = y * _silu(z_ref[...])
    y_ref[...] = _group_rmsnorm(y, wn_ref[...], SSM_GROUPS).astype(y_ref.dtype)


def _mlstm_prompt_init(c_ref, n_ref, m_ref):
    c_ref[...] = jnp.zeros_like(c_ref)
    n_ref[...] = jnp.zeros_like(n_ref)
    m_ref[...] = jnp.zeros_like(m_ref)


def _mlstm_prompt_body(q_ref, k_ref, v_ref, o_ref, g_ref, bi_ref, bf_ref, wn_ref, h_ref, c_ref, n_ref, m_ref):
    t = CHUNK
    d = MLSTM_HEAD_DIM

    gi = g_ref[:, 0:LANE] + bi_ref[...]
    gf = g_ref[:, LANE:2 * LANE] + bf_ref[...]
    b_col = _cumsum_rows(jax.nn.log_sigmoid(gf))
    b_row = b_col.T
    i_row = gi.T
    tril = _tril(t)
    k_all = k_ref[...] * (d ** -0.5)

    for h in range(MLSTM_HEADS):
        ln = GATE_LANE + h
        bq = b_col[:, ln:ln + 1]
        logw = jnp.where(tril, bq - b_row[ln:ln + 1, :] + i_row[ln:ln + 1, :], -jnp.inf)
        m_prev = m_ref[h:h + 1, 0:1]
        log_inter = bq + m_prev
        mt = jnp.maximum(log_inter, jnp.max(logw, axis=-1, keepdims=True))
        qh = q_ref[:, h * d:(h + 1) * d]
        kh = k_all[:, h * d:(h + 1) * d]
        vh = v_ref[:, h * d:(h + 1) * d]
        ch = c_ref[h * d:(h + 1) * d, :]
        nh = n_ref[h:h + 1, :]
        sw = jnp.exp(logw - mt) * _dot_nt(qh, kh)
        gq = jnp.exp(log_inter - mt)
        num = _dot(sw, vh) + _dot(qh, ch) * gq
        den = jnp.sum(sw, axis=-1, keepdims=True) + jnp.sum(qh * nh, axis=-1, keepdims=True) * gq
        hh = num / jnp.maximum(jnp.abs(den), jnp.exp(-mt))

        m_new = mt[t - 1:t, :]
        b_last = b_col[t - 1:t, ln:ln + 1]
        wk = jnp.exp(b_last - bq + gi[:, ln:ln + 1] - m_new)
        g_end = jnp.exp(b_last + m_prev - m_new)
        kw = kh * wk
        c_ref[h * d:(h + 1) * d, :] = ch * g_end + _dot_tn(kw, vh)
        n_ref[h:h + 1, :] = nh * g_end + jnp.sum(kw, axis=0, keepdims=True)
        m_ref[h:h + 1, :] = jnp.broadcast_to(m_new, (1, LANE))

        ms = jnp.mean(hh * hh, axis=-1, keepdims=True)
        hn = hh * lax.rsqrt(ms + EPS) * wn_ref[:, h * d:(h + 1) * d]
        h_ref[:, h * d:(h + 1) * d] = (hn * jax.nn.sigmoid(o_ref[:, h * d:(h + 1) * d])).astype(h_ref.dtype)
        yield

def _interleave(stages):
    live = [[gen, 0, n] for gen, n in stages]
    while live:
        item = min(live, key=lambda s: s[1] / s[2])
        try:
            next(item[0])
            item[1] += 1
        except StopIteration:
            live.remove(item)


PROJ_TILE = 256


def _project_next(x_ref, g_ref, w_ref, pnext_ref):
    x = x_ref[...]
    u = (x * lax.rsqrt(jnp.mean(x * x, axis=-1, keepdims=True) + EPS) * g_ref[...]).astype(BF16)
    for ct in range(NP // PROJ_TILE):
        cols = slice(ct * PROJ_TILE, (ct + 1) * PROJ_TILE)
        pnext_ref[:, cols] = jnp.dot(u, w_ref[:, cols], preferred_element_type=F32)
        yield


def _front_kernel(sink_ref, x_ref, x0_ref, g_ref, w_ref, c_ref, s1_ref, s2_ref,
                  cw_ref, cb_ref, dtb_ref, alog_ref, dskip_ref, wns_ref, bi_ref, bf_ref, wnm_ref,
                  att_ref, ko_ref, vo_ref, y_ref, s_ref, h_ref, mem_ref, n_ref, m_ref, ptail_ref,
                  pcur_ref, pnext_ref, tail_ref, ybuf_ref, xw_ref, *, nblk):
    j = pl.program_id(1)
    first = (pl.program_id(0) == 0) & (j == 0)
    t = CHUNK

    @pl.when(first)
    def _():
        for _ in _project_next(x0_ref, g_ref, w_ref, pcur_ref):
            pass

    @pl.when(jnp.logical_not(first))
    def _():
        pcur_ref[...] = pnext_ref[...]

    @pl.when(j == 0)
    def _():
        _attn_prompt_init(ko_ref, vo_ref)
        _ssd_prompt_init(s_ref, tail_ref)
        _mlstm_prompt_init(mem_ref, n_ref, m_ref)

    def attn(i):
        seg = lambda col, width: pcur_ref.at[i * t:(i + 1) * t, col:col + width]
        rows = lambda ref: ref.at[i * t:(i + 1) * t, :]
        return _attn_prompt_body(j * nblk + i, sink_ref, seg(C_Q, D_ATT), seg(C_K, D_KV), seg(C_V, D_KV),
                                 rows(c_ref), rows(s1_ref), rows(s2_ref), rows(att_ref), ko_ref, vo_ref)

    def ssd(i):
        seg = lambda col, width: pcur_ref.at[i * t:(i + 1) * t, col:col + width]
        return _ssd_prompt_body(seg(C_XBC, CONV_DIM), seg(C_Z, D_SSM), seg(C_G, 2 * LANE), cw_ref, cb_ref, dtb_ref,
                                alog_ref, dskip_ref, wns_ref, y_ref.at[i * t:(i + 1) * t, :], s_ref,
                                tail_ref, ybuf_ref, xw_ref)

    def mlstm(i):
        seg = lambda col, width: pcur_ref.at[i * t:(i + 1) * t, col:col + width]
        return _mlstm_prompt_body(seg(C_MQ, D_MLSTM), seg(C_MK, D_MLSTM), seg(C_MV, D_MLSTM), seg(C_MO, D_MLSTM),
                                  seg(C_G, 2 * LANE), bi_ref, bf_ref, wnm_ref, h_ref.at[i * t:(i + 1) * t, :],
                                  mem_ref, n_ref, m_ref)

    def blocks(body):
        for i in range(nblk):
            yield from body(i)

    ptail_ref[...] = pcur_ref[nblk * t - CONV_PAD:nblk * t, C_XBC:C_XBC + CONV_DIM]
    _interleave([
        (_project_next(x_ref, g_ref, w_ref, pnext_ref), NP // PROJ_TILE + 1),
        (blocks(attn), nblk * (ATT_HEADS + 1)),
        (blocks(ssd), nblk * (SSM_HEADS + SSM_GROUPS + 2)),
        (blocks(mlstm), nblk * (MLSTM_HEADS + 1)),
    ])


def _front(x, g_mix, w, sinks, tables, cw, cb, dtb, alog, dskip_rep, wns, bi, bf, wnm, bsz, seq):
    nblk = 2 if (seq // CHUNK) % 2 == 0 else 1
    t = nblk * CHUNK
    nb = seq // t
    total = bsz * nb
    c, s1, s2 = tables
    row = lambda b, j: b * nb + j
    full = lambda shape: pl.BlockSpec(shape, lambda b, j: (0,) * len(shape))
    vec = lambda n: full((1, n))
    tab = pl.BlockSpec((t, LANE), lambda b, j: (j, 0))
    rows_out = lambda width: pl.BlockSpec((t, width), lambda b, j: (row(b, j), 0))
    per_seq = lambda r, n: pl.BlockSpec((None, r, n), lambda b, j: (b, 0, 0))
    return pl.pallas_call(
        functools.partial(_front_kernel, nblk=nblk),
        grid=(bsz, nb),
        in_specs=[pl.BlockSpec(memory_space=pltpu.SMEM),
                  pl.BlockSpec((t, D_MODEL), lambda b, j: (jnp.minimum(row(b, j) + 1, total - 1), 0)),
                  pl.BlockSpec((t, D_MODEL), lambda b, j: (0, 0), pipeline_mode=pl.Buffered(1)),
                  vec(D_MODEL),
                  pl.BlockSpec((D_MODEL, NP), lambda b, j: (0, 0), pipeline_mode=pl.Buffered(1)),
                  tab, tab, tab,
                  full((CONV_WIDTH, CONV_DIM)), vec(CONV_DIM), vec(LANE), vec(LANE), vec(D_SSM), vec(D_SSM),
                  vec(LANE), vec(LANE), vec(D_MLSTM)],
        out_specs=[rows_out(D_ATT), per_seq(CHUNK, D_KV), per_seq(CHUNK, D_KV),
                   rows_out(D_SSM), per_seq(D_SSM, D_STATE),
                   rows_out(D_MLSTM), per_seq(D_MLSTM, MLSTM_HEAD_DIM), per_seq(8, LANE), per_seq(8, LANE),
                   per_seq(CONV_PAD, CONV_DIM)],
        out_shape=[jax.ShapeDtypeStruct((bsz * seq, D_ATT), BF16),
                   jax.ShapeDtypeStruct((bsz, CHUNK, D_KV), F32),
                   jax.ShapeDtypeStruct((bsz, CHUNK, D_KV), F32),
                   jax.ShapeDtypeStruct((bsz * seq, D_SSM), BF16),
                   jax.ShapeDtypeStruct((bsz, D_SSM, D_STATE), F32),
                   jax.ShapeDtypeStruct((bsz * seq, D_MLSTM), BF16),
                   jax.ShapeDtypeStruct((bsz, D_MLSTM, MLSTM_HEAD_DIM), F32),
                   jax.ShapeDtypeStruct((bsz, 8, LANE), F32),
                   jax.ShapeDtypeStruct((bsz, 8, LANE), F32),
                   jax.ShapeDtypeStruct((bsz, CONV_PAD, CONV_DIM), F32)],
        scratch_shapes=[pltpu.VMEM((t, NP), F32),
                        pltpu.VMEM((t, NP), F32),
                        pltpu.VMEM(((CONV_WIDTH - 1) * CONV_PAD, CONV_DIM), F32),
                        pltpu.VMEM((CHUNK, D_SSM), F32),
                        pltpu.VMEM((D_SSM // SSM_GROUPS, CHUNK), F32)],
        compiler_params=_cparams(("arbitrary", "arbitrary")),
        name="front",
    )(sinks, x, x, g_mix, w, c, s1, s2, cw, cb, dtb, alog, dskip_rep, wns, bi, bf, wnm)


def _stacked_out(prev, depth, shape, block, layer):
    spec = pl.BlockSpec((None,) + block, lambda i: (layer,) + (i,) + (0,) * (len(block) - 1))
    out_shape = jax.ShapeDtypeStruct((depth,) + shape, F32)
    extra_in = [] if prev is None else [prev]
    extra_spec = [] if prev is None else [pl.BlockSpec(memory_space=pl.ANY)]
    return spec, out_shape, extra_in, extra_spec


def _drop_aliased(kernel_fn, n_in, n_prev):
    def wrapped(*refs):
        return kernel_fn(*refs[:n_in], *refs[n_in + n_prev:])
    return wrapped


def _attn_sample_kernel(q_ref, k_ref, v_ref, kc_ref, vc_ref, sink_ref, c_ref, s1_ref, s2_ref,
                        o_ref, ko_ref, vo_ref):
    tb, _, w = kc_ref.shape
    c = c_ref[...]
    s1 = s1_ref[...]
    s2 = s2_ref[...]
    qb = _rope(q_ref[...], c, s1, s2)
    kn = _rope(k_ref[...], c, s1, s2)
    vn = v_ref[...]
    kc = kc_ref[...]
    vc = vc_ref[...]
    newest = lax.broadcasted_iota(jnp.int32, (D_KV, w), 1) == w - 1
    for b in range(tb):
        ko_ref[b] = jnp.where(newest, _column_tile(kn[b]), pltpu.roll(kc[b], w - 1, 1))
        vo_ref[b] = jnp.where(newest, _column_tile(vn[b]), pltpu.roll(vc[b], w - 1, 1))

    scale = ATT_HEAD_DIM ** -0.5
    s = jnp.einsum("bhd,bdk->bhk", qb.astype(BF16), kc.astype(BF16),
                   preferred_element_type=F32) * scale
    sn = jnp.sum(qb * kn, axis=-1, keepdims=True) * scale
    sink = sink_ref[...][None, :, 0:1]
    m = jnp.maximum(jnp.maximum(jnp.max(s, axis=-1, keepdims=True), sn), sink)
    p = jnp.exp(s - m)
    pn = jnp.exp(sn - m)
    denom = jnp.sum(p, axis=-1, keepdims=True) + pn + jnp.exp(sink - m)
    p = p / denom
    pn = pn / denom
    o = jnp.einsum("bhk,bdk->bhd", p.astype(BF16), vc.astype(BF16), preferred_element_type=F32)
    o = o + pn * vn
    head = lax.broadcasted_iota(jnp.int32, o.shape[:2] + (ATT_HEAD_DIM,), 1)
    o_ref[...] = jnp.where(head < ATT_GROUP, o[:, :, 0:ATT_HEAD_DIM], o[:, :, ATT_HEAD_DIM:]).astype(o_ref.dtype)


def _attn_sample(qb, kn, vn, kc, vc, sink_rows, tables, layer, prev, tb):
    depth, bd, _, w = kc.shape
    c, s1, s2 = tables
    blk3 = lambda r, n: pl.BlockSpec((tb, r, n), lambda i: (i, 0, 0))
    cache = pl.BlockSpec((None, tb, D_KV, w), lambda i: (layer, i, 0, 0))
    vec = pl.BlockSpec((1, LANE), lambda i: (0, 0))
    prev_k, prev_v = (None, None) if prev is None else prev
    k_spec, k_shape, k_in, k_in_spec = _stacked_out(prev_k, depth, (bd, D_KV, w), (tb, D_KV, w), layer)
    v_spec, v_shape, v_in, v_in_spec = _stacked_out(prev_v, depth, (bd, D_KV, w), (tb, D_KV, w), layer)
    n_in = 9
    n_prev = len(k_in) + len(v_in)
    return pl.pallas_call(
        _drop_aliased(_attn_sample_kernel, n_in, n_prev),
        grid=(bd // tb,),
        in_specs=[blk3(ATT_HEADS, LANE), blk3(1, LANE), blk3(1, LANE), cache, cache,
                  pl.BlockSpec((ATT_HEADS, LANE), lambda i: (0, 0)), vec, vec, vec] + k_in_spec + v_in_spec,
        out_specs=[blk3(ATT_HEADS, ATT_HEAD_DIM), k_spec, v_spec],
        out_shape=[jax.ShapeDtypeStruct((bd, ATT_HEADS, ATT_HEAD_DIM), BF16), k_shape, v_shape],
        input_output_aliases={n_in + t: 1 + t for t in range(n_prev)},
        compiler_params=_cparams(("parallel",)),
        name="attn_sample",
    )(qb, kn, vn, kc, vc, sink_rows, c, s1, s2, *k_in, *v_in)


def _lane_place(cols, lane0):
    m = cols[0].shape[0]
    lane = lax.broadcasted_iota(jnp.int32, (m, LANE), 1)
    out = jnp.zeros((m, LANE), F32)
    for i, col in enumerate(cols):
        out = jnp.where(lane == lane0 + i, col, out)
    return out


def _sample_pre_kernel(x_ref, gmix_ref, w_ref, cs_ref, cw_ref, cb_ref, dtb_ref, alog_ref, bi_ref, bf_ref, n_ref,
                       m_ref, proj_ref, conv_ref, xs_ref, xdt_ref, bm_ref, cm_ref, ea_ref,
                       ks_ref, nn_ref, g_ref, esw_ref, sw_ref, den_ref, mt_ref):
    for _ in _project_next(x_ref, gmix_ref, w_ref, proj_ref):
        pass
    xbc = proj_ref[:, C_XBC:C_XBC + CONV_DIM]
    s0 = cs_ref[0]
    s1 = cs_ref[1]
    s2 = cs_ref[2]
    cw = cw_ref[...]
    acc = s0 * cw[0:1, :]
    acc = acc + s1 * cw[1:2, :]
    acc = acc + s2 * cw[2:3, :]
    acc = acc + xbc * cw[3:4, :]
    xc = _silu(acc + cb_ref[...])
    conv_ref[0] = s1
    conv_ref[1] = s2
    conv_ref[2] = xbc

    xs = xc[:, 0:D_SSM]
    xs_ref[...] = xs
    bm_ref[...] = xc[:, D_SSM:D_SSM + SSM_GROUPS * D_STATE]
    cm_ref[...] = xc[:, D_SSM + SSM_GROUPS * D_STATE:]
    dt = jax.nn.softplus(proj_ref[:, C_G:C_G + LANE] + dtb_ref[...])
    ea_ref[...] = jnp.exp(dt * (-jnp.exp(alog_ref[...])))
    hp = SSM_HEAD_DIM
    for h in range(SSM_HEADS):
        xdt_ref[:, h * hp:(h + 1) * hp] = xs[:, h * hp:(h + 1) * hp] * dt[:, h:h + 1]

    d = MLSTM_HEAD_DIM
    gi = proj_ref[:, C_G:C_G + LANE] + bi_ref[...]
    gf = proj_ref[:, C_G + LANE:C_G + 2 * LANE] + bf_ref[...]
    log_inter = jax.nn.log_sigmoid(gf) + m_ref[...]
    mt = jnp.maximum(log_inter, gi)
    gq = jnp.exp(log_inter - mt)
    esw = jnp.exp(gi - mt)
    ks = proj_ref[:, C_MK:C_MK + D_MLSTM] * (d ** -0.5)
    ks_ref[...] = ks
    qk_cols = []
    qn_cols = []
    for h in range(MLSTM_HEADS):
        qh = proj_ref[:, C_MQ + h * d:C_MQ + (h + 1) * d]
        kh = ks[:, h * d:(h + 1) * d]
        nh = n_ref[:, h * d:(h + 1) * d]
        qk_cols.append(jnp.sum(qh * kh, axis=-1, keepdims=True))
        qn_cols.append(jnp.sum(qh * nh, axis=-1, keepdims=True))
        ln = GATE_LANE + h
        nn_ref[:, h * d:(h + 1) * d] = nh * gq[:, ln:ln + 1] + kh * esw[:, ln:ln + 1]
    sw = esw * _lane_place(qk_cols, GATE_LANE)
    den = sw + _lane_place(qn_cols, GATE_LANE) * gq
    g_ref[...] = gq
    esw_ref[...] = esw
    sw_ref[...] = sw
    den_ref[...] = jnp.maximum(jnp.abs(den), jnp.exp(-mt))
    mt_ref[...] = mt


def _sample_pre(x, g_mix, w, conv_state, layer, cw, cb, dtb, alog, bi, bf, n_state, m_tile):
    bd = x.shape[0]
    taps = CONV_WIDTH - 1
    tile = jax.ShapeDtypeStruct((bd, LANE), F32)
    whole = lambda a: pl.BlockSpec(a.shape, lambda i: (0,) * a.ndim)
    in_specs = [whole(x), whole(g_mix), pl.BlockSpec(w.shape, lambda i: (0, 0), pipeline_mode=pl.Buffered(1)),
                pl.BlockSpec((None, taps, bd, CONV_DIM), lambda i: (layer, 0, 0, 0))]
    in_specs += [whole(a) for a in (cw, cb, dtb, alog, bi, bf, n_state, m_tile)]
    outs = [jax.ShapeDtypeStruct((bd, NP), F32),
            jax.ShapeDtypeStruct((taps, bd, CONV_DIM), F32),
            jax.ShapeDtypeStruct((bd, D_SSM), F32),
            jax.ShapeDtypeStruct((bd, D_SSM), F32),
            jax.ShapeDtypeStruct((bd, SSM_GROUPS * D_STATE), F32),
            jax.ShapeDtypeStruct((bd, SSM_GROUPS * D_STATE), F32),
            tile,
            jax.ShapeDtypeStruct((bd, D_MLSTM), F32),
            jax.ShapeDtypeStruct((bd, D_MLSTM), F32),
            tile, tile, tile, tile, tile]
    return pl.pallas_call(
        _sample_pre_kernel,
        grid=(1,),
        in_specs=in_specs,
        out_specs=[pl.BlockSpec(o.shape, lambda i, nd=len(o.shape): (0,) * nd) for o in outs],
        out_shape=outs,
        compiler_params=_cparams(("arbitrary",)),
        name="sample_pre",
    )(x, g_mix, w, conv_state, cw, cb, dtb, alog, bi, bf, n_state, m_tile)


def _column_tile(row):
    return jnp.broadcast_to(row, (LANE, LANE)).T


SEQ_ROWS = 8


def _put_row(out_ref, acc_ref, vr, col, row):
    width = row.shape[1]
    rid = lax.broadcasted_iota(jnp.int32, (SEQ_ROWS, width), 0)
    blk = jnp.where(rid == vr, row, acc_ref[:, col:col + width])
    acc_ref[:, col:col + width] = blk
    out_ref[:, col:col + width] = blk


def _ssm_sample_one(b, vr, gb, ea_ref, s_ref, xdt_ref, b_ref, c_ref, so_ref, y_ref, yacc_ref):
    hp = SSM_HEAD_DIM
    heads_per_tile = LANE // hp
    n_tiles = D_SSM // LANE
    tiles_per_group = n_tiles // SSM_GROUPS
    top = lax.broadcasted_iota(jnp.int32, (LANE, 1), 0) < hp
    xrow = xdt_ref[pl.ds(vr, 1), :]
    brow = b_ref[pl.ds(vr, 1), :]
    crow = c_ref[pl.ds(vr, 1), :]
    for t in range(n_tiles):
        g = t // tiles_per_group
        xcol = _column_tile(xrow[:, t * LANE:(t + 1) * LANE])
        bg = brow[:, g * D_STATE:(g + 1) * D_STATE]
        cg = jnp.broadcast_to(crow[:, g * D_STATE:(g + 1) * D_STATE], (8, D_STATE))
        e0 = ea_ref[gb, heads_per_tile * t]
        e1 = ea_ref[gb, heads_per_tile * t + 1]
        decay = jnp.where(top, e0, e1)
        sn = s_ref[b, t * LANE:(t + 1) * LANE, :] * decay + xcol * bg
        so_ref[b, t * LANE:(t + 1) * LANE, :] = sn
        _put_row(y_ref, yacc_ref, vr, t * LANE, _dot_nt(cg, sn)[0:1, :])


def _mlstm_sample_one(b, vr, gb, g_ref, esw_ref, c_ref, q_ref, k_ref, v_ref, co_ref, qc_ref, qacc_ref):
    d = MLSTM_HEAD_DIM
    qrow = q_ref[pl.ds(vr, 1), :]
    krow = k_ref[pl.ds(vr, 1), :]
    vrow = v_ref[pl.ds(vr, 1), :]
    for h in range(MLSTM_HEADS):
        qcol = _column_tile(qrow[:, h * d:(h + 1) * d])
        kcol = _column_tile(krow[:, h * d:(h + 1) * d])
        ch = c_ref[b, h * d:(h + 1) * d, :]
        g_end = g_ref[gb, GATE_LANE + h]
        wk = esw_ref[gb, GATE_LANE + h]
        _put_row(qc_ref, qacc_ref, vr, h * d, jnp.sum(qcol * ch, axis=0, keepdims=True))
        co_ref[b, h * d:(h + 1) * d, :] = ch * g_end + (kcol * wk) * vrow[:, h * d:(h + 1) * d]


def _sample_post_kernel(proj_ref, y_ref, xs_ref, dskip_ref, wns_ref, qc_ref, g_ref, sw_ref, den_ref, wnm_ref,
                        ys_ref, hs_ref):
    y = y_ref[...] + dskip_ref[...] * xs_ref[...]
    y = y * _silu(proj_ref[:, C_Z:C_Z + D_SSM])
    ys_ref[...] = _group_rmsnorm(y, wns_ref[...], SSM_GROUPS).astype(ys_ref.dtype)

    d = MLSTM_HEAD_DIM
    for h in range(MLSTM_HEADS):
        ln = GATE_LANE + h
        vh = proj_ref[:, C_MV + h * d:C_MV + (h + 1) * d]
        num = sw_ref[:, ln:ln + 1] * vh + qc_ref[:, h * d:(h + 1) * d] * g_ref[:, ln:ln + 1]
        hh = num / den_ref[:, ln:ln + 1]
        ms = jnp.mean(hh * hh, axis=-1, keepdims=True)
        hn = hh * lax.rsqrt(ms + EPS) * wnm_ref[:, h * d:(h + 1) * d]
        gate = jax.nn.sigmoid(proj_ref[:, C_MO + h * d:C_MO + (h + 1) * d])
        hs_ref[:, h * d:(h + 1) * d] = (hn * gate).astype(hs_ref.dtype)


def _sample_post(proj, y, xs, dskip_rep, wns, qc, gq, sw, den, wnm):
    bd = proj.shape[0]
    return pl.pallas_call(
        _sample_post_kernel,
        out_shape=[jax.ShapeDtypeStruct((bd, D_SSM), BF16), jax.ShapeDtypeStruct((bd, D_MLSTM), BF16)],
        compiler_params=pltpu.CompilerParams(vmem_limit_bytes=VMEM_LIMIT),
        name="sample_post",
    )(proj, y, xs, dskip_rep, wns, qc, gq, sw, den, wnm)


def _lane_vec(v, lane0):
    return jnp.pad(v.astype(F32), (lane0, LANE - lane0 - v.shape[0]))[None, :]


def _pick(n, candidates):
    for c in candidates:
        if n % c == 0:
            return c
    return n


def kernel(x_prompt, x_sample, cache_swa_k, cache_swa_v, state_conv, state_ssm, state_mlstm_C, state_mlstm_n,
           state_mlstm_m, w_norm_mix, w_in, attn_sinks, conv_w, conv_b, dt_bias, a_log, d_skip, w_norm_ssm,
           igate_b, fgate_b, w_norm_mlstm, w_out, w_norm_mlp, w_up, w_down, w_norm_final):
    bsz, seq, d_model = x_prompt.shape
    bd = x_sample.shape[0]
    depth = w_in.shape[0]
    win = cache_swa_k.shape[2]
    assert x_sample.shape[1] == 1 and seq % CHUNK == 0 and d_model == D_MODEL

    mp = bsz * seq
    tm_mlp = _pick(mp, (1024, 512, 256, 128))
    tb = _pick(bd, (16, 8))

    hp = x_prompt.reshape(mp, d_model)
    hs = x_sample.reshape(bd, d_model)
    tab_p = _rope_tables(jnp.arange(seq, dtype=jnp.int32))
    tab_s = _rope_tables(jnp.arange(1, dtype=jnp.int32) + PAST_LEN)
    gf = w_norm_final[None, :]
    kc_all = jnp.transpose(cache_swa_k, (0, 1, 3, 4, 2)).reshape(depth, bd, D_KV, win)
    vc_all = jnp.transpose(cache_swa_v, (0, 1, 3, 4, 2)).reshape(depth, bd, D_KV, win)
    ssm_all = state_ssm.reshape(depth, bd, D_SSM, D_STATE)
    mem_all = state_mlstm_C.reshape(depth, bd, D_MLSTM, MLSTM_HEAD_DIM)
    conv_all = jnp.swapaxes(state_conv, 1, 2)

    st_p, st_s = [], []
    kv_new = s_new = c_new = None
    for l in range(depth):
        w_in_l = _prep_w_in(w_in, l, 256)
        g_mix = w_norm_mix[l][None, :]
        g_mlp = w_norm_mlp[l][None, :]
        cw = conv_w[l]
        cb = conv_b[l][None, :]
        dtb = _lane_vec(dt_bias[l], 0)
        alog = _lane_vec(a_log[l], 0)
        dskip_rep = jnp.repeat(d_skip[l].astype(F32), SSM_HEAD_DIM)[None, :]
        wns = w_norm_ssm[l][None, :]
        bi = _lane_vec(igate_b[l], GATE_LANE)
        bf = _lane_vec(fgate_b[l], GATE_LANE)
        wnm = w_norm_mlstm[l][None, :]
        sinks = attn_sinks[l].reshape(ATT_HEADS).astype(F32)
        last = l == depth - 1

        att, pk, pv, y, p_ssm, hm, p_c, p_n, p_m, p_tail = _front(
            hp, g_mix, w_in_l, sinks, tab_p, cw, cb, dtb, alog, dskip_rep, wns, bi, bf, wnm, bsz, seq)
        x1 = _outproj(att, y, hm, hp, w_out, l, _pick(mp, (512, 256, 128)))
        p_conv = p_tail[:, CONV_PAD - (CONV_WIDTH - 1):, :]
        st_p.append((pk.reshape(bsz, WINDOW, ATT_KV_HEADS, ATT_HEAD_DIM),
                     pv.reshape(bsz, WINDOW, ATT_KV_HEADS, ATT_HEAD_DIM),
                     p_conv,
                     p_ssm.reshape(bsz, SSM_HEADS, SSM_HEAD_DIM, D_STATE),
                     p_c.reshape(bsz, MLSTM_HEADS, MLSTM_HEAD_DIM, MLSTM_HEAD_DIM),
                     p_n[:, 0:MLSTM_HEADS, :],
                     p_m[:, 0:MLSTM_HEADS, 0]))

        m_tile = jnp.pad(state_mlstm_m[l], ((0, 0), (GATE_LANE, LANE - GATE_LANE - MLSTM_HEADS)))
        (proj_s, conv_new, xs, xdt, bm, cm, ea, ks, n_new, gq, esw, sw, den, mt) = _sample_pre(
            hs, g_mix, w_in_l, conv_all, l, cw, cb, dtb, alog, bi, bf, state_mlstm_n[l].reshape(bd, D_MLSTM),
            m_tile)
        q8 = proj_s[:, C_Q:C_Q + D_ATT].reshape(bd, ATT_HEADS, ATT_HEAD_DIM)
        zero = jnp.zeros_like(q8)
        first_kv = (jnp.arange(ATT_HEADS) < ATT_GROUP)[None, :, None]
        qb = jnp.where(first_kv, jnp.concatenate([q8, zero], -1), jnp.concatenate([zero, q8], -1))
        kn = proj_s[:, C_K:C_K + D_KV].reshape(bd, 1, D_KV)
        vn = proj_s[:, C_V:C_V + D_KV].reshape(bd, 1, D_KV)
        sink_rows = jnp.broadcast_to(sinks[:, None], (ATT_HEADS, LANE))
        att_s, sk, sv = _attn_sample(qb, kn, vn, kc_all, vc_all, sink_rows, tab_s, l, kv_new, tb)
        kv_new = (sk, sv)
        side = (ea, ssm_all, xdt, bm, cm, gq, esw, mem_all, proj_s, ks, s_new, c_new)
        hp, s_new, y_read, c_new, qc = _mlp(x1, g_mlp, w_up, w_down, gf, l, last, tm_mlp, 512, side)
        ys, hms = _sample_post(proj_s, y_read, xs, dskip_rep, wns, qc, gq, sw, den, wnm)
        x1s = _outproj(att_s.reshape(bd, D_ATT), ys, hms, hs, w_out, l, bd)
        hs = _mlp(x1s, g_mlp, w_up, w_down, gf, l, last, bd, 512)
        st_s.append((conv_new,
                     n_new.reshape(bd, MLSTM_HEADS, MLSTM_HEAD_DIM),
                     mt[:, GATE_LANE:GATE_LANE + MLSTM_HEADS]))

    y_prompt = hp.reshape(bsz, seq, d_model)
    y_sample = hs.reshape(bd, 1, d_model)
    p_out = [jnp.stack([s[i] for s in st_p]) for i in range(7)]
    s_conv, s_n, s_m = [jnp.stack([s[i] for s in st_s]) for i in range(3)]
    s_conv = jnp.swapaxes(s_conv, 1, 2)
    to_cache = lambda a: jnp.transpose(a.reshape(depth, bd, ATT_KV_HEADS, ATT_HEAD_DIM, win), (0, 1, 4, 2, 3))
    s_k = to_cache(kv_new[0])
    s_v = to_cache(kv_new[1])
    s_ssm = s_new.reshape(depth, bd, SSM_HEADS, SSM_HEAD_DIM, D_STATE)
    s_c = c_new.reshape(depth, bd, MLSTM_HEADS, MLSTM_HEAD_DIM, MLSTM_HEAD_DIM)
    return (y_prompt, y_sample, *p_out, s_k, s_v, s_conv, s_ssm, s_c, s_n, s_m)
```

```python
import functools

import jax
import jax.numpy as jnp
from jax import lax
from jax.experimental import pallas as pl
from jax.experimental.pallas import tpu as pltpu

F32 = jnp.float32
BF16 = jnp.bfloat16

D_MODEL = 2048
EPS = 1e-6
PAST_LEN = 8192
ATT_HEAD_DIM = 64
ATT_HEADS = 8
ATT_KV_HEADS = 2
ATT_GROUP = ATT_HEADS // ATT_KV_HEADS
D_ATT = ATT_HEADS * ATT_HEAD_DIM
D_KV = ATT_KV_HEADS * ATT_HEAD_DIM
WINDOW = 128
ROPE_THETA = 500000.0
ROPE_DIM = ATT_HEAD_DIM // 4
SSM_HEAD_DIM = 64
SSM_HEADS = 16
D_SSM = SSM_HEADS * SSM_HEAD_DIM
SSM_GROUPS = 2
D_STATE = 128
CONV_WIDTH = 4
CONV_DIM = D_SSM + 2 * SSM_GROUPS * D_STATE
CHUNK = 128
MLSTM_HEADS = 4
MLSTM_HEAD_DIM = 128
D_MLSTM = MLSTM_HEADS * MLSTM_HEAD_DIM
D_FF = 4 * D_MODEL

C_XBC = 0
C_Q = 1536
C_Z = 2048
C_MQ = 3072
C_MK = 3584
C_MV = 4096
C_MO = 4608
C_K = 5120
C_V = 5248
C_G = 5376
NP = 5632
GATE_LANE = 16

LANE = 128
VMEM_LIMIT = 60 * 1024 * 1024


def _cparams(sem):
    return pltpu.CompilerParams(dimension_semantics=sem, vmem_limit_bytes=VMEM_LIMIT)


def _silu(x):
    return x * jax.nn.sigmoid(x)


def _dot(a, b):
    return jnp.dot(a.astype(BF16), b.astype(BF16), preferred_element_type=F32)


def _dot_nt(a, b):
    return lax.dot_general(a.astype(BF16), b.astype(BF16), (((1,), (1,)), ((), ())),
                           preferred_element_type=F32)


def _dot_tn(a, b):
    return lax.dot_general(a.astype(BF16), b.astype(BF16), (((0,), (0,)), ((), ())),
                           preferred_element_type=F32)


def _tril(n):
    row = lax.broadcasted_iota(jnp.int32, (n, n), 0)
    col = lax.broadcasted_iota(jnp.int32, (n, n), 1)
    return row >= col


def _cumsum_rows(x):
    row = lax.broadcasted_iota(jnp.int32, x.shape, 0)
    step = 1
    while step < x.shape[0]:
        x = x + jnp.where(row >= step, pltpu.roll(x, step, 0), 0.0)
        step *= 2
    return x


_SRC_Q, _SRC_K, _SRC_V, _SRC_Z, _SRC_XBC, _SRC_DT, _SRC_MQ = 0, 512, 640, 768, 1792, 3328, 3344
_SRC_MI = _SRC_MQ + 4 * D_MLSTM
IN_WIDTH = _SRC_MI + 2 * MLSTM_HEADS


def _prep_w_in_kernel(w_ref, o_ref):
    def put(dst, src, n):
        o_ref[:, dst:dst + n] = w_ref[src:src + n, :].T.astype(BF16)

    put(C_XBC, _SRC_XBC, CONV_DIM)
    put(C_Q, _SRC_Q, D_ATT)
    put(C_Z, _SRC_Z, D_SSM)
    put(C_MQ, _SRC_MQ, 4 * D_MLSTM)
    put(C_K, _SRC_K, D_KV)
    put(C_V, _SRC_V, D_KV)
    tk = w_ref.shape[1]
    dt = w_ref[_SRC_DT:_SRC_DT + SSM_HEADS, :]
    gates = w_ref[_SRC_MI:_SRC_MI + 2 * MLSTM_HEADS, :]
    pad = jnp.zeros((LANE - GATE_LANE - 2 * MLSTM_HEADS, tk), F32)
    lo = jnp.concatenate([dt, gates, pad], axis=0)
    hi = jnp.concatenate([jnp.zeros_like(dt), pltpu.roll(gates, MLSTM_HEADS, 0), pad], axis=0)
    o_ref[:, C_G:C_G + LANE] = lo.T.astype(BF16)
    o_ref[:, C_G + LANE:C_G + 2 * LANE] = hi.T.astype(BF16)


def _prep_w_in(w_in, layer, tk):
    depth, k, n = w_in.shape
    assert n == IN_WIDTH and n % 8 == 0
    wt = jnp.swapaxes(w_in, 1, 2).reshape(depth * n, k)
    return pl.pallas_call(
        _prep_w_in_kernel,
        grid=(k // tk,),
        in_specs=[pl.BlockSpec((n, tk), lambda i: (layer, i))],
        out_specs=pl.BlockSpec((tk, NP), lambda i: (i, 0)),
        out_shape=jax.ShapeDtypeStruct((k, NP), BF16),
        compiler_params=_cparams(("parallel",)),
        name="prep_w_in",
    )(wt)


def _outproj_kernel(a_ref, y_ref, h_ref, x_ref, w_ref, o_ref, wb_ref):
    @pl.when(pl.program_id(0) == 0)
    def _():
        wb_ref[...] = w_ref[...].astype(BF16)

    acc = jnp.dot(a_ref[...], wb_ref[0:D_ATT, :], preferred_element_type=F32)
    acc = acc + jnp.dot(y_ref[...], wb_ref[D_ATT:D_ATT + D_SSM, :], preferred_element_type=F32)
    acc = acc + jnp.dot(h_ref[...], wb_ref[D_ATT + D_SSM:, :], preferred_element_type=F32)
    o_ref[...] = x_ref[...] + acc


def _outproj(att, y, h, x, w, layer, tm):
    m, n = x.shape
    k = w.shape[1]
    rows = lambda width: pl.BlockSpec((tm, width), lambda i: (i, 0))
    return pl.pallas_call(
        _outproj_kernel,
        grid=(m // tm,),
        in_specs=[rows(D_ATT), rows(D_SSM), rows(D_MLSTM), rows(n),
                  pl.BlockSpec((None, k, n), lambda i: (layer, 0, 0), pipeline_mode=pl.Buffered(1))],
        out_specs=rows(n),
        out_shape=jax.ShapeDtypeStruct((m, n), F32),
        scratch_shapes=[pltpu.VMEM((k, n), BF16)],
        compiler_params=_cparams(("arbitrary",)),
        name="outproj",
    )(att, y, h, x, w)


N_MLP_IN, N_SSM_SIDE_IN, N_MLSTM_SIDE_IN = 5, 5, 6


def _mlp_kernel(*refs, final_norm, side, n_prev):
    x_ref, g_ref, wu_ref, wd_ref, gf_ref = refs[:N_MLP_IN]
    n_side = N_SSM_SIDE_IN + N_MLSTM_SIDE_IN if side else 0
    side_in = refs[N_MLP_IN:N_MLP_IN + n_side]
    outs = refs[N_MLP_IN + n_side + n_prev:]
    o_ref = outs[0]
    u_ref = outs[5] if side else outs[1]
    j = pl.program_id(1)

    @pl.when(j == 0)
    def _():
        x = x_ref[...]
        ms = jnp.mean(x * x, axis=-1, keepdims=True)
        u_ref[...] = (x * lax.rsqrt(ms + EPS) * g_ref[...]).astype(BF16)
        o_ref[...] = x
        if side:
            outs[6][...] = jnp.zeros_like(outs[6])
            outs[7][...] = jnp.zeros_like(outs[7])

    h = _dot(u_ref[...], wu_ref[...])
    h = jnp.square(jnp.maximum(h, 0.0))
    o_ref[...] += _dot(h, wd_ref[...])

    if side:
        seq = pl.program_id(0) * pl.num_programs(1) + j
        row = seq % SEQ_ROWS
        so_ref, y_ref, co_ref, qc_ref = outs[1:5]
        yacc_ref, qacc_ref = outs[6:8]
        _ssm_sample_one(0, row, seq, *side_in[:N_SSM_SIDE_IN], so_ref, y_ref, yacc_ref)
        _mlstm_sample_one(0, row, seq, *side_in[N_SSM_SIDE_IN:], co_ref, qc_ref, qacc_ref)

    if final_norm:
        @pl.when(j == pl.num_programs(1) - 1)
        def _():
            y = o_ref[...]
            ms = jnp.mean(y * y, axis=-1, keepdims=True)
            o_ref[...] = y * lax.rsqrt(ms + EPS) * gf_ref[...]


def _mlp(x, g, wu, wd, gf, layer, final_norm, tm, tf, side=None):
    m, d = x.shape
    ff = wu.shape[2]
    nj = ff // tf
    in_specs = [pl.BlockSpec((tm, d), lambda i, j: (i, 0)),
                pl.BlockSpec((1, d), lambda i, j: (0, 0)),
                pl.BlockSpec((None, d, tf), lambda i, j: (layer, 0, j)),
                pl.BlockSpec((None, tf, d), lambda i, j: (layer, j, 0)),
                pl.BlockSpec((1, d), lambda i, j: (0, 0))]
    out_specs = [pl.BlockSpec((tm, d), lambda i, j: (i, 0))]
    out_shape = [jax.ShapeDtypeStruct((m, d), F32)]
    operands = [x, g, wu, wd, gf]
    aliases = {}
    n_prev = 0
    if side is not None:
        ea, ssm, xdt, bvec, cvec, gq, esw, mem, proj_s, ks, prev_s, prev_c = side
        depth, bd = ssm.shape[0], ssm.shape[1]
        assert (m // tm) * nj == bd and nj % SEQ_ROWS == 0
        smem = pl.BlockSpec(memory_space=pltpu.SMEM)
        rows = lambda n, col=0: pl.BlockSpec((SEQ_ROWS, n), lambda i, j: ((i * nj + j) // SEQ_ROWS, col // n))
        state = lambda r, n: pl.BlockSpec((None, 1, r, n), lambda i, j: (layer, i * nj + j, 0, 0))
        in_specs += [smem, state(D_SSM, D_STATE), rows(D_SSM), rows(SSM_GROUPS * D_STATE),
                     rows(SSM_GROUPS * D_STATE),
                     smem, smem, state(D_MLSTM, MLSTM_HEAD_DIM), rows(D_MLSTM, C_MQ), rows(D_MLSTM),
                     rows(D_MLSTM, C_MV)]
        operands += [ea, ssm, xdt, bvec, cvec, gq, esw, mem, proj_s, ks, proj_s]
        out_specs += [state(D_SSM, D_STATE), rows(D_SSM), state(D_MLSTM, MLSTM_HEAD_DIM), rows(D_MLSTM)]
        out_shape += [jax.ShapeDtypeStruct((depth, bd, D_SSM, D_STATE), F32),
                      jax.ShapeDtypeStruct((bd, D_SSM), F32),
                      jax.ShapeDtypeStruct((depth, bd, D_MLSTM, MLSTM_HEAD_DIM), F32),
                      jax.ShapeDtypeStruct((bd, D_MLSTM), F32)]
        if prev_s is not None:
            aliases = {len(operands): 1, len(operands) + 1: 3}
            operands += [prev_s, prev_c]
            in_specs += [pl.BlockSpec(memory_space=pl.ANY)] * 2
            n_prev = 2
    res = pl.pallas_call(
        functools.partial(_mlp_kernel, final_norm=final_norm, side=side is not None, n_prev=n_prev),
        grid=(m // tm, nj),
        in_specs=in_specs,
        out_specs=out_specs,
        out_shape=out_shape,
        input_output_aliases=aliases,
        scratch_shapes=[pltpu.VMEM((tm, d), BF16)] + ([] if side is None else [
            pltpu.VMEM((SEQ_ROWS, D_SSM), F32), pltpu.VMEM((SEQ_ROWS, D_MLSTM), F32)]),
        compiler_params=_cparams(("parallel", "arbitrary")),
        name="mlp",
    )(*operands)
    return res[0] if side is None else res


def _rope_tables(pos):
    half = ROPE_DIM // 2
    inv = jnp.power(jnp.float32(ROPE_THETA), -jnp.arange(half, dtype=jnp.float32) / half)
    ang = pos.astype(jnp.float32)[:, None] * inv[None, :]
    cos = jnp.cos(ang)
    sin = jnp.sin(ang)
    n = pos.shape[0]
    rest = ATT_HEAD_DIM - ROPE_DIM
    c = jnp.concatenate([cos, cos, jnp.ones((n, rest), F32)], axis=1)
    s1 = jnp.concatenate([-sin, jnp.zeros((n, half + rest), F32)], axis=1)
    s2 = jnp.concatenate([jnp.zeros((n, half), F32), sin, jnp.zeros((n, rest), F32)], axis=1)
    rep = LANE // ATT_HEAD_DIM
    return jnp.tile(c, (1, rep)), jnp.tile(s1, (1, rep)), jnp.tile(s2, (1, rep))


def _rope(x, c, s1, s2):
    width = x.shape[-1]
    rep = width // LANE
    half = ROPE_DIM // 2
    if rep > 1:
        c = jnp.concatenate([c] * rep, axis=-1)
        s1 = jnp.concatenate([s1] * rep, axis=-1)
        s2 = jnp.concatenate([s2] * rep, axis=-1)
    axis = x.ndim - 1
    return x * c + pltpu.roll(x, width - half, axis) * s1 + pltpu.roll(x, half, axis) * s2


def _attn_prompt_init(ko_ref, vo_ref):
    ko_ref[...] = jnp.zeros_like(ko_ref)
    vo_ref[...] = jnp.zeros_like(vo_ref)


def _attn_prompt_body(j, sink_ref, q_ref, k_ref, v_ref, c_ref, s1_ref, s2_ref, o_ref, ko_ref, vo_ref):
    w = WINDOW
    c = c_ref[...]
    s1 = s1_ref[...]
    s2 = s2_ref[...]
    krot = _rope(k_ref[...], c, s1, s2)
    v = v_ref[...]
    qrot = _rope(q_ref[...], c, s1, s2)
    kk = jnp.concatenate([ko_ref[...], krot], axis=0).astype(BF16)
    vv = jnp.concatenate([vo_ref[...], v], axis=0).astype(BF16)

    row = lax.broadcasted_iota(jnp.int32, (w, 2 * w), 0)
    col = lax.broadcasted_iota(jnp.int32, (w, 2 * w), 1)
    first_col = jnp.where(j > 0, 0, w)
    mask = (col >= row) & (col <= row + w) & (col >= first_col)
    scale = ATT_HEAD_DIM ** -0.5

    for h in range(ATT_KV_HEADS):
        kh = kk[:, h * ATT_HEAD_DIM:(h + 1) * ATT_HEAD_DIM]
        vh = vv[:, h * ATT_HEAD_DIM:(h + 1) * ATT_HEAD_DIM]
        for g in range(ATT_GROUP):
            hg = h * ATT_GROUP + g
            qh = qrot[:, hg * ATT_HEAD_DIM:(hg + 1) * ATT_HEAD_DIM]
            s = _dot_nt(qh, kh) * scale
            s = jnp.where(mask, s, -jnp.inf)
            sink = sink_ref[hg]
            m = jnp.maximum(jnp.max(s, axis=-1, keepdims=True), sink)
            p = jnp.exp(s - m)
            denom = jnp.sum(p, axis=-1, keepdims=True) + jnp.exp(sink - m)
            o = _dot(p, vh) / denom
            o_ref[:, hg * ATT_HEAD_DIM:(hg + 1) * ATT_HEAD_DIM] = o.astype(o_ref.dtype)
            yield

    ko_ref[...] = krot
    vo_ref[...] = v


def _group_rmsnorm(y, w, groups):
    width = y.shape[-1] // groups
    outs = []
    for g in range(groups):
        yg = y[:, g * width:(g + 1) * width]
        ms = jnp.mean(yg * yg, axis=-1, keepdims=True)
        outs.append(yg * lax.rsqrt(ms + EPS) * w[:, g * width:(g + 1) * width])
    return jnp.concatenate(outs, axis=-1)


CONV_PAD = 8


def _ssd_prompt_init(s_ref, tail_ref):
    tail_ref[...] = jnp.zeros_like(tail_ref)
    s_ref[...] = jnp.zeros_like(s_ref)


def _ssd_prompt_body(xbc_ref, z_ref, g_ref, cw_ref, cb_ref, dtb_ref, alog_ref, dskip_ref, wn_ref,
                     y_ref, s_ref, tail_ref, ybuf_ref, xw_ref):
    q = CHUNK
    pad = CONV_PAD
    hp = SSM_HEAD_DIM
    heads_per_group = SSM_HEADS // SSM_GROUPS
    gw = heads_per_group * hp

    x = xbc_ref[...]
    cw = cw_ref[...]
    row8 = lax.broadcasted_iota(jnp.int32, (pad, CONV_DIM), 0)
    acc = None
    for j in range(CONV_WIDTH - 1):
        shift = CONV_WIDTH - 1 - j
        rolled = pltpu.roll(x, shift, 0)
        head = jnp.where(row8 < shift, tail_ref[j * pad:(j + 1) * pad, :], rolled[0:pad, :])
        tail_ref[j * pad:(j + 1) * pad, :] = rolled[0:pad, :]
        term = jnp.concatenate([head, rolled[pad:, :]], axis=0) * cw[j:j + 1, :]
        acc = term if acc is None else acc + term
    acc = acc + x * cw[CONV_WIDTH - 1:CONV_WIDTH, :]
    xc = _silu(acc + cb_ref[...])

    xs = xc[:, 0:D_SSM]
    bm = xc[:, D_SSM:D_SSM + SSM_GROUPS * D_STATE]
    cm = xc[:, D_SSM + SSM_GROUPS * D_STATE:]

    dt = jax.nn.softplus(g_ref[:, 0:LANE] + dtb_ref[...])
    a_neg = -jnp.exp(alog_ref[...])
    a_col = _cumsum_rows(dt * a_neg)
    a_row = a_col.T
    dt_row = dt.T
    wk_row = jnp.exp(a_row[:, q - 1:q] - a_row) * dt_row
    tril = _tril(q)
    xs_t = xs.T
    ea_col = jnp.exp(a_col)
    left = lax.broadcasted_iota(jnp.int32, (q, LANE), 1) < hp
    ea_full = jnp.concatenate(
        [jnp.where(left, ea_col[:, 2 * t:2 * t + 1], ea_col[:, 2 * t + 1:2 * t + 2]) for t in range(SSM_HEADS // 2)],
        axis=1)
    yield

    cs_parts = []
    for g in range(SSM_GROUPS):
        bg = bm[:, g * D_STATE:(g + 1) * D_STATE]
        cg = cm[:, g * D_STATE:(g + 1) * D_STATE]
        cb = _dot_nt(cg, bg)
        cs_parts.append(_dot_nt(cg, s_ref[g * gw:(g + 1) * gw, :]))
        for r in range(heads_per_group):
            h = g * heads_per_group + r
            seg = a_col[:, h:h + 1] - a_row[h:h + 1, :]
            wmat = jnp.exp(jnp.where(tril, seg, -jnp.inf)) * cb * dt_row[h:h + 1, :]
            ybuf_ref[:, h * hp:(h + 1) * hp] = _dot(wmat, xs[:, h * hp:(h + 1) * hp])
            xw_ref[r * hp:(r + 1) * hp, :] = xs_t[h * hp:(h + 1) * hp, :] * wk_row[h:h + 1, :]
            yield
        upd = _dot(xw_ref[...], bg)
        for r in range(heads_per_group):
            h = g * heads_per_group + r
            decay = jnp.exp(a_row[h:h + 1, q - 1:q])
            s_ref[h * hp:(h + 1) * hp, :] = s_ref[h * hp:(h + 1) * hp, :] * decay + upd[r * hp:(r + 1) * hp, :]
        yield

    y = ybuf_ref[...] + jnp.concatenate(cs_parts, axis=1) * ea_full
    y = y + dskip_ref[...] * xs
    y = y * _silu(z_ref[...])
    y_ref[...] = _group_rmsnorm(y, wn_ref[...], SSM_GROUPS).astype(y_ref.dtype)


def _mlstm_prompt_init(c_ref, n_ref, m_ref):
    c_ref[...] = jnp.zeros_like(c_ref)
    n_ref[...] = jnp.zeros_like(n_ref)
    m_ref[...] = jnp.zeros_like(m_ref)


def _mlstm_prompt_body(q_ref, k_ref, v_ref, o_ref, g_ref, bi_ref, bf_ref, wn_ref, h_ref, c_ref, n_ref, m_ref):
    t = CHUNK
    d = MLSTM_HEAD_DIM

    gi = g_ref[:, 0:LANE] + bi_ref[...]
    gf = g_ref[:, LANE:2 * LANE] + bf_ref[...]
    b_col = _cumsum_rows(jax.nn.log_sigmoid(gf))
    b_row = b_col.T
    i_row = gi.T
    tril = _tril(t)
    k_all = k_ref[...] * (d ** -0.5)

    for h in range(MLSTM_HEADS):
        ln = GATE_LANE + h
        bq = b_col[:, ln:ln + 1]
        logw = jnp.where(tril, bq - b_row[ln:ln + 1, :] + i_row[ln:ln + 1, :], -jnp.inf)
        m_prev = m_ref[h:h + 1, 0:1]
        log_inter = bq + m_prev
        mt = jnp.maximum(log_inter, jnp.max(logw, axis=-1, keepdims=True))
        qh = q_ref[:, h * d:(h + 1) * d]
        kh = k_all[:, h * d:(h + 1) * d]
        vh = v_ref[:, h * d:(h + 1) * d]
        ch = c_ref[h * d:(h + 1) * d, :]
        nh = n_ref[h:h + 1, :]
        sw = jnp.exp(logw - mt) * _dot_nt(qh, kh)
        gq = jnp.exp(log_inter - mt)
        num = _dot(sw, vh) + _dot(qh, ch) * gq
        den = jnp.sum(sw, axis=-1, keepdims=True) + jnp.sum(qh * nh, axis=-1, keepdims=True) * gq
        hh = num / jnp.maximum(jnp.abs(den), jnp.exp(-mt))

        m_new = mt[t - 1:t, :]
        b_last = b_col[t - 1:t, ln:ln + 1]
        wk = jnp.exp(b_last - bq + gi[:, ln:ln + 1] - m_new)
        g_end = jnp.exp(b_last + m_prev - m_new)
        kw = kh * wk
        c_ref[h * d:(h + 1) * d, :] = ch * g_end + _dot_tn(kw, vh)
        n_ref[h:h + 1, :] = nh * g_end + jnp.sum(kw, axis=0, keepdims=True)
        m_ref[h:h + 1, :] = jnp.broadcast_to(m_new, (1, LANE))

        ms = jnp.mean(hh * hh, axis=-1, keepdims=True)
        hn = hh * lax.rsqrt(ms + EPS) * wn_ref[:, h * d:(h + 1) * d]
        h_ref[:, h * d:(h + 1) * d] = (hn * jax.nn.sigmoid(o_ref[:, h * d:(h + 1) * d])).astype(h_ref.dtype)
        yield

def _interleave(stages):
    live = [[gen, 0, n] for gen, n in stages]
    while live:
        item = min(live, key=lambda s: s[1] / s[2])
        try:
            next(item[0])
            item[1] += 1
        except StopIteration:
            live.remove(item)


PROJ_TILE = 256


def _project_next(x_ref, g_ref, w_ref, pnext_ref):
    x = x_ref[...]
    u = (x * lax.rsqrt(jnp.mean(x * x, axis=-1, keepdims=True) + EPS) * g_ref[...]).astype(BF16)
    for ct in range(NP // PROJ_TILE):
        cols = slice(ct * PROJ_TILE, (ct + 1) * PROJ_TILE)
        pnext_ref[:, cols] = jnp.dot(u, w_ref[:, cols], preferred_element_type=F32)
        yield


def _front_kernel(sink_ref, x_ref, x0_ref, g_ref, w_ref, c_ref, s1_ref, s2_ref,
                  cw_ref, cb_ref, dtb_ref, alog_ref, dskip_ref, wns_ref, bi_ref, bf_ref, wnm_ref,
                  att_ref, ko_ref, vo_ref, y_ref, s_ref, h_ref, mem_ref, n_ref, m_ref, ptail_ref,
                  pcur_ref, pnext_ref, tail_ref, ybuf_ref, xw_ref, *, nblk):
    j = pl.program_id(1)
    first = (pl.program_id(0) == 0) & (j == 0)
    t = CHUNK

    @pl.when(first)
    def _():
        for _ in _project_next(x0_ref, g_ref, w_ref, pnext_ref):
            pass

    @pl.when(j == 0)
    def _():
        _attn_prompt_init(ko_ref, vo_ref)
        _ssd_prompt_init(s_ref, tail_ref)
        _mlstm_prompt_init(mem_ref, n_ref, m_ref)

    pcur_ref[...] = pnext_ref[...]

    def attn(i):
        seg = lambda col, width: pcur_ref.at[i * t:(i + 1) * t, col:col + width]
        rows = lambda ref: ref.at[i * t:(i + 1) * t, :]
        return _attn_prompt_body(j * nblk + i, sink_ref, seg(C_Q, D_ATT), seg(C_K, D_KV), seg(C_V, D_KV),
                                 rows(c_ref), rows(s1_ref), rows(s2_ref), rows(att_ref), ko_ref, vo_ref)

    def ssd(i):
        seg = lambda col, width: pcur_ref.at[i * t:(i + 1) * t, col:col + width]
        return _ssd_prompt_body(seg(C_XBC, CONV_DIM), seg(C_Z, D_SSM), seg(C_G, 2 * LANE), cw_ref, cb_ref, dtb_ref,
                                alog_ref, dskip_ref, wns_ref, y_ref.at[i * t:(i + 1) * t, :], s_ref,
                                tail_ref, ybuf_ref, xw_ref)

    def mlstm(i):
        seg = lambda col, width: pcur_ref.at[i * t:(i + 1) * t, col:col + width]
        return _mlstm_prompt_body(seg(C_MQ, D_MLSTM), seg(C_MK, D_MLSTM), seg(C_MV, D_MLSTM), seg(C_MO, D_MLSTM),
                                  seg(C_G, 2 * LANE), bi_ref, bf_ref, wnm_ref, h_ref.at[i * t:(i + 1) * t, :],
                                  mem_ref, n_ref, m_ref)

    def blocks(body):
        for i in range(nblk):
            yield from body(i)

    ptail_ref[...] = pcur_ref[nblk * t - CONV_PAD:nblk * t, C_XBC:C_XBC + CONV_DIM]
    _interleave([
        (_project_next(x_ref, g_ref, w_ref, pnext_ref), NP // PROJ_TILE + 1),
        (blocks(attn), nblk * (ATT_HEADS + 1)),
        (blocks(ssd), nblk * (SSM_HEADS + SSM_GROUPS + 2)),
        (blocks(mlstm), nblk * (MLSTM_HEADS + 1)),
    ])


def _front(x, g_mix, w, sinks, tables, cw, cb, dtb, alog, dskip_rep, wns, bi, bf, wnm, bsz, seq):
    nblk = 2 if (seq // CHUNK) % 2 == 0 else 1
    t = nblk * CHUNK
    nb = seq // t
    total = bsz * nb
    c, s1, s2 = tables
    row = lambda b, j: b * nb + j
    full = lambda shape: pl.BlockSpec(shape, lambda b, j: (0,) * len(shape))
    vec = lambda n: full((1, n))
    tab = pl.BlockSpec((t, LANE), lambda b, j: (j, 0))
    rows_out = lambda width: pl.BlockSpec((t, width), lambda b, j: (row(b, j), 0))
    per_seq = lambda r, n: pl.BlockSpec((None, r, n), lambda b, j: (b, 0, 0))
    return pl.pallas_call(
        functools.partial(_front_kernel, nblk=nblk),
        grid=(bsz, nb),
        in_specs=[pl.BlockSpec(memory_space=pltpu.SMEM),
                  pl.BlockSpec((t, D_MODEL), lambda b, j: (jnp.minimum(row(b, j) + 1, total - 1), 0)),
                  pl.BlockSpec((t, D_MODEL), lambda b, j: (0, 0), pipeline_mode=pl.Buffered(1)),
                  vec(D_MODEL),
                  pl.BlockSpec((D_MODEL, NP), lambda b, j: (0, 0), pipeline_mode=pl.Buffered(1)),
                  tab, tab, tab,
                  full((CONV_WIDTH, CONV_DIM)), vec(CONV_DIM), vec(LANE), vec(LANE), vec(D_SSM), vec(D_SSM),
                  vec(LANE), vec(LANE), vec(D_MLSTM)],
        out_specs=[rows_out(D_ATT), per_seq(CHUNK, D_KV), per_seq(CHUNK, D_KV),
                   rows_out(D_SSM), per_seq(D_SSM, D_STATE),
                   rows_out(D_MLSTM), per_seq(D_MLSTM, MLSTM_HEAD_DIM), per_seq(8, LANE), per_seq(8, LANE),
                   per_seq(CONV_PAD, CONV_DIM)],
        out_shape=[jax.ShapeDtypeStruct((bsz * seq, D_ATT), BF16),
                   jax.ShapeDtypeStruct((bsz, CHUNK, D_KV), F32),
                   jax.ShapeDtypeStruct((bsz, CHUNK, D_KV), F32),
                   jax.ShapeDtypeStruct((bsz * seq, D_SSM), BF16),
                   jax.ShapeDtypeStruct((bsz, D_SSM, D_STATE), F32),
                   jax.ShapeDtypeStruct((bsz * seq, D_MLSTM), BF16),
                   jax.ShapeDtypeStruct((bsz, D_MLSTM, MLSTM_HEAD_DIM), F32),
                   jax.ShapeDtypeStruct((bsz, 8, LANE), F32),
                   jax.ShapeDtypeStruct((bsz, 8, LANE), F32),
                   jax.ShapeDtypeStruct((bsz, CONV_PAD, CONV_DIM), F32)],
        scratch_shapes=[pltpu.VMEM((t, NP), F32),
                        pltpu.VMEM((t, NP), F32),
                        pltpu.VMEM(((CONV_WIDTH - 1) * CONV_PAD, CONV_DIM), F32),
                        pltpu.VMEM((CHUNK, D_SSM), F32),
                        pltpu.VMEM((D_SSM // SSM_GROUPS, CHUNK), F32)],
        compiler_params=_cparams(("arbitrary", "arbitrary")),
        name="front",
    )(sinks, x, x, g_mix, w, c, s1, s2, cw, cb, dtb, alog, dskip_rep, wns, bi, bf, wnm)


def _stacked_out(prev, depth, shape, block, layer):
    spec = pl.BlockSpec((None,) + block, lambda i: (layer,) + (i,) + (0,) * (len(block) - 1))
    out_shape = jax.ShapeDtypeStruct((depth,) + shape, F32)
    extra_in = [] if prev is None else [prev]
    extra_spec = [] if prev is None else [pl.BlockSpec(memory_space=pl.ANY)]
    return spec, out_shape, extra_in, extra_spec


def _drop_aliased(kernel_fn, n_in, n_prev):
    def wrapped(*refs):
        return kernel_fn(*refs[:n_in], *refs[n_in + n_prev:])
    return wrapped


def _attn_sample_kernel(q_ref, k_ref, v_ref, kc_ref, vc_ref, sink_ref, c_ref, s1_ref, s2_ref,
                        o_ref, ko_ref, vo_ref):
    tb, _, w = kc_ref.shape
    c = c_ref[...]
    s1 = s1_ref[...]
    s2 = s2_ref[...]
    qb = _rope(q_ref[...], c, s1, s2)
    kn = _rope(k_ref[...], c, s1, s2)
    vn = v_ref[...]
    kc = kc_ref[...]
    vc = vc_ref[...]
    newest = lax.broadcasted_iota(jnp.int32, (D_KV, w), 1) == w - 1
    for b in range(tb):
        ko_ref[b] = jnp.where(newest, _column_tile(kn[b]), pltpu.roll(kc[b], w - 1, 1))
        vo_ref[b] = jnp.where(newest, _column_tile(vn[b]), pltpu.roll(vc[b], w - 1, 1))

    scale = ATT_HEAD_DIM ** -0.5
    s = jnp.einsum("bhd,bdk->bhk", qb.astype(BF16), kc.astype(BF16),
                   preferred_element_type=F32) * scale
    sn = jnp.sum(qb * kn, axis=-1, keepdims=True) * scale
    sink = sink_ref[...][None, :, 0:1]
    m = jnp.maximum(jnp.maximum(jnp.max(s, axis=-1, keepdims=True), sn), sink)
    p = jnp.exp(s - m)
    pn = jnp.exp(sn - m)
    denom = jnp.sum(p, axis=-1, keepdims=True) + pn + jnp.exp(sink - m)
    p = p / denom
    pn = pn / denom
    o = jnp.einsum("bhk,bdk->bhd", p.astype(BF16), vc.astype(BF16), preferred_element_type=F32)
    o = o + pn * vn
    head = lax.broadcasted_iota(jnp.int32, o.shape[:2] + (ATT_HEAD_DIM,), 1)
    o_ref[...] = jnp.where(head < ATT_GROUP, o[:, :, 0:ATT_HEAD_DIM], o[:, :, ATT_HEAD_DIM:]).astype(o_ref.dtype)


def _attn_sample(qb, kn, vn, kc, vc, sink_rows, tables, layer, prev, tb):
    depth, bd, _, w = kc.shape
    c, s1, s2 = tables
    blk3 = lambda r, n: pl.BlockSpec((tb, r, n), lambda i: (i, 0, 0))
    cache = pl.BlockSpec((None, tb, D_KV, w), lambda i: (layer, i, 0, 0))
    vec = pl.BlockSpec((1, LANE), lambda i: (0, 0))
    prev_k, prev_v = (None, None) if prev is None else prev
    k_spec, k_shape, k_in, k_in_spec = _stacked_out(prev_k, depth, (bd, D_KV, w), (tb, D_KV, w), layer)
    v_spec, v_shape, v_in, v_in_spec = _stacked_out(prev_v, depth, (bd, D_KV, w), (tb, D_KV, w), layer)
    n_in = 9
    n_prev = len(k_in) + len(v_in)
    return pl.pallas_call(
        _drop_aliased(_attn_sample_kernel, n_in, n_prev),
        grid=(bd // tb,),
        in_specs=[blk3(ATT_HEADS, LANE), blk3(1, LANE), blk3(1, LANE), cache, cache,
                  pl.BlockSpec((ATT_HEADS, LANE), lambda i: (0, 0)), vec, vec, vec] + k_in_spec + v_in_spec,
        out_specs=[blk3(ATT_HEADS, ATT_HEAD_DIM), k_spec, v_spec],
        out_shape=[jax.ShapeDtypeStruct((bd, ATT_HEADS, ATT_HEAD_DIM), BF16), k_shape, v_shape],
        input_output_aliases={n_in + t: 1 + t for t in range(n_prev)},
        compiler_params=_cparams(("parallel",)),
        name="attn_sample",
    )(qb, kn, vn, kc, vc, sink_rows, c, s1, s2, *k_in, *v_in)


def _lane_place(cols, lane0):
    m = cols[0].shape[0]
    lane = lax.broadcasted_iota(jnp.int32, (m, LANE), 1)
    out = jnp.zeros((m, LANE), F32)
    for i, col in enumerate(cols):
        out = jnp.where(lane == lane0 + i, col, out)
    return out


def _sample_pre_kernel(x_ref, gmix_ref, w_ref, cs_ref, cw_ref, cb_ref, dtb_ref, alog_ref, bi_ref, bf_ref, n_ref,
                       m_ref, proj_ref, conv_ref, xs_ref, xdt_ref, bm_ref, cm_ref, ea_ref,
                       ks_ref, nn_ref, g_ref, esw_ref, sw_ref, den_ref, mt_ref):
    for _ in _project_next(x_ref, gmix_ref, w_ref, proj_ref):
        pass
    xbc = proj_ref[:, C_XBC:C_XBC + CONV_DIM]
    s0 = cs_ref[0]
    s1 = cs_ref[1]
    s2 = cs_ref[2]
    cw = cw_ref[...]
    acc = s0 * cw[0:1, :]
    acc = acc + s1 * cw[1:2, :]
    acc = acc + s2 * cw[2:3, :]
    acc = acc + xbc * cw[3:4, :]
    xc = _silu(acc + cb_ref[...])
    conv_ref[0] = s1
    conv_ref[1] = s2
    conv_ref[2] = xbc

    xs = xc[:, 0:D_SSM]
    xs_ref[...] = xs
    bm_ref[...] = xc[:, D_SSM:D_SSM + SSM_GROUPS * D_STATE]
    cm_ref[...] = xc[:, D_SSM + SSM_GROUPS * D_STATE:]
    dt = jax.nn.softplus(proj_ref[:, C_G:C_G + LANE] + dtb_ref[...])
    ea_ref[...] = jnp.exp(dt * (-jnp.exp(alog_ref[...])))
    hp = SSM_HEAD_DIM
    for h in range(SSM_HEADS):
        xdt_ref[:, h * hp:(h + 1) * hp] = xs[:, h * hp:(h + 1) * hp] * dt[:, h:h + 1]

    d = MLSTM_HEAD_DIM
    gi = proj_ref[:, C_G:C_G + LANE] + bi_ref[...]
    gf = proj_ref[:, C_G + LANE:C_G + 2 * LANE] + bf_ref[...]
    log_inter = jax.nn.log_sigmoid(gf) + m_ref[...]
    mt = jnp.maximum(log_inter, gi)
    gq = jnp.exp(log_inter - mt)
    esw = jnp.exp(gi - mt)
    ks = proj_ref[:, C_MK:C_MK + D_MLSTM] * (d ** -0.5)
    ks_ref[...] = ks
    qk_cols = []
    qn_cols = []
    for h in range(MLSTM_HEADS):
        qh = proj_ref[:, C_MQ + h * d:C_MQ + (h + 1) * d]
        kh = ks[:, h * d:(h + 1) * d]
        nh = n_ref[:, h * d:(h + 1) * d]
        qk_cols.append(jnp.sum(qh * kh, axis=-1, keepdims=True))
        qn_cols.append(jnp.sum(qh * nh, axis=-1, keepdims=True))
        ln = GATE_LANE + h
        nn_ref[:, h * d:(h + 1) * d] = nh * gq[:, ln:ln + 1] + kh * esw[:, ln:ln + 1]
    sw = esw * _lane_place(qk_cols, GATE_LANE)
    den = sw + _lane_place(qn_cols, GATE_LANE) * gq
    g_ref[...] = gq
    esw_ref[...] = esw
    sw_ref[...] = sw
    den_ref[...] = jnp.maximum(jnp.abs(den), jnp.exp(-mt))
    mt_ref[...] = mt


def _sample_pre(x, g_mix, w, conv_state, layer, cw, cb, dtb, alog, bi, bf, n_state, m_tile):
    bd = x.shape[0]
    taps = CONV_WIDTH - 1
    tile = jax.ShapeDtypeStruct((bd, LANE), F32)
    whole = lambda a: pl.BlockSpec(a.shape, lambda i: (0,) * a.ndim)
    in_specs = [whole(x), whole(g_mix), pl.BlockSpec(w.shape, lambda i: (0, 0), pipeline_mode=pl.Buffered(1)),
                pl.BlockSpec((None, taps, bd, CONV_DIM), lambda i: (layer, 0, 0, 0))]
    in_specs += [whole(a) for a in (cw, cb, dtb, alog, bi, bf, n_state, m_tile)]
    outs = [jax.ShapeDtypeStruct((bd, NP), F32),
            jax.ShapeDtypeStruct((taps, bd, CONV_DIM), F32),
            jax.ShapeDtypeStruct((bd, D_SSM), F32),
            jax.ShapeDtypeStruct((bd, D_SSM), F32),
            jax.ShapeDtypeStruct((bd, SSM_GROUPS * D_STATE), F32),
            jax.ShapeDtypeStruct((bd, SSM_GROUPS * D_STATE), F32),
            tile,
            jax.ShapeDtypeStruct((bd, D_MLSTM), F32),
            jax.ShapeDtypeStruct((bd, D_MLSTM), F32),
            tile, tile, tile, tile, tile]
    return pl.pallas_call(
        _sample_pre_kernel,
        grid=(1,),
        in_specs=in_specs,
        out_specs=[pl.BlockSpec(o.shape, lambda i, nd=len(o.shape): (0,) * nd) for o in outs],
        out_shape=outs,
        compiler_params=_cparams(("arbitrary",)),
        name="sample_pre",
    )(x, g_mix, w, conv_state, cw, cb, dtb, alog, bi, bf, n_state, m_tile)


def _column_tile(row):
    return jnp.broadcast_to(row, (LANE, LANE)).T


SEQ_ROWS = 8


def _put_row(out_ref, acc_ref, vr, col, row):
    width = row.shape[1]
    rid = lax.broadcasted_iota(jnp.int32, (SEQ_ROWS, width), 0)
    blk = jnp.where(rid == vr, row, acc_ref[:, col:col + width])
    acc_ref[:, col:col + width] = blk
    out_ref[:, col:col + width] = blk


def _ssm_sample_one(b, vr, gb, ea_ref, s_ref, xdt_ref, b_ref, c_ref, so_ref, y_ref, yacc_ref):
    hp = SSM_HEAD_DIM
    heads_per_tile = LANE // hp
    n_tiles = D_SSM // LANE
    tiles_per_group = n_tiles // SSM_GROUPS
    top = lax.broadcasted_iota(jnp.int32, (LANE, 1), 0) < hp
    xrow = xdt_ref[pl.ds(vr, 1), :]
    brow = b_ref[pl.ds(vr, 1), :]
    crow = c_ref[pl.ds(vr, 1), :]
    for t in range(n_tiles):
        g = t // tiles_per_group
        xcol = _column_tile(xrow[:, t * LANE:(t + 1) * LANE])
        bg = brow[:, g * D_STATE:(g + 1) * D_STATE]
        cg = jnp.broadcast_to(crow[:, g * D_STATE:(g + 1) * D_STATE], (8, D_STATE))
        e0 = ea_ref[gb, heads_per_tile * t]
        e1 = ea_ref[gb, heads_per_tile * t + 1]
        decay = jnp.where(top, e0, e1)
        sn = s_ref[b, t * LANE:(t + 1) * LANE, :] * decay + xcol * bg
        so_ref[b, t * LANE:(t + 1) * LANE, :] = sn
        _put_row(y_ref, yacc_ref, vr, t * LANE, _dot_nt(cg, sn)[0:1, :])


def _mlstm_sample_one(b, vr, gb, g_ref, esw_ref, c_ref, q_ref, k_ref, v_ref, co_ref, qc_ref, qacc_ref):
    d = MLSTM_HEAD_DIM
    qrow = q_ref[pl.ds(vr, 1), :]
    krow = k_ref[pl.ds(vr, 1), :]
    vrow = v_ref[pl.ds(vr, 1), :]
    for h in range(MLSTM_HEADS):
        qcol = _column_tile(qrow[:, h * d:(h + 1) * d])
        kcol = _column_tile(krow[:, h * d:(h + 1) * d])
        ch = c_ref[b, h * d:(h + 1) * d, :]
        g_end = g_ref[gb, GATE_LANE + h]
        wk = esw_ref[gb, GATE_LANE + h]
        _put_row(qc_ref, qacc_ref, vr, h * d, jnp.sum(qcol * ch, axis=0, keepdims=True))
        co_ref[b, h * d:(h + 1) * d, :] = ch * g_end + (kcol * wk) * vrow[:, h * d:(h + 1) * d]


def _sample_post_kernel(proj_ref, y_ref, xs_ref, dskip_ref, wns_ref, qc_ref, g_ref, sw_ref, den_ref, wnm_ref,
                        ys_ref, hs_ref):
    y = y_ref[...] + dskip_ref[...] * xs_ref[...]
    y = y * _silu(proj_ref[:, C_Z:C_Z + D_SSM])
    ys_ref[...] = _group_rmsnorm(y, wns_ref[...], SSM_GROUPS).astype(ys_ref.dtype)

    d = MLSTM_HEAD_DIM
    for h in range(MLSTM_HEADS):
        ln = GATE_LANE + h
        vh = proj_ref[:, C_MV + h * d:C_MV + (h + 1) * d]
        num = sw_ref[:, ln:ln + 1] * vh + qc_ref[:, h * d:(h + 1) * d] * g_ref[:, ln:ln + 1]
        hh = num / den_ref[:, ln:ln + 1]
        ms = jnp.mean(hh * hh, axis=-1, keepdims=True)
        hn = hh * lax.rsqrt(ms + EPS) * wnm_ref[:, h * d:(h + 1) * d]
        gate = jax.nn.sigmoid(proj_ref[:, C_MO + h * d:C_MO + (h + 1) * d])
        hs_ref[:, h * d:(h + 1) * d] = (hn * gate).astype(hs_ref.dtype)


def _sample_post(proj, y, xs, dskip_rep, wns, qc, gq, sw, den, wnm):
    bd = proj.shape[0]
    return pl.pallas_call(
        _sample_post_kernel,
        out_shape=[jax.ShapeDtypeStruct((bd, D_SSM), BF16), jax.ShapeDtypeStruct((bd, D_MLSTM), BF16)],
        compiler_params=pltpu.CompilerParams(vmem_limit_bytes=VMEM_LIMIT),
        name="sample_post",
    )(proj, y, xs, dskip_rep, wns, qc, gq, sw, den, wnm)


def _lane_vec(v, lane0):
    return jnp.pad(v.astype(F32), (lane0, LANE - lane0 - v.shape[0]))[None, :]


def _pick(n, candidates):
    for c in candidates:
        if n % c == 0:
            return c
    return n


def kernel(x_prompt, x_sample, cache_swa_k, cache_swa_v, state_conv, state_ssm, state_mlstm_C, state_mlstm_n,
           state_mlstm_m, w_norm_mix, w_in, attn_sinks, conv_w, conv_b, dt_bias, a_log, d_skip, w_norm_ssm,
           igate_b, fgate_b, w_norm_mlstm, w_out, w_norm_mlp, w_up, w_down, w_norm_final):
    bsz, seq, d_model = x_prompt.shape
    bd = x_sample.shape[0]
    depth = w_in.shape[0]
    win = cache_swa_k.shape[2]
    assert x_sample.shape[1] == 1 and seq % CHUNK == 0 and d_model == D_MODEL

    mp = bsz * seq
    tm_mlp = _pick(mp, (1024, 512, 256, 128))
    tb = _pick(bd, (16, 8))

    hp = x_prompt.reshape(mp, d_model)
    hs = x_sample.reshape(bd, d_model)
    tab_p = _rope_tables(jnp.arange(seq, dtype=jnp.int32))
    tab_s = _rope_tables(jnp.arange(1, dtype=jnp.int32) + PAST_LEN)
    gf = w_norm_final[None, :]
    kc_all = jnp.transpose(cache_swa_k, (0, 1, 3, 4, 2)).reshape(depth, bd, D_KV, win)
    vc_all = jnp.transpose(cache_swa_v, (0, 1, 3, 4, 2)).reshape(depth, bd, D_KV, win)
    ssm_all = state_ssm.reshape(depth, bd, D_SSM, D_STATE)
    mem_all = state_mlstm_C.reshape(depth, bd, D_MLSTM, MLSTM_HEAD_DIM)
    conv_all = jnp.swapaxes(state_conv, 1, 2)

    st_p, st_s = [], []
    kv_new = s_new = c_new = None
    for l in range(depth):
        w_in_l = _prep_w_in(w_in, l, 256)
        g_mix = w_norm_mix[l][None, :]
        g_mlp = w_norm_mlp[l][None, :]
        cw = conv_w[l]
        cb = conv_b[l][None, :]
        dtb = _lane_vec(dt_bias[l], 0)
        alog = _lane_vec(a_log[l], 0)
        dskip_rep = jnp.repeat(d_skip[l].astype(F32), SSM_HEAD_DIM)[None, :]
        wns = w_norm_ssm[l][None, :]
        bi = _lane_vec(igate_b[l], GATE_LANE)
        bf = _lane_vec(fgate_b[l], GATE_LANE)
        wnm = w_norm_mlstm[l][None, :]
        sinks = attn_sinks[l].reshape(ATT_HEADS).astype(F32)
        last = l == depth - 1

        att, pk, pv, y, p_ssm, hm, p_c, p_n, p_m, p_tail = _front(
            hp, g_mix, w_in_l, sinks, tab_p, cw, cb, dtb, alog, dskip_rep, wns, bi, bf, wnm, bsz, seq)
        x1 = _outproj(att, y, hm, hp, w_out, l, _pick(mp, (512, 256, 128)))
        p_conv = p_tail[:, CONV_PAD - (CONV_WIDTH - 1):, :]
        st_p.append((pk.reshape(bsz, WINDOW, ATT_KV_HEADS, ATT_HEAD_DIM),
                     pv.reshape(bsz, WINDOW, ATT_KV_HEADS, ATT_HEAD_DIM),
                     p_conv,
                     p_ssm.reshape(bsz, SSM_HEADS, SSM_HEAD_DIM, D_STATE),
                     p_c.reshape(bsz, MLSTM_HEADS, MLSTM_HEAD_DIM, MLSTM_HEAD_DIM),
                     p_n[:, 0:MLSTM_HEADS, :],
                     p_m[:, 0:MLSTM_HEADS, 0]))

        m_tile = jnp.pad(state_mlstm_m[l], ((0, 0), (GATE_LANE, LANE - GATE_LANE - MLSTM_HEADS)))
        (proj_s, conv_new, xs, xdt, bm, cm, ea, ks, n_new, gq, esw, sw, den, mt) = _sample_pre(
            hs, g_mix, w_in_l, conv_all, l, cw, cb, dtb, alog, bi, bf, state_mlstm_n[l].reshape(bd, D_MLSTM),
            m_tile)
        q8 = proj_s[:, C_Q:C_Q + D_ATT].reshape(bd, ATT_HEADS, ATT_HEAD_DIM)
        zero = jnp.zeros_like(q8)
        first_kv = (jnp.arange(ATT_HEADS) < ATT_GROUP)[None, :, None]
        qb = jnp.where(first_kv, jnp.concatenate([q8, zero], -1), jnp.concatenate([zero, q8], -1))
        kn = proj_s[:, C_K:C_K + D_KV].reshape(bd, 1, D_KV)
        vn = proj_s[:, C_V:C_V + D_KV].reshape(bd, 1, D_KV)
        sink_rows = jnp.broadcast_to(sinks[:, None], (ATT_HEADS, LANE))
        att_s, sk, sv = _attn_sample(qb, kn, vn, kc_all, vc_all, sink_rows, tab_s, l, kv_new, tb)
        kv_new = (sk, sv)
        side = (ea, ssm_all, xdt, bm, cm, gq, esw, mem_all, proj_s, ks, s_new, c_new)
        hp, s_new, y_read, c_new, qc = _mlp(x1, g_mlp, w_up, w_down, gf, l, last, tm_mlp, 512, side)
        ys, hms = _sample_post(proj_s, y_read, xs, dskip_rep, wns, qc, gq, sw, den, wnm)
        x1s = _outproj(att_s.reshape(bd, D_ATT), ys, hms, hs, w_out, l, bd)
        hs = _mlp(x1s, g_mlp, w_up, w_down, gf, l, last, bd, 512)
        st_s.append((conv_new,
                     n_new.reshape(bd, MLSTM_HEADS, MLSTM_HEAD_DIM),
                     mt[:, GATE_LANE:GATE_LANE + MLSTM_HEADS]))

    y_prompt = hp.reshape(bsz, seq, d_model)
    y_sample = hs.reshape(bd, 1, d_model)
    p_out = [jnp.stack([s[i] for s in st_p]) for i in range(7)]
    s_conv, s_n, s_m = [jnp.stack([s[i] for s in st_s]) for i in range(3)]
    s_conv = jnp.swapaxes(s_conv, 1, 2)
    to_cache = lambda a: jnp.transpose(a.reshape(depth, bd, ATT_KV_HEADS, ATT_HEAD_DIM, win), (0, 1, 4, 2, 3))
    s_k = to_cache(kv_new[0])
    s_v = to_cache(kv_new[1])
    s_ssm = s_new.reshape(depth, bd, SSM_HEADS, SSM_HEAD_DIM, D_STATE)
    s_c = c_new.reshape(depth, bd, MLSTM_HEADS, MLSTM_HEAD_DIM, MLSTM_HEAD_DIM)
    return (y_prompt, y_sample, *p_out, s_k, s_v, s_conv, s_ssm, s_c, s_n, s_m)
```

```python
import functools

import jax
import jax.numpy as jnp
from jax import lax
from jax.experimental import pallas as pl
from jax.experimental.pallas import tpu as pltpu

F32 = jnp.float32
BF16 = jnp.bfloat16

D_MODEL = 2048
EPS = 1e-6
PAST_LEN = 8192
ATT_HEAD_DIM = 64
ATT_HEADS = 8
ATT_KV_HEADS = 2
ATT_GROUP = ATT_HEADS // ATT_KV_HEADS
D_ATT = ATT_HEADS * ATT_HEAD_DIM
D_KV = ATT_KV_HEADS * ATT_HEAD_DIM
WINDOW = 128
ROPE_THETA = 500000.0
ROPE_DIM = ATT_HEAD_DIM // 4
SSM_HEAD_DIM = 64
SSM_HEADS = 16
D_SSM = SSM_HEADS * SSM_HEAD_DIM
SSM_GROUPS = 2
D_STATE = 128
CONV_WIDTH = 4
CONV_DIM = D_SSM + 2 * SSM_GROUPS * D_STATE
CHUNK = 128
MLSTM_HEADS = 4
MLSTM_HEAD_DIM = 128
D_MLSTM = MLSTM_HEADS * MLSTM_HEAD_DIM
D_FF = 4 * D_MODEL

C_XBC = 0
C_Q = 1536
C_Z = 2048
C_MQ = 3072
C_MK = 3584
C_MV = 4096
C_MO = 4608
C_K = 5120
C_V = 5248
C_G = 5376
NP = 5632
GATE_LANE = 16

LANE = 128
VMEM_LIMIT = 60 * 1024 * 1024


def _cparams(sem):
    return pltpu.CompilerParams(dimension_semantics=sem, vmem_limit_bytes=VMEM_LIMIT)


def _silu(x):
    return x * jax.nn.sigmoid(x)


def _dot(a, b):
    return jnp.dot(a.astype(BF16), b.astype(BF16), preferred_element_type=F32)


def _dot_nt(a, b):
    return lax.dot_general(a.astype(BF16), b.astype(BF16), (((1,), (1,)), ((), ())),
                           preferred_element_type=F32)


def _dot_tn(a, b):
    return lax.dot_general(a.astype(BF16), b.astype(BF16), (((0,), (0,)), ((), ())),
                           preferred_element_type=F32)


def _tril(n):
    row = lax.broadcasted_iota(jnp.int32, (n, n), 0)
    col = lax.broadcasted_iota(jnp.int32, (n, n), 1)
    return row >= col


def _cumsum_rows(x):
    row = lax.broadcasted_iota(jnp.int32, x.shape, 0)
    step = 1
    while step < x.shape[0]:
        x = x + jnp.where(row >= step, pltpu.roll(x, step, 0), 0.0)
        step *= 2
    return x


_SRC_Q, _SRC_K, _SRC_V, _SRC_Z, _SRC_XBC, _SRC_DT, _SRC_MQ = 0, 512, 640, 768, 1792, 3328, 3344
_SRC_MI = _SRC_MQ + 4 * D_MLSTM
IN_WIDTH = _SRC_MI + 2 * MLSTM_HEADS


def _prep_w_in_kernel(w_ref, o_ref):
    def put(dst, src, n):
        o_ref[:, dst:dst + n] = w_ref[src:src + n, :].T.astype(BF16)

    put(C_XBC, _SRC_XBC, CONV_DIM)
    put(C_Q, _SRC_Q, D_ATT)
    put(C_Z, _SRC_Z, D_SSM)
    put(C_MQ, _SRC_MQ, 4 * D_MLSTM)
    put(C_K, _SRC_K, D_KV)
    put(C_V, _SRC_V, D_KV)
    tk = w_ref.shape[1]
    dt = w_ref[_SRC_DT:_SRC_DT + SSM_HEADS, :]
    gates = w_ref[_SRC_MI:_SRC_MI + 2 * MLSTM_HEADS, :]
    pad = jnp.zeros((LANE - GATE_LANE - 2 * MLSTM_HEADS, tk), F32)
    lo = jnp.concatenate([dt, gates, pad], axis=0)
    hi = jnp.concatenate([jnp.zeros_like(dt), pltpu.roll(gates, MLSTM_HEADS, 0), pad], axis=0)
    o_ref[:, C_G:C_G + LANE] = lo.T.astype(BF16)
    o_ref[:, C_G + LANE:C_G + 2 * LANE] = hi.T.astype(BF16)


def _prep_w_in(w_in, layer, tk):
    depth, k, n = w_in.shape
    assert n == IN_WIDTH and n % 8 == 0
    wt = jnp.swapaxes(w_in, 1, 2).reshape(depth * n, k)
    return pl.pallas_call(
        _prep_w_in_kernel,
        grid=(k // tk,),
        in_specs=[pl.BlockSpec((n, tk), lambda i: (layer, i))],
        out_specs=pl.BlockSpec((tk, NP), lambda i: (i, 0)),
        out_shape=jax.ShapeDtypeStruct((k, NP), BF16),
        compiler_params=_cparams(("parallel",)),
        name="prep_w_in",
    )(wt)


def _outproj_kernel(a_ref, y_ref, h_ref, x_ref, w_ref, o_ref, wb_ref):
    @pl.when(pl.program_id(0) == 0)
    def _():
        wb_ref[...] = w_ref[...].astype(BF16)

    acc = jnp.dot(a_ref[...], wb_ref[0:D_ATT, :], preferred_element_type=F32)
    acc = acc + jnp.dot(y_ref[...], wb_ref[D_ATT:D_ATT + D_SSM, :], preferred_element_type=F32)
    acc = acc + jnp.dot(h_ref[...], wb_ref[D_ATT + D_SSM:, :], preferred_element_type=F32)
    o_ref[...] = x_ref[...] + acc


def _outproj(att, y, h, x, w, layer, tm):
    m, n = x.shape
    k = w.shape[1]
    rows = lambda width: pl.BlockSpec((tm, width), lambda i: (i, 0))
    return pl.pallas_call(
        _outproj_kernel,
        grid=(m // tm,),
        in_specs=[rows(D_ATT), rows(D_SSM), rows(D_MLSTM), rows(n),
                  pl.BlockSpec((None, k, n), lambda i: (layer, 0, 0), pipeline_mode=pl.Buffered(1))],
        out_specs=rows(n),
        out_shape=jax.ShapeDtypeStruct((m, n), F32),
        scratch_shapes=[pltpu.VMEM((k, n), BF16)],
        compiler_params=_cparams(("arbitrary",)),
        name="outproj",
    )(att, y, h, x, w)


N_MLP_IN, N_SSM_SIDE_IN, N_MLSTM_SIDE_IN = 5, 5, 6


def _mlp_kernel(*refs, final_norm, side, n_prev):
    x_ref, g_ref, wu_ref, wd_ref, gf_ref = refs[:N_MLP_IN]
    n_side = N_SSM_SIDE_IN + N_MLSTM_SIDE_IN if side else 0
    side_in = refs[N_MLP_IN:N_MLP_IN + n_side]
    outs = refs[N_MLP_IN + n_side + n_prev:]
    o_ref = outs[0]
    u_ref = outs[5] if side else outs[1]
    j = pl.program_id(1)

    @pl.when(j == 0)
    def _():
        x = x_ref[...]
        ms = jnp.mean(x * x, axis=-1, keepdims=True)
        u_ref[...] = (x * lax.rsqrt(ms + EPS) * g_ref[...]).astype(BF16)
        o_ref[...] = x
        if side:
            outs[6][...] = jnp.zeros_like(outs[6])
            outs[7][...] = jnp.zeros_like(outs[7])

    h = _dot(u_ref[...], wu_ref[...])
    h = jnp.square(jnp.maximum(h, 0.0))
    o_ref[...] += _dot(h, wd_ref[...])

    if side:
        seq = pl.program_id(0) * pl.num_programs(1) + j
        row = seq % SEQ_ROWS
        so_ref, y_ref, co_ref, qc_ref = outs[1:5]
        yacc_ref, qacc_ref = outs[6:8]
        _ssm_sample_one(0, row, seq, *side_in[:N_SSM_SIDE_IN], so_ref, y_ref, yacc_ref)
        _mlstm_sample_one(0, row, seq, *side_in[N_SSM_SIDE_IN:], co_ref, qc_ref, qacc_ref)

    if final_norm:
        @pl.when(j == pl.num_programs(1) - 1)
        def _():
            y = o_ref[...]
            ms = jnp.mean(y * y, axis=-1, keepdims=True)
            o_ref[...] = y * lax.rsqrt(ms + EPS) * gf_ref[...]


def _mlp(x, g, wu, wd, gf, layer, final_norm, tm, tf, side=None):
    m, d = x.shape
    ff = wu.shape[2]
    nj = ff // tf
    in_specs = [pl.BlockSpec((tm, d), lambda i, j: (i, 0)),
                pl.BlockSpec((1, d), lambda i, j: (0, 0)),
                pl.BlockSpec((None, d, tf), lambda i, j: (layer, 0, j)),
                pl.BlockSpec((None, tf, d), lambda i, j: (layer, j, 0)),
                pl.BlockSpec((1, d), lambda i, j: (0, 0))]
    out_specs = [pl.BlockSpec((tm, d), lambda i, j: (i, 0))]
    out_shape = [jax.ShapeDtypeStruct((m, d), F32)]
    operands = [x, g, wu, wd, gf]
    aliases = {}
    n_prev = 0
    if side is not None:
        ea, ssm, xdt, bvec, cvec, gq, esw, mem, proj_s, ks, prev_s, prev_c = side
        depth, bd = ssm.shape[0], ssm.shape[1]
        assert (m // tm) * nj == bd and nj % SEQ_ROWS == 0
        smem = pl.BlockSpec(memory_space=pltpu.SMEM)
        rows = lambda n, col=0: pl.BlockSpec((SEQ_ROWS, n), lambda i, j: ((i * nj + j) // SEQ_ROWS, col // n))
        state = lambda r, n: pl.BlockSpec((None, 1, r, n), lambda i, j: (layer, i * nj + j, 0, 0))
        in_specs += [smem, state(D_SSM, D_STATE), rows(D_SSM), rows(SSM_GROUPS * D_STATE),
                     rows(SSM_GROUPS * D_STATE),
                     smem, smem, state(D_MLSTM, MLSTM_HEAD_DIM), rows(D_MLSTM, C_MQ), rows(D_MLSTM),
                     rows(D_MLSTM, C_MV)]
        operands += [ea, ssm, xdt, bvec, cvec, gq, esw, mem, proj_s, ks, proj_s]
        out_specs += [state(D_SSM, D_STATE), rows(D_SSM), state(D_MLSTM, MLSTM_HEAD_DIM), rows(D_MLSTM)]
        out_shape += [jax.ShapeDtypeStruct((depth, bd, D_SSM, D_STATE), F32),
                      jax.ShapeDtypeStruct((bd, D_SSM), F32),
                      jax.ShapeDtypeStruct((depth, bd, D_MLSTM, MLSTM_HEAD_DIM), F32),
                      jax.ShapeDtypeStruct((bd, D_MLSTM), F32)]
        if prev_s is not None:
            aliases = {len(operands): 1, len(operands) + 1: 3}
            operands += [prev_s, prev_c]
            in_specs += [pl.BlockSpec(memory_space=pl.ANY)] * 2
            n_prev = 2
    res = pl.pallas_call(
        functools.partial(_mlp_kernel, final_norm=final_norm, side=side is not None, n_prev=n_prev),
        grid=(m // tm, nj),
        in_specs=in_specs,
        out_specs=out_specs,
        out_shape=out_shape,
        input_output_aliases=aliases,
        scratch_shapes=[pltpu.VMEM((tm, d), BF16)] + ([] if side is None else [
            pltpu.VMEM((SEQ_ROWS, D_SSM), F32), pltpu.VMEM((SEQ_ROWS, D_MLSTM), F32)]),
        compiler_params=_cparams(("parallel", "arbitrary")),
        name="mlp",
    )(*operands)
    return res[0] if side is None else res


def _rope_tables(pos):
    half = ROPE_DIM // 2
    inv = jnp.power(jnp.float32(ROPE_THETA), -jnp.arange(half, dtype=jnp.float32) / half)
    ang = pos.astype(jnp.float32)[:, None] * inv[None, :]
    cos = jnp.cos(ang)
    sin = jnp.sin(ang)
    n = pos.shape[0]
    rest = ATT_HEAD_DIM - ROPE_DIM
    c = jnp.concatenate([cos, cos, jnp.ones((n, rest), F32)], axis=1)
    s1 = jnp.concatenate([-sin, jnp.zeros((n, half + rest), F32)], axis=1)
    s2 = jnp.concatenate([jnp.zeros((n, half), F32), sin, jnp.zeros((n, rest), F32)], axis=1)
    rep = LANE // ATT_HEAD_DIM
    return jnp.tile(c, (1, rep)), jnp.tile(s1, (1, rep)), jnp.tile(s2, (1, rep))


def _rope(x, c, s1, s2):
    width = x.shape[-1]
    rep = width // LANE
    half = ROPE_DIM // 2
    if rep > 1:
        c = jnp.concatenate([c] * rep, axis=-1)
        s1 = jnp.concatenate([s1] * rep, axis=-1)
        s2 = jnp.concatenate([s2] * rep, axis=-1)
    axis = x.ndim - 1
    return x * c + pltpu.roll(x, width - half, axis) * s1 + pltpu.roll(x, half, axis) * s2


def _attn_prompt_init(ko_ref, vo_ref):
    ko_ref[...] = jnp.zeros_like(ko_ref)
    vo_ref[...] = jnp.zeros_like(vo_ref)


def _attn_prompt_body(j, sink_ref, q_ref, k_ref, v_ref, c_ref, s1_ref, s2_ref, o_ref, ko_ref, vo_ref):
    w = WINDOW
    c = c_ref[...]
    s1 = s1_ref[...]
    s2 = s2_ref[...]
    krot = _rope(k_ref[...], c, s1, s2)
    v = v_ref[...]
    qrot = _rope(q_ref[...], c, s1, s2)
    kk = jnp.concatenate([ko_ref[...], krot], axis=0).astype(BF16)
    vv = jnp.concatenate([vo_ref[...], v], axis=0).astype(BF16)

    row = lax.broadcasted_iota(jnp.int32, (w, 2 * w), 0)
    col = lax.broadcasted_iota(jnp.int32, (w, 2 * w), 1)
    first_col = jnp.where(j > 0, 0, w)
    mask = (col >= row) & (col <= row + w) & (col >= first_col)
    scale = ATT_HEAD_DIM ** -0.5

    for h in range(ATT_KV_HEADS):
        kh = kk[:, h * ATT_HEAD_DIM:(h + 1) * ATT_HEAD_DIM]
        vh = vv[:, h * ATT_HEAD_DIM:(h + 1) * ATT_HEAD_DIM]
        for g in range(ATT_GROUP):
            hg = h * ATT_GROUP + g
            qh = qrot[:, hg * ATT_HEAD_DIM:(hg + 1) * ATT_HEAD_DIM]
            s = _dot_nt(qh, kh) * scale
            s = jnp.where(mask, s, -jnp.inf)
            sink = sink_ref[hg]
            m = jnp.maximum(jnp.max(s, axis=-1, keepdims=True), sink)
            p = jnp.exp(s - m)
            denom = jnp.sum(p, axis=-1, keepdims=True) + jnp.exp(sink - m)
            o = _dot(p, vh) / denom
            o_ref[:, hg * ATT_HEAD_DIM:(hg + 1) * ATT_HEAD_DIM] = o.astype(o_ref.dtype)
            yield

    ko_ref[...] = krot
    vo_ref[...] = v


def _group_rmsnorm(y, w, groups):
    width = y.shape[-1] // groups
    outs = []
    for g in range(groups):
        yg = y[:, g * width:(g + 1) * width]
        ms = jnp.mean(yg * yg, axis=-1, keepdims=True)
        outs.append(yg * lax.rsqrt(ms + EPS) * w[:, g * width:(g + 1) * width])
    return jnp.concatenate(outs, axis=-1)


CONV_PAD = 8


def _ssd_prompt_init(s_ref, tail_ref):
    tail_ref[...] = jnp.zeros_like(tail_ref)
    s_ref[...] = jnp.zeros_like(s_ref)


def _ssd_prompt_body(xbc_ref, z_ref, g_ref, cw_ref, cb_ref, dtb_ref, alog_ref, dskip_ref, wn_ref,
                     y_ref, s_ref, tail_ref, ybuf_ref, xw_ref):
    q = CHUNK
    pad = CONV_PAD
    hp = SSM_HEAD_DIM
    heads_per_group = SSM_HEADS // SSM_GROUPS
    gw = heads_per_group * hp

    x = xbc_ref[...]
    cw = cw_ref[...]
    row8 = lax.broadcasted_iota(jnp.int32, (pad, CONV_DIM), 0)
    acc = None
    for j in range(CONV_WIDTH - 1):
        shift = CONV_WIDTH - 1 - j
        rolled = pltpu.roll(x, shift, 0)
        head = jnp.where(row8 < shift, tail_ref[j * pad:(j + 1) * pad, :], rolled[0:pad, :])
        tail_ref[j * pad:(j + 1) * pad, :] = rolled[0:pad, :]
        term = jnp.concatenate([head, rolled[pad:, :]], axis=0) * cw[j:j + 1, :]
        acc = term if acc is None else acc + term
    acc = acc + x * cw[CONV_WIDTH - 1:CONV_WIDTH, :]
    xc = _silu(acc + cb_ref[...])

    xs = xc[:, 0:D_SSM]
    bm = xc[:, D_SSM:D_SSM + SSM_GROUPS * D_STATE]
    cm = xc[:, D_SSM + SSM_GROUPS * D_STATE:]

    dt = jax.nn.softplus(g_ref[:, 0:LANE] + dtb_ref[...])
    a_neg = -jnp.exp(alog_ref[...])
    a_col = _cumsum_rows(dt * a_neg)
    a_row = a_col.T
    dt_row = dt.T
    wk_row = jnp.exp(a_row[:, q - 1:q] - a_row) * dt_row
    tril = _tril(q)
    xs_t = xs.T
    ea_col = jnp.exp(a_col)
    left = lax.broadcasted_iota(jnp.int32, (q, LANE), 1) < hp
    ea_full = jnp.concatenate(
        [jnp.where(left, ea_col[:, 2 * t:2 * t + 1], ea_col[:, 2 * t + 1:2 * t + 2]) for t in range(SSM_HEADS // 2)],
        axis=1)
    yield

    cs_parts = []
    for g in range(SSM_GROUPS):
        bg = bm[:, g * D_STATE:(g + 1) * D_STATE]
        cg = cm[:, g * D_STATE:(g + 1) * D_STATE]
        cb = _dot_nt(cg, bg)
        cs_parts.append(_dot_nt(cg, s_ref[g * gw:(g + 1) * gw, :]))
        for r in range(heads_per_group):
            h = g * heads_per_group + r
            seg = a_col[:, h:h + 1] - a_row[h:h + 1, :]
            wmat = jnp.exp(jnp.where(tril, seg, -jnp.inf)) * cb * dt_row[h:h + 1, :]
            ybuf_ref[:, h * hp:(h + 1) * hp] = _dot(wmat, xs[:, h * hp:(h + 1) * hp])
            xw_ref[r * hp:(r + 1) * hp, :] = xs_t[h * hp:(h + 1) * hp, :] * wk_row[h:h + 1, :]
            yield
        upd = _dot(xw_ref[...], bg)
        for r in range(heads_per_group):
            h = g * heads_per_group + r
            decay = jnp.exp(a_row[h:h + 1, q - 1:q])
            s_ref[h * hp:(h + 1) * hp, :] = s_ref[h * hp:(h + 1) * hp, :] * decay + upd[r * hp:(r + 1) * hp, :]
        yield

    y = ybuf_ref[...] + jnp.concatenate(cs_parts, axis=1) * ea_full
    y = y + dskip_ref[...] * xs
    y = y * _silu(z_ref[...])
    y_ref[...] = _group_rmsnorm(y, wn_ref[...], SSM_GROUPS).astype(y_ref.dtype)


def _mlstm_prompt_init(c_ref, n_ref, m_ref):
    c_ref[...] = jnp.zeros_like(c_ref)
    n_ref[...] = jnp.zeros_like(n_ref)
    m_ref[...] = jnp.zeros_like(m_ref)


def _mlstm_prompt_body(q_ref, k_ref, v_ref, o_ref, g_ref, bi_ref, bf_ref, wn_ref, h_ref, c_ref, n_ref, m_ref):
    t = CHUNK
    d = MLSTM_HEAD_DIM

    gi = g_ref[:, 0:LANE] + bi_ref[...]
    gf = g_ref[:, LANE:2 * LANE] + bf_ref[...]
    b_col = _cumsum_rows(jax.nn.log_sigmoid(gf))
    b_row = b_col.T
    i_row = gi.T
    tril = _tril(t)
    k_all = k_ref[...] * (d ** -0.5)

    for h in range(MLSTM_HEADS):
        ln = GATE_LANE + h
        bq = b_col[:, ln:ln + 1]
        logw = jnp.where(tril, bq - b_row[ln:ln + 1, :] + i_row[ln:ln + 1, :], -jnp.inf)
        m_prev = m_ref[h:h + 1, 0:1]
        log_inter = bq + m_prev
        mt = jnp.maximum(log_inter, jnp.max(logw, axis=-1, keepdims=True))
        qh = q_ref[:, h * d:(h + 1) * d]
        kh = k_all[:, h * d:(h + 1) * d]
        vh = v_ref[:, h * d:(h + 1) * d]
        ch = c_ref[h * d:(h + 1) * d, :]
        nh = n_ref[h:h + 1, :]
        sw = jnp.exp(logw - mt) * _dot_nt(qh, kh)
        gq = jnp.exp(log_inter - mt)
        num = _dot(sw, vh) + _dot(qh, ch) * gq
        den = jnp.sum(sw, axis=-1, keepdims=True) + jnp.sum(qh * nh, axis=-1, keepdims=True) * gq
        hh = num / jnp.maximum(jnp.abs(den), jnp.exp(-mt))

        m_new = mt[t - 1:t, :]
        b_last = b_col[t - 1:t, ln:ln + 1]
        wk = jnp.exp(b_last - bq + gi[:, ln:ln + 1] - m_new)
        g_end = jnp.exp(b_last + m_prev - m_new)
        kw = kh * wk
        c_ref[h * d:(h + 1) * d, :] = ch * g_end + _dot_tn(kw, vh)
        n_ref[h:h + 1, :] = nh * g_end + jnp.sum(kw, axis=0, keepdims=True)
        m_ref[h:h + 1, :] = jnp.broadcast_to(m_new, (1, LANE))

        ms = jnp.mean(hh * hh, axis=-1, keepdims=True)
        hn = hh * lax.rsqrt(ms + EPS) * wn_ref[:, h * d:(h + 1) * d]
        h_ref[:, h * d:(h + 1) * d] = (hn * jax.nn.sigmoid(o_ref[:, h * d:(h + 1) * d])).astype(h_ref.dtype)
        yield

def _interleave(stages):
    live = [[gen, 0, n] for gen, n in stages]
    while live:
        item = min(live, key=lambda s: s[1] / s[2])
        try:
            next(item[0])
            item[1] += 1
        except StopIteration:
            live.remove(item)


PROJ_TILE = 256


def _project_next(x_ref, g_ref, w_ref, pnext_ref):
    x = x_ref[...]
    u = (x * lax.rsqrt(jnp.mean(x * x, axis=-1, keepdims=True) + EPS) * g_ref[...]).astype(BF16)
    for ct in range(NP // PROJ_TILE):
        cols = slice(ct * PROJ_TILE, (ct + 1) * PROJ_TILE)
        pnext_ref[:, cols] = jnp.dot(u, w_ref[:, cols], preferred_element_type=F32)
        yield


def _front_kernel(sink_ref, x_ref, x0_ref, g_ref, w_ref, c_ref, s1_ref, s2_ref,
                  cw_ref, cb_ref, dtb_ref, alog_ref, dskip_ref, wns_ref, bi_ref, bf_ref, wnm_ref,
                  att_ref, ko_ref, vo_ref, y_ref, s_ref, h_ref, mem_ref, n_ref, m_ref, ptail_ref,
                  pcur_ref, pnext_ref, tail_ref, ybuf_ref, xw_ref, *, nblk):
    j = pl.program_id(1)
    first = (pl.program_id(0) == 0) & (j == 0)
    t = CHUNK

    @pl.when(first)
    def _():
        for _ in _project_next(x0_ref, g_ref, w_ref, pnext_ref):
            pass

    @pl.when(j == 0)
    def _():
        _attn_prompt_init(ko_ref, vo_ref)
        _ssd_prompt_init(s_ref, tail_ref)
        _mlstm_prompt_init(mem_ref, n_ref, m_ref)

    pcur_ref[...] = pnext_ref[...]

    def attn(i):
        seg = lambda col, width: pcur_ref.at[i * t:(i + 1) * t, col:col + width]
        rows = lambda ref: ref.at[i * t:(i + 1) * t, :]
        return _attn_prompt_body(j * nblk + i, sink_ref, seg(C_Q, D_ATT), seg(C_K, D_KV), seg(C_V, D_KV),
                                 rows(c_ref), rows(s1_ref), rows(s2_ref), rows(att_ref), ko_ref, vo_ref)

    def ssd(i):
        seg = lambda col, width: pcur_ref.at[i * t:(i + 1) * t, col:col + width]
        return _ssd_prompt_body(seg(C_XBC, CONV_DIM), seg(C_Z, D_SSM), seg(C_G, 2 * LANE), cw_ref, cb_ref, dtb_ref,
                                alog_ref, dskip_ref, wns_ref, y_ref.at[i * t:(i + 1) * t, :], s_ref,
                                tail_ref, ybuf_ref, xw_ref)

    def mlstm(i):
        seg = lambda col, width: pcur_ref.at[i * t:(i + 1) * t, col:col + width]
        return _mlstm_prompt_body(seg(C_MQ, D_MLSTM), seg(C_MK, D_MLSTM), seg(C_MV, D_MLSTM), seg(C_MO, D_MLSTM),
                                  seg(C_G, 2 * LANE), bi_ref, bf_ref, wnm_ref, h_ref.at[i * t:(i + 1) * t, :],
                                  mem_ref, n_ref, m_ref)

    def blocks(body):
        for i in range(nblk):
            yield from body(i)

    ptail_ref[...] = pcur_ref[nblk * t - CONV_PAD:nblk * t, C_XBC:C_XBC + CONV_DIM]
    _interleave([
        (_project_next(x_ref, g_ref, w_ref, pnext_ref), NP // PROJ_TILE + 1),
        (blocks(attn), nblk * (ATT_HEADS + 1)),
        (blocks(ssd), nblk * (SSM_HEADS + SSM_GROUPS + 2)),
        (blocks(mlstm), nblk * (MLSTM_HEADS + 1)),
    ])


def _front(x, g_mix, w, sinks, tables, cw, cb, dtb, alog, dskip_rep, wns, bi, bf, wnm, bsz, seq):
    nblk = 2 if (seq // CHUNK) % 2 == 0 else 1
    t = nblk * CHUNK
    nb = seq // t
    total = bsz * nb
    c, s1, s2 = tables
    row = lambda b, j: b * nb + j
    full = lambda shape: pl.BlockSpec(shape, lambda b, j: (0,) * len(shape))
    vec = lambda n: full((1, n))
    tab = pl.BlockSpec((t, LANE), lambda b, j: (j, 0))
    rows_out = lambda width: pl.BlockSpec((t, width), lambda b, j: (row(b, j), 0))
    per_seq = lambda r, n: pl.BlockSpec((None, r, n), lambda b, j: (b, 0, 0))
    return pl.pallas_call(
        functools.partial(_front_kernel, nblk=nblk),
        grid=(bsz, nb),
        in_specs=[pl.BlockSpec(memory_space=pltpu.SMEM),
                  pl.BlockSpec((t, D_MODEL), lambda b, j: (jnp.minimum(row(b, j) + 1, total - 1), 0)),
                  pl.BlockSpec((t, D_MODEL), lambda b, j: (0, 0), pipeline_mode=pl.Buffered(1)),
                  vec(D_MODEL),
                  pl.BlockSpec((D_MODEL, NP), lambda b, j: (0, 0), pipeline_mode=pl.Buffered(1)),
                  tab, tab, tab,
                  full((CONV_WIDTH, CONV_DIM)), vec(CONV_DIM), vec(LANE), vec(LANE), vec(D_SSM), vec(D_SSM),
                  vec(LANE), vec(LANE), vec(D_MLSTM)],
        out_specs=[rows_out(D_ATT), per_seq(CHUNK, D_KV), per_seq(CHUNK, D_KV),
                   rows_out(D_SSM), per_seq(D_SSM, D_STATE),
                   rows_out(D_MLSTM), per_seq(D_MLSTM, MLSTM_HEAD_DIM), per_seq(8, LANE), per_seq(8, LANE),
                   per_seq(CONV_PAD, CONV_DIM)],
        out_shape=[jax.ShapeDtypeStruct((bsz * seq, D_ATT), BF16),
                   jax.ShapeDtypeStruct((bsz, CHUNK, D_KV), F32),
                   jax.ShapeDtypeStruct((bsz, CHUNK, D_KV), F32),
                   jax.ShapeDtypeStruct((bsz * seq, D_SSM), BF16),
                   jax.ShapeDtypeStruct((bsz, D_SSM, D_STATE), F32),
                   jax.ShapeDtypeStruct((bsz * seq, D_MLSTM), BF16),
                   jax.ShapeDtypeStruct((bsz, D_MLSTM, MLSTM_HEAD_DIM), F32),
                   jax.ShapeDtypeStruct((bsz, 8, LANE), F32),
                   jax.ShapeDtypeStruct((bsz, 8, LANE), F32),
                   jax.ShapeDtypeStruct((bsz, CONV_PAD, CONV_DIM), F32)],
        scratch_shapes=[pltpu.VMEM((t, NP), F32),
                        pltpu.VMEM((t, NP), F32),
                        pltpu.VMEM(((CONV_WIDTH - 1) * CONV_PAD, CONV_DIM), F32),
                        pltpu.VMEM((CHUNK, D_SSM), F32),
                        pltpu.VMEM((D_SSM // SSM_GROUPS, CHUNK), F32)],
        compiler_params=_cparams(("arbitrary", "arbitrary")),
        name="front",
    )(sinks, x, x, g_mix, w, c, s1, s2, cw, cb, dtb, alog, dskip_rep, wns, bi, bf, wnm)


def _stacked_out(prev, depth, shape, block, layer):
    spec = pl.BlockSpec((None,) + block, lambda i: (layer,) + (i,) + (0,) * (len(block) - 1))
    out_shape = jax.ShapeDtypeStruct((depth,) + shape, F32)
    extra_in = [] if prev is None else [prev]
    extra_spec = [] if prev is None else [pl.BlockSpec(memory_space=pl.ANY)]
    return spec, out_shape, extra_in, extra_spec


def _drop_aliased(kernel_fn, n_in, n_prev):
    def wrapped(*refs):
        return kernel_fn(*refs[:n_in], *refs[n_in + n_prev:])
    return wrapped


def _attn_sample_kernel(q_ref, k_ref, v_ref, kc_ref, vc_ref, sink_ref, c_ref, s1_ref, s2_ref,
                        o_ref, ko_ref, vo_ref):
    tb, _, w = kc_ref.shape
    c = c_ref[...]
    s1 = s1_ref[...]
    s2 = s2_ref[...]
    qb = _rope(q_ref[...], c, s1, s2)
    kn = _rope(k_ref[...], c, s1, s2)
    vn = v_ref[...]
    kc = kc_ref[...]
    vc = vc_ref[...]
    newest = lax.broadcasted_iota(jnp.int32, (D_KV, w), 1) == w - 1
    for b in range(tb):
        ko_ref[b] = jnp.where(newest, _column_tile(kn[b]), pltpu.roll(kc[b], w - 1, 1))
        vo_ref[b] = jnp.where(newest, _column_tile(vn[b]), pltpu.roll(vc[b], w - 1, 1))

    scale = ATT_HEAD_DIM ** -0.5
    s = jnp.einsum("bhd,bdk->bhk", qb.astype(BF16), kc.astype(BF16),
                   preferred_element_type=F32) * scale
    sn = jnp.sum(qb * kn, axis=-1, keepdims=True) * scale
    sink = sink_ref[...][None, :, 0:1]
    m = jnp.maximum(jnp.maximum(jnp.max(s, axis=-1, keepdims=True), sn), sink)
    p = jnp.exp(s - m)
    pn = jnp.exp(sn - m)
    denom = jnp.sum(p, axis=-1, keepdims=True) + pn + jnp.exp(sink - m)
    p = p / denom
    pn = pn / denom
    o = jnp.einsum("bhk,bdk->bhd", p.astype(BF16), vc.astype(BF16), preferred_element_type=F32)
    o = o + pn * vn
    head = lax.broadcasted_iota(jnp.int32, o.shape[:2] + (ATT_HEAD_DIM,), 1)
    o_ref[...] = jnp.where(head < ATT_GROUP, o[:, :, 0:ATT_HEAD_DIM], o[:, :, ATT_HEAD_DIM:]).astype(o_ref.dtype)


def _attn_sample(qb, kn, vn, kc, vc, sink_rows, tables, layer, prev, tb):
    depth, bd, _, w = kc.shape
    c, s1, s2 = tables
    blk3 = lambda r, n: pl.BlockSpec((tb, r, n), lambda i: (i, 0, 0))
    cache = pl.BlockSpec((None, tb, D_KV, w), lambda i: (layer, i, 0, 0))
    vec = pl.BlockSpec((1, LANE), lambda i: (0, 0))
    prev_k, prev_v = (None, None) if prev is None else prev
    k_spec, k_shape, k_in, k_in_spec = _stacked_out(prev_k, depth, (bd, D_KV, w), (tb, D_KV, w), layer)
    v_spec, v_shape, v_in, v_in_spec = _stacked_out(prev_v, depth, (bd, D_KV, w), (tb, D_KV, w), layer)
    n_in = 9
    n_prev = len(k_in) + len(v_in)
    return pl.pallas_call(
        _drop_aliased(_attn_sample_kernel, n_in, n_prev),
        grid=(bd // tb,),
        in_specs=[blk3(ATT_HEADS, LANE), blk3(1, LANE), blk3(1, LANE), cache, cache,
                  pl.BlockSpec((ATT_HEADS, LANE), lambda i: (0, 0)), vec, vec, vec] + k_in_spec + v_in_spec,
        out_specs=[blk3(ATT_HEADS, ATT_HEAD_DIM), k_spec, v_spec],
        out_shape=[jax.ShapeDtypeStruct((bd, ATT_HEADS, ATT_HEAD_DIM), BF16), k_shape, v_shape],
        input_output_aliases={n_in + t: 1 + t for t in range(n_prev)},
        compiler_params=_cparams(("parallel",)),
        name="attn_sample",
    )(qb, kn, vn, kc, vc, sink_rows, c, s1, s2, *k_in, *v_in)


def _lane_place(cols, lane0):
    m = cols[0].shape[0]
    lane = lax.broadcasted_iota(jnp.int32, (m, LANE), 1)
    out = jnp.zeros((m, LANE), F32)
    for i, col in enumerate(cols):
        out = jnp.where(lane == lane0 + i, col, out)
    return out


PRE_STEPS = 4


def _sample_pre_kernel(x_ref, gmix_ref, w_ref, cs_ref, cw_ref, cb_ref, dtb_ref, alog_ref, bi_ref, bf_ref, n_ref,
                       m_ref, proj_ref, *rest):
    out_refs, u_ref = rest[:-1], rest[-1]
    i = pl.program_id(0)
    tn = NP // PRE_STEPS

    @pl.when(i == 0)
    def _():
        x = x_ref[...]
        u_ref[...] = (x * lax.rsqrt(jnp.mean(x * x, axis=-1, keepdims=True) + EPS) * gmix_ref[...]).astype(BF16)

    for k in range(PRE_STEPS):
        @pl.when(i == k)
        def _(k=k):
            proj_ref[:, k * tn:(k + 1) * tn] = jnp.dot(u_ref[...], w_ref[...], preferred_element_type=F32)

    @pl.when(i == PRE_STEPS - 1)
    def _():
        _sample_pre_tail(proj_ref, cs_ref, cw_ref, cb_ref, dtb_ref, alog_ref, bi_ref, bf_ref, n_ref, m_ref, *out_refs)


def _sample_pre_tail(proj_ref, cs_ref, cw_ref, cb_ref, dtb_ref, alog_ref, bi_ref, bf_ref, n_ref, m_ref,
                     conv_ref, xs_ref, xdt_ref, bm_ref, cm_ref, ea_ref,
                     ks_ref, nn_ref, g_ref, esw_ref, sw_ref, den_ref, mt_ref):
    xbc = proj_ref[:, C_XBC:C_XBC + CONV_DIM]
    s0 = cs_ref[0]
    s1 = cs_ref[1]
    s2 = cs_ref[2]
    cw = cw_ref[...]
    acc = s0 * cw[0:1, :]
    acc = acc + s1 * cw[1:2, :]
    acc = acc + s2 * cw[2:3, :]
    acc = acc + xbc * cw[3:4, :]
    xc = _silu(acc + cb_ref[...])
    conv_ref[0] = s1
    conv_ref[1] = s2
    conv_ref[2] = xbc

    xs = xc[:, 0:D_SSM]
    xs_ref[...] = xs
    bm_ref[...] = xc[:, D_SSM:D_SSM + SSM_GROUPS * D_STATE]
    cm_ref[...] = xc[:, D_SSM + SSM_GROUPS * D_STATE:]
    dt = jax.nn.softplus(proj_ref[:, C_G:C_G + LANE] + dtb_ref[...])
    ea_ref[...] = jnp.exp(dt * (-jnp.exp(alog_ref[...])))
    hp = SSM_HEAD_DIM
    for h in range(SSM_HEADS):
        xdt_ref[:, h * hp:(h + 1) * hp] = xs[:, h * hp:(h + 1) * hp] * dt[:, h:h + 1]

    d = MLSTM_HEAD_DIM
    gi = proj_ref[:, C_G:C_G + LANE] + bi_ref[...]
    gf = proj_ref[:, C_G + LANE:C_G + 2 * LANE] + bf_ref[...]
    log_inter = jax.nn.log_sigmoid(gf) + m_ref[...]
    mt = jnp.maximum(log_inter, gi)
    gq = jnp.exp(log_inter - mt)
    esw = jnp.exp(gi - mt)
    ks = proj_ref[:, C_MK:C_MK + D_MLSTM] * (d ** -0.5)
    ks_ref[...] = ks
    qk_cols = []
    qn_cols = []
    for h in range(MLSTM_HEADS):
        qh = proj_ref[:, C_MQ + h * d:C_MQ + (h + 1) * d]
        kh = ks[:, h * d:(h + 1) * d]
        nh = n_ref[:, h * d:(h + 1) * d]
        qk_cols.append(jnp.sum(qh * kh, axis=-1, keepdims=True))
        qn_cols.append(jnp.sum(qh * nh, axis=-1, keepdims=True))
        ln = GATE_LANE + h
        nn_ref[:, h * d:(h + 1) * d] = nh * gq[:, ln:ln + 1] + kh * esw[:, ln:ln + 1]
    sw = esw * _lane_place(qk_cols, GATE_LANE)
    den = sw + _lane_place(qn_cols, GATE_LANE) * gq
    g_ref[...] = gq
    esw_ref[...] = esw
    sw_ref[...] = sw
    den_ref[...] = jnp.maximum(jnp.abs(den), jnp.exp(-mt))
    mt_ref[...] = mt


def _sample_pre(x, g_mix, w, conv_state, layer, cw, cb, dtb, alog, bi, bf, n_state, m_tile):
    bd = x.shape[0]
    taps = CONV_WIDTH - 1
    tile = jax.ShapeDtypeStruct((bd, LANE), F32)
    whole = lambda a: pl.BlockSpec(a.shape, lambda i: (0,) * a.ndim)
    in_specs = [whole(x), whole(g_mix), pl.BlockSpec((w.shape[0], NP // PRE_STEPS), lambda i: (0, i)),
                pl.BlockSpec((None, taps, bd, CONV_DIM), lambda i: (layer, 0, 0, 0))]
    in_specs += [whole(a) for a in (cw, cb, dtb, alog, bi, bf, n_state, m_tile)]
    outs = [jax.ShapeDtypeStruct((bd, NP), F32),
            jax.ShapeDtypeStruct((taps, bd, CONV_DIM), F32),
            jax.ShapeDtypeStruct((bd, D_SSM), F32),
            jax.ShapeDtypeStruct((bd, D_SSM), F32),
            jax.ShapeDtypeStruct((bd, SSM_GROUPS * D_STATE), F32),
            jax.ShapeDtypeStruct((bd, SSM_GROUPS * D_STATE), F32),
            tile,
            jax.ShapeDtypeStruct((bd, D_MLSTM), F32),
            jax.ShapeDtypeStruct((bd, D_MLSTM), F32),
            tile, tile, tile, tile, tile]
    return pl.pallas_call(
        _sample_pre_kernel,
        grid=(PRE_STEPS,),
        in_specs=in_specs,
        out_specs=[pl.BlockSpec(o.shape, lambda i, nd=len(o.shape): (0,) * nd) for o in outs],
        out_shape=outs,
        scratch_shapes=[pltpu.VMEM((bd, x.shape[1]), BF16)],
        compiler_params=_cparams(("arbitrary",)),
        name="sample_pre",
    )(x, g_mix, w, conv_state, cw, cb, dtb, alog, bi, bf, n_state, m_tile)


def _column_tile(row):
    return jnp.broadcast_to(row, (LANE, LANE)).T


SEQ_ROWS = 8


def _put_row(out_ref, acc_ref, vr, col, row):
    width = row.shape[1]
    rid = lax.broadcasted_iota(jnp.int32, (SEQ_ROWS, width), 0)
    blk = jnp.where(rid == vr, row, acc_ref[:, col:col + width])
    acc_ref[:, col:col + width] = blk
    out_ref[:, col:col + width] = blk


def _ssm_sample_one(b, vr, gb, ea_ref, s_ref, xdt_ref, b_ref, c_ref, so_ref, y_ref, yacc_ref):
    hp = SSM_HEAD_DIM
    heads_per_tile = LANE // hp
    n_tiles = D_SSM // LANE
    tiles_per_group = n_tiles // SSM_GROUPS
    top = lax.broadcasted_iota(jnp.int32, (LANE, 1), 0) < hp
    xrow = xdt_ref[pl.ds(vr, 1), :]
    brow = b_ref[pl.ds(vr, 1), :]
    crow = c_ref[pl.ds(vr, 1), :]
    for t in range(n_tiles):
        g = t // tiles_per_group
        xcol = _column_tile(xrow[:, t * LANE:(t + 1) * LANE])
        bg = brow[:, g * D_STATE:(g + 1) * D_STATE]
        cg = jnp.broadcast_to(crow[:, g * D_STATE:(g + 1) * D_STATE], (8, D_STATE))
        e0 = ea_ref[gb, heads_per_tile * t]
        e1 = ea_ref[gb, heads_per_tile * t + 1]
        decay = jnp.where(top, e0, e1)
        sn = s_ref[b, t * LANE:(t + 1) * LANE, :] * decay + xcol * bg
        so_ref[b, t * LANE:(t + 1) * LANE, :] = sn
        _put_row(y_ref, yacc_ref, vr, t * LANE, _dot_nt(cg, sn)[0:1, :])


def _mlstm_sample_one(b, vr, gb, g_ref, esw_ref, c_ref, q_ref, k_ref, v_ref, co_ref, qc_ref, qacc_ref):
    d = MLSTM_HEAD_DIM
    qrow = q_ref[pl.ds(vr, 1), :]
    krow = k_ref[pl.ds(vr, 1), :]
    vrow = v_ref[pl.ds(vr, 1), :]
    for h in range(MLSTM_HEADS):
        qcol = _column_tile(qrow[:, h * d:(h + 1) * d])
        kcol = _column_tile(krow[:, h * d:(h + 1) * d])
        ch = c_ref[b, h * d:(h + 1) * d, :]
        g_end = g_ref[gb, GATE_LANE + h]
        wk = esw_ref[gb, GATE_LANE + h]
        _put_row(qc_ref, qacc_ref, vr, h * d, jnp.sum(qcol * ch, axis=0, keepdims=True))
        co_ref[b, h * d:(h + 1) * d, :] = ch * g_end + (kcol * wk) * vrow[:, h * d:(h + 1) * d]


def _sample_post_kernel(proj_ref, y_ref, xs_ref, dskip_ref, wns_ref, qc_ref, g_ref, sw_ref, den_ref, wnm_ref,
                        ys_ref, hs_ref):
    y = y_ref[...] + dskip_ref[...] * xs_ref[...]
    y = y * _silu(proj_ref[:, C_Z:C_Z + D_SSM])
    ys_ref[...] = _group_rmsnorm(y, wns_ref[...], SSM_GROUPS).astype(ys_ref.dtype)

    d = MLSTM_HEAD_DIM
    for h in range(MLSTM_HEADS):
        ln = GATE_LANE + h
        vh = proj_ref[:, C_MV + h * d:C_MV + (h + 1) * d]
        num = sw_ref[:, ln:ln + 1] * vh + qc_ref[:, h * d:(h + 1) * d] * g_ref[:, ln:ln + 1]
        hh = num / den_ref[:, ln:ln + 1]
        ms = jnp.mean(hh * hh, axis=-1, keepdims=True)
        hn = hh * lax.rsqrt(ms + EPS) * wnm_ref[:, h * d:(h + 1) * d]
        gate = jax.nn.sigmoid(proj_ref[:, C_MO + h * d:C_MO + (h + 1) * d])
        hs_ref[:, h * d:(h + 1) * d] = (hn * gate).astype(hs_ref.dtype)


def _sample_post(proj, y, xs, dskip_rep, wns, qc, gq, sw, den, wnm):
    bd = proj.shape[0]
    return pl.pallas_call(
        _sample_post_kernel,
        out_shape=[jax.ShapeDtypeStruct((bd, D_SSM), BF16), jax.ShapeDtypeStruct((bd, D_MLSTM), BF16)],
        compiler_params=pltpu.CompilerParams(vmem_limit_bytes=VMEM_LIMIT),
        name="sample_post",
    )(proj, y, xs, dskip_rep, wns, qc, gq, sw, den, wnm)


def _lane_vec(v, lane0):
    return jnp.pad(v.astype(F32), (lane0, LANE - lane0 - v.shape[0]))[None, :]


def _pick(n, candidates):
    for c in candidates:
        if n % c == 0:
            return c
    return n


def kernel(x_prompt, x_sample, cache_swa_k, cache_swa_v, state_conv, state_ssm, state_mlstm_C, state_mlstm_n,
           state_mlstm_m, w_norm_mix, w_in, attn_sinks, conv_w, conv_b, dt_bias, a_log, d_skip, w_norm_ssm,
           igate_b, fgate_b, w_norm_mlstm, w_out, w_norm_mlp, w_up, w_down, w_norm_final):
    bsz, seq, d_model = x_prompt.shape
    bd = x_sample.shape[0]
    depth = w_in.shape[0]
    win = cache_swa_k.shape[2]
    assert x_sample.shape[1] == 1 and seq % CHUNK == 0 and d_model == D_MODEL

    mp = bsz * seq
    tm_mlp = _pick(mp, (1024, 512, 256, 128))
    tb = _pick(bd, (32, 16, 8))

    hp = x_prompt.reshape(mp, d_model)
    hs = x_sample.reshape(bd, d_model)
    tab_p = _rope_tables(jnp.arange(seq, dtype=jnp.int32))
    tab_s = _rope_tables(jnp.arange(1, dtype=jnp.int32) + PAST_LEN)
    gf = w_norm_final[None, :]
    kc_all = jnp.transpose(cache_swa_k, (0, 1, 3, 4, 2)).reshape(depth, bd, D_KV, win)
    vc_all = jnp.transpose(cache_swa_v, (0, 1, 3, 4, 2)).reshape(depth, bd, D_KV, win)
    ssm_all = state_ssm.reshape(depth, bd, D_SSM, D_STATE)
    mem_all = state_mlstm_C.reshape(depth, bd, D_MLSTM, MLSTM_HEAD_DIM)
    conv_all = jnp.swapaxes(state_conv, 1, 2)

    st_p, st_s = [], []
    kv_new = s_new = c_new = None
    for l in range(depth):
        w_in_l = _prep_w_in(w_in, l, 256)
        g_mix = w_norm_mix[l][None, :]
        g_mlp = w_norm_mlp[l][None, :]
        cw = conv_w[l]
        cb = conv_b[l][None, :]
        dtb = _lane_vec(dt_bias[l], 0)
        alog = _lane_vec(a_log[l], 0)
        dskip_rep = jnp.repeat(d_skip[l].astype(F32), SSM_HEAD_DIM)[None, :]
        wns = w_norm_ssm[l][None, :]
        bi = _lane_vec(igate_b[l], GATE_LANE)
        bf = _lane_vec(fgate_b[l], GATE_LANE)
        wnm = w_norm_mlstm[l][None, :]
        sinks = attn_sinks[l].reshape(ATT_HEADS).astype(F32)
        last = l == depth - 1

        att, pk, pv, y, p_ssm, hm, p_c, p_n, p_m, p_tail = _front(
            hp, g_mix, w_in_l, sinks, tab_p, cw, cb, dtb, alog, dskip_rep, wns, bi, bf, wnm, bsz, seq)
        x1 = _outproj(att, y, hm, hp, w_out, l, _pick(mp, (512, 256, 128)))
        p_conv = p_tail[:, CONV_PAD - (CONV_WIDTH - 1):, :]
        st_p.append((pk.reshape(bsz, WINDOW, ATT_KV_HEADS, ATT_HEAD_DIM),
                     pv.reshape(bsz, WINDOW, ATT_KV_HEADS, ATT_HEAD_DIM),
                     p_conv,
                     p_ssm.reshape(bsz, SSM_HEADS, SSM_HEAD_DIM, D_STATE),
                     p_c.reshape(bsz, MLSTM_HEADS, MLSTM_HEAD_DIM, MLSTM_HEAD_DIM),
                     p_n[:, 0:MLSTM_HEADS, :],
                     p_m[:, 0:MLSTM_HEADS, 0]))

        m_tile = jnp.pad(state_mlstm_m[l], ((0, 0), (GATE_LANE, LANE - GATE_LANE - MLSTM_HEADS)))
        (proj_s, conv_new, xs, xdt, bm, cm, ea, ks, n_new, gq, esw, sw, den, mt) = _sample_pre(
            hs, g_mix, w_in_l, conv_all, l, cw, cb, dtb, alog, bi, bf, state_mlstm_n[l].reshape(bd, D_MLSTM),
            m_tile)
        q8 = proj_s[:, C_Q:C_Q + D_ATT].reshape(bd, ATT_HEADS, ATT_HEAD_DIM)
        zero = jnp.zeros_like(q8)
        first_kv = (jnp.arange(ATT_HEADS) < ATT_GROUP)[None, :, None]
        qb = jnp.where(first_kv, jnp.concatenate([q8, zero], -1), jnp.concatenate([zero, q8], -1))
        kn = proj_s[:, C_K:C_K + D_KV].reshape(bd, 1, D_KV)
        vn = proj_s[:, C_V:C_V + D_KV].reshape(bd, 1, D_KV)
        sink_rows = jnp.broadcast_to(sinks[:, None], (ATT_HEADS, LANE))
        att_s, sk, sv = _attn_sample(qb, kn, vn, kc_all, vc_all, sink_rows, tab_s, l, kv_new, tb)
        kv_new = (sk, sv)
        side = (ea, ssm_all, xdt, bm, cm, gq, esw, mem_all, proj_s, ks, s_new, c_new)
        hp, s_new, y_read, c_new, qc = _mlp(x1, g_mlp, w_up, w_down, gf, l, last, tm_mlp, 512, side)
        ys, hms = _sample_post(proj_s, y_read, xs, dskip_rep, wns, qc, gq, sw, den, wnm)
        x1s = _outproj(att_s.reshape(bd, D_ATT), ys, hms, hs, w_out, l, bd)
        hs = _mlp(x1s, g_mlp, w_up, w_down, gf, l, last, bd, 512)
        st_s.append((conv_new,
                     n_new.reshape(bd, MLSTM_HEADS, MLSTM_HEAD_DIM),
                     mt[:, GATE_LANE:GATE_LANE + MLSTM_HEADS]))

    y_prompt = hp.reshape(bsz, seq, d_model)
    y_sample = hs.reshape(bd, 1, d_model)
    p_out = [jnp.stack([s[i] for s in st_p]) for i in range(7)]
    s_conv, s_n, s_m = [jnp.stack([s[i] for s in st_s]) for i in range(3)]
    s_conv = jnp.swapaxes(s_conv, 1, 2)
    to_cache = lambda a: jnp.transpose(a.reshape(depth, bd, ATT_KV_HEADS, ATT_HEAD_DIM, win), (0, 1, 4, 2, 3))
    s_k = to_cache(kv_new[0])
    s_v = to_cache(kv_new[1])
    s_ssm = s_new.reshape(depth, bd, SSM_HEADS, SSM_HEAD_DIM, D_STATE)
    s_c = c_new.reshape(depth, bd, MLSTM_HEADS, MLSTM_HEAD_DIM, MLSTM_HEAD_DIM)
    return (y_prompt, y_sample, *p_out, s_k, s_v, s_conv, s_ssm, s_c, s_n, s_m)
```

```python
import functools

import jax
import jax.numpy as jnp
from jax import lax
from jax.experimental import pallas as pl
from jax.experimental.pallas import tpu as pltpu

F32 = jnp.float32
BF16 = jnp.bfloat16

D_MODEL = 2048
EPS = 1e-6
PAST_LEN = 8192
ATT_HEAD_DIM = 64
ATT_HEADS = 8
ATT_KV_HEADS = 2
ATT_GROUP = ATT_HEADS // ATT_KV_HEADS
D_ATT = ATT_HEADS * ATT_HEAD_DIM
D_KV = ATT_KV_HEADS * ATT_HEAD_DIM
WINDOW = 128
ROPE_THETA = 500000.0
ROPE_DIM = ATT_HEAD_DIM // 4
SSM_HEAD_DIM = 64
SSM_HEADS = 16
D_SSM = SSM_HEADS * SSM_HEAD_DIM
SSM_GROUPS = 2
D_STATE = 128
CONV_WIDTH = 4
CONV_DIM = D_SSM + 2 * SSM_GROUPS * D_STATE
CHUNK = 128
MLSTM_HEADS = 4
MLSTM_HEAD_DIM = 128
D_MLSTM = MLSTM_HEADS * MLSTM_HEAD_DIM
D_FF = 4 * D_MODEL

C_XBC = 0
C_Q = 1536
C_Z = 2048
C_MQ = 3072
C_MK = 3584
C_MV = 4096
C_MO = 4608
C_K = 5120
C_V = 5248
C_G = 5376
NP = 5632
GATE_LANE = 16

LANE = 128
VMEM_LIMIT = 60 * 1024 * 1024


def _cparams(sem):
    return pltpu.CompilerParams(dimension_semantics=sem, vmem_limit_bytes=VMEM_LIMIT)


def _silu(x):
    return x * jax.nn.sigmoid(x)


def _dot(a, b):
    return jnp.dot(a.astype(BF16), b.astype(BF16), preferred_element_type=F32)


def _dot_nt(a, b):
    return lax.dot_general(a.astype(BF16), b.astype(BF16), (((1,), (1,)), ((), ())),
                           preferred_element_type=F32)


def _dot_tn(a, b):
    return lax.dot_general(a.astype(BF16), b.astype(BF16), (((0,), (0,)), ((), ())),
                           preferred_element_type=F32)


def _tril(n):
    row = lax.broadcasted_iota(jnp.int32, (n, n), 0)
    col = lax.broadcasted_iota(jnp.int32, (n, n), 1)
    return row >= col


def _cumsum_rows(x):
    row = lax.broadcasted_iota(jnp.int32, x.shape, 0)
    step = 1
    while step < x.shape[0]:
        x = x + jnp.where(row >= step, pltpu.roll(x, step, 0), 0.0)
        step *= 2
    return x


_SRC_Q, _SRC_K, _SRC_V, _SRC_Z, _SRC_XBC, _SRC_DT, _SRC_MQ = 0, 512, 640, 768, 1792, 3328, 3344
_SRC_MI = _SRC_MQ + 4 * D_MLSTM
IN_WIDTH = _SRC_MI + 2 * MLSTM_HEADS


def _prep_w_in_kernel(w_ref, o_ref):
    def put(dst, src, n):
        o_ref[:, dst:dst + n] = w_ref[src:src + n, :].T.astype(BF16)

    put(C_XBC, _SRC_XBC, CONV_DIM)
    put(C_Q, _SRC_Q, D_ATT)
    put(C_Z, _SRC_Z, D_SSM)
    put(C_MQ, _SRC_MQ, 4 * D_MLSTM)
    put(C_K, _SRC_K, D_KV)
    put(C_V, _SRC_V, D_KV)
    tk = w_ref.shape[1]
    dt = w_ref[_SRC_DT:_SRC_DT + SSM_HEADS, :]
    gates = w_ref[_SRC_MI:_SRC_MI + 2 * MLSTM_HEADS, :]
    pad = jnp.zeros((LANE - GATE_LANE - 2 * MLSTM_HEADS, tk), F32)
    lo = jnp.concatenate([dt, gates, pad], axis=0)
    hi = jnp.concatenate([jnp.zeros_like(dt), pltpu.roll(gates, MLSTM_HEADS, 0), pad], axis=0)
    o_ref[:, C_G:C_G + LANE] = lo.T.astype(BF16)
    o_ref[:, C_G + LANE:C_G + 2 * LANE] = hi.T.astype(BF16)


def _prep_w_in(w_in, layer, tk):
    depth, k, n = w_in.shape
    assert n == IN_WIDTH and n % 8 == 0
    wt = jnp.swapaxes(w_in, 1, 2).reshape(depth * n, k)
    return pl.pallas_call(
        _prep_w_in_kernel,
        grid=(k // tk,),
        in_specs=[pl.BlockSpec((n, tk), lambda i: (layer, i))],
        out_specs=pl.BlockSpec((tk, NP), lambda i: (i, 0)),
        out_shape=jax.ShapeDtypeStruct((k, NP), BF16),
        compiler_params=_cparams(("parallel",)),
        name="prep_w_in",
    )(wt)


def _outproj_kernel(a_ref, y_ref, h_ref, x_ref, w_ref, o_ref, wb_ref):
    @pl.when(pl.program_id(0) == 0)
    def _():
        wb_ref[...] = w_ref[...].astype(BF16)

    acc = jnp.dot(a_ref[...], wb_ref[0:D_ATT, :], preferred_element_type=F32)
    acc = acc + jnp.dot(y_ref[...], wb_ref[D_ATT:D_ATT + D_SSM, :], preferred_element_type=F32)
    acc = acc + jnp.dot(h_ref[...], wb_ref[D_ATT + D_SSM:, :], preferred_element_type=F32)
    o_ref[...] = x_ref[...] + acc


def _outproj(att, y, h, x, w, layer, tm):
    m, n = x.shape
    k = w.shape[1]
    rows = lambda width: pl.BlockSpec((tm, width), lambda i: (i, 0))
    return pl.pallas_call(
        _outproj_kernel,
        grid=(m // tm,),
        in_specs=[rows(D_ATT), rows(D_SSM), rows(D_MLSTM), rows(n),
                  pl.BlockSpec((None, k, n), lambda i: (layer, 0, 0), pipeline_mode=pl.Buffered(1))],
        out_specs=rows(n),
        out_shape=jax.ShapeDtypeStruct((m, n), F32),
        scratch_shapes=[pltpu.VMEM((k, n), BF16)],
        compiler_params=_cparams(("arbitrary",)),
        name="outproj",
    )(att, y, h, x, w)


N_MLP_IN, N_SSM_SIDE_IN, N_MLSTM_SIDE_IN = 5, 5, 6


def _mlp_kernel(*refs, final_norm, side, n_prev):
    x_ref, g_ref, wu_ref, wd_ref, gf_ref = refs[:N_MLP_IN]
    n_side = N_SSM_SIDE_IN + N_MLSTM_SIDE_IN if side else 0
    side_in = refs[N_MLP_IN:N_MLP_IN + n_side]
    outs = refs[N_MLP_IN + n_side + n_prev:]
    o_ref = outs[0]
    u_ref = outs[5] if side else outs[1]
    j = pl.program_id(1)

    @pl.when(j == 0)
    def _():
        x = x_ref[...]
        ms = jnp.mean(x * x, axis=-1, keepdims=True)
        u_ref[...] = (x * lax.rsqrt(ms + EPS) * g_ref[...]).astype(BF16)
        o_ref[...] = x
        if side:
            outs[6][...] = jnp.zeros_like(outs[6])
            outs[7][...] = jnp.zeros_like(outs[7])

    h = _dot(u_ref[...], wu_ref[...])
    h = jnp.square(jnp.maximum(h, 0.0))
    o_ref[...] += _dot(h, wd_ref[...])

    if side:
        seq = pl.program_id(0) * pl.num_programs(1) + j
        row = seq % SEQ_ROWS
        so_ref, y_ref, co_ref, qc_ref = outs[1:5]
        yacc_ref, qacc_ref = outs[6:8]
        _ssm_sample_one(0, row, seq, *side_in[:N_SSM_SIDE_IN], so_ref, y_ref, yacc_ref)
        _mlstm_sample_one(0, row, seq, *side_in[N_SSM_SIDE_IN:], co_ref, qc_ref, qacc_ref)

    if final_norm:
        @pl.when(j == pl.num_programs(1) - 1)
        def _():
            y = o_ref[...]
            ms = jnp.mean(y * y, axis=-1, keepdims=True)
            o_ref[...] = y * lax.rsqrt(ms + EPS) * gf_ref[...]


def _mlp(x, g, wu, wd, gf, layer, final_norm, tm, tf, side=None):
    m, d = x.shape
    ff = wu.shape[2]
    nj = ff // tf
    in_specs = [pl.BlockSpec((tm, d), lambda i, j: (i, 0)),
                pl.BlockSpec((1, d), lambda i, j: (0, 0)),
                pl.BlockSpec((None, d, tf), lambda i, j: (layer, 0, j)),
                pl.BlockSpec((None, tf, d), lambda i, j: (layer, j, 0)),
                pl.BlockSpec((1, d), lambda i, j: (0, 0))]
    out_specs = [pl.BlockSpec((tm, d), lambda i, j: (i, 0))]
    out_shape = [jax.ShapeDtypeStruct((m, d), F32)]
    operands = [x, g, wu, wd, gf]
    aliases = {}
    n_prev = 0
    if side is not None:
        ea, ssm, xdt, bvec, cvec, gq, esw, mem, proj_s, ks, prev_s, prev_c = side
        depth, bd = ssm.shape[0], ssm.shape[1]
        assert (m // tm) * nj == bd and nj % SEQ_ROWS == 0
        smem = pl.BlockSpec(memory_space=pltpu.SMEM)
        rows = lambda n, col=0: pl.BlockSpec((SEQ_ROWS, n), lambda i, j: ((i * nj + j) // SEQ_ROWS, col // n))
        state = lambda r, n: pl.BlockSpec((None, 1, r, n), lambda i, j: (layer, i * nj + j, 0, 0))
        in_specs += [smem, state(D_SSM, D_STATE), rows(D_SSM), rows(SSM_GROUPS * D_STATE),
                     rows(SSM_GROUPS * D_STATE),
                     smem, smem, state(D_MLSTM, MLSTM_HEAD_DIM), rows(D_MLSTM, C_MQ), rows(D_MLSTM),
                     rows(D_MLSTM, C_MV)]
        operands += [ea, ssm, xdt, bvec, cvec, gq, esw, mem, proj_s, ks, proj_s]
        out_specs += [state(D_SSM, D_STATE), rows(D_SSM), state(D_MLSTM, MLSTM_HEAD_DIM), rows(D_MLSTM)]
        out_shape += [jax.ShapeDtypeStruct((depth, bd, D_SSM, D_STATE), F32),
                      jax.ShapeDtypeStruct((bd, D_SSM), F32),
                      jax.ShapeDtypeStruct((depth, bd, D_MLSTM, MLSTM_HEAD_DIM), F32),
                      jax.ShapeDtypeStruct((bd, D_MLSTM), F32)]
        if prev_s is not None:
            aliases = {len(operands): 1, len(operands) + 1: 3}
            operands += [prev_s, prev_c]
            in_specs += [pl.BlockSpec(memory_space=pl.ANY)] * 2
            n_prev = 2
    res = pl.pallas_call(
        functools.partial(_mlp_kernel, final_norm=final_norm, side=side is not None, n_prev=n_prev),
        grid=(m // tm, nj),
        in_specs=in_specs,
        out_specs=out_specs,
        out_shape=out_shape,
        input_output_aliases=aliases,
        scratch_shapes=[pltpu.VMEM((tm, d), BF16)] + ([] if side is None else [
            pltpu.VMEM((SEQ_ROWS, D_SSM), F32), pltpu.VMEM((SEQ_ROWS, D_MLSTM), F32)]),
        compiler_params=_cparams(("parallel", "arbitrary")),
        name="mlp",
    )(*operands)
    return res[0] if side is None else res


def _rope_tables(pos):
    half = ROPE_DIM // 2
    inv = jnp.power(jnp.float32(ROPE_THETA), -jnp.arange(half, dtype=jnp.float32) / half)
    ang = pos.astype(jnp.float32)[:, None] * inv[None, :]
    cos = jnp.cos(ang)
    sin = jnp.sin(ang)
    n = pos.shape[0]
    rest = ATT_HEAD_DIM - ROPE_DIM
    c = jnp.concatenate([cos, cos, jnp.ones((n, rest), F32)], axis=1)
    s1 = jnp.concatenate([-sin, jnp.zeros((n, half + rest), F32)], axis=1)
    s2 = jnp.concatenate([jnp.zeros((n, half), F32), sin, jnp.zeros((n, rest), F32)], axis=1)
    rep = LANE // ATT_HEAD_DIM
    return jnp.tile(c, (1, rep)), jnp.tile(s1, (1, rep)), jnp.tile(s2, (1, rep))


def _rope(x, c, s1, s2):
    width = x.shape[-1]
    rep = width // LANE
    half = ROPE_DIM // 2
    if rep > 1:
        c = jnp.concatenate([c] * rep, axis=-1)
        s1 = jnp.concatenate([s1] * rep, axis=-1)
        s2 = jnp.concatenate([s2] * rep, axis=-1)
    axis = x.ndim - 1
    return x * c + pltpu.roll(x, width - half, axis) * s1 + pltpu.roll(x, half, axis) * s2


def _attn_prompt_init(ko_ref, vo_ref):
    ko_ref[...] = jnp.zeros_like(ko_ref)
    vo_ref[...] = jnp.zeros_like(vo_ref)


def _attn_prompt_body(j, sink_ref, q_ref, k_ref, v_ref, c_ref, s1_ref, s2_ref, o_ref, ko_ref, vo_ref):
    w = WINDOW
    c = c_ref[...]
    s1 = s1_ref[...]
    s2 = s2_ref[...]
    krot = _rope(k_ref[...], c, s1, s2)
    v = v_ref[...]
    qrot = _rope(q_ref[...], c, s1, s2)
    kk = jnp.concatenate([ko_ref[...], krot], axis=0).astype(BF16)
    vv = jnp.concatenate([vo_ref[...], v], axis=0).astype(BF16)

    row = lax.broadcasted_iota(jnp.int32, (w, 2 * w), 0)
    col = lax.broadcasted_iota(jnp.int32, (w, 2 * w), 1)
    first_col = jnp.where(j > 0, 0, w)
    mask = (col >= row) & (col <= row + w) & (col >= first_col)
    scale = ATT_HEAD_DIM ** -0.5

    for h in range(ATT_KV_HEADS):
        kh = kk[:, h * ATT_HEAD_DIM:(h + 1) * ATT_HEAD_DIM]
        vh = vv[:, h * ATT_HEAD_DIM:(h + 1) * ATT_HEAD_DIM]
        for g in range(ATT_GROUP):
            hg = h * ATT_GROUP + g
            qh = qrot[:, hg * ATT_HEAD_DIM:(hg + 1) * ATT_HEAD_DIM]
            s = _dot_nt(qh, kh) * scale
            s = jnp.where(mask, s, -jnp.inf)
            sink = sink_ref[hg]
            m = jnp.maximum(jnp.max(s, axis=-1, keepdims=True), sink)
            p = jnp.exp(s - m)
            denom = jnp.sum(p, axis=-1, keepdims=True) + jnp.exp(sink - m)
            o = _dot(p, vh) / denom
            o_ref[:, hg * ATT_HEAD_DIM:(hg + 1) * ATT_HEAD_DIM] = o.astype(o_ref.dtype)
            yield

    ko_ref[...] = krot
    vo_ref[...] = v


def _group_rmsnorm(y, w, groups):
    width = y.shape[-1] // groups
    outs = []
    for g in range(groups):
        yg = y[:, g * width:(g + 1) * width]
        ms = jnp.mean(yg * yg, axis=-1, keepdims=True)
        outs.append(yg * lax.rsqrt(ms + EPS) * w[:, g * width:(g + 1) * width])
    return jnp.concatenate(outs, axis=-1)


CONV_PAD = 8


def _ssd_prompt_init(s_ref, tail_ref):
    tail_ref[...] = jnp.zeros_like(tail_ref)
    s_ref[...] = jnp.zeros_like(s_ref)


def _ssd_prompt_body(xbc_ref, z_ref, g_ref, cw_ref, cb_ref, dtb_ref, alog_ref, dskip_ref, wn_ref,
                     y_ref, s_ref, tail_ref, ybuf_ref, xw_ref):
    q = CHUNK
    pad = CONV_PAD
    hp = SSM_HEAD_DIM
    heads_per_group = SSM_HEADS // SSM_GROUPS
    gw = heads_per_group * hp

    x = xbc_ref[...]
    cw = cw_ref[...]
    row8 = lax.broadcasted_iota(jnp.int32, (pad, CONV_DIM), 0)
    acc = None
    for j in range(CONV_WIDTH - 1):
        shift = CONV_WIDTH - 1 - j
        rolled = pltpu.roll(x, shift, 0)
        head = jnp.where(row8 < shift, tail_ref[j * pad:(j + 1) * pad, :], rolled[0:pad, :])
        tail_ref[j * pad:(j + 1) * pad, :] = rolled[0:pad, :]
        term = jnp.concatenate([head, rolled[pad:, :]], axis=0) * cw[j:j + 1, :]
        acc = term if acc is None else acc + term
    acc = acc + x * cw[CONV_WIDTH - 1:CONV_WIDTH, :]
    xc = _silu(acc + cb_ref[...])

    xs = xc[:, 0:D_SSM]
    bm = xc[:, D_SSM:D_SSM + SSM_GROUPS * D_STATE]
    cm = xc[:, D_SSM + SSM_GROUPS * D_STATE:]

    dt = jax.nn.softplus(g_ref[:, 0:LANE] + dtb_ref[...])
    a_neg = -jnp.exp(alog_ref[...])
    a_col = _cumsum_rows(dt * a_neg)
    a_row = a_col.T
    dt_row = dt.T
    wk_row = jnp.exp(a_row[:, q - 1:q] - a_row) * dt_row
    tril = _tril(q)
    xs_t = xs.T
    ea_col = jnp.exp(a_col)
    left = lax.broadcasted_iota(jnp.int32, (q, LANE), 1) < hp
    ea_full = jnp.concatenate(
        [jnp.where(left, ea_col[:, 2 * t:2 * t + 1], ea_col[:, 2 * t + 1:2 * t + 2]) for t in range(SSM_HEADS // 2)],
        axis=1)
    yield

    cs_parts = []
    for g in range(SSM_GROUPS):
        bg = bm[:, g * D_STATE:(g + 1) * D_STATE]
        cg = cm[:, g * D_STATE:(g + 1) * D_STATE]
        cb = _dot_nt(cg, bg)
        cs_parts.append(_dot_nt(cg, s_ref[g * gw:(g + 1) * gw, :]))
        for r in range(heads_per_group):
            h = g * heads_per_group + r
            seg = a_col[:, h:h + 1] - a_row[h:h + 1, :]
            wmat = jnp.exp(jnp.where(tril, seg, -jnp.inf)) * cb * dt_row[h:h + 1, :]
            ybuf_ref[:, h * hp:(h + 1) * hp] = _dot(wmat, xs[:, h * hp:(h + 1) * hp])
            xw_ref[r * hp:(r + 1) * hp, :] = xs_t[h * hp:(h + 1) * hp, :] * wk_row[h:h + 1, :]
            yield
        upd = _dot(xw_ref[...], bg)
        for r in range(heads_per_group):
            h = g * heads_per_group + r
            decay = jnp.exp(a_row[h:h + 1, q - 1:q])
            s_ref[h * hp:(h + 1) * hp, :] = s_ref[h * hp:(h + 1) * hp, :] * decay + upd[r * hp:(r + 1) * hp, :]
        yield

    y = ybuf_ref[...] + jnp.concatenate(cs_parts, axis=1) * ea_full
    y = y + dskip_ref[...] * xs
    y = y * _silu(z_ref[...])
    y_ref[...] = _group_rmsnorm(y, wn_ref[...], SSM_GROUPS).astype(y_ref.dtype)


def _mlstm_prompt_init(c_ref, n_ref, m_ref):
    c_ref[...] = jnp.zeros_like(c_ref)
    n_ref[...] = jnp.zeros_like(n_ref)
    m_ref[...] = jnp.zeros_like(m_ref)


def _mlstm_prompt_body(q_ref, k_ref, v_ref, o_ref, g_ref, bi_ref, bf_ref, wn_ref, h_ref, c_ref, n_ref, m_ref):
    t = CHUNK
    d = MLSTM_HEAD_DIM

    gi = g_ref[:, 0:LANE] + bi_ref[...]
    gf = g_ref[:, LANE:2 * LANE] + bf_ref[...]
    b_col = _cumsum_rows(jax.nn.log_sigmoid(gf))
    b_row = b_col.T
    i_row = gi.T
    tril = _tril(t)
    k_all = k_ref[...] * (d ** -0.5)

    for h in range(MLSTM_HEADS):
        ln = GATE_LANE + h
        bq = b_col[:, ln:ln + 1]
        logw = jnp.where(tril, bq - b_row[ln:ln + 1, :] + i_row[ln:ln + 1, :], -jnp.inf)
        m_prev = m_ref[h:h + 1, 0:1]
        log_inter = bq + m_prev
        mt = jnp.maximum(log_inter, jnp.max(logw, axis=-1, keepdims=True))
        qh = q_ref[:, h * d:(h + 1) * d]
        kh = k_all[:, h * d:(h + 1) * d]
        vh = v_ref[:, h * d:(h + 1) * d]
        ch = c_ref[h * d:(h + 1) * d, :]
        nh = n_ref[h:h + 1, :]
        sw = jnp.exp(logw - mt) * _dot_nt(qh, kh)
        gq = jnp.exp(log_inter - mt)
        num = _dot(sw, vh) + _dot(qh, ch) * gq
        den = jnp.sum(sw, axis=-1, keepdims=True) + jnp.sum(qh * nh, axis=-1, keepdims=True) * gq
        hh = num / jnp.maximum(jnp.abs(den), jnp.exp(-mt))

        m_new = mt[t - 1:t, :]
        b_last = b_col[t - 1:t, ln:ln + 1]
        wk = jnp.exp(b_last - bq + gi[:, ln:ln + 1] - m_new)
        g_end = jnp.exp(b_last + m_prev - m_new)
        kw = kh * wk
        c_ref[h * d:(h + 1) * d, :] = ch * g_end + _dot_tn(kw, vh)
        n_ref[h:h + 1, :] = nh * g_end + jnp.sum(kw, axis=0, keepdims=True)
        m_ref[h:h + 1, :] = jnp.broadcast_to(m_new, (1, LANE))

        ms = jnp.mean(hh * hh, axis=-1, keepdims=True)
        hn = hh * lax.rsqrt(ms + EPS) * wn_ref[:, h * d:(h + 1) * d]
        h_ref[:, h * d:(h + 1) * d] = (hn * jax.nn.sigmoid(o_ref[:, h * d:(h + 1) * d])).astype(h_ref.dtype)
        yield

def _interleave(stages):
    live = [[gen, 0, n] for gen, n in stages]
    while live:
        item = min(live, key=lambda s: s[1] / s[2])
        try:
            next(item[0])
            item[1] += 1
        except StopIteration:
            live.remove(item)


PROJ_TILE = 256


def _project_next(x_ref, g_ref, w_ref, pnext_ref):
    x = x_ref[...]
    u = (x * lax.rsqrt(jnp.mean(x * x, axis=-1, keepdims=True) + EPS) * g_ref[...]).astype(BF16)
    for ct in range(NP // PROJ_TILE):
        cols = slice(ct * PROJ_TILE, (ct + 1) * PROJ_TILE)
        pnext_ref[:, cols] = jnp.dot(u, w_ref[:, cols], preferred_element_type=F32)
        yield


def _front_kernel(sink_ref, x_ref, x0_ref, g_ref, w_ref, c_ref, s1_ref, s2_ref,
                  cw_ref, cb_ref, dtb_ref, alog_ref, dskip_ref, wns_ref, bi_ref, bf_ref, wnm_ref,
                  att_ref, ko_ref, vo_ref, y_ref, s_ref, h_ref, mem_ref, n_ref, m_ref, ptail_ref,
                  pcur_ref, pnext_ref, tail_ref, ybuf_ref, xw_ref, *, nblk):
    j = pl.program_id(1)
    first = (pl.program_id(0) == 0) & (j == 0)
    t = CHUNK

    @pl.when(first)
    def _():
        for _ in _project_next(x0_ref, g_ref, w_ref, pnext_ref):
            pass

    @pl.when(j == 0)
    def _():
        _attn_prompt_init(ko_ref, vo_ref)
        _ssd_prompt_init(s_ref, tail_ref)
        _mlstm_prompt_init(mem_ref, n_ref, m_ref)

    pcur_ref[...] = pnext_ref[...]

    def attn(i):
        seg = lambda col, width: pcur_ref.at[i * t:(i + 1) * t, col:col + width]
        rows = lambda ref: ref.at[i * t:(i + 1) * t, :]
        return _attn_prompt_body(j * nblk + i, sink_ref, seg(C_Q, D_ATT), seg(C_K, D_KV), seg(C_V, D_KV),
                                 rows(c_ref), rows(s1_ref), rows(s2_ref), rows(att_ref), ko_ref, vo_ref)

    def ssd(i):
        seg = lambda col, width: pcur_ref.at[i * t:(i + 1) * t, col:col + width]
        return _ssd_prompt_body(seg(C_XBC, CONV_DIM), seg(C_Z, D_SSM), seg(C_G, 2 * LANE), cw_ref, cb_ref, dtb_ref,
                                alog_ref, dskip_ref, wns_ref, y_ref.at[i * t:(i + 1) * t, :], s_ref,
                                tail_ref, ybuf_ref, xw_ref)

    def mlstm(i):
        seg = lambda col, width: pcur_ref.at[i * t:(i + 1) * t, col:col + width]
        return _mlstm_prompt_body(seg(C_MQ, D_MLSTM), seg(C_MK, D_MLSTM), seg(C_MV, D_MLSTM), seg(C_MO, D_MLSTM),
                                  seg(C_G, 2 * LANE), bi_ref, bf_ref, wnm_ref, h_ref.at[i * t:(i + 1) * t, :],
                                  mem_ref, n_ref, m_ref)

    def blocks(body):
        for i in range(nblk):
            yield from body(i)

    ptail_ref[...] = pcur_ref[nblk * t - CONV_PAD:nblk * t, C_XBC:C_XBC + CONV_DIM]
    _interleave([
        (_project_next(x_ref, g_ref, w_ref, pnext_ref), NP // PROJ_TILE + 1),
        (blocks(attn), nblk * (ATT_HEADS + 1)),
        (blocks(ssd), nblk * (SSM_HEADS + SSM_GROUPS + 2)),
        (blocks(mlstm), nblk * (MLSTM_HEADS + 1)),
    ])


def _front(x, g_mix, w, sinks, tables, cw, cb, dtb, alog, dskip_rep, wns, bi, bf, wnm, bsz, seq):
    nblk = 2 if (seq // CHUNK) % 2 == 0 else 1
    t = nblk * CHUNK
    nb = seq // t
    total = bsz * nb
    c, s1, s2 = tables
    row = lambda b, j: b * nb + j
    full = lambda shape: pl.BlockSpec(shape, lambda b, j: (0,) * len(shape))
    vec = lambda n: full((1, n))
    tab = pl.BlockSpec((t, LANE), lambda b, j: (j, 0))
    rows_out = lambda width: pl.BlockSpec((t, width), lambda b, j: (row(b, j), 0))
    per_seq = lambda r, n: pl.BlockSpec((None, r, n), lambda b, j: (b, 0, 0))
    return pl.pallas_call(
        functools.partial(_front_kernel, nblk=nblk),
        grid=(bsz, nb),
        in_specs=[pl.BlockSpec(memory_space=pltpu.SMEM),
                  pl.BlockSpec((t, D_MODEL), lambda b, j: (jnp.minimum(row(b, j) + 1, total - 1), 0)),
                  pl.BlockSpec((t, D_MODEL), lambda b, j: (0, 0), pipeline_mode=pl.Buffered(1)),
                  vec(D_MODEL),
                  pl.BlockSpec((D_MODEL, NP), lambda b, j: (0, 0), pipeline_mode=pl.Buffered(1)),
                  tab, tab, tab,
                  full((CONV_WIDTH, CONV_DIM)), vec(CONV_DIM), vec(LANE), vec(LANE), vec(D_SSM), vec(D_SSM),
                  vec(LANE), vec(LANE), vec(D_MLSTM)],
        out_specs=[rows_out(D_ATT), per_seq(CHUNK, D_KV), per_seq(CHUNK, D_KV),
                   rows_out(D_SSM), per_seq(D_SSM, D_STATE),
                   rows_out(D_MLSTM), per_seq(D_MLSTM, MLSTM_HEAD_DIM), per_seq(8, LANE), per_seq(8, LANE),
                   per_seq(CONV_PAD, CONV_DIM)],
        out_shape=[jax.ShapeDtypeStruct((bsz * seq, D_ATT), BF16),
                   jax.ShapeDtypeStruct((bsz, CHUNK, D_KV), F32),
                   jax.ShapeDtypeStruct((bsz, CHUNK, D_KV), F32),
                   jax.ShapeDtypeStruct((bsz * seq, D_SSM), BF16),
                   jax.ShapeDtypeStruct((bsz, D_SSM, D_STATE), F32),
                   jax.ShapeDtypeStruct((bsz * seq, D_MLSTM), BF16),
                   jax.ShapeDtypeStruct((bsz, D_MLSTM, MLSTM_HEAD_DIM), F32),
                   jax.ShapeDtypeStruct((bsz, 8, LANE), F32),
                   jax.ShapeDtypeStruct((bsz, 8, LANE), F32),
                   jax.ShapeDtypeStruct((bsz, CONV_PAD, CONV_DIM), F32)],
        scratch_shapes=[pltpu.VMEM((t, NP), F32),
                        pltpu.VMEM((t, NP), F32),
                        pltpu.VMEM(((CONV_WIDTH - 1) * CONV_PAD, CONV_DIM), F32),
                        pltpu.VMEM((CHUNK, D_SSM), F32),
                        pltpu.VMEM((D_SSM // SSM_GROUPS, CHUNK), F32)],
        compiler_params=_cparams(("arbitrary", "arbitrary")),
        name="front",
    )(sinks, x, x, g_mix, w, c, s1, s2, cw, cb, dtb, alog, dskip_rep, wns, bi, bf, wnm)


def _stacked_out(prev, depth, shape, block, layer):
    spec = pl.BlockSpec((None,) + block, lambda i: (layer,) + (i,) + (0,) * (len(block) - 1))
    out_shape = jax.ShapeDtypeStruct((depth,) + shape, F32)
    extra_in = [] if prev is None else [prev]
    extra_spec = [] if prev is None else [pl.BlockSpec(memory_space=pl.ANY)]
    return spec, out_shape, extra_in, extra_spec


def _drop_aliased(kernel_fn, n_in, n_prev):
    def wrapped(*refs):
        return kernel_fn(*refs[:n_in], *refs[n_in + n_prev:])
    return wrapped


def _attn_sample_kernel(q_ref, k_ref, v_ref, kc_ref, vc_ref, sink_ref, c_ref, s1_ref, s2_ref,
                        o_ref, ko_ref, vo_ref):
    tb, _, w = kc_ref.shape
    c = c_ref[...]
    s1 = s1_ref[...]
    s2 = s2_ref[...]
    qb = _rope(q_ref[...], c, s1, s2)
    kn = _rope(k_ref[...], c, s1, s2)
    vn = v_ref[...]
    kc = kc_ref[...]
    vc = vc_ref[...]
    newest = lax.broadcasted_iota(jnp.int32, (D_KV, w), 1) == w - 1
    for b in range(tb):
        ko_ref[b] = jnp.where(newest, _column_tile(kn[b]), pltpu.roll(kc[b], w - 1, 1))
        vo_ref[b] = jnp.where(newest, _column_tile(vn[b]), pltpu.roll(vc[b], w - 1, 1))

    scale = ATT_HEAD_DIM ** -0.5
    s = jnp.einsum("bhd,bdk->bhk", qb.astype(BF16), kc.astype(BF16),
                   preferred_element_type=F32) * scale
    sn = jnp.sum(qb * kn, axis=-1, keepdims=True) * scale
    sink = sink_ref[...][None, :, 0:1]
    m = jnp.maximum(jnp.maximum(jnp.max(s, axis=-1, keepdims=True), sn), sink)
    p = jnp.exp(s - m)
    pn = jnp.exp(sn - m)
    denom = jnp.sum(p, axis=-1, keepdims=True) + pn + jnp.exp(sink - m)
    p = p / denom
    pn = pn / denom
    o = jnp.einsum("bhk,bdk->bhd", p.astype(BF16), vc.astype(BF16), preferred_element_type=F32)
    o = o + pn * vn
    head = lax.broadcasted_iota(jnp.int32, o.shape[:2] + (ATT_HEAD_DIM,), 1)
    o_ref[...] = jnp.where(head < ATT_GROUP, o[:, :, 0:ATT_HEAD_DIM], o[:, :, ATT_HEAD_DIM:]).astype(o_ref.dtype)


def _attn_sample(qb, kn, vn, kc, vc, sink_rows, tables, layer, prev, tb):
    depth, bd, _, w = kc.shape
    c, s1, s2 = tables
    blk3 = lambda r, n: pl.BlockSpec((tb, r, n), lambda i: (i, 0, 0))
    cache = pl.BlockSpec((None, tb, D_KV, w), lambda i: (layer, i, 0, 0))
    vec = pl.BlockSpec((1, LANE), lambda i: (0, 0))
    prev_k, prev_v = (None, None) if prev is None else prev
    k_spec, k_shape, k_in, k_in_spec = _stacked_out(prev_k, depth, (bd, D_KV, w), (tb, D_KV, w), layer)
    v_spec, v_shape, v_in, v_in_spec = _stacked_out(prev_v, depth, (bd, D_KV, w), (tb, D_KV, w), layer)
    n_in = 9
    n_prev = len(k_in) + len(v_in)
    return pl.pallas_call(
        _drop_aliased(_attn_sample_kernel, n_in, n_prev),
        grid=(bd // tb,),
        in_specs=[blk3(ATT_HEADS, LANE), blk3(1, LANE), blk3(1, LANE), cache, cache,
                  pl.BlockSpec((ATT_HEADS, LANE), lambda i: (0, 0)), vec, vec, vec] + k_in_spec + v_in_spec,
        out_specs=[blk3(ATT_HEADS, ATT_HEAD_DIM), k_spec, v_spec],
        out_shape=[jax.ShapeDtypeStruct((bd, ATT_HEADS, ATT_HEAD_DIM), BF16), k_shape, v_shape],
        input_output_aliases={n_in + t: 1 + t for t in range(n_prev)},
        compiler_params=_cparams(("parallel",)),
        name="attn_sample",
    )(qb, kn, vn, kc, vc, sink_rows, c, s1, s2, *k_in, *v_in)


def _lane_place(cols, lane0):
    m = cols[0].shape[0]
    lane = lax.broadcasted_iota(jnp.int32, (m, LANE), 1)
    out = jnp.zeros((m, LANE), F32)
    for i, col in enumerate(cols):
        out = jnp.where(lane == lane0 + i, col, out)
    return out


def _sample_pre_kernel(x_ref, gmix_ref, w_ref, cs_ref, cw_ref, cb_ref, dtb_ref, alog_ref, bi_ref, bf_ref, n_ref,
                       m_ref, proj_ref, conv_ref, xs_ref, xdt_ref, bm_ref, cm_ref, ea_ref,
                       ks_ref, nn_ref, g_ref, esw_ref, sw_ref, den_ref, mt_ref):
    for _ in _project_next(x_ref, gmix_ref, w_ref, proj_ref):
        pass
    xbc = proj_ref[:, C_XBC:C_XBC + CONV_DIM]
    s0 = cs_ref[0]
    s1 = cs_ref[1]
    s2 = cs_ref[2]
    cw = cw_ref[...]
    acc = s0 * cw[0:1, :]
    acc = acc + s1 * cw[1:2, :]
    acc = acc + s2 * cw[2:3, :]
    acc = acc + xbc * cw[3:4, :]
    xc = _silu(acc + cb_ref[...])
    conv_ref[0] = s1
    conv_ref[1] = s2
    conv_ref[2] = xbc

    xs = xc[:, 0:D_SSM]
    xs_ref[...] = xs
    bm_ref[...] = xc[:, D_SSM:D_SSM + SSM_GROUPS * D_STATE]
    cm_ref[...] = xc[:, D_SSM + SSM_GROUPS * D_STATE:]
    dt = jax.nn.softplus(proj_ref[:, C_G:C_G + LANE] + dtb_ref[...])
    ea_ref[...] = jnp.exp(dt * (-jnp.exp(alog_ref[...])))
    hp = SSM_HEAD_DIM
    for h in range(SSM_HEADS):
        xdt_ref[:, h * hp:(h + 1) * hp] = xs[:, h * hp:(h + 1) * hp] * dt[:, h:h + 1]

    d = MLSTM_HEAD_DIM
    gi = proj_ref[:, C_G:C_G + LANE] + bi_ref[...]
    gf = proj_ref[:, C_G + LANE:C_G + 2 * LANE] + bf_ref[...]
    log_inter = jax.nn.log_sigmoid(gf) + m_ref[...]
    mt = jnp.maximum(log_inter, gi)
    gq = jnp.exp(log_inter - mt)
    esw = jnp.exp(gi - mt)
    ks = proj_ref[:, C_MK:C_MK + D_MLSTM] * (d ** -0.5)
    ks_ref[...] = ks
    qk_cols = []
    qn_cols = []
    for h in range(MLSTM_HEADS):
        qh = proj_ref[:, C_MQ + h * d:C_MQ + (h + 1) * d]
        kh = ks[:, h * d:(h + 1) * d]
        nh = n_ref[:, h * d:(h + 1) * d]
        qk_cols.append(jnp.sum(qh * kh, axis=-1, keepdims=True))
        qn_cols.append(jnp.sum(qh * nh, axis=-1, keepdims=True))
        ln = GATE_LANE + h
        nn_ref[:, h * d:(h + 1) * d] = nh * gq[:, ln:ln + 1] + kh * esw[:, ln:ln + 1]
    sw = esw * _lane_place(qk_cols, GATE_LANE)
    den = sw + _lane_place(qn_cols, GATE_LANE) * gq
    g_ref[...] = gq
    esw_ref[...] = esw
    sw_ref[...] = sw
    den_ref[...] = jnp.maximum(jnp.abs(den), jnp.exp(-mt))
    mt_ref[...] = mt


def _sample_pre(x, g_mix, w, conv_state, layer, cw, cb, dtb, alog, bi, bf, n_state, m_tile):
    bd = x.shape[0]
    taps = CONV_WIDTH - 1
    tile = jax.ShapeDtypeStruct((bd, LANE), F32)
    whole = lambda a: pl.BlockSpec(a.shape, lambda i: (0,) * a.ndim)
    in_specs = [whole(x), whole(g_mix), pl.BlockSpec(w.shape, lambda i: (0, 0), pipeline_mode=pl.Buffered(1)),
                pl.BlockSpec((None, taps, bd, CONV_DIM), lambda i: (layer, 0, 0, 0))]
    in_specs += [whole(a) for a in (cw, cb, dtb, alog, bi, bf, n_state, m_tile)]
    outs = [jax.ShapeDtypeStruct((bd, NP), F32),
            jax.ShapeDtypeStruct((taps, bd, CONV_DIM), F32),
            jax.ShapeDtypeStruct((bd, D_SSM), F32),
            jax.ShapeDtypeStruct((bd, D_SSM), F32),
            jax.ShapeDtypeStruct((bd, SSM_GROUPS * D_STATE), F32),
            jax.ShapeDtypeStruct((bd, SSM_GROUPS * D_STATE), F32),
            tile,
            jax.ShapeDtypeStruct((bd, D_MLSTM), F32),
            jax.ShapeDtypeStruct((bd, D_MLSTM), F32),
            tile, tile, tile, tile, tile]
    return pl.pallas_call(
        _sample_pre_kernel,
        grid=(1,),
        in_specs=in_specs,
        out_specs=[pl.BlockSpec(o.shape, lambda i, nd=len(o.shape): (0,) * nd) for o in outs],
        out_shape=outs,
        compiler_params=_cparams(("arbitrary",)),
        name="sample_pre",
    )(x, g_mix, w, conv_state, cw, cb, dtb, alog, bi, bf, n_state, m_tile)


def _column_tile(row):
    return jnp.broadcast_to(row, (LANE, LANE)).T


SEQ_ROWS = 8


def _put_row(out_ref, acc_ref, vr, col, row):
    width = row.shape[1]
    rid = lax.broadcasted_iota(jnp.int32, (SEQ_ROWS, width), 0)
    blk = jnp.where(rid == vr, row, acc_ref[:, col:col + width])
    acc_ref[:, col:col + width] = blk
    out_ref[:, col:col + width] = blk


def _ssm_sample_one(b, vr, gb, ea_ref, s_ref, xdt_ref, b_ref, c_ref, so_ref, y_ref, yacc_ref):
    hp = SSM_HEAD_DIM
    heads_per_tile = LANE // hp
    n_tiles = D_SSM // LANE
    tiles_per_group = n_tiles // SSM_GROUPS
    top = lax.broadcasted_iota(jnp.int32, (LANE, 1), 0) < hp
    xrow = xdt_ref[pl.ds(vr, 1), :]
    brow = b_ref[pl.ds(vr, 1), :]
    crow = c_ref[pl.ds(vr, 1), :]
    for t in range(n_tiles):
        g = t // tiles_per_group
        xcol = _column_tile(xrow[:, t * LANE:(t + 1) * LANE])
        bg = brow[:, g * D_STATE:(g + 1) * D_STATE]
        cg = jnp.broadcast_to(crow[:, g * D_STATE:(g + 1) * D_STATE], (8, D_STATE))
        e0 = ea_ref[gb, heads_per_tile * t]
        e1 = ea_ref[gb, heads_per_tile * t + 1]
        decay = jnp.where(top, e0, e1)
        sn = s_ref[b, t * LANE:(t + 1) * LANE, :] * decay + xcol * bg
        so_ref[b, t * LANE:(t + 1) * LANE, :] = sn
        _put_row(y_ref, yacc_ref, vr, t * LANE, _dot_nt(cg, sn)[0:1, :])


def _mlstm_sample_one(b, vr, gb, g_ref, esw_ref, c_ref, q_ref, k_ref, v_ref, co_ref, qc_ref, qacc_ref):
    d = MLSTM_HEAD_DIM
    qrow = q_ref[pl.ds(vr, 1), :]
    krow = k_ref[pl.ds(vr, 1), :]
    vrow = v_ref[pl.ds(vr, 1), :]
    for h in range(MLSTM_HEADS):
        qcol = _column_tile(qrow[:, h * d:(h + 1) * d])
        kcol = _column_tile(krow[:, h * d:(h + 1) * d])
        ch = c_ref[b, h * d:(h + 1) * d, :]
        g_end = g_ref[gb, GATE_LANE + h]
        wk = esw_ref[gb, GATE_LANE + h]
        _put_row(qc_ref, qacc_ref, vr, h * d, jnp.sum(qcol * ch, axis=0, keepdims=True))
        co_ref[b, h * d:(h + 1) * d, :] = ch * g_end + (kcol * wk) * vrow[:, h * d:(h + 1) * d]


def _sample_post_kernel(proj_ref, y_ref, xs_ref, dskip_ref, wns_ref, qc_ref, g_ref, sw_ref, den_ref, wnm_ref,
                        ys_ref, hs_ref):
    y = y_ref[...] + dskip_ref[...] * xs_ref[...]
    y = y * _silu(proj_ref[:, C_Z:C_Z + D_SSM])
    ys_ref[...] = _group_rmsnorm(y, wns_ref[...], SSM_GROUPS).astype(ys_ref.dtype)

    d = MLSTM_HEAD_DIM
    for h in range(MLSTM_HEADS):
        ln = GATE_LANE + h
        vh = proj_ref[:, C_MV + h * d:C_MV + (h + 1) * d]
        num = sw_ref[:, ln:ln + 1] * vh + qc_ref[:, h * d:(h + 1) * d] * g_ref[:, ln:ln + 1]
        hh = num / den_ref[:, ln:ln + 1]
        ms = jnp.mean(hh * hh, axis=-1, keepdims=True)
        hn = hh * lax.rsqrt(ms + EPS) * wnm_ref[:, h * d:(h + 1) * d]
        gate = jax.nn.sigmoid(proj_ref[:, C_MO + h * d:C_MO + (h + 1) * d])
        hs_ref[:, h * d:(h + 1) * d] = (hn * gate).astype(hs_ref.dtype)


N_POST_IN = 10


def _decode_tail_kernel(*refs, final_norm):
    post_in = refs[:N_POST_IN]
    att_ref, x_ref, wo_ref, g_ref, wu_ref, wd_ref, gf_ref, o_ref, u_ref, ys_ref, hm_ref = refs[N_POST_IN:]
    j = pl.program_id(0)

    @pl.when(j == 0)
    def _():
        _sample_post_kernel(*post_in, ys_ref, hm_ref)
        acc = _dot(att_ref[...], wo_ref[0:D_ATT, :])
        acc = acc + _dot(ys_ref[...], wo_ref[D_ATT:D_ATT + D_SSM, :])
        acc = acc + _dot(hm_ref[...], wo_ref[D_ATT + D_SSM:, :])
        x1 = x_ref[...] + acc
        ms = jnp.mean(x1 * x1, axis=-1, keepdims=True)
        u_ref[...] = (x1 * lax.rsqrt(ms + EPS) * g_ref[...]).astype(BF16)
        o_ref[...] = x1

    h = _dot(u_ref[...], wu_ref[...])
    h = jnp.square(jnp.maximum(h, 0.0))
    o_ref[...] += _dot(h, wd_ref[...])

    if final_norm:
        @pl.when(j == pl.num_programs(0) - 1)
        def _():
            y = o_ref[...]
            ms = jnp.mean(y * y, axis=-1, keepdims=True)
            o_ref[...] = y * lax.rsqrt(ms + EPS) * gf_ref[...]


def _decode_tail(post_args, att, x, w_out, g, wu, wd, gf, layer, final_norm, tf):
    bd, d = x.shape
    ff = wu.shape[2]
    whole = lambda a: pl.BlockSpec(a.shape, lambda j: (0,) * a.ndim)
    in_specs = [whole(a) for a in post_args] + [
        whole(att), whole(x),
        pl.BlockSpec((None, w_out.shape[1], d), lambda j: (layer, 0, 0), pipeline_mode=pl.Buffered(1)),
        whole(g),
        pl.BlockSpec((None, d, tf), lambda j: (layer, 0, j)),
        pl.BlockSpec((None, tf, d), lambda j: (layer, j, 0)),
        whole(gf)]
    assert len(post_args) == N_POST_IN
    return pl.pallas_call(
        functools.partial(_decode_tail_kernel, final_norm=final_norm),
        grid=(ff // tf,),
        in_specs=in_specs,
        out_specs=pl.BlockSpec((bd, d), lambda j: (0, 0)),
        out_shape=jax.ShapeDtypeStruct((bd, d), F32),
        scratch_shapes=[pltpu.VMEM((bd, d), BF16), pltpu.VMEM((bd, D_SSM), BF16), pltpu.VMEM((bd, D_MLSTM), BF16)],
        compiler_params=_cparams(("arbitrary",)),
        name="decode_tail",
    )(*post_args, att, x, w_out, g, wu, wd, gf)


def _lane_vec(v, lane0):
    return jnp.pad(v.astype(F32), (lane0, LANE - lane0 - v.shape[0]))[None, :]


def _pick(n, candidates):
    for c in candidates:
        if n % c == 0:
            return c
    return n


def kernel(x_prompt, x_sample, cache_swa_k, cache_swa_v, state_conv, state_ssm, state_mlstm_C, state_mlstm_n,
           state_mlstm_m, w_norm_mix, w_in, attn_sinks, conv_w, conv_b, dt_bias, a_log, d_skip, w_norm_ssm,
           igate_b, fgate_b, w_norm_mlstm, w_out, w_norm_mlp, w_up, w_down, w_norm_final):
    bsz, seq, d_model = x_prompt.shape
    bd = x_sample.shape[0]
    depth = w_in.shape[0]
    win = cache_swa_k.shape[2]
    assert x_sample.shape[1] == 1 and seq % CHUNK == 0 and d_model == D_MODEL

    mp = bsz * seq
    tm_mlp = _pick(mp, (1024, 512, 256, 128))
    tb = _pick(bd, (16, 8))

    hp = x_prompt.reshape(mp, d_model)
    hs = x_sample.reshape(bd, d_model)
    tab_p = _rope_tables(jnp.arange(seq, dtype=jnp.int32))
    tab_s = _rope_tables(jnp.arange(1, dtype=jnp.int32) + PAST_LEN)
    gf = w_norm_final[None, :]
    kc_all = jnp.transpose(cache_swa_k, (0, 1, 3, 4, 2)).reshape(depth, bd, D_KV, win)
    vc_all = jnp.transpose(cache_swa_v, (0, 1, 3, 4, 2)).reshape(depth, bd, D_KV, win)
    ssm_all = state_ssm.reshape(depth, bd, D_SSM, D_STATE)
    mem_all = state_mlstm_C.reshape(depth, bd, D_MLSTM, MLSTM_HEAD_DIM)
    conv_all = jnp.swapaxes(state_conv, 1, 2)

    st_p, st_s = [], []
    kv_new = s_new = c_new = None
    for l in range(depth):
        w_in_l = _prep_w_in(w_in, l, 256)
        g_mix = w_norm_mix[l][None, :]
        g_mlp = w_norm_mlp[l][None, :]
        cw = conv_w[l]
        cb = conv_b[l][None, :]
        dtb = _lane_vec(dt_bias[l], 0)
        alog = _lane_vec(a_log[l], 0)
        dskip_rep = jnp.repeat(d_skip[l].astype(F32), SSM_HEAD_DIM)[None, :]
        wns = w_norm_ssm[l][None, :]
        bi = _lane_vec(igate_b[l], GATE_LANE)
        bf = _lane_vec(fgate_b[l], GATE_LANE)
        wnm = w_norm_mlstm[l][None, :]
        sinks = attn_sinks[l].reshape(ATT_HEADS).astype(F32)
        last = l == depth - 1

        att, pk, pv, y, p_ssm, hm, p_c, p_n, p_m, p_tail = _front(
            hp, g_mix, w_in_l, sinks, tab_p, cw, cb, dtb, alog, dskip_rep, wns, bi, bf, wnm, bsz, seq)
        x1 = _outproj(att, y, hm, hp, w_out, l, _pick(mp, (512, 256, 128)))
        p_conv = p_tail[:, CONV_PAD - (CONV_WIDTH - 1):, :]
        st_p.append((pk.reshape(bsz, WINDOW, ATT_KV_HEADS, ATT_HEAD_DIM),
                     pv.reshape(bsz, WINDOW, ATT_KV_HEADS, ATT_HEAD_DIM),
                     p_conv,
                     p_ssm.reshape(bsz, SSM_HEADS, SSM_HEAD_DIM, D_STATE),
                     p_c.reshape(bsz, MLSTM_HEADS, MLSTM_HEAD_DIM, MLSTM_HEAD_DIM),
                     p_n[:, 0:MLSTM_HEADS, :],
                     p_m[:, 0:MLSTM_HEADS, 0]))

        m_tile = jnp.pad(state_mlstm_m[l], ((0, 0), (GATE_LANE, LANE - GATE_LANE - MLSTM_HEADS)))
        (proj_s, conv_new, xs, xdt, bm, cm, ea, ks, n_new, gq, esw, sw, den, mt) = _sample_pre(
            hs, g_mix, w_in_l, conv_all, l, cw, cb, dtb, alog, bi, bf, state_mlstm_n[l].reshape(bd, D_MLSTM),
            m_tile)
        q8 = proj_s[:, C_Q:C_Q + D_ATT].reshape(bd, ATT_HEADS, ATT_HEAD_DIM)
        zero = jnp.zeros_like(q8)
        first_kv = (jnp.arange(ATT_HEADS) < ATT_GROUP)[None, :, None]
        qb = jnp.where(first_kv, jnp.concatenate([q8, zero], -1), jnp.concatenate([zero, q8], -1))
        kn = proj_s[:, C_K:C_K + D_KV].reshape(bd, 1, D_KV)
        vn = proj_s[:, C_V:C_V + D_KV].reshape(bd, 1, D_KV)
        sink_rows = jnp.broadcast_to(sinks[:, None], (ATT_HEADS, LANE))
        att_s, sk, sv = _attn_sample(qb, kn, vn, kc_all, vc_all, sink_rows, tab_s, l, kv_new, tb)
        kv_new = (sk, sv)
        side = (ea, ssm_all, xdt, bm, cm, gq, esw, mem_all, proj_s, ks, s_new, c_new)
        hp, s_new, y_read, c_new, qc = _mlp(x1, g_mlp, w_up, w_down, gf, l, last, tm_mlp, 512, side)
        hs = _decode_tail((proj_s, y_read, xs, dskip_rep, wns, qc, gq, sw, den, wnm), att_s.reshape(bd, D_ATT), hs,
                          w_out, g_mlp, w_up, w_down, gf, l, last, 512)
        st_s.append((conv_new,
                     n_new.reshape(bd, MLSTM_HEADS, MLSTM_HEAD_DIM),
                     mt[:, GATE_LANE:GATE_LANE + MLSTM_HEADS]))

    y_prompt = hp.reshape(bsz, seq, d_model)
    y_sample = hs.reshape(bd, 1, d_model)
    p_out = [jnp.stack([s[i] for s in st_p]) for i in range(7)]
    s_conv, s_n, s_m = [jnp.stack([s[i] for s in st_s]) for i in range(3)]
    s_conv = jnp.swapaxes(s_conv, 1, 2)
    to_cache = lambda a: jnp.transpose(a.reshape(depth, bd, ATT_KV_HEADS, ATT_HEAD_DIM, win), (0, 1, 4, 2, 3))
    s_k = to_cache(kv_new[0])
    s_v = to_cache(kv_new[1])
    s_ssm = s_new.reshape(depth, bd, SSM_HEADS, SSM_HEAD_DIM, D_STATE)
    s_c = c_new.reshape(depth, bd, MLSTM_HEADS, MLSTM_HEAD_DIM, MLSTM_HEAD_DIM)
    return (y_prompt, y_sample, *p_out, s_k, s_v, s_conv, s_ssm, s_c, s_n, s_m)
```

```python
import functools

import jax
import jax.numpy as jnp
from jax import lax
from jax.experimental import pallas as pl
from jax.experimental.pallas import tpu as pltpu

F32 = jnp.float32
BF16 = jnp.bfloat16

D_MODEL = 2048
EPS = 1e-6
PAST_LEN = 8192
ATT_HEAD_DIM = 64
ATT_HEADS = 8
ATT_KV_HEADS = 2
ATT_GROUP = ATT_HEADS // ATT_KV_HEADS
D_ATT = ATT_HEADS * ATT_HEAD_DIM
D_KV = ATT_KV_HEADS * ATT_HEAD_DIM
WINDOW = 128
ROPE_THETA = 500000.0
ROPE_DIM = ATT_HEAD_DIM // 4
SSM_HEAD_DIM = 64
SSM_HEADS = 16
D_SSM = SSM_HEADS * SSM_HEAD_DIM
SSM_GROUPS = 2
D_STATE = 128
CONV_WIDTH = 4
CONV_DIM = D_SSM + 2 * SSM_GROUPS * D_STATE
CHUNK = 128
MLSTM_HEADS = 4
MLSTM_HEAD_DIM = 128
D_MLSTM = MLSTM_HEADS * MLSTM_HEAD_DIM
D_FF = 4 * D_MODEL

C_XBC = 0
C_Q = 1536
C_Z = 2048
C_MQ = 3072
C_MK = 3584
C_MV = 4096
C_MO = 4608
C_K = 5120
C_V = 5248
C_G = 5376
NP = 5632
GATE_LANE = 16

LANE = 128
VMEM_LIMIT = 60 * 1024 * 1024


def _cparams(sem):
    return pltpu.CompilerParams(dimension_semantics=sem, vmem_limit_bytes=VMEM_LIMIT)


def _silu(x):
    return x * jax.nn.sigmoid(x)


def _dot(a, b):
    return jnp.dot(a.astype(BF16), b.astype(BF16), preferred_element_type=F32)


def _dot_nt(a, b):
    return lax.dot_general(a.astype(BF16), b.astype(BF16), (((1,), (1,)), ((), ())),
                           preferred_element_type=F32)


def _dot_tn(a, b):
    return lax.dot_general(a.astype(BF16), b.astype(BF16), (((0,), (0,)), ((), ())),
                           preferred_element_type=F32)


def _tril(n):
    row = lax.broadcasted_iota(jnp.int32, (n, n), 0)
    col = lax.broadcasted_iota(jnp.int32, (n, n), 1)
    return row >= col


def _cumsum_rows(x):
    row = lax.broadcasted_iota(jnp.int32, x.shape, 0)
    step = 1
    while step < x.shape[0]:
        x = x + jnp.where(row >= step, pltpu.roll(x, step, 0), 0.0)
        step *= 2
    return x


_SRC_Q, _SRC_K, _SRC_V, _SRC_Z, _SRC_XBC, _SRC_DT, _SRC_MQ = 0, 512, 640, 768, 1792, 3328, 3344
_SRC_MI = _SRC_MQ + 4 * D_MLSTM
IN_WIDTH = _SRC_MI + 2 * MLSTM_HEADS


def _prep_w_in_kernel(w_ref, o_ref):
    def put(dst, src, n):
        o_ref[:, dst:dst + n] = w_ref[src:src + n, :].T.astype(BF16)

    put(C_XBC, _SRC_XBC, CONV_DIM)
    put(C_Q, _SRC_Q, D_ATT)
    put(C_Z, _SRC_Z, D_SSM)
    put(C_MQ, _SRC_MQ, 4 * D_MLSTM)
    put(C_K, _SRC_K, D_KV)
    put(C_V, _SRC_V, D_KV)
    tk = w_ref.shape[1]
    dt = w_ref[_SRC_DT:_SRC_DT + SSM_HEADS, :]
    gates = w_ref[_SRC_MI:_SRC_MI + 2 * MLSTM_HEADS, :]
    pad = jnp.zeros((LANE - GATE_LANE - 2 * MLSTM_HEADS, tk), F32)
    lo = jnp.concatenate([dt, gates, pad], axis=0)
    hi = jnp.concatenate([jnp.zeros_like(dt), pltpu.roll(gates, MLSTM_HEADS, 0), pad], axis=0)
    o_ref[:, C_G:C_G + LANE] = lo.T.astype(BF16)
    o_ref[:, C_G + LANE:C_G + 2 * LANE] = hi.T.astype(BF16)


def _prep_w_in(w_in, tk):
    depth, k, n = w_in.shape
    assert n == IN_WIDTH and n % 8 == 0
    wt = jnp.swapaxes(w_in, 1, 2).reshape(depth * n, k)
    return pl.pallas_call(
        _prep_w_in_kernel,
        grid=(depth, k // tk),
        in_specs=[pl.BlockSpec((n, tk), lambda l, i: (l, i))],
        out_specs=pl.BlockSpec((None, tk, NP), lambda l, i: (l, i, 0)),
        out_shape=jax.ShapeDtypeStruct((depth, k, NP), BF16),
        compiler_params=_cparams(("parallel", "parallel")),
        name="prep_w_in",
    )(wt)


def _outproj_kernel(a_ref, y_ref, h_ref, x_ref, w_ref, o_ref, wb_ref):
    @pl.when(pl.program_id(0) == 0)
    def _():
        wb_ref[...] = w_ref[...].astype(BF16)

    acc = jnp.dot(a_ref[...], wb_ref[0:D_ATT, :], preferred_element_type=F32)
    acc = acc + jnp.dot(y_ref[...], wb_ref[D_ATT:D_ATT + D_SSM, :], preferred_element_type=F32)
    acc = acc + jnp.dot(h_ref[...], wb_ref[D_ATT + D_SSM:, :], preferred_element_type=F32)
    o_ref[...] = x_ref[...] + acc


def _outproj(att, y, h, x, w, layer, tm):
    m, n = x.shape
    k = w.shape[1]
    rows = lambda width: pl.BlockSpec((tm, width), lambda i: (i, 0))
    return pl.pallas_call(
        _outproj_kernel,
        grid=(m // tm,),
        in_specs=[rows(D_ATT), rows(D_SSM), rows(D_MLSTM), rows(n),
                  pl.BlockSpec((None, k, n), lambda i: (layer, 0, 0), pipeline_mode=pl.Buffered(1))],
        out_specs=rows(n),
        out_shape=jax.ShapeDtypeStruct((m, n), F32),
        scratch_shapes=[pltpu.VMEM((k, n), BF16)],
        compiler_params=_cparams(("arbitrary",)),
        name="outproj",
    )(att, y, h, x, w)


N_MLP_IN, N_SSM_SIDE_IN, N_MLSTM_SIDE_IN = 5, 5, 6


def _mlp_kernel(*refs, final_norm, side, n_prev):
    x_ref, g_ref, wu_ref, wd_ref, gf_ref = refs[:N_MLP_IN]
    n_side = N_SSM_SIDE_IN + N_MLSTM_SIDE_IN if side else 0
    side_in = refs[N_MLP_IN:N_MLP_IN + n_side]
    outs = refs[N_MLP_IN + n_side + n_prev:]
    o_ref = outs[0]
    u_ref = outs[5] if side else outs[1]
    j = pl.program_id(1)

    @pl.when(j == 0)
    def _():
        x = x_ref[...]
        ms = jnp.mean(x * x, axis=-1, keepdims=True)
        u_ref[...] = (x * lax.rsqrt(ms + EPS) * g_ref[...]).astype(BF16)
        o_ref[...] = x
        if side:
            outs[6][...] = jnp.zeros_like(outs[6])
            outs[7][...] = jnp.zeros_like(outs[7])

    h = _dot(u_ref[...], wu_ref[...])
    h = jnp.square(jnp.maximum(h, 0.0))
    o_ref[...] += _dot(h, wd_ref[...])

    if side:
        seq = pl.program_id(0) * pl.num_programs(1) + j
        row = seq % SEQ_ROWS
        so_ref, y_ref, co_ref, qc_ref = outs[1:5]
        yacc_ref, qacc_ref = outs[6:8]
        _ssm_sample_one(0, row, seq, *side_in[:N_SSM_SIDE_IN], so_ref, y_ref, yacc_ref)
        _mlstm_sample_one(0, row, seq, *side_in[N_SSM_SIDE_IN:], co_ref, qc_ref, qacc_ref)

    if final_norm:
        @pl.when(j == pl.num_programs(1) - 1)
        def _():
            y = o_ref[...]
            ms = jnp.mean(y * y, axis=-1, keepdims=True)
            o_ref[...] = y * lax.rsqrt(ms + EPS) * gf_ref[...]


def _mlp(x, g, wu, wd, gf, layer, final_norm, tm, tf, side=None):
    m, d = x.shape
    ff = wu.shape[2]
    nj = ff // tf
    in_specs = [pl.BlockSpec((tm, d), lambda i, j: (i, 0)),
                pl.BlockSpec((1, d), lambda i, j: (0, 0)),
                pl.BlockSpec((None, d, tf), lambda i, j: (layer, 0, j)),
                pl.BlockSpec((None, tf, d), lambda i, j: (layer, j, 0)),
                pl.BlockSpec((1, d), lambda i, j: (0, 0))]
    out_specs = [pl.BlockSpec((tm, d), lambda i, j: (i, 0))]
    out_shape = [jax.ShapeDtypeStruct((m, d), F32)]
    operands = [x, g, wu, wd, gf]
    aliases = {}
    n_prev = 0
    if side is not None:
        ea, ssm, xdt, bvec, cvec, gq, esw, mem, proj_s, ks, prev_s, prev_c = side
        depth, bd = ssm.shape[0], ssm.shape[1]
        assert (m // tm) * nj == bd and nj % SEQ_ROWS == 0
        smem = pl.BlockSpec(memory_space=pltpu.SMEM)
        rows = lambda n, col=0: pl.BlockSpec((SEQ_ROWS, n), lambda i, j: ((i * nj + j) // SEQ_ROWS, col // n))
        state = lambda r, n: pl.BlockSpec((None, 1, r, n), lambda i, j: (layer, i * nj + j, 0, 0))
        in_specs += [smem, state(D_SSM, D_STATE), rows(D_SSM), rows(SSM_GROUPS * D_STATE),
                     rows(SSM_GROUPS * D_STATE),
                     smem, smem, state(D_MLSTM, MLSTM_HEAD_DIM), rows(D_MLSTM, C_MQ), rows(D_MLSTM),
                     rows(D_MLSTM, C_MV)]
        operands += [ea, ssm, xdt, bvec, cvec, gq, esw, mem, proj_s, ks, proj_s]
        out_specs += [state(D_SSM, D_STATE), rows(D_SSM), state(D_MLSTM, MLSTM_HEAD_DIM), rows(D_MLSTM)]
        out_shape += [jax.ShapeDtypeStruct((depth, bd, D_SSM, D_STATE), F32),
                      jax.ShapeDtypeStruct((bd, D_SSM), F32),
                      jax.ShapeDtypeStruct((depth, bd, D_MLSTM, MLSTM_HEAD_DIM), F32),
                      jax.ShapeDtypeStruct((bd, D_MLSTM), F32)]
        if prev_s is not None:
            aliases = {len(operands): 1, len(operands) + 1: 3}
            operands += [prev_s, prev_c]
            in_specs += [pl.BlockSpec(memory_space=pl.ANY)] * 2
            n_prev = 2
    res = pl.pallas_call(
        functools.partial(_mlp_kernel, final_norm=final_norm, side=side is not None, n_prev=n_prev),
        grid=(m // tm, nj),
        in_specs=in_specs,
        out_specs=out_specs,
        out_shape=out_shape,
        input_output_aliases=aliases,
        scratch_shapes=[pltpu.VMEM((tm, d), BF16)] + ([] if side is None else [
            pltpu.VMEM((SEQ_ROWS, D_SSM), F32), pltpu.VMEM((SEQ_ROWS, D_MLSTM), F32)]),
        compiler_params=_cparams(("parallel", "arbitrary")),
        name="mlp",
    )(*operands)
    return res[0] if side is None else res


def _rope_tables(pos):
    half = ROPE_DIM // 2
    inv = jnp.power(jnp.float32(ROPE_THETA), -jnp.arange(half, dtype=jnp.float32) / half)
    ang = pos.astype(jnp.float32)[:, None] * inv[None, :]
    cos = jnp.cos(ang)
    sin = jnp.sin(ang)
    n = pos.shape[0]
    rest = ATT_HEAD_DIM - ROPE_DIM
    c = jnp.concatenate([cos, cos, jnp.ones((n, rest), F32)], axis=1)
    s1 = jnp.concatenate([-sin, jnp.zeros((n, half + rest), F32)], axis=1)
    s2 = jnp.concatenate([jnp.zeros((n, half), F32), sin, jnp.zeros((n, rest), F32)], axis=1)
    rep = LANE // ATT_HEAD_DIM
    return jnp.tile(c, (1, rep)), jnp.tile(s1, (1, rep)), jnp.tile(s2, (1, rep))


def _rope(x, c, s1, s2):
    width = x.shape[-1]
    rep = width // LANE
    half = ROPE_DIM // 2
    if rep > 1:
        c = jnp.concatenate([c] * rep, axis=-1)
        s1 = jnp.concatenate([s1] * rep, axis=-1)
        s2 = jnp.concatenate([s2] * rep, axis=-1)
    axis = x.ndim - 1
    return x * c + pltpu.roll(x, width - half, axis) * s1 + pltpu.roll(x, half, axis) * s2


def _attn_prompt_init(ko_ref, vo_ref):
    ko_ref[...] = jnp.zeros_like(ko_ref)
    vo_ref[...] = jnp.zeros_like(vo_ref)


def _attn_prompt_body(j, sink_ref, q_ref, k_ref, v_ref, c_ref, s1_ref, s2_ref, o_ref, ko_ref, vo_ref):
    w = WINDOW
    c = c_ref[...]
    s1 = s1_ref[...]
    s2 = s2_ref[...]
    krot = _rope(k_ref[...], c, s1, s2)
    v = v_ref[...]
    qrot = _rope(q_ref[...], c, s1, s2)
    kk = jnp.concatenate([ko_ref[...], krot], axis=0).astype(BF16)
    vv = jnp.concatenate([vo_ref[...], v], axis=0).astype(BF16)

    row = lax.broadcasted_iota(jnp.int32, (w, 2 * w), 0)
    col = lax.broadcasted_iota(jnp.int32, (w, 2 * w), 1)
    first_col = jnp.where(j > 0, 0, w)
    mask = (col >= row) & (col <= row + w) & (col >= first_col)
    scale = ATT_HEAD_DIM ** -0.5

    for h in range(ATT_KV_HEADS):
        kh = kk[:, h * ATT_HEAD_DIM:(h + 1) * ATT_HEAD_DIM]
        vh = vv[:, h * ATT_HEAD_DIM:(h + 1) * ATT_HEAD_DIM]
        for g in range(ATT_GROUP):
            hg = h * ATT_GROUP + g
            qh = qrot[:, hg * ATT_HEAD_DIM:(hg + 1) * ATT_HEAD_DIM]
            s = _dot_nt(qh, kh) * scale
            s = jnp.where(mask, s, -jnp.inf)
            sink = sink_ref[hg]
            m = jnp.maximum(jnp.max(s, axis=-1, keepdims=True), sink)
            p = jnp.exp(s - m)
            denom = jnp.sum(p, axis=-1, keepdims=True) + jnp.exp(sink - m)
            o = _dot(p, vh) / denom
            o_ref[:, hg * ATT_HEAD_DIM:(hg + 1) * ATT_HEAD_DIM] = o.astype(o_ref.dtype)
            yield

    ko_ref[...] = krot
    vo_ref[...] = v


def _group_rmsnorm(y, w, groups):
    width = y.shape[-1] // groups
    outs = []
    for g in range(groups):
        yg = y[:, g * width:(g + 1) * width]
        ms = jnp.mean(yg * yg, axis=-1, keepdims=True)
        outs.append(yg * lax.rsqrt(ms + EPS) * w[:, g * width:(g + 1) * width])
    return jnp.concatenate(outs, axis=-1)


CONV_PAD = 8


def _ssd_prompt_init(s_ref, tail_ref):
    tail_ref[...] = jnp.zeros_like(tail_ref)
    s_ref[...] = jnp.zeros_like(s_ref)


def _ssd_prompt_body(xbc_ref, z_ref, g_ref, cw_ref, cb_ref, dtb_ref, alog_ref, dskip_ref, wn_ref,
                     y_ref, s_ref, tail_ref, ybuf_ref, xw_ref):
    q = CHUNK
    pad = CONV_PAD
    hp = SSM_HEAD_DIM
    heads_per_group = SSM_HEADS // SSM_GROUPS
    gw = heads_per_group * hp

    x = xbc_ref[...]
    cw = cw_ref[...]
    row8 = lax.broadcasted_iota(jnp.int32, (pad, CONV_DIM), 0)
    acc = None
    for j in range(CONV_WIDTH - 1):
        shift = CONV_WIDTH - 1 - j
        rolled = pltpu.roll(x, shift, 0)
        head = jnp.where(row8 < shift, tail_ref[j * pad:(j + 1) * pad, :], rolled[0:pad, :])
        tail_ref[j * pad:(j + 1) * pad, :] = rolled[0:pad, :]
        term = jnp.concatenate([head, rolled[pad:, :]], axis=0) * cw[j:j + 1, :]
        acc = term if acc is None else acc + term
    acc = acc + x * cw[CONV_WIDTH - 1:CONV_WIDTH, :]
    xc = _silu(acc + cb_ref[...])

    xs = xc[:, 0:D_SSM]
    bm = xc[:, D_SSM:D_SSM + SSM_GROUPS * D_STATE]
    cm = xc[:, D_SSM + SSM_GROUPS * D_STATE:]

    dt = jax.nn.softplus(g_ref[:, 0:LANE] + dtb_ref[...])
    a_neg = -jnp.exp(alog_ref[...])
    a_col = _cumsum_rows(dt * a_neg)
    a_row = a_col.T
    dt_row = dt.T
    wk_row = jnp.exp(a_row[:, q - 1:q] - a_row) * dt_row
    tril = _tril(q)
    xs_t = xs.T
    ea_col = jnp.exp(a_col)
    left = lax.broadcasted_iota(jnp.int32, (q, LANE), 1) < hp
    ea_full = jnp.concatenate(
        [jnp.where(left, ea_col[:, 2 * t:2 * t + 1], ea_col[:, 2 * t + 1:2 * t + 2]) for t in range(SSM_HEADS // 2)],
        axis=1)
    yield

    cs_parts = []
    for g in range(SSM_GROUPS):
        bg = bm[:, g * D_STATE:(g + 1) * D_STATE]
        cg = cm[:, g * D_STATE:(g + 1) * D_STATE]
        cb = _dot_nt(cg, bg)
        cs_parts.append(_dot_nt(cg, s_ref[g * gw:(g + 1) * gw, :]))
        for r in range(heads_per_group):
            h = g * heads_per_group + r
            seg = a_col[:, h:h + 1] - a_row[h:h + 1, :]
            wmat = jnp.exp(jnp.where(tril, seg, -jnp.inf)) * cb * dt_row[h:h + 1, :]
            ybuf_ref[:, h * hp:(h + 1) * hp] = _dot(wmat, xs[:, h * hp:(h + 1) * hp])
            xw_ref[r * hp:(r + 1) * hp, :] = xs_t[h * hp:(h + 1) * hp, :] * wk_row[h:h + 1, :]
            yield
        upd = _dot(xw_ref[...], bg)
        for r in range(heads_per_group):
            h = g * heads_per_group + r
            decay = jnp.exp(a_row[h:h + 1, q - 1:q])
            s_ref[h * hp:(h + 1) * hp, :] = s_ref[h * hp:(h + 1) * hp, :] * decay + upd[r * hp:(r + 1) * hp, :]
        yield

    y = ybuf_ref[...] + jnp.concatenate(cs_parts, axis=1) * ea_full
    y = y + dskip_ref[...] * xs
    y = y * _silu(z_ref[...])
    y_ref[...] = _group_rmsnorm(y, wn_ref[...], SSM_GROUPS).astype(y_ref.dtype)


def _mlstm_prompt_init(c_ref, n_ref, m_ref):
    c_ref[...] = jnp.zeros_like(c_ref)
    n_ref[...] = jnp.zeros_like(n_ref)
    m_ref[...] = jnp.zeros_like(m_ref)


def _mlstm_prompt_body(q_ref, k_ref, v_ref, o_ref, g_ref, bi_ref, bf_ref, wn_ref, h_ref, c_ref, n_ref, m_ref):
    t = CHUNK
    d = MLSTM_HEAD_DIM

    gi = g_ref[:, 0:LANE] + bi_ref[...]
    gf = g_ref[:, LANE:2 * LANE] + bf_ref[...]
    b_col = _cumsum_rows(jax.nn.log_sigmoid(gf))
    b_row = b_col.T
    i_row = gi.T
    tril = _tril(t)
    k_all = k_ref[...] * (d ** -0.5)

    for h in range(MLSTM_HEADS):
        ln = GATE_LANE + h
        bq = b_col[:, ln:ln + 1]
        logw = jnp.where(tril, bq - b_row[ln:ln + 1, :] + i_row[ln:ln + 1, :], -jnp.inf)
        m_prev = m_ref[h:h + 1, 0:1]
        log_inter = bq + m_prev
        mt = jnp.maximum(log_inter, jnp.max(logw, axis=-1, keepdims=True))
        qh = q_ref[:, h * d:(h + 1) * d]
        kh = k_all[:, h * d:(h + 1) * d]
        vh = v_ref[:, h * d:(h + 1) * d]
        ch = c_ref[h * d:(h + 1) * d, :]
        nh = n_ref[h:h + 1, :]
        sw = jnp.exp(logw - mt) * _dot_nt(qh, kh)
        gq = jnp.exp(log_inter - mt)
        num = _dot(sw, vh) + _dot(qh, ch) * gq
        den = jnp.sum(sw, axis=-1, keepdims=True) + jnp.sum(qh * nh, axis=-1, keepdims=True) * gq
        hh = num / jnp.maximum(jnp.abs(den), jnp.exp(-mt))

        m_new = mt[t - 1:t, :]
        b_last = b_col[t - 1:t, ln:ln + 1]
        wk = jnp.exp(b_last - bq + gi[:, ln:ln + 1] - m_new)
        g_end = jnp.exp(b_last + m_prev - m_new)
        kw = kh * wk
        c_ref[h * d:(h + 1) * d, :] = ch * g_end + _dot_tn(kw, vh)
        n_ref[h:h + 1, :] = nh * g_end + jnp.sum(kw, axis=0, keepdims=True)
        m_ref[h:h + 1, :] = jnp.broadcast_to(m_new, (1, LANE))

        ms = jnp.mean(hh * hh, axis=-1, keepdims=True)
        hn = hh * lax.rsqrt(ms + EPS) * wn_ref[:, h * d:(h + 1) * d]
        h_ref[:, h * d:(h + 1) * d] = (hn * jax.nn.sigmoid(o_ref[:, h * d:(h + 1) * d])).astype(h_ref.dtype)
        yield

def _interleave(stages):
    live = [[gen, 0, n] for gen, n in stages]
    while live:
        item = min(live, key=lambda s: s[1] / s[2])
        try:
            next(item[0])
            item[1] += 1
        except StopIteration:
            live.remove(item)


PROJ_TILE = 256


def _project_next(x_ref, g_ref, w_ref, pnext_ref):
    x = x_ref[...]
    u = (x * lax.rsqrt(jnp.mean(x * x, axis=-1, keepdims=True) + EPS) * g_ref[...]).astype(BF16)
    for ct in range(NP // PROJ_TILE):
        cols = slice(ct * PROJ_TILE, (ct + 1) * PROJ_TILE)
        pnext_ref[:, cols] = jnp.dot(u, w_ref[:, cols], preferred_element_type=F32)
        yield


def _front_kernel(sink_ref, x_ref, x0_ref, g_ref, w_ref, c_ref, s1_ref, s2_ref,
                  cw_ref, cb_ref, dtb_ref, alog_ref, dskip_ref, wns_ref, bi_ref, bf_ref, wnm_ref,
                  att_ref, ko_ref, vo_ref, y_ref, s_ref, h_ref, mem_ref, n_ref, m_ref, ptail_ref,
                  pcur_ref, pnext_ref, tail_ref, ybuf_ref, xw_ref, *, nblk):
    j = pl.program_id(1)
    first = (pl.program_id(0) == 0) & (j == 0)
    t = CHUNK

    @pl.when(first)
    def _():
        for _ in _project_next(x0_ref, g_ref, w_ref, pnext_ref):
            pass

    @pl.when(j == 0)
    def _():
        _attn_prompt_init(ko_ref, vo_ref)
        _ssd_prompt_init(s_ref, tail_ref)
        _mlstm_prompt_init(mem_ref, n_ref, m_ref)

    pcur_ref[...] = pnext_ref[...]

    def attn(i):
        seg = lambda col, width: pcur_ref.at[i * t:(i + 1) * t, col:col + width]
        rows = lambda ref: ref.at[i * t:(i + 1) * t, :]
        return _attn_prompt_body(j * nblk + i, sink_ref, seg(C_Q, D_ATT), seg(C_K, D_KV), seg(C_V, D_KV),
                                 rows(c_ref), rows(s1_ref), rows(s2_ref), rows(att_ref), ko_ref, vo_ref)

    def ssd(i):
        seg = lambda col, width: pcur_ref.at[i * t:(i + 1) * t, col:col + width]
        return _ssd_prompt_body(seg(C_XBC, CONV_DIM), seg(C_Z, D_SSM), seg(C_G, 2 * LANE), cw_ref, cb_ref, dtb_ref,
                                alog_ref, dskip_ref, wns_ref, y_ref.at[i * t:(i + 1) * t, :], s_ref,
                                tail_ref, ybuf_ref, xw_ref)

    def mlstm(i):
        seg = lambda col, width: pcur_ref.at[i * t:(i + 1) * t, col:col + width]
        return _mlstm_prompt_body(seg(C_MQ, D_MLSTM), seg(C_MK, D_MLSTM), seg(C_MV, D_MLSTM), seg(C_MO, D_MLSTM),
                                  seg(C_G, 2 * LANE), bi_ref, bf_ref, wnm_ref, h_ref.at[i * t:(i + 1) * t, :],
                                  mem_ref, n_ref, m_ref)

    def blocks(body):
        for i in range(nblk):
            yield from body(i)

    ptail_ref[...] = pcur_ref[nblk * t - CONV_PAD:nblk * t, C_XBC:C_XBC + CONV_DIM]
    _interleave([
        (_project_next(x_ref, g_ref, w_ref, pnext_ref), NP // PROJ_TILE + 1),
        (blocks(attn), nblk * (ATT_HEADS + 1)),
        (blocks(ssd), nblk * (SSM_HEADS + SSM_GROUPS + 2)),
        (blocks(mlstm), nblk * (MLSTM_HEADS + 1)),
    ])


def _front(x, g_mix, w, layer, sinks, tables, cw, cb, dtb, alog, dskip_rep, wns, bi, bf, wnm, bsz, seq):
    nblk = 2 if (seq // CHUNK) % 2 == 0 else 1
    t = nblk * CHUNK
    nb = seq // t
    total = bsz * nb
    c, s1, s2 = tables
    row = lambda b, j: b * nb + j
    full = lambda shape: pl.BlockSpec(shape, lambda b, j: (0,) * len(shape))
    vec = lambda n: full((1, n))
    tab = pl.BlockSpec((t, LANE), lambda b, j: (j, 0))
    rows_out = lambda width: pl.BlockSpec((t, width), lambda b, j: (row(b, j), 0))
    per_seq = lambda r, n: pl.BlockSpec((None, r, n), lambda b, j: (b, 0, 0))
    return pl.pallas_call(
        functools.partial(_front_kernel, nblk=nblk),
        grid=(bsz, nb),
        in_specs=[pl.BlockSpec(memory_space=pltpu.SMEM),
                  pl.BlockSpec((t, D_MODEL), lambda b, j: (jnp.minimum(row(b, j) + 1, total - 1), 0)),
                  pl.BlockSpec((t, D_MODEL), lambda b, j: (0, 0), pipeline_mode=pl.Buffered(1)),
                  vec(D_MODEL),
                  pl.BlockSpec((None, D_MODEL, NP), lambda b, j: (layer, 0, 0), pipeline_mode=pl.Buffered(1)),
                  tab, tab, tab,
                  full((CONV_WIDTH, CONV_DIM)), vec(CONV_DIM), vec(LANE), vec(LANE), vec(D_SSM), vec(D_SSM),
                  vec(LANE), vec(LANE), vec(D_MLSTM)],
        out_specs=[rows_out(D_ATT), per_seq(CHUNK, D_KV), per_seq(CHUNK, D_KV),
                   rows_out(D_SSM), per_seq(D_SSM, D_STATE),
                   rows_out(D_MLSTM), per_seq(D_MLSTM, MLSTM_HEAD_DIM), per_seq(8, LANE), per_seq(8, LANE),
                   per_seq(CONV_PAD, CONV_DIM)],
        out_shape=[jax.ShapeDtypeStruct((bsz * seq, D_ATT), BF16),
                   jax.ShapeDtypeStruct((bsz, CHUNK, D_KV), F32),
                   jax.ShapeDtypeStruct((bsz, CHUNK, D_KV), F32),
                   jax.ShapeDtypeStruct((bsz * seq, D_SSM), BF16),
                   jax.ShapeDtypeStruct((bsz, D_SSM, D_STATE), F32),
                   jax.ShapeDtypeStruct((bsz * seq, D_MLSTM), BF16),
                   jax.ShapeDtypeStruct((bsz, D_MLSTM, MLSTM_HEAD_DIM), F32),
                   jax.ShapeDtypeStruct((bsz, 8, LANE), F32),
                   jax.ShapeDtypeStruct((bsz, 8, LANE), F32),
                   jax.ShapeDtypeStruct((bsz, CONV_PAD, CONV_DIM), F32)],
        scratch_shapes=[pltpu.VMEM((t, NP), F32),
                        pltpu.VMEM((t, NP), F32),
                        pltpu.VMEM(((CONV_WIDTH - 1) * CONV_PAD, CONV_DIM), F32),
                        pltpu.VMEM((CHUNK, D_SSM), F32),
                        pltpu.VMEM((D_SSM // SSM_GROUPS, CHUNK), F32)],
        compiler_params=_cparams(("arbitrary", "arbitrary")),
        name="front",
    )(sinks, x, x, g_mix, w, c, s1, s2, cw, cb, dtb, alog, dskip_rep, wns, bi, bf, wnm)


def _stacked_out(prev, depth, shape, block, layer):
    spec = pl.BlockSpec((None,) + block, lambda i: (layer,) + (i,) + (0,) * (len(block) - 1))
    out_shape = jax.ShapeDtypeStruct((depth,) + shape, F32)
    extra_in = [] if prev is None else [prev]
    extra_spec = [] if prev is None else [pl.BlockSpec(memory_space=pl.ANY)]
    return spec, out_shape, extra_in, extra_spec


def _drop_aliased(kernel_fn, n_in, n_prev):
    def wrapped(*refs):
        return kernel_fn(*refs[:n_in], *refs[n_in + n_prev:])
    return wrapped


def _attn_sample_kernel(q_ref, k_ref, v_ref, kc_ref, vc_ref, sink_ref, c_ref, s1_ref, s2_ref,
                        o_ref, ko_ref, vo_ref):
    tb, _, w = kc_ref.shape
    c = c_ref[...]
    s1 = s1_ref[...]
    s2 = s2_ref[...]
    qb = _rope(q_ref[...], c, s1, s2)
    kn = _rope(k_ref[...], c, s1, s2)
    vn = v_ref[...]
    kc = kc_ref[...]
    vc = vc_ref[...]
    newest = lax.broadcasted_iota(jnp.int32, (D_KV, w), 1) == w - 1
    for b in range(tb):
        ko_ref[b] = jnp.where(newest, _column_tile(kn[b]), pltpu.roll(kc[b], w - 1, 1))
        vo_ref[b] = jnp.where(newest, _column_tile(vn[b]), pltpu.roll(vc[b], w - 1, 1))

    scale = ATT_HEAD_DIM ** -0.5
    s = jnp.einsum("bhd,bdk->bhk", qb.astype(BF16), kc.astype(BF16),
                   preferred_element_type=F32) * scale
    sn = jnp.sum(qb * kn, axis=-1, keepdims=True) * scale
    sink = sink_ref[...][None, :, 0:1]
    m = jnp.maximum(jnp.maximum(jnp.max(s, axis=-1, keepdims=True), sn), sink)
    p = jnp.exp(s - m)
    pn = jnp.exp(sn - m)
    denom = jnp.sum(p, axis=-1, keepdims=True) + pn + jnp.exp(sink - m)
    p = p / denom
    pn = pn / denom
    o = jnp.einsum("bhk,bdk->bhd", p.astype(BF16), vc.astype(BF16), preferred_element_type=F32)
    o = o + pn * vn
    head = lax.broadcasted_iota(jnp.int32, o.shape[:2] + (ATT_HEAD_DIM,), 1)
    o_ref[...] = jnp.where(head < ATT_GROUP, o[:, :, 0:ATT_HEAD_DIM], o[:, :, ATT_HEAD_DIM:]).astype(o_ref.dtype)


def _attn_sample(qb, kn, vn, kc, vc, sink_rows, tables, layer, prev, tb):
    depth, bd, _, w = kc.shape
    c, s1, s2 = tables
    blk3 = lambda r, n: pl.BlockSpec((tb, r, n), lambda i: (i, 0, 0))
    cache = pl.BlockSpec((None, tb, D_KV, w), lambda i: (layer, i, 0, 0))
    vec = pl.BlockSpec((1, LANE), lambda i: (0, 0))
    prev_k, prev_v = (None, None) if prev is None else prev
    k_spec, k_shape, k_in, k_in_spec = _stacked_out(prev_k, depth, (bd, D_KV, w), (tb, D_KV, w), layer)
    v_spec, v_shape, v_in, v_in_spec = _stacked_out(prev_v, depth, (bd, D_KV, w), (tb, D_KV, w), layer)
    n_in = 9
    n_prev = len(k_in) + len(v_in)
    return pl.pallas_call(
        _drop_aliased(_attn_sample_kernel, n_in, n_prev),
        grid=(bd // tb,),
        in_specs=[blk3(ATT_HEADS, LANE), blk3(1, LANE), blk3(1, LANE), cache, cache,
                  pl.BlockSpec((ATT_HEADS, LANE), lambda i: (0, 0)), vec, vec, vec] + k_in_spec + v_in_spec,
        out_specs=[blk3(ATT_HEADS, ATT_HEAD_DIM), k_spec, v_spec],
        out_shape=[jax.ShapeDtypeStruct((bd, ATT_HEADS, ATT_HEAD_DIM), BF16), k_shape, v_shape],
        input_output_aliases={n_in + t: 1 + t for t in range(n_prev)},
        compiler_params=_cparams(("parallel",)),
        name="attn_sample",
    )(qb, kn, vn, kc, vc, sink_rows, c, s1, s2, *k_in, *v_in)


def _lane_place(cols, lane0):
    m = cols[0].shape[0]
    lane = lax.broadcasted_iota(jnp.int32, (m, LANE), 1)
    out = jnp.zeros((m, LANE), F32)
    for i, col in enumerate(cols):
        out = jnp.where(lane == lane0 + i, col, out)
    return out


def _sample_pre_kernel(x_ref, gmix_ref, w_ref, cs_ref, cw_ref, cb_ref, dtb_ref, alog_ref, bi_ref, bf_ref, n_ref,
                       m_ref, proj_ref, conv_ref, xs_ref, xdt_ref, bm_ref, cm_ref, ea_ref,
                       ks_ref, nn_ref, g_ref, esw_ref, sw_ref, den_ref, mt_ref):
    for _ in _project_next(x_ref, gmix_ref, w_ref, proj_ref):
        pass
    xbc = proj_ref[:, C_XBC:C_XBC + CONV_DIM]
    s0 = cs_ref[0]
    s1 = cs_ref[1]
    s2 = cs_ref[2]
    cw = cw_ref[...]
    acc = s0 * cw[0:1, :]
    acc = acc + s1 * cw[1:2, :]
    acc = acc + s2 * cw[2:3, :]
    acc = acc + xbc * cw[3:4, :]
    xc = _silu(acc + cb_ref[...])
    conv_ref[0] = s1
    conv_ref[1] = s2
    conv_ref[2] = xbc

    xs = xc[:, 0:D_SSM]
    xs_ref[...] = xs
    bm_ref[...] = xc[:, D_SSM:D_SSM + SSM_GROUPS * D_STATE]
    cm_ref[...] = xc[:, D_SSM + SSM_GROUPS * D_STATE:]
    dt = jax.nn.softplus(proj_ref[:, C_G:C_G + LANE] + dtb_ref[...])
    ea_ref[...] = jnp.exp(dt * (-jnp.exp(alog_ref[...])))
    hp = SSM_HEAD_DIM
    for h in range(SSM_HEADS):
        xdt_ref[:, h * hp:(h + 1) * hp] = xs[:, h * hp:(h + 1) * hp] * dt[:, h:h + 1]

    d = MLSTM_HEAD_DIM
    gi = proj_ref[:, C_G:C_G + LANE] + bi_ref[...]
    gf = proj_ref[:, C_G + LANE:C_G + 2 * LANE] + bf_ref[...]
    log_inter = jax.nn.log_sigmoid(gf) + m_ref[...]
    mt = jnp.maximum(log_inter, gi)
    gq = jnp.exp(log_inter - mt)
    esw = jnp.exp(gi - mt)
    ks = proj_ref[:, C_MK:C_MK + D_MLSTM] * (d ** -0.5)
    ks_ref[...] = ks
    qk_cols = []
    qn_cols = []
    for h in range(MLSTM_HEADS):
        qh = proj_ref[:, C_MQ + h * d:C_MQ + (h + 1) * d]
        kh = ks[:, h * d:(h + 1) * d]
        nh = n_ref[:, h * d:(h + 1) * d]
        qk_cols.append(jnp.sum(qh * kh, axis=-1, keepdims=True))
        qn_cols.append(jnp.sum(qh * nh, axis=-1, keepdims=True))
        ln = GATE_LANE + h
        nn_ref[:, h * d:(h + 1) * d] = nh * gq[:, ln:ln + 1] + kh * esw[:, ln:ln + 1]
    sw = esw * _lane_place(qk_cols, GATE_LANE)
    den = sw + _lane_place(qn_cols, GATE_LANE) * gq
    g_ref[...] = gq
    esw_ref[...] = esw
    sw_ref[...] = sw
    den_ref[...] = jnp.maximum(jnp.abs(den), jnp.exp(-mt))
    mt_ref[...] = mt


def _sample_pre(x, g_mix, w, conv_state, layer, cw, cb, dtb, alog, bi, bf, n_state, m_tile):
    bd = x.shape[0]
    taps = CONV_WIDTH - 1
    tile = jax.ShapeDtypeStruct((bd, LANE), F32)
    whole = lambda a: pl.BlockSpec(a.shape, lambda i: (0,) * a.ndim)
    in_specs = [whole(x), whole(g_mix),
                pl.BlockSpec((None,) + w.shape[1:], lambda i: (layer, 0, 0), pipeline_mode=pl.Buffered(1)),
                pl.BlockSpec((None, taps, bd, CONV_DIM), lambda i: (layer, 0, 0, 0))]
    in_specs += [whole(a) for a in (cw, cb, dtb, alog, bi, bf, n_state, m_tile)]
    outs = [jax.ShapeDtypeStruct((bd, NP), F32),
            jax.ShapeDtypeStruct((taps, bd, CONV_DIM), F32),
            jax.ShapeDtypeStruct((bd, D_SSM), F32),
            jax.ShapeDtypeStruct((bd, D_SSM), F32),
            jax.ShapeDtypeStruct((bd, SSM_GROUPS * D_STATE), F32),
            jax.ShapeDtypeStruct((bd, SSM_GROUPS * D_STATE), F32),
            tile,
            jax.ShapeDtypeStruct((bd, D_MLSTM), F32),
            jax.ShapeDtypeStruct((bd, D_MLSTM), F32),
            tile, tile, tile, tile, tile]
    return pl.pallas_call(
        _sample_pre_kernel,
        grid=(1,),
        in_specs=in_specs,
        out_specs=[pl.BlockSpec(o.shape, lambda i, nd=len(o.shape): (0,) * nd) for o in outs],
        out_shape=outs,
        compiler_params=_cparams(("arbitrary",)),
        name="sample_pre",
    )(x, g_mix, w, conv_state, cw, cb, dtb, alog, bi, bf, n_state, m_tile)


def _column_tile(row):
    return jnp.broadcast_to(row, (LANE, LANE)).T


SEQ_ROWS = 8


def _put_row(out_ref, acc_ref, vr, col, row):
    width = row.shape[1]
    rid = lax.broadcasted_iota(jnp.int32, (SEQ_ROWS, width), 0)
    blk = jnp.where(rid == vr, row, acc_ref[:, col:col + width])
    acc_ref[:, col:col + width] = blk
    out_ref[:, col:col + width] = blk


def _ssm_sample_one(b, vr, gb, ea_ref, s_ref, xdt_ref, b_ref, c_ref, so_ref, y_ref, yacc_ref):
    hp = SSM_HEAD_DIM
    heads_per_tile = LANE // hp
    n_tiles = D_SSM // LANE
    tiles_per_group = n_tiles // SSM_GROUPS
    top = lax.broadcasted_iota(jnp.int32, (LANE, 1), 0) < hp
    xrow = xdt_ref[pl.ds(vr, 1), :]
    brow = b_ref[pl.ds(vr, 1), :]
    crow = c_ref[pl.ds(vr, 1), :]
    for t in range(n_tiles):
        g = t // tiles_per_group
        xcol = _column_tile(xrow[:, t * LANE:(t + 1) * LANE])
        bg = brow[:, g * D_STATE:(g + 1) * D_STATE]
        cg = jnp.broadcast_to(crow[:, g * D_STATE:(g + 1) * D_STATE], (8, D_STATE))
        e0 = ea_ref[gb, heads_per_tile * t]
        e1 = ea_ref[gb, heads_per_tile * t + 1]
        decay = jnp.where(top, e0, e1)
        sn = s_ref[b, t * LANE:(t + 1) * LANE, :] * decay + xcol * bg
        so_ref[b, t * LANE:(t + 1) * LANE, :] = sn
        _put_row(y_ref, yacc_ref, vr, t * LANE, _dot_nt(cg, sn)[0:1, :])


def _mlstm_sample_one(b, vr, gb, g_ref, esw_ref, c_ref, q_ref, k_ref, v_ref, co_ref, qc_ref, qacc_ref):
    d = MLSTM_HEAD_DIM
    qrow = q_ref[pl.ds(vr, 1), :]
    krow = k_ref[pl.ds(vr, 1), :]
    vrow = v_ref[pl.ds(vr, 1), :]
    for h in range(MLSTM_HEADS):
        qcol = _column_tile(qrow[:, h * d:(h + 1) * d])
        kcol = _column_tile(krow[:, h * d:(h + 1) * d])
        ch = c_ref[b, h * d:(h + 1) * d, :]
        g_end = g_ref[gb, GATE_LANE + h]
        wk = esw_ref[gb, GATE_LANE + h]
        _put_row(qc_ref, qacc_ref, vr, h * d, jnp.sum(qcol * ch, axis=0, keepdims=True))
        co_ref[b, h * d:(h + 1) * d, :] = ch * g_end + (kcol * wk) * vrow[:, h * d:(h + 1) * d]


def _sample_post_kernel(proj_ref, y_ref, xs_ref, dskip_ref, wns_ref, qc_ref, g_ref, sw_ref, den_ref, wnm_ref,
                        ys_ref, hs_ref):
    y = y_ref[...] + dskip_ref[...] * xs_ref[...]
    y = y * _silu(proj_ref[:, C_Z:C_Z + D_SSM])
    ys_ref[...] = _group_rmsnorm(y, wns_ref[...], SSM_GROUPS).astype(ys_ref.dtype)

    d = MLSTM_HEAD_DIM
    for h in range(MLSTM_HEADS):
        ln = GATE_LANE + h
        vh = proj_ref[:, C_MV + h * d:C_MV + (h + 1) * d]
        num = sw_ref[:, ln:ln + 1] * vh + qc_ref[:, h * d:(h + 1) * d] * g_ref[:, ln:ln + 1]
        hh = num / den_ref[:, ln:ln + 1]
        ms = jnp.mean(hh * hh, axis=-1, keepdims=True)
        hn = hh * lax.rsqrt(ms + EPS) * wnm_ref[:, h * d:(h + 1) * d]
        gate = jax.nn.sigmoid(proj_ref[:, C_MO + h * d:C_MO + (h + 1) * d])
        hs_ref[:, h * d:(h + 1) * d] = (hn * gate).astype(hs_ref.dtype)


N_POST_IN = 10


def _decode_tail_kernel(*refs, final_norm):
    post_in = refs[:N_POST_IN]
    att_ref, x_ref, wo_ref, g_ref, wu_ref, wd_ref, gf_ref, o_ref, u_ref, ys_ref, hm_ref = refs[N_POST_IN:]
    j = pl.program_id(0)

    @pl.when(j == 0)
    def _():
        _sample_post_kernel(*post_in, ys_ref, hm_ref)
        acc = _dot(att_ref[...], wo_ref[0:D_ATT, :])
        acc = acc + _dot(ys_ref[...], wo_ref[D_ATT:D_ATT + D_SSM, :])
        acc = acc + _dot(hm_ref[...], wo_ref[D_ATT + D_SSM:, :])
        x1 = x_ref[...] + acc
        ms = jnp.mean(x1 * x1, axis=-1, keepdims=True)
        u_ref[...] = (x1 * lax.rsqrt(ms + EPS) * g_ref[...]).astype(BF16)
        o_ref[...] = x1

    h = _dot(u_ref[...], wu_ref[...])
    h = jnp.square(jnp.maximum(h, 0.0))
    o_ref[...] += _dot(h, wd_ref[...])

    if final_norm:
        @pl.when(j == pl.num_programs(0) - 1)
        def _():
            y = o_ref[...]
            ms = jnp.mean(y * y, axis=-1, keepdims=True)
            o_ref[...] = y * lax.rsqrt(ms + EPS) * gf_ref[...]


def _decode_tail(post_args, att, x, w_out, g, wu, wd, gf, layer, final_norm, tf):
    bd, d = x.shape
    ff = wu.shape[2]
    whole = lambda a: pl.BlockSpec(a.shape, lambda j: (0,) * a.ndim)
    in_specs = [whole(a) for a in post_args] + [
        whole(att), whole(x),
        pl.BlockSpec((None, w_out.shape[1], d), lambda j: (layer, 0, 0), pipeline_mode=pl.Buffered(1)),
        whole(g),
        pl.BlockSpec((None, d, tf), lambda j: (layer, 0, j)),
        pl.BlockSpec((None, tf, d), lambda j: (layer, j, 0)),
        whole(gf)]
    assert len(post_args) == N_POST_IN
    return pl.pallas_call(
        functools.partial(_decode_tail_kernel, final_norm=final_norm),
        grid=(ff // tf,),
        in_specs=in_specs,
        out_specs=pl.BlockSpec((bd, d), lambda j: (0, 0)),
        out_shape=jax.ShapeDtypeStruct((bd, d), F32),
        scratch_shapes=[pltpu.VMEM((bd, d), BF16), pltpu.VMEM((bd, D_SSM), BF16), pltpu.VMEM((bd, D_MLSTM), BF16)],
        compiler_params=_cparams(("arbitrary",)),
        name="decode_tail",
    )(*post_args, att, x, w_out, g, wu, wd, gf)


def _lane_vec(v, lane0):
    return jnp.pad(v.astype(F32), (lane0, LANE - lane0 - v.shape[0]))[None, :]


def _pick(n, candidates):
    for c in candidates:
        if n % c == 0:
            return c
    return n


def kernel(x_prompt, x_sample, cache_swa_k, cache_swa_v, state_conv, state_ssm, state_mlstm_C, state_mlstm_n,
           state_mlstm_m, w_norm_mix, w_in, attn_sinks, conv_w, conv_b, dt_bias, a_log, d_skip, w_norm_ssm,
           igate_b, fgate_b, w_norm_mlstm, w_out, w_norm_mlp, w_up, w_down, w_norm_final):
    bsz, seq, d_model = x_prompt.shape
    bd = x_sample.shape[0]
    depth = w_in.shape[0]
    win = cache_swa_k.shape[2]
    assert x_sample.shape[1] == 1 and seq % CHUNK == 0 and d_model == D_MODEL

    mp = bsz * seq
    tm_mlp = _pick(mp, (1024, 512, 256, 128))
    tb = _pick(bd, (16, 8))

    hp = x_prompt.reshape(mp, d_model)
    hs = x_sample.reshape(bd, d_model)
    tab_p = _rope_tables(jnp.arange(seq, dtype=jnp.int32))
    tab_s = _rope_tables(jnp.arange(1, dtype=jnp.int32) + PAST_LEN)
    gf = w_norm_final[None, :]
    kc_all = jnp.transpose(cache_swa_k, (0, 1, 3, 4, 2)).reshape(depth, bd, D_KV, win)
    vc_all = jnp.transpose(cache_swa_v, (0, 1, 3, 4, 2)).reshape(depth, bd, D_KV, win)
    ssm_all = state_ssm.reshape(depth, bd, D_SSM, D_STATE)
    mem_all = state_mlstm_C.reshape(depth, bd, D_MLSTM, MLSTM_HEAD_DIM)
    conv_all = jnp.swapaxes(state_conv, 1, 2)

    w_in_all = _prep_w_in(w_in, 256)
    st_p, st_s = [], []
    kv_new = s_new = c_new = None
    for l in range(depth):
        g_mix = w_norm_mix[l][None, :]
        g_mlp = w_norm_mlp[l][None, :]
        cw = conv_w[l]
        cb = conv_b[l][None, :]
        dtb = _lane_vec(dt_bias[l], 0)
        alog = _lane_vec(a_log[l], 0)
        dskip_rep = jnp.repeat(d_skip[l].astype(F32), SSM_HEAD_DIM)[None, :]
        wns = w_norm_ssm[l][None, :]
        bi = _lane_vec(igate_b[l], GATE_LANE)
        bf = _lane_vec(fgate_b[l], GATE_LANE)
        wnm = w_norm_mlstm[l][None, :]
        sinks = attn_sinks[l].reshape(ATT_HEADS).astype(F32)
        last = l == depth - 1

        att, pk, pv, y, p_ssm, hm, p_c, p_n, p_m, p_tail = _front(
            hp, g_mix, w_in_all, l, sinks, tab_p, cw, cb, dtb, alog, dskip_rep, wns, bi, bf, wnm, bsz, seq)
        x1 = _outproj(att, y, hm, hp, w_out, l, _pick(mp, (512, 256, 128)))
        p_conv = p_tail[:, CONV_PAD - (CONV_WIDTH - 1):, :]
        st_p.append((pk.reshape(bsz, WINDOW, ATT_KV_HEADS, ATT_HEAD_DIM),
                     pv.reshape(bsz, WINDOW, ATT_KV_HEADS, ATT_HEAD_DIM),
                     p_conv,
                     p_ssm.reshape(bsz, SSM_HEADS, SSM_HEAD_DIM, D_STATE),
                     p_c.reshape(bsz, MLSTM_HEADS, MLSTM_HEAD_DIM, MLSTM_HEAD_DIM),
                     p_n[:, 0:MLSTM_HEADS, :],
                     p_m[:, 0:MLSTM_HEADS, 0]))

        m_tile = jnp.pad(state_mlstm_m[l], ((0, 0), (GATE_LANE, LANE - GATE_LANE - MLSTM_HEADS)))
        (proj_s, conv_new, xs, xdt, bm, cm, ea, ks, n_new, gq, esw, sw, den, mt) = _sample_pre(
            hs, g_mix, w_in_all, conv_all, l, cw, cb, dtb, alog, bi, bf, state_mlstm_n[l].reshape(bd, D_MLSTM),
            m_tile)
        q8 = proj_s[:, C_Q:C_Q + D_ATT].reshape(bd, ATT_HEADS, ATT_HEAD_DIM)
        zero = jnp.zeros_like(q8)
        first_kv = (jnp.arange(ATT_HEADS) < ATT_GROUP)[None, :, None]
        qb = jnp.where(first_kv, jnp.concatenate([q8, zero], -1), jnp.concatenate([zero, q8], -1))
        kn = proj_s[:, C_K:C_K + D_KV].reshape(bd, 1, D_KV)
        vn = proj_s[:, C_V:C_V + D_KV].reshape(bd, 1, D_KV)
        sink_rows = jnp.broadcast_to(sinks[:, None], (ATT_HEADS, LANE))
        att_s, sk, sv = _attn_sample(qb, kn, vn, kc_all, vc_all, sink_rows, tab_s, l, kv_new, tb)
        kv_new = (sk, sv)
        side = (ea, ssm_all, xdt, bm, cm, gq, esw, mem_all, proj_s, ks, s_new, c_new)
        hp, s_new, y_read, c_new, qc = _mlp(x1, g_mlp, w_up, w_down, gf, l, last, tm_mlp, 512, side)
        hs = _decode_tail((proj_s, y_read, xs, dskip_rep, wns, qc, gq, sw, den, wnm), att_s.reshape(bd, D_ATT), hs,
                          w_out, g_mlp, w_up, w_down, gf, l, last, 512)
        st_s.append((conv_new,
                     n_new.reshape(bd, MLSTM_HEADS, MLSTM_HEAD_DIM),
                     mt[:, GATE_LANE:GATE_LANE + MLSTM_HEADS]))

    y_prompt = hp.reshape(bsz, seq, d_model)
    y_sample = hs.reshape(bd, 1, d_model)
    p_out = [jnp.stack([s[i] for s in st_p]) for i in range(7)]
    s_conv, s_n, s_m = [jnp.stack([s[i] for s in st_s]) for i in range(3)]
    s_conv = jnp.swapaxes(s_conv, 1, 2)
    to_cache = lambda a: jnp.transpose(a.reshape(depth, bd, ATT_KV_HEADS, ATT_HEAD_DIM, win), (0, 1, 4, 2, 3))
    s_k = to_cache(kv_new[0])
    s_v = to_cache(kv_new[1])
    s_ssm = s_new.reshape(depth, bd, SSM_HEADS, SSM_HEAD_DIM, D_STATE)
    s_c = c_new.reshape(depth, bd, MLSTM_HEADS, MLSTM_HEAD_DIM, MLSTM_HEAD_DIM)
    return (y_prompt, y_sample, *p_out, s_k, s_v, s_conv, s_ssm, s_c, s_n, s_m)
```
